```python
import math
import jax, jax.numpy as jnp
from jax import lax
import numpy as np

D_MODEL = 1024
BATCH = 8
SEQ = 2048
DEPTH = 2

N_MIXERS = 2
A_HEADS = 16
A_KV_HEADS = 2
A_HEAD_DIM = 64
WINDOW = 128
BLOCK = 128
NUM_BUCKETS = 32
MAX_DISTANCE = 128
B_HEADS = 16
Q_LORA = 256
KV_LORA = 128
QK_NOPE = 64
QK_ROPE = 32
V_DIM = 64
ROPE_BASE = 10000.0
D_FF = 2816
EPS = 1e-6
NEG = -1e30

N_A_LAYERS = (DEPTH + 1) // 2
N_B_LAYERS = DEPTH // 2
A_IN_COLS = (A_HEADS + 2 * A_KV_HEADS) * A_HEAD_DIM
B_IN_COLS = Q_LORA + KV_LORA + QK_ROPE

kernel_name = "hybrid_swa_sink_mla_macaron"


def rmsnorm(x, g):
    xf = x.astype(jnp.float32)
    y = xf * lax.rsqrt(jnp.mean(xf * xf, axis=-1, keepdims=True) + EPS)
    return (y * g.astype(jnp.float32)).astype(x.dtype)


def swiglu(h, wg, wu, wd):
    return (jax.nn.silu(h @ wg) * (h @ wu)) @ wd


def t5_bucket(dist):
    n = jnp.maximum(dist, 0)
    max_exact = NUM_BUCKETS // 2
    large = max_exact + (jnp.log(jnp.maximum(n, 1).astype(jnp.float32) / max_exact)
                         / math.log(MAX_DISTANCE / max_exact)
                         * (NUM_BUCKETS - max_exact)).astype(jnp.int32)
    large = jnp.minimum(large, NUM_BUCKETS - 1)
    return jnp.where(n < max_exact, n, large)


def apply_rope(t, pos):
    half = QK_ROPE // 2
    inv_freq = ROPE_BASE ** (-jnp.arange(0, QK_ROPE, 2, dtype=jnp.float32) / QK_ROPE)
    ang = pos.astype(jnp.float32)[..., None] * inv_freq
    cos = jnp.cos(ang)[:, :, None, :]
    sin = jnp.sin(ang)[:, :, None, :]
    t1, t2 = t[..., :half].astype(jnp.float32), t[..., half:].astype(jnp.float32)
    return jnp.concatenate([t1 * cos - t2 * sin, t2 * cos + t1 * sin], axis=-1).astype(t.dtype)


def band(t, nb):
    B = t.shape[0]
    rest = t.shape[2:]
    pad = jnp.zeros((B, BLOCK) + rest, t.dtype)
    tp = jnp.concatenate([pad, t], axis=1).reshape((B, nb + 1, BLOCK) + rest)
    return jnp.concatenate([tp[:, :-1], tp[:, 1:]], axis=2)


def sliding_window_attention(h, pos, rel_bias, w_in, q_gain, k_gain, sinks, w_out):
    B, S, _ = h.shape
    nb = S // BLOCK
    G = A_HEADS // A_KV_HEADS
    qkv = h @ w_in
    q, k, v = jnp.split(qkv, [A_HEADS * A_HEAD_DIM, (A_HEADS + A_KV_HEADS) * A_HEAD_DIM], axis=-1)
    q = rmsnorm(q.reshape(B, S, A_HEADS, A_HEAD_DIM), q_gain)
    k = rmsnorm(k.reshape(B, S, A_KV_HEADS, A_HEAD_DIM), k_gain)
    v = v.reshape(B, S, A_KV_HEADS, A_HEAD_DIM)
    q = q.reshape(B, nb, BLOCK, A_KV_HEADS, G, A_HEAD_DIM)
    kb, vb = band(k, nb), band(v, nb)
    posb = band(pos, nb)
    scale = A_HEAD_DIM ** -0.5
    scores = jnp.einsum('bnqkgd,bnskd->bnkgqs', q, kb).astype(jnp.float32) * scale
    dist = pos.reshape(B, nb, BLOCK)[..., :, None] - posb[..., None, :]
    bias = rel_bias[t5_bucket(dist)].astype(jnp.float32)
    bias = bias.reshape(B, nb, BLOCK, 2 * BLOCK, A_KV_HEADS, G).transpose(0, 1, 4, 5, 2, 3)
    scores = scores + bias
    qi = jnp.arange(BLOCK)[:, None] + BLOCK
    si = jnp.arange(2 * BLOCK)[None, :]
    rel = qi - si
    blk = jnp.arange(nb)[:, None, None]
    valid = (rel >= 0) & (rel < WINDOW) & (blk * BLOCK + si - BLOCK >= 0)
    scores = jnp.where(valid[None, :, None, None], scores, NEG)
    sink = sinks.astype(jnp.float32).reshape(A_KV_HEADS, G)[None, None, :, :, None, None]
    m = jnp.maximum(jnp.max(scores, axis=-1, keepdims=True), sink)
    p = jnp.exp(scores - m)
    denom = jnp.sum(p, axis=-1, keepdims=True) + jnp.exp(sink - m)
    probs = (p / denom).astype(v.dtype)
    o = jnp.einsum('bnkgqs,bnskd->bnqkgd', probs, vb).reshape(B, S, A_HEADS * A_HEAD_DIM)
    return o @ w_out


def latent_attention(h, pos, w_in, q_norm, kv_norm, w_uq, w_ukv, q_gain, k_gain, w_out):
    B, S, _ = h.shape
    nb = S // BLOCK
    dqk = QK_NOPE + QK_ROPE
    c = h @ w_in
    c_q, c_kv, k_rope = jnp.split(c, [Q_LORA, Q_LORA + KV_LORA], axis=-1)
    q = (rmsnorm(c_q, q_norm) @ w_uq).reshape(B, S, B_HEADS, dqk)
    kv = (rmsnorm(c_kv, kv_norm) @ w_ukv).reshape(B, S, B_HEADS, QK_NOPE + V_DIM)
    k_nope, v = jnp.split(kv, [QK_NOPE], axis=-1)
    k = jnp.concatenate(
        [k_nope, jnp.broadcast_to(k_rope[:, :, None, :], (B, S, B_HEADS, QK_ROPE))], axis=-1)
    q = rmsnorm(q, q_gain)
    k = rmsnorm(k, k_gain)
    q = jnp.concatenate([q[..., :QK_NOPE], apply_rope(q[..., QK_NOPE:], pos)], axis=-1)
    k = jnp.concatenate([k[..., :QK_NOPE], apply_rope(k[..., QK_NOPE:], pos)], axis=-1)
    scale = dqk ** -0.5
    qb = q.reshape(B, nb, BLOCK, B_HEADS, dqk).transpose(1, 0, 2, 3, 4)
    key_idx = jnp.arange(S)

    def attend(args):
        q_blk, n = args
        s = jnp.einsum('bqhd,bshd->bhqs', q_blk, k).astype(jnp.float32) * scale
        q_idx = n * BLOCK + jnp.arange(BLOCK)
        s = jnp.where(key_idx[None, :] <= q_idx[:, None], s, NEG)
        p = jax.nn.softmax(s, axis=-1).astype(v.dtype)
        return jnp.einsum('bhqs,bshd->bqhd', p, v)

    o = lax.map(attend, (qb, jnp.arange(nb)))
    o = o.transpose(1, 0, 2, 3, 4).reshape(B, S, B_HEADS * V_DIM)
    return o @ w_out


def setup_inputs(seed: int = 0) -> dict:
    key = jax.random.key(seed)
    ks = jax.random.split(key, 32)
    f32 = jnp.float32

    def w(k, shape, fan_in):
        return jax.random.normal(k, shape, f32) * (fan_in ** -0.5)

    def gain(k, shape):
        return 1.0 + 0.05 * jax.random.normal(k, shape, f32)

    x = jax.random.normal(ks[0], (BATCH, SEQ, D_MODEL), f32)
    offsets = jax.random.randint(ks[1], (BATCH, 1), 0, 64, dtype=jnp.int32)
    positions = (jnp.arange(SEQ, dtype=jnp.int32)[None, :] + offsets).astype(jnp.int32)
    return {
        "x": x,
        "positions": positions,
        "rel_bias": 0.5 * jax.random.normal(ks[2], (NUM_BUCKETS, A_HEADS), f32),
        "ffn_norm1": gain(ks[3], (DEPTH, D_MODEL)),
        "ffn1_wg": w(ks[4], (DEPTH, D_MODEL, D_FF), D_MODEL),
        "ffn1_wu": w(ks[5], (DEPTH, D_MODEL, D_FF), D_MODEL),
        "ffn1_wd": w(ks[6], (DEPTH, D_FF, D_MODEL), D_FF),
        "mix_norm": gain(ks[7], (DEPTH, D_MODEL)),
        "ffn_norm2": gain(ks[8], (DEPTH, D_MODEL)),
        "ffn2_wg": w(ks[9], (DEPTH, D_MODEL, D_FF), D_MODEL),
        "ffn2_wu": w(ks[10], (DEPTH, D_MODEL, D_FF), D_MODEL),
        "ffn2_wd": w(ks[11], (DEPTH, D_FF, D_MODEL), D_FF),
        "a_w_in": w(ks[12], (N_A_LAYERS, D_MODEL, A_IN_COLS), D_MODEL),
        "a_q_gain": gain(ks[13], (N_A_LAYERS, A_HEAD_DIM)),
        "a_k_gain": gain(ks[14], (N_A_LAYERS, A_HEAD_DIM)),
        "a_sinks": jax.random.normal(ks[15], (N_A_LAYERS, A_HEADS), f32),
        "a_w_out": w(ks[16], (N_A_LAYERS, A_HEADS * A_HEAD_DIM, D_MODEL), A_HEADS * A_HEAD_DIM),
        "b_w_in": w(ks[17], (N_B_LAYERS, D_MODEL, B_IN_COLS), D_MODEL),
        "b_q_norm": gain(ks[18], (N_B_LAYERS, Q_LORA)),
        "b_kv_norm": gain(ks[19], (N_B_LAYERS, KV_LORA)),
        "b_w_uq": w(ks[20], (N_B_LAYERS, Q_LORA, B_HEADS * (QK_NOPE + QK_ROPE)), Q_LORA),
        "b_w_ukv": w(ks[21], (N_B_LAYERS, KV_LORA, B_HEADS * (QK_NOPE + V_DIM)), KV_LORA),
        "b_q_gain": gain(ks[22], (N_B_LAYERS, QK_NOPE + QK_ROPE)),
        "b_k_gain": gain(ks[23], (N_B_LAYERS, QK_NOPE + QK_ROPE)),
        "b_w_out": w(ks[24], (N_B_LAYERS, B_HEADS * V_DIM, D_MODEL), B_HEADS * V_DIM),
    }


def reference(x, positions, rel_bias, ffn_norm1, ffn1_wg, ffn1_wu, ffn1_wd, mix_norm,
              ffn_norm2, ffn2_wg, ffn2_wu, ffn2_wd, a_w_in, a_q_gain, a_k_gain, a_sinks,
              a_w_out, b_w_in, b_q_norm, b_kv_norm, b_w_uq, b_w_ukv, b_q_gain, b_k_gain,
              b_w_out):
    for i in range(DEPTH):
        x = x + 0.5 * swiglu(rmsnorm(x, ffn_norm1[i]), ffn1_wg[i], ffn1_wu[i], ffn1_wd[i])
        h = rmsnorm(x, mix_norm[i])
        j = i // N_MIXERS
        if i % N_MIXERS == 0:
            x = x + sliding_window_attention(h, positions, rel_bias, a_w_in[j], a_q_gain[j],
                                             a_k_gain[j], a_sinks[j], a_w_out[j])
        else:
            x = x + latent_attention(h, positions, b_w_in[j], b_q_norm[j], b_kv_norm[j],
                                     b_w_uq[j], b_w_ukv[j], b_q_gain[j], b_k_gain[j], b_w_out[j])
        x = x + 0.5 * swiglu(rmsnorm(x, ffn_norm2[i]), ffn2_wg[i], ffn2_wu[i], ffn2_wd[i])
    return x
```

```python
import functools
import math

import numpy as np
import jax
import jax.numpy as jnp
from jax import lax
from jax.experimental import pallas as pl
from jax.experimental.pallas import tpu as pltpu

D_MODEL = 1024
BATCH = 8
SEQ = 2048
DEPTH = 2
N_MIXERS = 2
A_HEADS = 16
A_KV_HEADS = 2
A_HEAD_DIM = 64
WINDOW = 128
BLOCK = 128
NUM_BUCKETS = 32
MAX_DISTANCE = 128
B_HEADS = 16
Q_LORA = 256
KV_LORA = 128
QK_NOPE = 64
QK_ROPE = 32
V_DIM = 64
ROPE_BASE = 10000.0
D_FF = 2816
EPS = 1e-6
NEG = -1e30

TOKENS = BATCH * SEQ
LANES = 128
HALF = LANES // 2
B_DQK = QK_NOPE + QK_ROPE
VMEM_LIMIT = 56 * 1024 * 1024

F32 = jnp.float32
BF16 = jnp.bfloat16


def _rms_rows(x, gain):
    return x * lax.rsqrt(jnp.mean(x * x, axis=-1, keepdims=True) + EPS) * gain


def _dot(a, b):
    return jnp.dot(a, b, preferred_element_type=F32)


def _dot_nt(a, b):
    return lax.dot_general(a, b, (((1,), (1,)), ((), ())), preferred_element_type=F32)


def _const_spec(shape):
    nd = len(shape)
    return pl.BlockSpec(shape, lambda *_: (0,) * nd, pipeline_mode=pl.Buffered(1))


FFN_TM = 512


def _ffn_body(x, g_ref, wg_ref, wu_ref, wd_ref, o_ref):
    h = _rms_rows(x, g_ref[...]).astype(BF16)
    gate = _dot(h, wg_ref[...])
    up = _dot(h, wu_ref[...])
    act = (gate * jax.nn.sigmoid(gate) * up).astype(BF16)
    o_ref[...] = x + 0.5 * _dot(act, wd_ref[...])


def _ffn_kernel(x_ref, g_ref, wg_ref, wu_ref, wd_ref, o_ref):
    _ffn_body(x_ref[...], g_ref, wg_ref, wu_ref, wd_ref, o_ref)


def _proj_ffn_kernel(x_ref, a_ref, wo_ref, g_ref, wg_ref, wu_ref, wd_ref, o_ref):
    x = x_ref[...] + _dot(a_ref[...], wo_ref[...])
    _ffn_body(x, g_ref, wg_ref, wu_ref, wd_ref, o_ref)


def _ffn(x, gain, wg, wu, wd, attn=None, w_out=None):
    tm = FFN_TM
    row_spec = pl.BlockSpec((tm, D_MODEL), lambda i: (i, 0))
    w_specs = [_const_spec((1, D_MODEL)), _const_spec((D_MODEL, D_FF)), _const_spec((D_MODEL, D_FF)),
               _const_spec((D_FF, D_MODEL))]
    if attn is None:
        kern, in_specs, args = _ffn_kernel, [row_spec] + w_specs, (x, gain, wg, wu, wd)
    else:
        kern = _proj_ffn_kernel
        in_specs = [row_spec, row_spec, _const_spec((D_MODEL, D_MODEL))] + w_specs
        args = (x, attn, w_out, gain, wg, wu, wd)
    return pl.pallas_call(
        kern,
        grid=(TOKENS // tm,),
        in_specs=in_specs,
        out_specs=row_spec,
        out_shape=jax.ShapeDtypeStruct((TOKENS, D_MODEL), F32),
        compiler_params=pltpu.CompilerParams(vmem_limit_bytes=VMEM_LIMIT),
        name="ffn" if attn is None else "proj_ffn",
    )(*args)


A_TM = 512
A_QW = A_HEADS * A_HEAD_DIM
A_KW = 4 * LANES
A_PAIRS = A_HEADS // 2
A_ROWS = 256


def _a_qkv_kernel(x_ref, g_ref, w_ref, qg_ref, kg_ref, q_ref, k_ref, v_ref):
    h = _rms_rows(x_ref[...], g_ref[...]).astype(BF16)
    qkv = _dot(h, w_ref[...])
    lane = lax.broadcasted_iota(jnp.int32, (1, LANES), 1)
    low = lane < HALF
    scale = A_HEAD_DIM ** -0.5
    for p in range(A_PAIRS):
        qp = qkv[:, p * LANES:(p + 1) * LANES]
        sq = qp * qp
        ms_lo = jnp.sum(jnp.where(low, sq, 0.0), axis=-1, keepdims=True) * (1.0 / A_HEAD_DIM)
        ms_hi = jnp.sum(jnp.where(low, 0.0, sq), axis=-1, keepdims=True) * (1.0 / A_HEAD_DIM)
        r = jnp.where(low, lax.rsqrt(ms_lo + EPS), lax.rsqrt(ms_hi + EPS))
        q_ref[:, p * LANES:(p + 1) * LANES] = (qp * r * qg_ref[...] * scale).astype(BF16)
    for j in range(A_KW // LANES):
        kj = qkv[:, A_QW + j * LANES:A_QW + (j + 1) * LANES]
        ms = jnp.sum(kj * kj, axis=-1, keepdims=True) * (1.0 / A_HEAD_DIM)
        k_ref[:, j * LANES:(j + 1) * LANES] = (
            kj * lax.rsqrt(ms + EPS) * kg_ref[:, j * LANES:(j + 1) * LANES]).astype(BF16)
    v_ref[...] = qkv[:, A_QW + A_KW:].astype(BF16)


def _a_qkv(x, gain, w, qg, kg):
    tm = A_TM
    width = A_QW + 2 * A_KW
    return pl.pallas_call(
        _a_qkv_kernel,
        grid=(TOKENS // tm,),
        in_specs=[pl.BlockSpec((tm, D_MODEL), lambda i: (i, 0)), _const_spec((1, D_MODEL)),
                  _const_spec((D_MODEL, width)), _const_spec((1, LANES)), _const_spec((1, A_KW))],
        out_specs=[pl.BlockSpec((tm, A_QW), lambda i: (i, 0)), pl.BlockSpec((tm, A_KW), lambda i: (i, 0)),
                   pl.BlockSpec((tm, A_KW), lambda i: (i, 0))],
        out_shape=[jax.ShapeDtypeStruct((TOKENS, A_QW), BF16), jax.ShapeDtypeStruct((TOKENS, A_KW), BF16),
                   jax.ShapeDtypeStruct((TOKENS, A_KW), BF16)],
        compiler_params=pltpu.CompilerParams(vmem_limit_bytes=VMEM_LIMIT),
        name="a_qkv",
    )(x, gain, w, qg, kg)


def _t5_bucket(dist):
    n = jnp.maximum(dist, 0)
    max_exact = NUM_BUCKETS // 2
    large = max_exact + (jnp.log(jnp.maximum(n, 1).astype(F32) / max_exact)
                         / math.log(MAX_DISTANCE / max_exact)
                         * (NUM_BUCKETS - max_exact)).astype(jnp.int32)
    large = jnp.minimum(large, NUM_BUCKETS - 1)
    return jnp.where(n < max_exact, n, large)


def _a_attn_kernel(sink_ref, q_ref, kc_ref, kp_ref, vc_ref, vp_ref, pq_ref, pkc_ref, pkp_ref, tbl_ref, o_ref):
    row = lax.broadcasted_iota(jnp.int32, (BLOCK, BLOCK), 0)
    col = lax.broadcasted_iota(jnp.int32, (BLOCK, BLOCK), 1)
    cur_ok = col <= row
    tri_prev = col > row
    no_prev = jnp.where(pl.program_id(1) == 0, 2 * BLOCK, 0)
    for r in range(A_ROWS // BLOCK):
        rows = slice(r * BLOCK, (r + 1) * BLOCK)
        if r == 0:
            k_prev, v_prev, pos_prev = kp_ref[...], vp_ref[...], pkp_ref[0]
            prev_ok = col > row + no_prev
        else:
            before = slice((r - 1) * BLOCK, r * BLOCK)
            k_prev, v_prev, pos_prev = kc_ref[before, :], vc_ref[before, :], pkc_ref[r - 1]
            prev_ok = tri_prev
        k_band = jnp.concatenate([k_prev, kc_ref[rows, :]], axis=0)
        v_band = jnp.concatenate([v_prev, vc_ref[rows, :]], axis=0)
        pos_q = pq_ref[rows, :]
        bucket_prev = _t5_bucket(pos_q - pos_prev)
        bucket_cur = _t5_bucket(pos_q - pkc_ref[r])
        for p in range(A_PAIRS):
            kv = (2 * p) // (A_HEADS // A_KV_HEADS)
            qp = q_ref[rows, p * LANES:(p + 1) * LANES]
            pair_out = None
            for parity in range(2):
                head = 2 * p + parity
                group = slice((2 * kv + parity) * LANES, (2 * kv + parity + 1) * LANES)
                s = _dot_nt(qp, k_band[:, group])
                table = jnp.broadcast_to(tbl_ref[head:head + 1, :], (BLOCK, LANES))
                s_prev = s[:, :BLOCK] + jnp.take_along_axis(table, bucket_prev, axis=1)
                s_cur = s[:, BLOCK:] + jnp.take_along_axis(table, bucket_cur, axis=1)
                s_prev = jnp.where(prev_ok, s_prev, NEG)
                s_cur = jnp.where(cur_ok, s_cur, NEG)
                sink = sink_ref[head]
                m = jnp.maximum(jnp.maximum(jnp.max(s_prev, axis=-1, keepdims=True),
                                            jnp.max(s_cur, axis=-1, keepdims=True)), sink)
                e_prev = jnp.exp(s_prev - m)
                e_cur = jnp.exp(s_cur - m)
                denom = (jnp.sum(e_prev, axis=-1, keepdims=True) + jnp.sum(e_cur, axis=-1, keepdims=True)
                         + jnp.exp(sink - m))
                probs = jnp.concatenate([e_prev, e_cur], axis=1) * (1.0 / denom)
                out = _dot(probs.astype(BF16), v_band[:, group])
                pair_out = out if pair_out is None else pair_out + out
            o_ref[rows, p * LANES:(p + 1) * LANES] = pair_out.astype(BF16)


def _a_attn(q, k4, v4, pos_col, pos_row, table, sinks):
    steps = SEQ // A_ROWS
    blocks_per_step = A_ROWS // BLOCK
    blocks_per_seq = SEQ // BLOCK

    def cur(b, i):
        return (b * steps + i, 0)

    def prev(b, i):
        return (b * blocks_per_seq + jnp.maximum(i * blocks_per_step - 1, 0), 0)

    def cur3(b, i):
        return (b * steps + i, 0, 0)

    def prev3(b, i):
        return (b * blocks_per_seq + jnp.maximum(i * blocks_per_step - 1, 0), 0, 0)

    return pl.pallas_call(
        _a_attn_kernel,
        grid=(BATCH, steps),
        in_specs=[
            pl.BlockSpec(memory_space=pltpu.SMEM),
            pl.BlockSpec((A_ROWS, A_QW), cur),
            pl.BlockSpec((A_ROWS, A_KW), cur),
            pl.BlockSpec((BLOCK, A_KW), prev),
            pl.BlockSpec((A_ROWS, A_KW), cur),
            pl.BlockSpec((BLOCK, A_KW), prev),
            pl.BlockSpec((A_ROWS, 1), cur),
            pl.BlockSpec((blocks_per_step, 1, BLOCK), cur3),
            pl.BlockSpec((1, 1, BLOCK), prev3),
            _const_spec((A_HEADS, LANES)),
        ],
        out_specs=pl.BlockSpec((A_ROWS, A_QW), cur),
        out_shape=jax.ShapeDtypeStruct((TOKENS, A_QW), BF16),
        compiler_params=pltpu.CompilerParams(vmem_limit_bytes=VMEM_LIMIT),
        name="a_attn",
    )(sinks, q, k4, k4, v4, v4, pos_col, pos_row, pos_row, table)


B_TM = 256
B_CW = Q_LORA + KV_LORA + 2 * LANES
B_HW = B_HEADS * LANES
B_TQ = 256
B_TK = 256


def _b_proj_kernel(x_ref, pos_ref, g_ref, win_ref, qn_ref, kvn_ref, wq_ref, wqs_ref, wk_ref, wv_ref,
                   qg_ref, qgs_ref, kg_ref, krg_ref, krgs_ref, freq_ref, q_ref, k_ref, v_ref):
    h = _rms_rows(x_ref[...], g_ref[...]).astype(BF16)
    c = _dot(h, win_ref[...])
    cq = _rms_rows(c[:, :Q_LORA], qn_ref[...]).astype(BF16)
    ckv = _rms_rows(c[:, Q_LORA:Q_LORA + KV_LORA], kvn_ref[...]).astype(BF16)
    kr = c[:, Q_LORA + KV_LORA:Q_LORA + KV_LORA + LANES]
    kr_partner = c[:, Q_LORA + KV_LORA + LANES:]
    q = _dot(cq, wq_ref[...])
    q_partner = _dot(cq, wqs_ref[...])
    kn = _dot(ckv, wk_ref[...])
    v_ref[...] = _dot(ckv, wv_ref[...]).astype(BF16)

    lane = lax.broadcasted_iota(jnp.int32, (1, LANES), 1)
    ang = pos_ref[...].astype(F32) * freq_ref[...]
    cos = jnp.cos(ang)
    sin = jnp.sin(ang)
    half = QK_ROPE // 2
    sin_signed = jnp.where(lane < QK_NOPE + half, -sin, sin)

    k_rope = kr * krg_ref[...] * cos + kr_partner * krgs_ref[...] * sin_signed
    ss_rope = jnp.sum(kr * kr, axis=-1, keepdims=True)
    scale = B_DQK ** -0.5
    for hd in range(B_HEADS):
        g = slice(hd * LANES, (hd + 1) * LANES)
        qh = q[:, g]
        r = lax.rsqrt(jnp.sum(qh * qh, axis=-1, keepdims=True) * (1.0 / B_DQK) + EPS)
        q_rot = qh * qg_ref[...] * cos + q_partner[:, g] * qgs_ref[...] * sin_signed
        q_ref[:, g] = (q_rot * (r * scale)).astype(BF16)
        kh = kn[:, g]
        rk = lax.rsqrt((jnp.sum(kh * kh, axis=-1, keepdims=True) + ss_rope) * (1.0 / B_DQK) + EPS)
        k_ref[:, g] = ((kh * kg_ref[...] + k_rope) * rk).astype(BF16)


def _b_proj(x, pos_col, gain, w_in, qn, kvn, wq, wqs, wk, wv, qg, qgs, kg, krg, krgs, freq):
    tm = B_TM
    lane_vec = _const_spec((1, LANES))
    out_spec = pl.BlockSpec((tm, B_HW), lambda i: (i, 0))
    out_sds = jax.ShapeDtypeStruct((TOKENS, B_HW), BF16)
    return pl.pallas_call(
        _b_proj_kernel,
        grid=(TOKENS // tm,),
        in_specs=[pl.BlockSpec((tm, D_MODEL), lambda i: (i, 0)), pl.BlockSpec((tm, 1), lambda i: (i, 0)),
                  _const_spec((1, D_MODEL)), _const_spec((D_MODEL, B_CW)), _const_spec((1, Q_LORA)),
                  _const_spec((1, KV_LORA)), _const_spec((Q_LORA, B_HW)), _const_spec((Q_LORA, B_HW)),
                  _const_spec((KV_LORA, B_HW)), _const_spec((KV_LORA, B_HW)),
                  lane_vec, lane_vec, lane_vec, lane_vec, lane_vec, lane_vec],
        out_specs=[out_spec, out_spec, out_spec],
        out_shape=[out_sds, out_sds, out_sds],
        compiler_params=pltpu.CompilerParams(vmem_limit_bytes=VMEM_LIMIT),
        name="b_proj",
    )(x, pos_col, gain, w_in, qn, kvn, wq, wqs, wk, wv, qg, qgs, kg, krg, krgs, freq)


def _b_attn_kernel(q_ref, k_ref, v_ref, o_ref):
    qi = pl.program_id(2)
    row = lax.broadcasted_iota(jnp.int32, (B_TQ, B_TK), 0)
    col = lax.broadcasted_iota(jnp.int32, (B_TQ, B_TK), 1)
    diag_ok = col <= row

    out = None
    for parity in range(2):
        g = slice(parity * LANES, (parity + 1) * LANES)
        q = q_ref[:, g]

        def tile(j, carry, masked):
            m, l, acc = carry
            keys = pl.ds(pl.multiple_of(j * B_TK, B_TK), B_TK)
            s = _dot_nt(q, k_ref[keys, g])
            if masked:
                s = jnp.where(diag_ok, s, NEG)
            m_new = jnp.maximum(m, jnp.max(s, axis=-1, keepdims=True))
            alpha = jnp.exp(m - m_new)
            e = jnp.exp(s - m_new)
            l_new = alpha * l + jnp.sum(e, axis=-1, keepdims=True)
            acc_new = alpha * acc + _dot(e.astype(BF16), v_ref[keys, g])
            return m_new, l_new, acc_new

        init = (jnp.full((B_TQ, 1), NEG, F32), jnp.zeros((B_TQ, 1), F32), jnp.zeros((B_TQ, LANES), F32))
        carry = lax.fori_loop(0, qi, functools.partial(tile, masked=False), init)
        _, l, acc = tile(qi, carry, masked=True)
        part = acc * (1.0 / l)
        out = part if out is None else out + part
    o_ref[...] = out.astype(BF16)


def _b_attn(q, k, v):
    nq = SEQ // B_TQ
    pairs = B_HEADS // 2
    return pl.pallas_call(
        _b_attn_kernel,
        grid=(BATCH, pairs, nq),
        in_specs=[pl.BlockSpec((B_TQ, 2 * LANES), lambda b, p, i: (b * nq + i, p)),
                  pl.BlockSpec((SEQ, 2 * LANES), lambda b, p, i: (b, p)),
                  pl.BlockSpec((SEQ, 2 * LANES), lambda b, p, i: (b, p))],
        out_specs=pl.BlockSpec((B_TQ, LANES), lambda b, p, i: (b * nq + i, p)),
        out_shape=jax.ShapeDtypeStruct((TOKENS, B_HEADS * V_DIM), BF16),
        compiler_params=pltpu.CompilerParams(vmem_limit_bytes=VMEM_LIMIT),
        name="b_attn",
    )(q, k, v)


def _lohi(cols):
    z = jnp.zeros_like(cols)
    return jnp.concatenate([cols, z, z, cols], axis=1)


def _prep_a(w_in, q_gain, k_gain):
    wq = w_in[:, :A_QW]
    k0 = w_in[:, A_QW:A_QW + A_HEAD_DIM]
    k1 = w_in[:, A_QW + A_HEAD_DIM:A_QW + 2 * A_HEAD_DIM]
    v0 = w_in[:, A_QW + 2 * A_HEAD_DIM:A_QW + 3 * A_HEAD_DIM]
    v1 = w_in[:, A_QW + 3 * A_HEAD_DIM:]
    w = jnp.concatenate([wq, _lohi(k0), _lohi(k1), _lohi(v0), _lohi(v1)], axis=1).astype(BF16)
    qg = jnp.concatenate([q_gain, q_gain])[None, :]
    z = jnp.zeros_like(k_gain)
    kg = jnp.concatenate([k_gain, z, z, k_gain, k_gain, z, z, k_gain])[None, :]
    return w, qg, kg


def _head_groups(w, per_head, src_lo, src_hi, dst_lo):
    rows = w.shape[0]
    w3 = w.reshape(rows, B_HEADS, per_head)[:, :, src_lo:src_hi]
    out = jnp.zeros((rows, B_HEADS, LANES), w.dtype)
    out = out.at[:, :, dst_lo:dst_lo + (src_hi - src_lo)].set(w3)
    return out


def _prep_b(w_in, w_uq, w_ukv, q_gain, k_gain):
    half = QK_ROPE // 2
    t1 = slice(QK_NOPE, QK_NOPE + half)
    t2 = slice(QK_NOPE + half, B_DQK)
    rows = w_in.shape[0]
    rope_cols = w_in[:, Q_LORA + KV_LORA:]
    zeros = lambda n: jnp.zeros((rows, n), w_in.dtype)
    rope_group = jnp.concatenate([zeros(QK_NOPE), rope_cols, zeros(LANES - B_DQK)], axis=1)
    partner_group = jnp.concatenate(
        [zeros(QK_NOPE), rope_cols[:, half:], rope_cols[:, :half], zeros(LANES - B_DQK)], axis=1)
    win = jnp.concatenate([w_in[:, :Q_LORA + KV_LORA], rope_group, partner_group], axis=1).astype(BF16)

    wq = _head_groups(w_uq, B_DQK, 0, B_DQK, 0)
    wqs = (_head_groups(w_uq, B_DQK, t2.start, t2.stop, t1.start)
           + _head_groups(w_uq, B_DQK, t1.start, t1.stop, t2.start))
    wk = _head_groups(w_ukv, QK_NOPE + V_DIM, 0, QK_NOPE, 0)
    v_lo = _head_groups(w_ukv, QK_NOPE + V_DIM, QK_NOPE, QK_NOPE + V_DIM, 0)
    v_hi = _head_groups(w_ukv, QK_NOPE + V_DIM, QK_NOPE, QK_NOPE + V_DIM, HALF)
    even = (jnp.arange(B_HEADS) % 2 == 0)[None, :, None]
    wv = jnp.where(even, v_lo, v_hi)
    flat = lambda a: a.reshape(a.shape[0], B_HW).astype(BF16)

    def lane_vec(pieces):
        out = jnp.zeros((LANES,), F32)
        for lo, vals in pieces:
            out = out.at[lo:lo + vals.shape[0]].set(vals)
        return out[None, :]

    qg = lane_vec([(0, q_gain)])
    qgs = lane_vec([(t1.start, q_gain[t2]), (t2.start, q_gain[t1])])
    kg = lane_vec([(0, k_gain[:QK_NOPE])])
    krg = lane_vec([(QK_NOPE, k_gain[QK_NOPE:])])
    krgs = lane_vec([(t1.start, k_gain[t2]), (t2.start, k_gain[t1])])
    return win, flat(wq), flat(wqs), flat(wk), flat(wv), qg, qgs, kg, krg, krgs


def _rope_freq():
    inv = ROPE_BASE ** (-np.arange(0, QK_ROPE, 2, dtype=np.float32) / QK_ROPE)
    out = np.zeros((1, LANES), np.float32)
    half = QK_ROPE // 2
    out[0, QK_NOPE:QK_NOPE + half] = inv
    out[0, QK_NOPE + half:B_DQK] = inv
    return jnp.asarray(out)


def kernel(x, positions, rel_bias, ffn_norm1, ffn1_wg, ffn1_wu, ffn1_wd, mix_norm, ffn_norm2, ffn2_wg,
           ffn2_wu, ffn2_wd, a_w_in, a_q_gain, a_k_gain, a_sinks, a_w_out, b_w_in, b_q_norm, b_kv_norm,
           b_w_uq, b_w_ukv, b_q_gain, b_k_gain, b_w_out):
    assert x.shape == (BATCH, SEQ, D_MODEL) and positions.shape == (BATCH, SEQ)
    xt = x.reshape(TOKENS, D_MODEL)
    pos_col = positions.reshape(TOKENS, 1)
    pos_row = positions.reshape(TOKENS // BLOCK, 1, BLOCK)
    table = jnp.zeros((A_HEADS, LANES), F32).at[:, :NUM_BUCKETS].set(rel_bias.T)
    bf = lambda w: w.astype(BF16)
    row = lambda v: v[None, :]

    for i in range(DEPTH):
        xt = _ffn(xt, row(ffn_norm1[i]), bf(ffn1_wg[i]), bf(ffn1_wu[i]), bf(ffn1_wd[i]))
        j = i // N_MIXERS
        if i % N_MIXERS == 0:
            w, qg, kg = _prep_a(a_w_in[j], a_q_gain[j], a_k_gain[j])
            q, k4, v4 = _a_qkv(xt, row(mix_norm[i]), w, qg, kg)
            attn = _a_attn(q, k4, v4, pos_col, pos_row, table, a_sinks[j])
            w_out = bf(a_w_out[j])
        else:
            prep = _prep_b(b_w_in[j], b_w_uq[j], b_w_ukv[j], b_q_gain[j], b_k_gain[j])
            win, wq, wqs, wk, wv, qg, qgs, kg, krg, krgs = prep
            q, k, v = _b_proj(xt, pos_col, row(mix_norm[i]), win, row(b_q_norm[j]), row(b_kv_norm[j]),
                              wq, wqs, wk, wv, qg, qgs, kg, krg, krgs, _rope_freq())
            attn = _b_attn(q, k, v)
            w_out = bf(b_w_out[j])
        xt = _ffn(xt, row(ffn_norm2[i]), bf(ffn2_wg[i]), bf(ffn2_wu[i]), bf(ffn2_wd[i]), attn, w_out)
    return xt.reshape(BATCH, SEQ, D_MODEL)
```

```python
import functools
import math

import numpy as np
import jax
import jax.numpy as jnp
from jax import lax
from jax.experimental import pallas as pl
from jax.experimental.pallas import tpu as pltpu

D_MODEL = 1024
BATCH = 8
SEQ = 2048
DEPTH = 2
N_MIXERS = 2
A_HEADS = 16
A_KV_HEADS = 2
A_HEAD_DIM = 64
WINDOW = 128
BLOCK = 128
NUM_BUCKETS = 32
MAX_DISTANCE = 128
B_HEADS = 16
Q_LORA = 256
KV_LORA = 128
QK_NOPE = 64
QK_ROPE = 32
V_DIM = 64
ROPE_BASE = 10000.0
D_FF = 2816
EPS = 1e-6
NEG = -1e30

TOKENS = BATCH * SEQ
LANES = 128
HALF = LANES // 2
B_DQK = QK_NOPE + QK_ROPE
VMEM_LIMIT = 56 * 1024 * 1024

F32 = jnp.float32
BF16 = jnp.bfloat16


def _rms_rows(x, gain):
    return x * lax.rsqrt(jnp.mean(x * x, axis=-1, keepdims=True) + EPS) * gain


def _dot(a, b):
    return jnp.dot(a, b, preferred_element_type=F32)


def _dot_nt(a, b):
    return lax.dot_general(a, b, (((1,), (1,)), ((), ())), preferred_element_type=F32)


def _const_spec(shape):
    nd = len(shape)
    return pl.BlockSpec(shape, lambda *_: (0,) * nd, pipeline_mode=pl.Buffered(1))


FFN_TM = 512


def _ffn_body(x, g_ref, wg_ref, wu_ref, wd_ref, o_ref):
    h = _rms_rows(x, g_ref[...]).astype(BF16)
    gate = _dot(h, wg_ref[...])
    up = _dot(h, wu_ref[...])
    act = (gate * jax.nn.sigmoid(gate) * up).astype(BF16)
    o_ref[...] = x + 0.5 * _dot(act, wd_ref[...])


def _ffn_kernel(x_ref, g_ref, wg_ref, wu_ref, wd_ref, o_ref):
    _ffn_body(x_ref[...], g_ref, wg_ref, wu_ref, wd_ref, o_ref)


def _proj_ffn_kernel(x_ref, a_ref, wo_ref, g_ref, wg_ref, wu_ref, wd_ref, o_ref):
    x = x_ref[...] + _dot(a_ref[...], wo_ref[...])
    _ffn_body(x, g_ref, wg_ref, wu_ref, wd_ref, o_ref)


def _ffn(x, gain, wg, wu, wd, attn=None, w_out=None):
    tm = FFN_TM
    row_spec = pl.BlockSpec((tm, D_MODEL), lambda i: (i, 0))
    w_specs = [_const_spec((1, D_MODEL)), _const_spec((D_MODEL, D_FF)), _const_spec((D_MODEL, D_FF)),
               _const_spec((D_FF, D_MODEL))]
    if attn is None:
        kern, in_specs, args = _ffn_kernel, [row_spec] + w_specs, (x, gain, wg, wu, wd)
    else:
        kern = _proj_ffn_kernel
        in_specs = [row_spec, row_spec, _const_spec((D_MODEL, D_MODEL))] + w_specs
        args = (x, attn, w_out, gain, wg, wu, wd)
    return pl.pallas_call(
        kern,
        grid=(TOKENS // tm,),
        in_specs=in_specs,
        out_specs=row_spec,
        out_shape=jax.ShapeDtypeStruct((TOKENS, D_MODEL), F32),
        compiler_params=pltpu.CompilerParams(vmem_limit_bytes=VMEM_LIMIT),
        name="ffn" if attn is None else "proj_ffn",
    )(*args)


A_TM = 512
A_QW = A_HEADS * A_HEAD_DIM
A_KW = 4 * LANES
A_PAIRS = A_HEADS // 2
A_ROWS = 256


def _a_qkv_kernel(x_ref, g_ref, w_ref, qg_ref, kg_ref, q_ref, k_ref, v_ref):
    h = _rms_rows(x_ref[...], g_ref[...]).astype(BF16)
    qkv = _dot(h, w_ref[...])
    lane = lax.broadcasted_iota(jnp.int32, (1, LANES), 1)
    low = lane < HALF
    scale = A_HEAD_DIM ** -0.5
    for p in range(A_PAIRS):
        qp = qkv[:, p * LANES:(p + 1) * LANES]
        sq = qp * qp
        ms_lo = jnp.sum(jnp.where(low, sq, 0.0), axis=-1, keepdims=True) * (1.0 / A_HEAD_DIM)
        ms_hi = jnp.sum(jnp.where(low, 0.0, sq), axis=-1, keepdims=True) * (1.0 / A_HEAD_DIM)
        r = jnp.where(low, lax.rsqrt(ms_lo + EPS), lax.rsqrt(ms_hi + EPS))
        q_ref[:, p * LANES:(p + 1) * LANES] = (qp * r * qg_ref[...] * scale).astype(BF16)
    for j in range(A_KW // LANES):
        kj = qkv[:, A_QW + j * LANES:A_QW + (j + 1) * LANES]
        ms = jnp.sum(kj * kj, axis=-1, keepdims=True) * (1.0 / A_HEAD_DIM)
        k_ref[:, j * LANES:(j + 1) * LANES] = (
            kj * lax.rsqrt(ms + EPS) * kg_ref[:, j * LANES:(j + 1) * LANES]).astype(BF16)
    v_ref[...] = qkv[:, A_QW + A_KW:].astype(BF16)


def _a_qkv(x, gain, w, qg, kg):
    tm = A_TM
    width = A_QW + 2 * A_KW
    return pl.pallas_call(
        _a_qkv_kernel,
        grid=(TOKENS // tm,),
        in_specs=[pl.BlockSpec((tm, D_MODEL), lambda i: (i, 0)), _const_spec((1, D_MODEL)),
                  _const_spec((D_MODEL, width)), _const_spec((1, LANES)), _const_spec((1, A_KW))],
        out_specs=[pl.BlockSpec((tm, A_QW), lambda i: (i, 0)), pl.BlockSpec((tm, A_KW), lambda i: (i, 0)),
                   pl.BlockSpec((tm, A_KW), lambda i: (i, 0))],
        out_shape=[jax.ShapeDtypeStruct((TOKENS, A_QW), BF16), jax.ShapeDtypeStruct((TOKENS, A_KW), BF16),
                   jax.ShapeDtypeStruct((TOKENS, A_KW), BF16)],
        compiler_params=pltpu.CompilerParams(vmem_limit_bytes=VMEM_LIMIT),
        name="a_qkv",
    )(x, gain, w, qg, kg)


def _t5_bucket(dist):
    n = jnp.maximum(dist, 0)
    max_exact = NUM_BUCKETS // 2
    large = max_exact + (jnp.log(jnp.maximum(n, 1).astype(F32) / max_exact)
                         / math.log(MAX_DISTANCE / max_exact)
                         * (NUM_BUCKETS - max_exact)).astype(jnp.int32)
    large = jnp.minimum(large, NUM_BUCKETS - 1)
    return jnp.where(n < max_exact, n, large)


def _a_attn_kernel(sink_ref, q_ref, kc_ref, kp_ref, vc_ref, vp_ref, pq_ref, pkc_ref, pkp_ref, tbl_ref, o_ref):
    row = lax.broadcasted_iota(jnp.int32, (BLOCK, BLOCK), 0)
    col = lax.broadcasted_iota(jnp.int32, (BLOCK, BLOCK), 1)
    cur_ok = col <= row
    tri_prev = col > row
    no_prev = jnp.where(pl.program_id(1) == 0, 2 * BLOCK, 0)
    for r in range(A_ROWS // BLOCK):
        rows = slice(r * BLOCK, (r + 1) * BLOCK)
        if r == 0:
            k_prev, v_prev, pos_prev = kp_ref[...], vp_ref[...], pkp_ref[0]
            prev_ok = col > row + no_prev
        else:
            before = slice((r - 1) * BLOCK, r * BLOCK)
            k_prev, v_prev, pos_prev = kc_ref[before, :], vc_ref[before, :], pkc_ref[r - 1]
            prev_ok = tri_prev
        k_band = jnp.concatenate([k_prev, kc_ref[rows, :]], axis=0)
        v_band = jnp.concatenate([v_prev, vc_ref[rows, :]], axis=0)
        pos_q = pq_ref[rows, :]
        bucket_prev = _t5_bucket(pos_q - pos_prev)
        bucket_cur = _t5_bucket(pos_q - pkc_ref[r])
        for p in range(A_PAIRS):
            kv = (2 * p) // (A_HEADS // A_KV_HEADS)
            qp = q_ref[rows, p * LANES:(p + 1) * LANES]
            pair_out = None
            for parity in range(2):
                head = 2 * p + parity
                group = slice((2 * kv + parity) * LANES, (2 * kv + parity + 1) * LANES)
                s = _dot_nt(qp, k_band[:, group])
                table = jnp.broadcast_to(tbl_ref[head:head + 1, :], (BLOCK, LANES))
                s_prev = s[:, :BLOCK] + jnp.take_along_axis(table, bucket_prev, axis=1)
                s_cur = s[:, BLOCK:] + jnp.take_along_axis(table, bucket_cur, axis=1)
                s_prev = jnp.where(prev_ok, s_prev, NEG)
                s_cur = jnp.where(cur_ok, s_cur, NEG)
                sink = sink_ref[head]
                m = jnp.maximum(jnp.maximum(jnp.max(s_prev, axis=-1, keepdims=True),
                                            jnp.max(s_cur, axis=-1, keepdims=True)), sink)
                e_prev = jnp.exp(s_prev - m)
                e_cur = jnp.exp(s_cur - m)
                denom = (jnp.sum(e_prev, axis=-1, keepdims=True) + jnp.sum(e_cur, axis=-1, keepdims=True)
                         + jnp.exp(sink - m))
                probs = jnp.concatenate([e_prev, e_cur], axis=1) * (1.0 / denom)
                out = _dot(probs.astype(BF16), v_band[:, group])
                pair_out = out if pair_out is None else pair_out + out
            o_ref[rows, p * LANES:(p + 1) * LANES] = pair_out.astype(BF16)


def _a_attn(q, k4, v4, pos_col, pos_row, table, sinks):
    steps = SEQ // A_ROWS
    blocks_per_step = A_ROWS // BLOCK
    blocks_per_seq = SEQ // BLOCK

    def cur(b, i):
        return (b * steps + i, 0)

    def prev(b, i):
        return (b * blocks_per_seq + jnp.maximum(i * blocks_per_step - 1, 0), 0)

    def cur3(b, i):
        return (b * steps + i, 0, 0)

    def prev3(b, i):
        return (b * blocks_per_seq + jnp.maximum(i * blocks_per_step - 1, 0), 0, 0)

    return pl.pallas_call(
        _a_attn_kernel,
        grid=(BATCH, steps),
        in_specs=[
            pl.BlockSpec(memory_space=pltpu.SMEM),
            pl.BlockSpec((A_ROWS, A_QW), cur),
            pl.BlockSpec((A_ROWS, A_KW), cur),
            pl.BlockSpec((BLOCK, A_KW), prev),
            pl.BlockSpec((A_ROWS, A_KW), cur),
            pl.BlockSpec((BLOCK, A_KW), prev),
            pl.BlockSpec((A_ROWS, 1), cur),
            pl.BlockSpec((blocks_per_step, 1, BLOCK), cur3),
            pl.BlockSpec((1, 1, BLOCK), prev3),
            _const_spec((A_HEADS, LANES)),
        ],
        out_specs=pl.BlockSpec((A_ROWS, A_QW), cur),
        out_shape=jax.ShapeDtypeStruct((TOKENS, A_QW), BF16),
        compiler_params=pltpu.CompilerParams(vmem_limit_bytes=VMEM_LIMIT),
        name="a_attn",
    )(sinks, q, k4, k4, v4, v4, pos_col, pos_row, pos_row, table)


B_TM = 256
B_CW = Q_LORA + KV_LORA + 2 * LANES
B_HW = B_HEADS * LANES
B_TQ = 512
B_TK = 512


def _b_proj_kernel(x_ref, pos_ref, g_ref, win_ref, qn_ref, kvn_ref, wq_ref, wqs_ref, wk_ref, wv_ref,
                   qg_ref, qgs_ref, kg_ref, krg_ref, krgs_ref, freq_ref, q_ref, k_ref, v_ref):
    h = _rms_rows(x_ref[...], g_ref[...]).astype(BF16)
    c = _dot(h, win_ref[...])
    cq = _rms_rows(c[:, :Q_LORA], qn_ref[...]).astype(BF16)
    ckv = _rms_rows(c[:, Q_LORA:Q_LORA + KV_LORA], kvn_ref[...]).astype(BF16)
    kr = c[:, Q_LORA + KV_LORA:Q_LORA + KV_LORA + LANES]
    kr_partner = c[:, Q_LORA + KV_LORA + LANES:]
    q = _dot(cq, wq_ref[...])
    q_partner = _dot(cq, wqs_ref[...])
    kn = _dot(ckv, wk_ref[...])
    v_ref[...] = _dot(ckv, wv_ref[...]).astype(BF16)

    lane = lax.broadcasted_iota(jnp.int32, (1, LANES), 1)
    ang = pos_ref[...].astype(F32) * freq_ref[...]
    cos = jnp.cos(ang)
    sin = jnp.sin(ang)
    half = QK_ROPE // 2
    sin_signed = jnp.where(lane < QK_NOPE + half, -sin, sin)

    k_rope = kr * krg_ref[...] * cos + kr_partner * krgs_ref[...] * sin_signed
    ss_rope = jnp.sum(kr * kr, axis=-1, keepdims=True)
    scale = B_DQK ** -0.5
    for hd in range(B_HEADS):
        g = slice(hd * LANES, (hd + 1) * LANES)
        qh = q[:, g]
        r = lax.rsqrt(jnp.sum(qh * qh, axis=-1, keepdims=True) * (1.0 / B_DQK) + EPS)
        q_rot = qh * qg_ref[...] * cos + q_partner[:, g] * qgs_ref[...] * sin_signed
        q_ref[:, g] = (q_rot * (r * scale)).astype(BF16)
        kh = kn[:, g]
        rk = lax.rsqrt((jnp.sum(kh * kh, axis=-1, keepdims=True) + ss_rope) * (1.0 / B_DQK) + EPS)
        k_ref[:, g] = ((kh * kg_ref[...] + k_rope) * rk).astype(BF16)


def _b_proj(x, pos_col, gain, w_in, qn, kvn, wq, wqs, wk, wv, qg, qgs, kg, krg, krgs, freq):
    tm = B_TM
    lane_vec = _const_spec((1, LANES))
    out_spec = pl.BlockSpec((tm, B_HW), lambda i: (i, 0))
    out_sds = jax.ShapeDtypeStruct((TOKENS, B_HW), BF16)
    return pl.pallas_call(
        _b_proj_kernel,
        grid=(TOKENS // tm,),
        in_specs=[pl.BlockSpec((tm, D_MODEL), lambda i: (i, 0)), pl.BlockSpec((tm, 1), lambda i: (i, 0)),
                  _const_spec((1, D_MODEL)), _const_spec((D_MODEL, B_CW)), _const_spec((1, Q_LORA)),
                  _const_spec((1, KV_LORA)), _const_spec((Q_LORA, B_HW)), _const_spec((Q_LORA, B_HW)),
                  _const_spec((KV_LORA, B_HW)), _const_spec((KV_LORA, B_HW)),
                  lane_vec, lane_vec, lane_vec, lane_vec, lane_vec, lane_vec],
        out_specs=[out_spec, out_spec, out_spec],
        out_shape=[out_sds, out_sds, out_sds],
        compiler_params=pltpu.CompilerParams(vmem_limit_bytes=VMEM_LIMIT),
        name="b_proj",
    )(x, pos_col, gain, w_in, qn, kvn, wq, wqs, wk, wv, qg, qgs, kg, krg, krgs, freq)


def _b_attn_kernel(q_ref, k_ref, v_ref, o_ref):
    qi = pl.program_id(2)
    row = lax.broadcasted_iota(jnp.int32, (B_TQ, B_TK), 0)
    col = lax.broadcasted_iota(jnp.int32, (B_TQ, B_TK), 1)
    diag_ok = col <= row
    groups = [slice(parity * LANES, (parity + 1) * LANES) for parity in range(2)]

    def tile(j, carry, masked):
        keys = pl.ds(pl.multiple_of(j * B_TK, B_TK), B_TK)
        new = []
        for g, (m, l, acc) in zip(groups, carry):
            s = _dot_nt(q_ref[:, g], k_ref[keys, g])
            if masked:
                s = jnp.where(diag_ok, s, NEG)
            m_new = jnp.maximum(m, jnp.max(s, axis=-1, keepdims=True))
            alpha = jnp.exp(m - m_new)
            e = jnp.exp(s - m_new)
            l_new = alpha * l + jnp.sum(e, axis=-1, keepdims=True)
            acc_new = alpha * acc + _dot(e.astype(BF16), v_ref[keys, g])
            new.append((m_new, l_new, acc_new))
        return tuple(new)

    init = (jnp.full((B_TQ, 1), NEG, F32), jnp.zeros((B_TQ, 1), F32), jnp.zeros((B_TQ, LANES), F32))
    carry = lax.fori_loop(0, qi, functools.partial(tile, masked=False), (init, init))
    (_, l0, acc0), (_, l1, acc1) = tile(qi, carry, masked=True)
    o_ref[...] = (acc0 * (1.0 / l0) + acc1 * (1.0 / l1)).astype(BF16)


def _b_attn(q, k, v):
    nq = SEQ // B_TQ
    pairs = B_HEADS // 2
    return pl.pallas_call(
        _b_attn_kernel,
        grid=(BATCH, pairs, nq),
        in_specs=[pl.BlockSpec((B_TQ, 2 * LANES), lambda b, p, i: (b * nq + i, p)),
                  pl.BlockSpec((SEQ, 2 * LANES), lambda b, p, i: (b, p)),
                  pl.BlockSpec((SEQ, 2 * LANES), lambda b, p, i: (b, p))],
        out_specs=pl.BlockSpec((B_TQ, LANES), lambda b, p, i: (b * nq + i, p)),
        out_shape=jax.ShapeDtypeStruct((TOKENS, B_HEADS * V_DIM), BF16),
        compiler_params=pltpu.CompilerParams(vmem_limit_bytes=VMEM_LIMIT),
        name="b_attn",
    )(q, k, v)


def _lohi(cols):
    z = jnp.zeros_like(cols)
    return jnp.concatenate([cols, z, z, cols], axis=1)


def _prep_a(w_in, q_gain, k_gain):
    wq = w_in[:, :A_QW]
    k0 = w_in[:, A_QW:A_QW + A_HEAD_DIM]
    k1 = w_in[:, A_QW + A_HEAD_DIM:A_QW + 2 * A_HEAD_DIM]
    v0 = w_in[:, A_QW + 2 * A_HEAD_DIM:A_QW + 3 * A_HEAD_DIM]
    v1 = w_in[:, A_QW + 3 * A_HEAD_DIM:]
    w = jnp.concatenate([wq, _lohi(k0), _lohi(k1), _lohi(v0), _lohi(v1)], axis=1).astype(BF16)
    qg = jnp.concatenate([q_gain, q_gain])[None, :]
    z = jnp.zeros_like(k_gain)
    kg = jnp.concatenate([k_gain, z, z, k_gain, k_gain, z, z, k_gain])[None, :]
    return w, qg, kg


def _head_groups(w, per_head, src_lo, src_hi, dst_lo):
    rows = w.shape[0]
    w3 = w.reshape(rows, B_HEADS, per_head)[:, :, src_lo:src_hi]
    out = jnp.zeros((rows, B_HEADS, LANES), w.dtype)
    out = out.at[:, :, dst_lo:dst_lo + (src_hi - src_lo)].set(w3)
    return out


def _prep_b(w_in, w_uq, w_ukv, q_gain, k_gain):
    half = QK_ROPE // 2
    t1 = slice(QK_NOPE, QK_NOPE + half)
    t2 = slice(QK_NOPE + half, B_DQK)
    rows = w_in.shape[0]
    rope_cols = w_in[:, Q_LORA + KV_LORA:]
    zeros = lambda n: jnp.zeros((rows, n), w_in.dtype)
    rope_group = jnp.concatenate([zeros(QK_NOPE), rope_cols, zeros(LANES - B_DQK)], axis=1)
    partner_group = jnp.concatenate(
        [zeros(QK_NOPE), rope_cols[:, half:], rope_cols[:, :half], zeros(LANES - B_DQK)], axis=1)
    win = jnp.concatenate([w_in[:, :Q_LORA + KV_LORA], rope_group, partner_group], axis=1).astype(BF16)

    wq = _head_groups(w_uq, B_DQK, 0, B_DQK, 0)
    wqs = (_head_groups(w_uq, B_DQK, t2.start, t2.stop, t1.start)
           + _head_groups(w_uq, B_DQK, t1.start, t1.stop, t2.start))
    wk = _head_groups(w_ukv, QK_NOPE + V_DIM, 0, QK_NOPE, 0)
    v_lo = _head_groups(w_ukv, QK_NOPE + V_DIM, QK_NOPE, QK_NOPE + V_DIM, 0)
    v_hi = _head_groups(w_ukv, QK_NOPE + V_DIM, QK_NOPE, QK_NOPE + V_DIM, HALF)
    even = (jnp.arange(B_HEADS) % 2 == 0)[None, :, None]
    wv = jnp.where(even, v_lo, v_hi)
    flat = lambda a: a.reshape(a.shape[0], B_HW).astype(BF16)

    def lane_vec(pieces):
        out = jnp.zeros((LANES,), F32)
        for lo, vals in pieces:
            out = out.at[lo:lo + vals.shape[0]].set(vals)
        return out[None, :]

    qg = lane_vec([(0, q_gain)])
    qgs = lane_vec([(t1.start, q_gain[t2]), (t2.start, q_gain[t1])])
    kg = lane_vec([(0, k_gain[:QK_NOPE])])
    krg = lane_vec([(QK_NOPE, k_gain[QK_NOPE:])])
    krgs = lane_vec([(t1.start, k_gain[t2]), (t2.start, k_gain[t1])])
    return win, flat(wq), flat(wqs), flat(wk), flat(wv), qg, qgs, kg, krg, krgs


def _rope_freq():
    inv = ROPE_BASE ** (-np.arange(0, QK_ROPE, 2, dtype=np.float32) / QK_ROPE)
    out = np.zeros((1, LANES), np.float32)
    half = QK_ROPE // 2
    out[0, QK_NOPE:QK_NOPE + half] = inv
    out[0, QK_NOPE + half:B_DQK] = inv
    return jnp.asarray(out)


def kernel(x, positions, rel_bias, ffn_norm1, ffn1_wg, ffn1_wu, ffn1_wd, mix_norm, ffn_norm2, ffn2_wg,
           ffn2_wu, ffn2_wd, a_w_in, a_q_gain, a_k_gain, a_sinks, a_w_out, b_w_in, b_q_norm, b_kv_norm,
           b_w_uq, b_w_ukv, b_q_gain, b_k_gain, b_w_out):
    assert x.shape == (BATCH, SEQ, D_MODEL) and positions.shape == (BATCH, SEQ)
    xt = x.reshape(TOKENS, D_MODEL)
    pos_col = positions.reshape(TOKENS, 1)
    pos_row = positions.reshape(TOKENS // BLOCK, 1, BLOCK)
    table = jnp.zeros((A_HEADS, LANES), F32).at[:, :NUM_BUCKETS].set(rel_bias.T)
    bf = lambda w: w.astype(BF16)
    row = lambda v: v[None, :]

    for i in range(DEPTH):
        xt = _ffn(xt, row(ffn_norm1[i]), bf(ffn1_wg[i]), bf(ffn1_wu[i]), bf(ffn1_wd[i]))
        j = i // N_MIXERS
        if i % N_MIXERS == 0:
            w, qg, kg = _prep_a(a_w_in[j], a_q_gain[j], a_k_gain[j])
            q, k4, v4 = _a_qkv(xt, row(mix_norm[i]), w, qg, kg)
            attn = _a_attn(q, k4, v4, pos_col, pos_row, table, a_sinks[j])
            w_out = bf(a_w_out[j])
        else:
            prep = _prep_b(b_w_in[j], b_w_uq[j], b_w_ukv[j], b_q_gain[j], b_k_gain[j])
            win, wq, wqs, wk, wv, qg, qgs, kg, krg, krgs = prep
            q, k, v = _b_proj(xt, pos_col, row(mix_norm[i]), win, row(b_q_norm[j]), row(b_kv_norm[j]),
                              wq, wqs, wk, wv, qg, qgs, kg, krg, krgs, _rope_freq())
            attn = _b_attn(q, k, v)
            w_out = bf(b_w_out[j])
        xt = _ffn(xt, row(ffn_norm2[i]), bf(ffn2_wg[i]), bf(ffn2_wu[i]), bf(ffn2_wd[i]), attn, w_out)
    return xt.reshape(BATCH, SEQ, D_MODEL)
```

```python
import functools
import math

import numpy as np
import jax
import jax.numpy as jnp
from jax import lax
from jax.experimental import pallas as pl
from jax.experimental.pallas import tpu as pltpu

D_MODEL = 1024
BATCH = 8
SEQ = 2048
DEPTH = 2
N_MIXERS = 2
A_HEADS = 16
A_KV_HEADS = 2
A_HEAD_DIM = 64
WINDOW = 128
BLOCK = 128
NUM_BUCKETS = 32
MAX_DISTANCE = 128
B_HEADS = 16
Q_LORA = 256
KV_LORA = 128
QK_NOPE = 64
QK_ROPE = 32
V_DIM = 64
ROPE_BASE = 10000.0
D_FF = 2816
EPS = 1e-6
NEG = -1e30

TOKENS = BATCH * SEQ
LANES = 128
HALF = LANES // 2
B_DQK = QK_NOPE + QK_ROPE
VMEM_LIMIT = 56 * 1024 * 1024

F32 = jnp.float32
BF16 = jnp.bfloat16


def _rms_rows(x, gain):
    return x * lax.rsqrt(jnp.mean(x * x, axis=-1, keepdims=True) + EPS) * gain


def _dot(a, b):
    return jnp.dot(a, b, preferred_element_type=F32)


def _dot_nt(a, b):
    return lax.dot_general(a, b, (((1,), (1,)), ((), ())), preferred_element_type=F32)


def _const_spec(shape):
    nd = len(shape)
    return pl.BlockSpec(shape, lambda *_: (0,) * nd, pipeline_mode=pl.Buffered(1))


FFN_TM = 512


def _ffn_body(x, g_ref, wg_ref, wu_ref, wd_ref, o_ref):
    h = _rms_rows(x, g_ref[...]).astype(BF16)
    gate = _dot(h, wg_ref[...])
    up = _dot(h, wu_ref[...])
    act = (gate * jax.nn.sigmoid(gate) * up).astype(BF16)
    o_ref[...] = x + 0.5 * _dot(act, wd_ref[...])


def _ffn_kernel(x_ref, g_ref, wg_ref, wu_ref, wd_ref, o_ref):
    _ffn_body(x_ref[...], g_ref, wg_ref, wu_ref, wd_ref, o_ref)


def _proj_ffn_kernel(x_ref, a_ref, wo_ref, g_ref, wg_ref, wu_ref, wd_ref, o_ref):
    x = x_ref[...] + _dot(a_ref[...], wo_ref[...])
    _ffn_body(x, g_ref, wg_ref, wu_ref, wd_ref, o_ref)


def _ffn(x, gain, wg, wu, wd, attn=None, w_out=None):
    tm = FFN_TM
    row_spec = pl.BlockSpec((tm, D_MODEL), lambda i: (i, 0))
    w_specs = [_const_spec((1, D_MODEL)), _const_spec((D_MODEL, D_FF)), _const_spec((D_MODEL, D_FF)),
               _const_spec((D_FF, D_MODEL))]
    if attn is None:
        kern, in_specs, args = _ffn_kernel, [row_spec] + w_specs, (x, gain, wg, wu, wd)
    else:
        kern = _proj_ffn_kernel
        in_specs = [row_spec, row_spec, _const_spec((D_MODEL, D_MODEL))] + w_specs
        args = (x, attn, w_out, gain, wg, wu, wd)
    return pl.pallas_call(
        kern,
        grid=(TOKENS // tm,),
        in_specs=in_specs,
        out_specs=row_spec,
        out_shape=jax.ShapeDtypeStruct((TOKENS, D_MODEL), F32),
        compiler_params=pltpu.CompilerParams(vmem_limit_bytes=VMEM_LIMIT),
        name="ffn" if attn is None else "proj_ffn",
    )(*args)


A_TM = 512
A_QW = A_HEADS * A_HEAD_DIM
A_KW = 4 * LANES
A_PAIRS = A_HEADS // 2
A_ROWS = 256


def _a_qkv_kernel(x_ref, g_ref, w_ref, qg_ref, kg_ref, q_ref, k_ref, v_ref):
    h = _rms_rows(x_ref[...], g_ref[...]).astype(BF16)
    qkv = _dot(h, w_ref[...])
    lane = lax.broadcasted_iota(jnp.int32, (1, LANES), 1)
    low = lane < HALF
    scale = A_HEAD_DIM ** -0.5
    for p in range(A_PAIRS):
        qp = qkv[:, p * LANES:(p + 1) * LANES]
        sq = qp * qp
        ms_lo = jnp.sum(jnp.where(low, sq, 0.0), axis=-1, keepdims=True) * (1.0 / A_HEAD_DIM)
        ms_hi = jnp.sum(jnp.where(low, 0.0, sq), axis=-1, keepdims=True) * (1.0 / A_HEAD_DIM)
        r = jnp.where(low, lax.rsqrt(ms_lo + EPS), lax.rsqrt(ms_hi + EPS))
        q_ref[:, p * LANES:(p + 1) * LANES] = (qp * r * qg_ref[...] * scale).astype(BF16)
    for j in range(A_KW // LANES):
        kj = qkv[:, A_QW + j * LANES:A_QW + (j + 1) * LANES]
        ms = jnp.sum(kj * kj, axis=-1, keepdims=True) * (1.0 / A_HEAD_DIM)
        k_ref[:, j * LANES:(j + 1) * LANES] = (
            kj * lax.rsqrt(ms + EPS) * kg_ref[:, j * LANES:(j + 1) * LANES]).astype(BF16)
    v_ref[...] = qkv[:, A_QW + A_KW:].astype(BF16)


def _a_qkv(x, gain, w, qg, kg):
    tm = A_TM
    width = A_QW + 2 * A_KW
    return pl.pallas_call(
        _a_qkv_kernel,
        grid=(TOKENS // tm,),
        in_specs=[pl.BlockSpec((tm, D_MODEL), lambda i: (i, 0)), _const_spec((1, D_MODEL)),
                  _const_spec((D_MODEL, width)), _const_spec((1, LANES)), _const_spec((1, A_KW))],
        out_specs=[pl.BlockSpec((tm, A_QW), lambda i: (i, 0)), pl.BlockSpec((tm, A_KW), lambda i: (i, 0)),
                   pl.BlockSpec((tm, A_KW), lambda i: (i, 0))],
        out_shape=[jax.ShapeDtypeStruct((TOKENS, A_QW), BF16), jax.ShapeDtypeStruct((TOKENS, A_KW), BF16),
                   jax.ShapeDtypeStruct((TOKENS, A_KW), BF16)],
        compiler_params=pltpu.CompilerParams(vmem_limit_bytes=VMEM_LIMIT),
        name="a_qkv",
    )(x, gain, w, qg, kg)


def _t5_bucket(dist):
    n = jnp.maximum(dist, 0)
    max_exact = NUM_BUCKETS // 2
    large = max_exact + (jnp.log(jnp.maximum(n, 1).astype(F32) / max_exact)
                         / math.log(MAX_DISTANCE / max_exact)
                         * (NUM_BUCKETS - max_exact)).astype(jnp.int32)
    large = jnp.minimum(large, NUM_BUCKETS - 1)
    return jnp.where(n < max_exact, n, large)


SUBLANES = 8


def _a_attn_kernel(sink_ref, q_ref, kc_ref, kp_ref, vc_ref, vp_ref, pq_ref, pkc_ref, pkp_ref, tbl_ref, o_ref,
                   bias_ref, shared_ref):
    row = lax.broadcasted_iota(jnp.int32, (BLOCK, BLOCK), 0)
    col = lax.broadcasted_iota(jnp.int32, (BLOCK, BLOCK), 1)
    cur_ok = col <= row
    tri_prev = col > row
    first_in_seq = pl.program_id(1) == 0
    no_prev = jnp.where(first_in_seq, 2 * BLOCK, 0)
    tables = [jnp.broadcast_to(tbl_ref[h:h + 1, :], (SUBLANES, LANES)) for h in range(A_HEADS)]
    ones = jnp.ones((2 * BLOCK, LANES), BF16)
    group_heads = A_HEADS // A_KV_HEADS
    lane = lax.broadcasted_iota(jnp.int32, (1, LANES), 1)

    @pl.when(jnp.logical_and(pl.program_id(0) == 0, first_in_seq))
    def _():
        shared_ref[0] = 0

    def fill_shift_invariant():
        back = jnp.broadcast_to((BLOCK - lane) & (BLOCK - 1), (SUBLANES, LANES))
        idx = _t5_bucket(back)
        for h in range(A_HEADS):
            base = jnp.take_along_axis(tables[h], idx, axis=1)
            base = jnp.broadcast_to(base[0:1, :], (BLOCK, BLOCK))
            bias_ref[h] = pltpu.roll(base, 0, 1, stride=1, stride_axis=0)

    def fill_general(pos_q, pos_cur, pos_prev):
        bucket = jnp.where(cur_ok, _t5_bucket(pos_q - pos_cur), _t5_bucket(pos_q - pos_prev))
        for c in range(BLOCK // SUBLANES):
            idx = bucket[c * SUBLANES:(c + 1) * SUBLANES, :]
            for h in range(A_HEADS):
                bias_ref[h, c * SUBLANES:(c + 1) * SUBLANES, :] = jnp.take_along_axis(tables[h], idx, axis=1)

    for r in range(A_ROWS // BLOCK):
        rows = slice(r * BLOCK, (r + 1) * BLOCK)
        pos_cur = pkc_ref[r]
        if r == 0:
            k_prev, v_prev, pos_prev = kp_ref[...], vp_ref[...], pkp_ref[0]
            prev_ok = col > row + no_prev
        else:
            before = slice((r - 1) * BLOCK, r * BLOCK)
            k_prev, v_prev, pos_prev = kc_ref[before, :], vc_ref[before, :], pkc_ref[r - 1]
            prev_ok = tri_prev
        k_band = jnp.concatenate([k_prev, kc_ref[rows, :]], axis=0)
        v_band = jnp.concatenate([v_prev, vc_ref[rows, :]], axis=0)

        start = jnp.min(pos_cur, axis=-1, keepdims=True)
        off_cur = jnp.sum(jnp.where(pos_cur - start == lane, 0, 1))
        off_prev = jnp.sum(jnp.where(pos_prev - start == lane - BLOCK, 0, 1))
        if r == 0:
            off_prev = jnp.where(first_in_seq, 0, off_prev)
        consecutive = (off_cur + off_prev) == 0

        @pl.when(jnp.logical_and(consecutive, shared_ref[0] == 0))
        def _():
            fill_shift_invariant()
            shared_ref[0] = 1

        @pl.when(jnp.logical_not(consecutive))
        def _():
            fill_general(pq_ref[rows, :], pos_cur, pos_prev)
            shared_ref[0] = 0

        bias = [bias_ref[h] for h in range(A_HEADS)]
        for kv in range(A_KV_HEADS):
            pairs = range(kv * group_heads // 2, (kv + 1) * group_heads // 2)
            q_stack = jnp.concatenate([q_ref[rows, p * LANES:(p + 1) * LANES] for p in pairs], axis=0)
            halves = []
            for parity in range(2):
                group = slice((2 * kv + parity) * LANES, (2 * kv + parity + 1) * LANES)
                s_all = _dot_nt(q_stack, k_band[:, group])
                probs, sink_terms = [], []
                for i, p in enumerate(pairs):
                    head = 2 * p + parity
                    s = s_all[i * BLOCK:(i + 1) * BLOCK, :]
                    s_prev = jnp.where(prev_ok, s[:, :BLOCK] + bias[head], NEG)
                    s_cur = jnp.where(cur_ok, s[:, BLOCK:] + bias[head], NEG)
                    sink = sink_ref[head]
                    m = jnp.maximum(jnp.max(jnp.maximum(s_prev, s_cur), axis=-1, keepdims=True), sink)
                    probs.append(jnp.concatenate([jnp.exp(s_prev - m), jnp.exp(s_cur - m)], axis=1).astype(BF16))
                    sink_terms.append(jnp.exp(sink - m))
                v_ext = jnp.concatenate([v_band[:, group], ones], axis=1)
                out = _dot(jnp.concatenate(probs, axis=0), v_ext)
                scaled = []
                for i in range(len(pairs)):
                    o_i = out[i * BLOCK:(i + 1) * BLOCK, :]
                    scaled.append(o_i[:, :LANES] * (1.0 / (o_i[:, LANES:] + sink_terms[i])))
                halves.append(scaled)
            for i, p in enumerate(pairs):
                o_ref[rows, p * LANES:(p + 1) * LANES] = (halves[0][i] + halves[1][i]).astype(BF16)


def _a_attn(q, k4, v4, pos_col, pos_row, table, sinks):
    steps = SEQ // A_ROWS
    blocks_per_step = A_ROWS // BLOCK
    blocks_per_seq = SEQ // BLOCK

    def cur(b, i):
        return (b * steps + i, 0)

    def prev(b, i):
        return (b * blocks_per_seq + jnp.maximum(i * blocks_per_step - 1, 0), 0)

    def cur3(b, i):
        return (b * steps + i, 0, 0)

    def prev3(b, i):
        return (b * blocks_per_seq + jnp.maximum(i * blocks_per_step - 1, 0), 0, 0)

    return pl.pallas_call(
        _a_attn_kernel,
        grid=(BATCH, steps),
        in_specs=[
            pl.BlockSpec(memory_space=pltpu.SMEM),
            pl.BlockSpec((A_ROWS, A_QW), cur),
            pl.BlockSpec((A_ROWS, A_KW), cur),
            pl.BlockSpec((BLOCK, A_KW), prev),
            pl.BlockSpec((A_ROWS, A_KW), cur),
            pl.BlockSpec((BLOCK, A_KW), prev),
            pl.BlockSpec((A_ROWS, 1), cur),
            pl.BlockSpec((blocks_per_step, 1, BLOCK), cur3),
            pl.BlockSpec((1, 1, BLOCK), prev3),
            _const_spec((A_HEADS, LANES)),
        ],
        out_specs=pl.BlockSpec((A_ROWS, A_QW), cur),
        out_shape=jax.ShapeDtypeStruct((TOKENS, A_QW), BF16),
        scratch_shapes=[pltpu.VMEM((A_HEADS, BLOCK, BLOCK), F32), pltpu.SMEM((1,), jnp.int32)],
        compiler_params=pltpu.CompilerParams(dimension_semantics=("arbitrary", "arbitrary"),
                                             vmem_limit_bytes=VMEM_LIMIT),
        name="a_attn",
    )(sinks, q, k4, k4, v4, v4, pos_col, pos_row, pos_row, table)


B_TM = 256
B_CW = Q_LORA + KV_LORA + 2 * LANES
B_HW = B_HEADS * LANES
B_TQ = 512
B_TK = 512


def _b_proj_kernel(x_ref, pos_ref, g_ref, win_ref, qn_ref, kvn_ref, wq_ref, wqs_ref, wk_ref, wv_ref,
                   qg_ref, qgs_ref, kg_ref, krg_ref, krgs_ref, freq_ref, q_ref, k_ref, v_ref):
    h = _rms_rows(x_ref[...], g_ref[...]).astype(BF16)
    c = _dot(h, win_ref[...])
    cq = _rms_rows(c[:, :Q_LORA], qn_ref[...]).astype(BF16)
    ckv = _rms_rows(c[:, Q_LORA:Q_LORA + KV_LORA], kvn_ref[...]).astype(BF16)
    kr = c[:, Q_LORA + KV_LORA:Q_LORA + KV_LORA + LANES]
    kr_partner = c[:, Q_LORA + KV_LORA + LANES:]
    q = _dot(cq, wq_ref[...])
    q_partner = _dot(cq, wqs_ref[...])
    kn = _dot(ckv, wk_ref[...])
    v_ref[...] = _dot(ckv, wv_ref[...]).astype(BF16)

    lane = lax.broadcasted_iota(jnp.int32, (1, LANES), 1)
    ang = pos_ref[...].astype(F32) * freq_ref[...]
    cos = jnp.cos(ang)
    sin = jnp.sin(ang)
    half = QK_ROPE // 2
    sin_signed = jnp.where(lane < QK_NOPE + half, -sin, sin)

    k_rope = kr * krg_ref[...] * cos + kr_partner * krgs_ref[...] * sin_signed
    ss_rope = jnp.sum(kr * kr, axis=-1, keepdims=True)
    scale = B_DQK ** -0.5
    for hd in range(B_HEADS):
        g = slice(hd * LANES, (hd + 1) * LANES)
        qh = q[:, g]
        r = lax.rsqrt(jnp.sum(qh * qh, axis=-1, keepdims=True) * (1.0 / B_DQK) + EPS)
        q_rot = qh * qg_ref[...] * cos + q_partner[:, g] * qgs_ref[...] * sin_signed
        q_ref[:, g] = (q_rot * (r * scale)).astype(BF16)
        kh = kn[:, g]
        rk = lax.rsqrt((jnp.sum(kh * kh, axis=-1, keepdims=True) + ss_rope) * (1.0 / B_DQK) + EPS)
        k_ref[:, g] = ((kh * kg_ref[...] + k_rope) * rk).astype(BF16)


def _b_proj(x, pos_col, gain, w_in, qn, kvn, wq, wqs, wk, wv, qg, qgs, kg, krg, krgs, freq):
    tm = B_TM
    lane_vec = _const_spec((1, LANES))
    out_spec = pl.BlockSpec((tm, B_HW), lambda i: (i, 0))
    out_sds = jax.ShapeDtypeStruct((TOKENS, B_HW), BF16)
    return pl.pallas_call(
        _b_proj_kernel,
        grid=(TOKENS // tm,),
        in_specs=[pl.BlockSpec((tm, D_MODEL), lambda i: (i, 0)), pl.BlockSpec((tm, 1), lambda i: (i, 0)),
                  _const_spec((1, D_MODEL)), _const_spec((D_MODEL, B_CW)), _const_spec((1, Q_LORA)),
                  _const_spec((1, KV_LORA)), _const_spec((Q_LORA, B_HW)), _const_spec((Q_LORA, B_HW)),
                  _const_spec((KV_LORA, B_HW)), _const_spec((KV_LORA, B_HW)),
                  lane_vec, lane_vec, lane_vec, lane_vec, lane_vec, lane_vec],
        out_specs=[out_spec, out_spec, out_spec],
        out_shape=[out_sds, out_sds, out_sds],
        compiler_params=pltpu.CompilerParams(vmem_limit_bytes=VMEM_LIMIT),
        name="b_proj",
    )(x, pos_col, gain, w_in, qn, kvn, wq, wqs, wk, wv, qg, qgs, kg, krg, krgs, freq)


def _b_attn_kernel(q_ref, k_ref, v_ref, o_ref):
    qi = pl.program_id(2)
    row = lax.broadcasted_iota(jnp.int32, (B_TQ, B_TK), 0)
    col = lax.broadcasted_iota(jnp.int32, (B_TQ, B_TK), 1)
    diag_ok = col <= row
    groups = [slice(parity * LANES, (parity + 1) * LANES) for parity in range(2)]

    def tile(j, carry, masked):
        keys = pl.ds(pl.multiple_of(j * B_TK, B_TK), B_TK)
        new = []
        for g, (m, l, acc) in zip(groups, carry):
            s = _dot_nt(q_ref[:, g], k_ref[keys, g])
            if masked:
                s = jnp.where(diag_ok, s, NEG)
            m_new = jnp.maximum(m, jnp.max(s, axis=-1, keepdims=True))
            alpha = jnp.exp(m - m_new)
            e = jnp.exp(s - m_new)
            l_new = alpha * l + jnp.sum(e, axis=-1, keepdims=True)
            acc_new = alpha * acc + _dot(e.astype(BF16), v_ref[keys, g])
            new.append((m_new, l_new, acc_new))
        return tuple(new)

    init = (jnp.full((B_TQ, 1), NEG, F32), jnp.zeros((B_TQ, 1), F32), jnp.zeros((B_TQ, LANES), F32))
    carry = lax.fori_loop(0, qi, functools.partial(tile, masked=False), (init, init))
    (_, l0, acc0), (_, l1, acc1) = tile(qi, carry, masked=True)
    o_ref[...] = (acc0 * (1.0 / l0) + acc1 * (1.0 / l1)).astype(BF16)


def _b_attn(q, k, v):
    nq = SEQ // B_TQ
    pairs = B_HEADS // 2
    return pl.pallas_call(
        _b_attn_kernel,
        grid=(BATCH, pairs, nq),
        in_specs=[pl.BlockSpec((B_TQ, 2 * LANES), lambda b, p, i: (b * nq + i, p)),
                  pl.BlockSpec((SEQ, 2 * LANES), lambda b, p, i: (b, p)),
                  pl.BlockSpec((SEQ, 2 * LANES), lambda b, p, i: (b, p))],
        out_specs=pl.BlockSpec((B_TQ, LANES), lambda b, p, i: (b * nq + i, p)),
        out_shape=jax.ShapeDtypeStruct((TOKENS, B_HEADS * V_DIM), BF16),
        compiler_params=pltpu.CompilerParams(vmem_limit_bytes=VMEM_LIMIT),
        name="b_attn",
    )(q, k, v)


def _lohi(cols):
    z = jnp.zeros_like(cols)
    return jnp.concatenate([cols, z, z, cols], axis=1)


def _prep_a(w_in, q_gain, k_gain):
    wq = w_in[:, :A_QW]
    k0 = w_in[:, A_QW:A_QW + A_HEAD_DIM]
    k1 = w_in[:, A_QW + A_HEAD_DIM:A_QW + 2 * A_HEAD_DIM]
    v0 = w_in[:, A_QW + 2 * A_HEAD_DIM:A_QW + 3 * A_HEAD_DIM]
    v1 = w_in[:, A_QW + 3 * A_HEAD_DIM:]
    w = jnp.concatenate([wq, _lohi(k0), _lohi(k1), _lohi(v0), _lohi(v1)], axis=1).astype(BF16)
    qg = jnp.concatenate([q_gain, q_gain])[None, :]
    z = jnp.zeros_like(k_gain)
    kg = jnp.concatenate([k_gain, z, z, k_gain, k_gain, z, z, k_gain])[None, :]
    return w, qg, kg


def _head_groups(w, per_head, src_lo, src_hi, dst_lo):
    rows = w.shape[0]
    w3 = w.reshape(rows, B_HEADS, per_head)[:, :, src_lo:src_hi]
    out = jnp.zeros((rows, B_HEADS, LANES), w.dtype)
    out = out.at[:, :, dst_lo:dst_lo + (src_hi - src_lo)].set(w3)
    return out


def _prep_b(w_in, w_uq, w_ukv, q_gain, k_gain):
    half = QK_ROPE // 2
    t1 = slice(QK_NOPE, QK_NOPE + half)
    t2 = slice(QK_NOPE + half, B_DQK)
    rows = w_in.shape[0]
    rope_cols = w_in[:, Q_LORA + KV_LORA:]
    zeros = lambda n: jnp.zeros((rows, n), w_in.dtype)
    rope_group = jnp.concatenate([zeros(QK_NOPE), rope_cols, zeros(LANES - B_DQK)], axis=1)
    partner_group = jnp.concatenate(
        [zeros(QK_NOPE), rope_cols[:, half:], rope_cols[:, :half], zeros(LANES - B_DQK)], axis=1)
    win = jnp.concatenate([w_in[:, :Q_LORA + KV_LORA], rope_group, partner_group], axis=1).astype(BF16)

    wq = _head_groups(w_uq, B_DQK, 0, B_DQK, 0)
    wqs = (_head_groups(w_uq, B_DQK, t2.start, t2.stop, t1.start)
           + _head_groups(w_uq, B_DQK, t1.start, t1.stop, t2.start))
    wk = _head_groups(w_ukv, QK_NOPE + V_DIM, 0, QK_NOPE, 0)
    v_lo = _head_groups(w_ukv, QK_NOPE + V_DIM, QK_NOPE, QK_NOPE + V_DIM, 0)
    v_hi = _head_groups(w_ukv, QK_NOPE + V_DIM, QK_NOPE, QK_NOPE + V_DIM, HALF)
    even = (jnp.arange(B_HEADS) % 2 == 0)[None, :, None]
    wv = jnp.where(even, v_lo, v_hi)
    flat = lambda a: a.reshape(a.shape[0], B_HW).astype(BF16)

    def lane_vec(pieces):
        out = jnp.zeros((LANES,), F32)
        for lo, vals in pieces:
            out = out.at[lo:lo + vals.shape[0]].set(vals)
        return out[None, :]

    qg = lane_vec([(0, q_gain)])
    qgs = lane_vec([(t1.start, q_gain[t2]), (t2.start, q_gain[t1])])
    kg = lane_vec([(0, k_gain[:QK_NOPE])])
    krg = lane_vec([(QK_NOPE, k_gain[QK_NOPE:])])
    krgs = lane_vec([(t1.start, k_gain[t2]), (t2.start, k_gain[t1])])
    return win, flat(wq), flat(wqs), flat(wk), flat(wv), qg, qgs, kg, krg, krgs


def _rope_freq():
    inv = ROPE_BASE ** (-np.arange(0, QK_ROPE, 2, dtype=np.float32) / QK_ROPE)
    out = np.zeros((1, LANES), np.float32)
    half = QK_ROPE // 2
    out[0, QK_NOPE:QK_NOPE + half] = inv
    out[0, QK_NOPE + half:B_DQK] = inv
    return jnp.asarray(out)


def kernel(x, positions, rel_bias, ffn_norm1, ffn1_wg, ffn1_wu, ffn1_wd, mix_norm, ffn_norm2, ffn2_wg,
           ffn2_wu, ffn2_wd, a_w_in, a_q_gain, a_k_gain, a_sinks, a_w_out, b_w_in, b_q_norm, b_kv_norm,
           b_w_uq, b_w_ukv, b_q_gain, b_k_gain, b_w_out):
    assert x.shape == (BATCH, SEQ, D_MODEL) and positions.shape == (BATCH, SEQ)
    xt = x.reshape(TOKENS, D_MODEL)
    pos_col = positions.reshape(TOKENS, 1)
    pos_row = positions.reshape(TOKENS // BLOCK, 1, BLOCK)
    table = jnp.zeros((A_HEADS, LANES), F32).at[:, :NUM_BUCKETS].set(rel_bias.T)
    bf = lambda w: w.astype(BF16)
    row = lambda v: v[None, :]

    for i in range(DEPTH):
        xt = _ffn(xt, row(ffn_norm1[i]), bf(ffn1_wg[i]), bf(ffn1_wu[i]), bf(ffn1_wd[i]))
        j = i // N_MIXERS
        if i % N_MIXERS == 0:
            w, qg, kg = _prep_a(a_w_in[j], a_q_gain[j], a_k_gain[j])
            q, k4, v4 = _a_qkv(xt, row(mix_norm[i]), w, qg, kg)
            attn = _a_attn(q, k4, v4, pos_col, pos_row, table, a_sinks[j])
            w_out = bf(a_w_out[j])
        else:
            prep = _prep_b(b_w_in[j], b_w_uq[j], b_w_ukv[j], b_q_gain[j], b_k_gain[j])
            win, wq, wqs, wk, wv, qg, qgs, kg, krg, krgs = prep
            q, k, v = _b_proj(xt, pos_col, row(mix_norm[i]), win, row(b_q_norm[j]), row(b_kv_norm[j]),
                              wq, wqs, wk, wv, qg, qgs, kg, krg, krgs, _rope_freq())
            attn = _b_attn(q, k, v)
            w_out = bf(b_w_out[j])
        xt = _ffn(xt, row(ffn_norm2[i]), bf(ffn2_wg[i]), bf(ffn2_wu[i]), bf(ffn2_wd[i]), attn, w_out)
    return xt.reshape(BATCH, SEQ, D_MODEL)
```

```python
import functools
import math

import numpy as np
import jax
import jax.numpy as jnp
from jax import lax
from jax.experimental import pallas as pl
from jax.experimental.pallas import tpu as pltpu

D_MODEL = 1024
BATCH = 8
SEQ = 2048
DEPTH = 2
N_MIXERS = 2
A_HEADS = 16
A_KV_HEADS = 2
A_HEAD_DIM = 64
WINDOW = 128
BLOCK = 128
NUM_BUCKETS = 32
MAX_DISTANCE = 128
B_HEADS = 16
Q_LORA = 256
KV_LORA = 128
QK_NOPE = 64
QK_ROPE = 32
V_DIM = 64
ROPE_BASE = 10000.0
D_FF = 2816
EPS = 1e-6
NEG = -1e30

TOKENS = BATCH * SEQ
LANES = 128
HALF = LANES // 2
B_DQK = QK_NOPE + QK_ROPE
VMEM_LIMIT = 56 * 1024 * 1024

F32 = jnp.float32
BF16 = jnp.bfloat16


def _rms_rows(x, gain):
    return x * lax.rsqrt(jnp.mean(x * x, axis=-1, keepdims=True) + EPS) * gain


def _dot(a, b):
    return jnp.dot(a, b, preferred_element_type=F32)


def _dot_nt(a, b):
    return lax.dot_general(a, b, (((1,), (1,)), ((), ())), preferred_element_type=F32)


def _const_spec(shape):
    nd = len(shape)
    return pl.BlockSpec(shape, lambda *_: (0,) * nd, pipeline_mode=pl.Buffered(1))


FFN_TM = 512


def _ffn_body(x, g_ref, wg_ref, wu_ref, wd_ref, o_ref):
    h = _rms_rows(x, g_ref[...]).astype(BF16)
    gate = _dot(h, wg_ref[...])
    up = _dot(h, wu_ref[...])
    act = (gate * jax.nn.sigmoid(gate) * up).astype(BF16)
    o_ref[...] = x + 0.5 * _dot(act, wd_ref[...])


def _ffn_kernel(x_ref, g_ref, wg_ref, wu_ref, wd_ref, o_ref):
    _ffn_body(x_ref[...], g_ref, wg_ref, wu_ref, wd_ref, o_ref)


def _proj_ffn_kernel(x_ref, a_ref, wo_ref, g_ref, wg_ref, wu_ref, wd_ref, o_ref):
    x = x_ref[...] + _dot(a_ref[...], wo_ref[...])
    _ffn_body(x, g_ref, wg_ref, wu_ref, wd_ref, o_ref)


def _ffn(x, gain, wg, wu, wd, attn=None, w_out=None):
    tm = FFN_TM
    row_spec = pl.BlockSpec((tm, D_MODEL), lambda i: (i, 0))
    w_specs = [_const_spec((1, D_MODEL)), _const_spec((D_MODEL, D_FF)), _const_spec((D_MODEL, D_FF)),
               _const_spec((D_FF, D_MODEL))]
    if attn is None:
        kern, in_specs, args = _ffn_kernel, [row_spec] + w_specs, (x, gain, wg, wu, wd)
    else:
        kern = _proj_ffn_kernel
        in_specs = [row_spec, row_spec, _const_spec((D_MODEL, D_MODEL))] + w_specs
        args = (x, attn, w_out, gain, wg, wu, wd)
    return pl.pallas_call(
        kern,
        grid=(TOKENS // tm,),
        in_specs=in_specs,
        out_specs=row_spec,
        out_shape=jax.ShapeDtypeStruct((TOKENS, D_MODEL), F32),
        compiler_params=pltpu.CompilerParams(vmem_limit_bytes=VMEM_LIMIT),
        name="ffn" if attn is None else "proj_ffn",
    )(*args)


A_TM = 512
A_QW = A_HEADS * A_HEAD_DIM
A_KW = 4 * LANES
A_PAIRS = A_HEADS // 2
A_ROWS = 256


def _a_qkv_kernel(x_ref, g_ref, w_ref, qg_ref, kg_ref, q_ref, k_ref, v_ref):
    h = _rms_rows(x_ref[...], g_ref[...]).astype(BF16)
    qkv = _dot(h, w_ref[...])
    lane = lax.broadcasted_iota(jnp.int32, (1, LANES), 1)
    low = lane < HALF
    scale = A_HEAD_DIM ** -0.5
    for p in range(A_PAIRS):
        qp = qkv[:, p * LANES:(p + 1) * LANES]
        sq = qp * qp
        ms_lo = jnp.sum(jnp.where(low, sq, 0.0), axis=-1, keepdims=True) * (1.0 / A_HEAD_DIM)
        ms_hi = jnp.sum(jnp.where(low, 0.0, sq), axis=-1, keepdims=True) * (1.0 / A_HEAD_DIM)
        r = jnp.where(low, lax.rsqrt(ms_lo + EPS), lax.rsqrt(ms_hi + EPS))
        q_ref[:, p * LANES:(p + 1) * LANES] = (qp * r * qg_ref[...] * scale).astype(BF16)
    for j in range(A_KW // LANES):
        kj = qkv[:, A_QW + j * LANES:A_QW + (j + 1) * LANES]
        ms = jnp.sum(kj * kj, axis=-1, keepdims=True) * (1.0 / A_HEAD_DIM)
        k_ref[:, j * LANES:(j + 1) * LANES] = (
            kj * lax.rsqrt(ms + EPS) * kg_ref[:, j * LANES:(j + 1) * LANES]).astype(BF16)
    v_ref[...] = qkv[:, A_QW + A_KW:].astype(BF16)


def _a_qkv(x, gain, w, qg, kg):
    tm = A_TM
    width = A_QW + 2 * A_KW
    return pl.pallas_call(
        _a_qkv_kernel,
        grid=(TOKENS // tm,),
        in_specs=[pl.BlockSpec((tm, D_MODEL), lambda i: (i, 0)), _const_spec((1, D_MODEL)),
                  _const_spec((D_MODEL, width)), _const_spec((1, LANES)), _const_spec((1, A_KW))],
        out_specs=[pl.BlockSpec((tm, A_QW), lambda i: (i, 0)), pl.BlockSpec((tm, A_KW), lambda i: (i, 0)),
                   pl.BlockSpec((tm, A_KW), lambda i: (i, 0))],
        out_shape=[jax.ShapeDtypeStruct((TOKENS, A_QW), BF16), jax.ShapeDtypeStruct((TOKENS, A_KW), BF16),
                   jax.ShapeDtypeStruct((TOKENS, A_KW), BF16)],
        compiler_params=pltpu.CompilerParams(vmem_limit_bytes=VMEM_LIMIT),
        name="a_qkv",
    )(x, gain, w, qg, kg)


def _t5_bucket(dist):
    n = jnp.maximum(dist, 0)
    max_exact = NUM_BUCKETS // 2
    large = max_exact + (jnp.log(jnp.maximum(n, 1).astype(F32) / max_exact)
                         / math.log(MAX_DISTANCE / max_exact)
                         * (NUM_BUCKETS - max_exact)).astype(jnp.int32)
    large = jnp.minimum(large, NUM_BUCKETS - 1)
    return jnp.where(n < max_exact, n, large)


SUBLANES = 8


def _a_attn_kernel(sink_ref, q_ref, kc_ref, kp_ref, vc_ref, vp_ref, pq_ref, pkc_ref, pkp_ref, tbl_ref, o_ref,
                   bias_ref, shared_ref):
    row = lax.broadcasted_iota(jnp.int32, (BLOCK, BLOCK), 0)
    col = lax.broadcasted_iota(jnp.int32, (BLOCK, BLOCK), 1)
    cur_ok = col <= row
    tri_prev = col > row
    first_in_seq = pl.program_id(1) == 0
    no_prev = jnp.where(first_in_seq, 2 * BLOCK, 0)
    tables = [jnp.broadcast_to(tbl_ref[h:h + 1, :], (SUBLANES, LANES)) for h in range(A_HEADS)]
    ones = jnp.ones((2 * BLOCK, LANES), BF16)
    group_heads = A_HEADS // A_KV_HEADS
    lane = lax.broadcasted_iota(jnp.int32, (1, LANES), 1)

    @pl.when(jnp.logical_and(pl.program_id(0) == 0, first_in_seq))
    def _():
        shared_ref[0] = 0

    def fill_shift_invariant():
        back = jnp.broadcast_to((BLOCK - lane) & (BLOCK - 1), (SUBLANES, LANES))
        idx = _t5_bucket(back)
        for h in range(A_HEADS):
            base = jnp.take_along_axis(tables[h], idx, axis=1)
            base = jnp.broadcast_to(base[0:1, :], (BLOCK, BLOCK))
            bias_ref[h] = pltpu.roll(base, 0, 1, stride=1, stride_axis=0)

    def fill_general(pos_q, pos_cur, pos_prev):
        bucket = jnp.where(cur_ok, _t5_bucket(pos_q - pos_cur), _t5_bucket(pos_q - pos_prev))
        for c in range(BLOCK // SUBLANES):
            idx = bucket[c * SUBLANES:(c + 1) * SUBLANES, :]
            for h in range(A_HEADS):
                bias_ref[h, c * SUBLANES:(c + 1) * SUBLANES, :] = jnp.take_along_axis(tables[h], idx, axis=1)

    for r in range(A_ROWS // BLOCK):
        rows = slice(r * BLOCK, (r + 1) * BLOCK)
        pos_cur = pkc_ref[r]
        if r == 0:
            k_prev, v_prev, pos_prev = kp_ref[...], vp_ref[...], pkp_ref[0]
            prev_ok = col > row + no_prev
        else:
            before = slice((r - 1) * BLOCK, r * BLOCK)
            k_prev, v_prev, pos_prev = kc_ref[before, :], vc_ref[before, :], pkc_ref[r - 1]
            prev_ok = tri_prev
        k_band = jnp.concatenate([k_prev, kc_ref[rows, :]], axis=0)
        v_band = jnp.concatenate([v_prev, vc_ref[rows, :]], axis=0)

        start = jnp.min(pos_cur, axis=-1, keepdims=True)
        off_cur = jnp.sum(jnp.where(pos_cur - start == lane, 0, 1))
        off_prev = jnp.sum(jnp.where(pos_prev - start == lane - BLOCK, 0, 1))
        if r == 0:
            off_prev = jnp.where(first_in_seq, 0, off_prev)
        consecutive = (off_cur + off_prev) == 0

        @pl.when(jnp.logical_and(consecutive, shared_ref[0] == 0))
        def _():
            fill_shift_invariant()
            shared_ref[0] = 1

        @pl.when(jnp.logical_not(consecutive))
        def _():
            fill_general(pq_ref[rows, :], pos_cur, pos_prev)
            shared_ref[0] = 0

        bias = [bias_ref[h] for h in range(A_HEADS)]
        for kv in range(A_KV_HEADS):
            pairs = range(kv * group_heads // 2, (kv + 1) * group_heads // 2)
            q_stack = jnp.concatenate([q_ref[rows, p * LANES:(p + 1) * LANES] for p in pairs], axis=0)
            halves = []
            for parity in range(2):
                group = slice((2 * kv + parity) * LANES, (2 * kv + parity + 1) * LANES)
                s_all = _dot_nt(q_stack, k_band[:, group])
                probs, sink_terms = [], []
                for i, p in enumerate(pairs):
                    head = 2 * p + parity
                    s = s_all[i * BLOCK:(i + 1) * BLOCK, :]
                    s_prev = jnp.where(prev_ok, s[:, :BLOCK] + bias[head], NEG)
                    s_cur = jnp.where(cur_ok, s[:, BLOCK:] + bias[head], NEG)
                    sink = sink_ref[head]
                    m = jnp.maximum(jnp.max(jnp.maximum(s_prev, s_cur), axis=-1, keepdims=True), sink)
                    probs.append(jnp.concatenate([jnp.exp(s_prev - m), jnp.exp(s_cur - m)], axis=1).astype(BF16))
                    sink_terms.append(jnp.exp(sink - m))
                v_ext = jnp.concatenate([v_band[:, group], ones], axis=1)
                out = _dot(jnp.concatenate(probs, axis=0), v_ext)
                scaled = []
                for i in range(len(pairs)):
                    o_i = out[i * BLOCK:(i + 1) * BLOCK, :]
                    scaled.append(o_i[:, :LANES] * (1.0 / (o_i[:, LANES:] + sink_terms[i])))
                halves.append(scaled)
            for i, p in enumerate(pairs):
                o_ref[rows, p * LANES:(p + 1) * LANES] = (halves[0][i] + halves[1][i]).astype(BF16)


def _a_attn(q, k4, v4, pos_col, pos_row, table, sinks):
    steps = SEQ // A_ROWS
    blocks_per_step = A_ROWS // BLOCK
    blocks_per_seq = SEQ // BLOCK

    def cur(b, i):
        return (b * steps + i, 0)

    def prev(b, i):
        return (b * blocks_per_seq + jnp.maximum(i * blocks_per_step - 1, 0), 0)

    def cur3(b, i):
        return (b * steps + i, 0, 0)

    def prev3(b, i):
        return (b * blocks_per_seq + jnp.maximum(i * blocks_per_step - 1, 0), 0, 0)

    return pl.pallas_call(
        _a_attn_kernel,
        grid=(BATCH, steps),
        in_specs=[
            pl.BlockSpec(memory_space=pltpu.SMEM),
            pl.BlockSpec((A_ROWS, A_QW), cur),
            pl.BlockSpec((A_ROWS, A_KW), cur),
            pl.BlockSpec((BLOCK, A_KW), prev),
            pl.BlockSpec((A_ROWS, A_KW), cur),
            pl.BlockSpec((BLOCK, A_KW), prev),
            pl.BlockSpec((A_ROWS, 1), cur),
            pl.BlockSpec((blocks_per_step, 1, BLOCK), cur3),
            pl.BlockSpec((1, 1, BLOCK), prev3),
            _const_spec((A_HEADS, LANES)),
        ],
        out_specs=pl.BlockSpec((A_ROWS, A_QW), cur),
        out_shape=jax.ShapeDtypeStruct((TOKENS, A_QW), BF16),
        scratch_shapes=[pltpu.VMEM((A_HEADS, BLOCK, BLOCK), F32), pltpu.SMEM((1,), jnp.int32)],
        compiler_params=pltpu.CompilerParams(dimension_semantics=("arbitrary", "arbitrary"),
                                             vmem_limit_bytes=VMEM_LIMIT),
        name="a_attn",
    )(sinks, q, k4, k4, v4, v4, pos_col, pos_row, pos_row, table)


B_TM = 256
B_CW = Q_LORA + KV_LORA + 2 * LANES
B_HW = B_HEADS * LANES
B_TQ = 512
B_TK = 512


def _b_proj_kernel(x_ref, pos_ref, g_ref, win_ref, qn_ref, kvn_ref, wq_ref, wqs_ref, wk_ref, wv_ref,
                   qg_ref, qgs_ref, kg_ref, krg_ref, krgs_ref, freq_ref, q_ref, k_ref, v_ref):
    h = _rms_rows(x_ref[...], g_ref[...]).astype(BF16)
    c = _dot(h, win_ref[...])
    cq = _rms_rows(c[:, :Q_LORA], qn_ref[...]).astype(BF16)
    ckv = _rms_rows(c[:, Q_LORA:Q_LORA + KV_LORA], kvn_ref[...]).astype(BF16)
    kr = c[:, Q_LORA + KV_LORA:Q_LORA + KV_LORA + LANES]
    kr_partner = c[:, Q_LORA + KV_LORA + LANES:]
    q = _dot(cq, wq_ref[...])
    q_partner = _dot(cq, wqs_ref[...])
    kn = _dot(ckv, wk_ref[...])
    v_ref[...] = _dot(ckv, wv_ref[...]).astype(BF16)

    lane = lax.broadcasted_iota(jnp.int32, (1, LANES), 1)
    ang = pos_ref[...].astype(F32) * freq_ref[...]
    cos = jnp.cos(ang)
    sin = jnp.sin(ang)
    half = QK_ROPE // 2
    sin_signed = jnp.where(lane < QK_NOPE + half, -sin, sin)

    k_rope = kr * krg_ref[...] * cos + kr_partner * krgs_ref[...] * sin_signed
    ss_rope = jnp.sum(kr * kr, axis=-1, keepdims=True)
    scale = B_DQK ** -0.5
    for hd in range(B_HEADS):
        g = slice(hd * LANES, (hd + 1) * LANES)
        qh = q[:, g]
        r = lax.rsqrt(jnp.sum(qh * qh, axis=-1, keepdims=True) * (1.0 / B_DQK) + EPS)
        q_rot = qh * qg_ref[...] * cos + q_partner[:, g] * qgs_ref[...] * sin_signed
        q_ref[:, g] = (q_rot * (r * scale)).astype(BF16)
        kh = kn[:, g]
        rk = lax.rsqrt((jnp.sum(kh * kh, axis=-1, keepdims=True) + ss_rope) * (1.0 / B_DQK) + EPS)
        k_ref[:, g] = ((kh * kg_ref[...] + k_rope) * rk).astype(BF16)


def _b_proj(x, pos_col, gain, w_in, qn, kvn, wq, wqs, wk, wv, qg, qgs, kg, krg, krgs, freq):
    tm = B_TM
    lane_vec = _const_spec((1, LANES))
    out_spec = pl.BlockSpec((tm, B_HW), lambda i: (i, 0))
    out_sds = jax.ShapeDtypeStruct((TOKENS, B_HW), BF16)
    return pl.pallas_call(
        _b_proj_kernel,
        grid=(TOKENS // tm,),
        in_specs=[pl.BlockSpec((tm, D_MODEL), lambda i: (i, 0)), pl.BlockSpec((tm, 1), lambda i: (i, 0)),
                  _const_spec((1, D_MODEL)), _const_spec((D_MODEL, B_CW)), _const_spec((1, Q_LORA)),
                  _const_spec((1, KV_LORA)), _const_spec((Q_LORA, B_HW)), _const_spec((Q_LORA, B_HW)),
                  _const_spec((KV_LORA, B_HW)), _const_spec((KV_LORA, B_HW)),
                  lane_vec, lane_vec, lane_vec, lane_vec, lane_vec, lane_vec],
        out_specs=[out_spec, out_spec, out_spec],
        out_shape=[out_sds, out_sds, out_sds],
        compiler_params=pltpu.CompilerParams(vmem_limit_bytes=VMEM_LIMIT),
        name="b_proj",
    )(x, pos_col, gain, w_in, qn, kvn, wq, wqs, wk, wv, qg, qgs, kg, krg, krgs, freq)


def _b_attn_kernel(q_ref, k_ref, v_ref, o_ref):
    qi = pl.program_id(2)
    row = lax.broadcasted_iota(jnp.int32, (B_TQ, B_TK), 0)
    col = lax.broadcasted_iota(jnp.int32, (B_TQ, B_TK), 1)
    diag_ok = col <= row
    groups = [slice(parity * LANES, (parity + 1) * LANES) for parity in range(2)]
    ones = jnp.ones((B_TK, LANES), BF16)

    def tile(j, carry, masked):
        keys = slice(j * B_TK, (j + 1) * B_TK)
        new = []
        for g, (m, acc) in zip(groups, carry):
            s = _dot_nt(q_ref[:, g], k_ref[keys, g])
            if masked:
                s = jnp.where(diag_ok, s, NEG)
            m_new = jnp.maximum(m, jnp.max(s, axis=-1, keepdims=True))
            alpha = jnp.exp(m - m_new)
            e = jnp.exp(s - m_new).astype(BF16)
            v_ext = jnp.concatenate([v_ref[keys, g], ones], axis=1)
            new.append((m_new, alpha * acc + _dot(e, v_ext)))
        return tuple(new)

    def attend(n_full):
        init = (jnp.full((B_TQ, 1), NEG, F32), jnp.zeros((B_TQ, 2 * LANES), F32))
        carry = (init, init)
        for j in range(n_full + 1):
            carry = tile(j, carry, masked=(j == n_full))
        (_, acc0), (_, acc1) = carry
        o_ref[...] = (acc0[:, :LANES] * (1.0 / acc0[:, LANES:])
                      + acc1[:, :LANES] * (1.0 / acc1[:, LANES:])).astype(BF16)

    for n_full in range(SEQ // B_TQ):
        pl.when(qi == n_full)(functools.partial(attend, n_full))


def _b_attn(q, k, v):
    nq = SEQ // B_TQ
    pairs = B_HEADS // 2
    return pl.pallas_call(
        _b_attn_kernel,
        grid=(BATCH, pairs, nq),
        in_specs=[pl.BlockSpec((B_TQ, 2 * LANES), lambda b, p, i: (b * nq + i, p)),
                  pl.BlockSpec((SEQ, 2 * LANES), lambda b, p, i: (b, p)),
                  pl.BlockSpec((SEQ, 2 * LANES), lambda b, p, i: (b, p))],
        out_specs=pl.BlockSpec((B_TQ, LANES), lambda b, p, i: (b * nq + i, p)),
        out_shape=jax.ShapeDtypeStruct((TOKENS, B_HEADS * V_DIM), BF16),
        compiler_params=pltpu.CompilerParams(vmem_limit_bytes=VMEM_LIMIT),
        name="b_attn",
    )(q, k, v)


def _lohi(cols):
    z = jnp.zeros_like(cols)
    return jnp.concatenate([cols, z, z, cols], axis=1)


def _prep_a(w_in, q_gain, k_gain):
    wq = w_in[:, :A_QW]
    k0 = w_in[:, A_QW:A_QW + A_HEAD_DIM]
    k1 = w_in[:, A_QW + A_HEAD_DIM:A_QW + 2 * A_HEAD_DIM]
    v0 = w_in[:, A_QW + 2 * A_HEAD_DIM:A_QW + 3 * A_HEAD_DIM]
    v1 = w_in[:, A_QW + 3 * A_HEAD_DIM:]
    w = jnp.concatenate([wq, _lohi(k0), _lohi(k1), _lohi(v0), _lohi(v1)], axis=1).astype(BF16)
    qg = jnp.concatenate([q_gain, q_gain])[None, :]
    z = jnp.zeros_like(k_gain)
    kg = jnp.concatenate([k_gain, z, z, k_gain, k_gain, z, z, k_gain])[None, :]
    return w, qg, kg


def _head_groups(w, per_head, src_lo, src_hi, dst_lo):
    rows = w.shape[0]
    w3 = w.reshape(rows, B_HEADS, per_head)[:, :, src_lo:src_hi]
    out = jnp.zeros((rows, B_HEADS, LANES), w.dtype)
    out = out.at[:, :, dst_lo:dst_lo + (src_hi - src_lo)].set(w3)
    return out


def _prep_b(w_in, w_uq, w_ukv, q_gain, k_gain):
    half = QK_ROPE // 2
    t1 = slice(QK_NOPE, QK_NOPE + half)
    t2 = slice(QK_NOPE + half, B_DQK)
    rows = w_in.shape[0]
    rope_cols = w_in[:, Q_LORA + KV_LORA:]
    zeros = lambda n: jnp.zeros((rows, n), w_in.dtype)
    rope_group = jnp.concatenate([zeros(QK_NOPE), rope_cols, zeros(LANES - B_DQK)], axis=1)
    partner_group = jnp.concatenate(
        [zeros(QK_NOPE), rope_cols[:, half:], rope_cols[:, :half], zeros(LANES - B_DQK)], axis=1)
    win = jnp.concatenate([w_in[:, :Q_LORA + KV_LORA], rope_group, partner_group], axis=1).astype(BF16)

    wq = _head_groups(w_uq, B_DQK, 0, B_DQK, 0)
    wqs = (_head_groups(w_uq, B_DQK, t2.start, t2.stop, t1.start)
           + _head_groups(w_uq, B_DQK, t1.start, t1.stop, t2.start))
    wk = _head_groups(w_ukv, QK_NOPE + V_DIM, 0, QK_NOPE, 0)
    v_lo = _head_groups(w_ukv, QK_NOPE + V_DIM, QK_NOPE, QK_NOPE + V_DIM, 0)
    v_hi = _head_groups(w_ukv, QK_NOPE + V_DIM, QK_NOPE, QK_NOPE + V_DIM, HALF)
    even = (jnp.arange(B_HEADS) % 2 == 0)[None, :, None]
    wv = jnp.where(even, v_lo, v_hi)
    flat = lambda a: a.reshape(a.shape[0], B_HW).astype(BF16)

    def lane_vec(pieces):
        out = jnp.zeros((LANES,), F32)
        for lo, vals in pieces:
            out = out.at[lo:lo + vals.shape[0]].set(vals)
        return out[None, :]

    qg = lane_vec([(0, q_gain)])
    qgs = lane_vec([(t1.start, q_gain[t2]), (t2.start, q_gain[t1])])
    kg = lane_vec([(0, k_gain[:QK_NOPE])])
    krg = lane_vec([(QK_NOPE, k_gain[QK_NOPE:])])
    krgs = lane_vec([(t1.start, k_gain[t2]), (t2.start, k_gain[t1])])
    return win, flat(wq), flat(wqs), flat(wk), flat(wv), qg, qgs, kg, krg, krgs


def _rope_freq():
    inv = ROPE_BASE ** (-np.arange(0, QK_ROPE, 2, dtype=np.float32) / QK_ROPE)
    out = np.zeros((1, LANES), np.float32)
    half = QK_ROPE // 2
    out[0, QK_NOPE:QK_NOPE + half] = inv
    out[0, QK_NOPE + half:B_DQK] = inv
    return jnp.asarray(out)


def kernel(x, positions, rel_bias, ffn_norm1, ffn1_wg, ffn1_wu, ffn1_wd, mix_norm, ffn_norm2, ffn2_wg,
           ffn2_wu, ffn2_wd, a_w_in, a_q_gain, a_k_gain, a_sinks, a_w_out, b_w_in, b_q_norm, b_kv_norm,
           b_w_uq, b_w_ukv, b_q_gain, b_k_gain, b_w_out):
    assert x.shape == (BATCH, SEQ, D_MODEL) and positions.shape == (BATCH, SEQ)
    xt = x.reshape(TOKENS, D_MODEL)
    pos_col = positions.reshape(TOKENS, 1)
    pos_row = positions.reshape(TOKENS // BLOCK, 1, BLOCK)
    table = jnp.zeros((A_HEADS, LANES), F32).at[:, :NUM_BUCKETS].set(rel_bias.T)
    bf = lambda w: w.astype(BF16)
    row = lambda v: v[None, :]

    for i in range(DEPTH):
        xt = _ffn(xt, row(ffn_norm1[i]), bf(ffn1_wg[i]), bf(ffn1_wu[i]), bf(ffn1_wd[i]))
        j = i // N_MIXERS
        if i % N_MIXERS == 0:
            w, qg, kg = _prep_a(a_w_in[j], a_q_gain[j], a_k_gain[j])
            q, k4, v4 = _a_qkv(xt, row(mix_norm[i]), w, qg, kg)
            attn = _a_attn(q, k4, v4, pos_col, pos_row, table, a_sinks[j])
            w_out = bf(a_w_out[j])
        else:
            prep = _prep_b(b_w_in[j], b_w_uq[j], b_w_ukv[j], b_q_gain[j], b_k_gain[j])
            win, wq, wqs, wk, wv, qg, qgs, kg, krg, krgs = prep
            q, k, v = _b_proj(xt, pos_col, row(mix_norm[i]), win, row(b_q_norm[j]), row(b_kv_norm[j]),
                              wq, wqs, wk, wv, qg, qgs, kg, krg, krgs, _rope_freq())
            attn = _b_attn(q, k, v)
            w_out = bf(b_w_out[j])
        xt = _ffn(xt, row(ffn_norm2[i]), bf(ffn2_wg[i]), bf(ffn2_wu[i]), bf(ffn2_wd[i]), attn, w_out)
    return xt.reshape(BATCH, SEQ, D_MODEL)
```

```python
import functools
import math

import numpy as np
import jax
import jax.numpy as jnp
from jax import lax
from jax.experimental import pallas as pl
from jax.experimental.pallas import tpu as pltpu

D_MODEL = 1024
BATCH = 8
SEQ = 2048
DEPTH = 2
N_MIXERS = 2
A_HEADS = 16
A_KV_HEADS = 2
A_HEAD_DIM = 64
WINDOW = 128
BLOCK = 128
NUM_BUCKETS = 32
MAX_DISTANCE = 128
B_HEADS = 16
Q_LORA = 256
KV_LORA = 128
QK_NOPE = 64
QK_ROPE = 32
V_DIM = 64
ROPE_BASE = 10000.0
D_FF = 2816
EPS = 1e-6
NEG = -1e30

TOKENS = BATCH * SEQ
LANES = 128
HALF = LANES // 2
B_DQK = QK_NOPE + QK_ROPE
VMEM_LIMIT = 56 * 1024 * 1024

F32 = jnp.float32
BF16 = jnp.bfloat16


def _rms_rows(x, gain):
    return x * lax.rsqrt(jnp.mean(x * x, axis=-1, keepdims=True) + EPS) * gain


def _dot(a, b):
    return jnp.dot(a, b, preferred_element_type=F32)


def _dot_nt(a, b):
    return lax.dot_general(a, b, (((1,), (1,)), ((), ())), preferred_element_type=F32)


def _const_spec(shape):
    nd = len(shape)
    return pl.BlockSpec(shape, lambda *_: (0,) * nd, pipeline_mode=pl.Buffered(1))


FFN_TM = 512


def _ffn_body(x, g_ref, wg_ref, wu_ref, wd_ref, o_ref):
    h = _rms_rows(x, g_ref[...]).astype(BF16)
    gate = _dot(h, wg_ref[...])
    up = _dot(h, wu_ref[...])
    act = (gate * jax.nn.sigmoid(gate) * up).astype(BF16)
    o_ref[...] = x + 0.5 * _dot(act, wd_ref[...])


def _ffn_kernel(x_ref, g_ref, wg_ref, wu_ref, wd_ref, o_ref):
    _ffn_body(x_ref[...], g_ref, wg_ref, wu_ref, wd_ref, o_ref)


def _proj_ffn_kernel(x_ref, a_ref, wo_ref, g_ref, wg_ref, wu_ref, wd_ref, o_ref):
    x = x_ref[...] + _dot(a_ref[...], wo_ref[...])
    _ffn_body(x, g_ref, wg_ref, wu_ref, wd_ref, o_ref)


def _layer_spec(layer, shape):
    nd = len(shape)
    return pl.BlockSpec((None,) + shape, lambda *_: (layer,) + (0,) * nd, pipeline_mode=pl.Buffered(1))


def _ffn(x, layer, gain, wg, wu, wd, attn=None, w_out=None):
    tm = FFN_TM
    row_spec = pl.BlockSpec((tm, D_MODEL), lambda i: (i, 0))
    w_specs = [_layer_spec(layer, (1, D_MODEL)), _layer_spec(layer, (D_MODEL, D_FF)),
               _layer_spec(layer, (D_MODEL, D_FF)), _layer_spec(layer, (D_FF, D_MODEL))]
    if attn is None:
        kern, in_specs, args = _ffn_kernel, [row_spec] + w_specs, (x, gain, wg, wu, wd)
    else:
        kern = _proj_ffn_kernel
        in_specs = [row_spec, row_spec, _const_spec((D_MODEL, D_MODEL))] + w_specs
        args = (x, attn, w_out, gain, wg, wu, wd)
    return pl.pallas_call(
        kern,
        grid=(TOKENS // tm,),
        in_specs=in_specs,
        out_specs=row_spec,
        out_shape=jax.ShapeDtypeStruct((TOKENS, D_MODEL), F32),
        compiler_params=pltpu.CompilerParams(vmem_limit_bytes=VMEM_LIMIT),
        name="ffn" if attn is None else "proj_ffn",
    )(*args)


A_TM = 512
A_QW = A_HEADS * A_HEAD_DIM
A_KW = 4 * LANES
A_PAIRS = A_HEADS // 2
A_ROWS = 256


def _a_qkv_kernel(x_ref, g_ref, w_ref, qg_ref, kg_ref, q_ref, k_ref, v_ref):
    h = _rms_rows(x_ref[...], g_ref[...]).astype(BF16)
    qkv = _dot(h, w_ref[...])
    lane = lax.broadcasted_iota(jnp.int32, (1, LANES), 1)
    low = lane < HALF
    scale = A_HEAD_DIM ** -0.5
    for p in range(A_PAIRS):
        qp = qkv[:, p * LANES:(p + 1) * LANES]
        sq = qp * qp
        ms_lo = jnp.sum(jnp.where(low, sq, 0.0), axis=-1, keepdims=True) * (1.0 / A_HEAD_DIM)
        ms_hi = jnp.sum(jnp.where(low, 0.0, sq), axis=-1, keepdims=True) * (1.0 / A_HEAD_DIM)
        r = jnp.where(low, lax.rsqrt(ms_lo + EPS), lax.rsqrt(ms_hi + EPS))
        q_ref[:, p * LANES:(p + 1) * LANES] = (qp * r * qg_ref[...] * scale).astype(BF16)
    for j in range(A_KW // LANES):
        kj = qkv[:, A_QW + j * LANES:A_QW + (j + 1) * LANES]
        ms = jnp.sum(kj * kj, axis=-1, keepdims=True) * (1.0 / A_HEAD_DIM)
        k_ref[:, j * LANES:(j + 1) * LANES] = (
            kj * lax.rsqrt(ms + EPS) * kg_ref[:, j * LANES:(j + 1) * LANES]).astype(BF16)
    v_ref[...] = qkv[:, A_QW + A_KW:].astype(BF16)


def _a_qkv(x, gain, w, qg, kg):
    tm = A_TM
    width = A_QW + 2 * A_KW
    return pl.pallas_call(
        _a_qkv_kernel,
        grid=(TOKENS // tm,),
        in_specs=[pl.BlockSpec((tm, D_MODEL), lambda i: (i, 0)), _const_spec((1, D_MODEL)),
                  _const_spec((D_MODEL, width)), _const_spec((1, LANES)), _const_spec((1, A_KW))],
        out_specs=[pl.BlockSpec((tm, A_QW), lambda i: (i, 0)), pl.BlockSpec((tm, A_KW), lambda i: (i, 0)),
                   pl.BlockSpec((tm, A_KW), lambda i: (i, 0))],
        out_shape=[jax.ShapeDtypeStruct((TOKENS, A_QW), BF16), jax.ShapeDtypeStruct((TOKENS, A_KW), BF16),
                   jax.ShapeDtypeStruct((TOKENS, A_KW), BF16)],
        compiler_params=pltpu.CompilerParams(vmem_limit_bytes=VMEM_LIMIT),
        name="a_qkv",
    )(x, gain, w, qg, kg)


def _t5_bucket(dist):
    n = jnp.maximum(dist, 0)
    max_exact = NUM_BUCKETS // 2
    large = max_exact + (jnp.log(jnp.maximum(n, 1).astype(F32) / max_exact)
                         / math.log(MAX_DISTANCE / max_exact)
                         * (NUM_BUCKETS - max_exact)).astype(jnp.int32)
    large = jnp.minimum(large, NUM_BUCKETS - 1)
    return jnp.where(n < max_exact, n, large)


SUBLANES = 8


def _a_attn_kernel(sink_ref, q_ref, kc_ref, kp_ref, vc_ref, vp_ref, pq_ref, pkc_ref, pkp_ref, tbl_ref, o_ref,
                   bias_ref, shared_ref):
    row = lax.broadcasted_iota(jnp.int32, (BLOCK, BLOCK), 0)
    col = lax.broadcasted_iota(jnp.int32, (BLOCK, BLOCK), 1)
    cur_ok = col <= row
    tri_prev = col > row
    first_in_seq = pl.program_id(1) == 0
    no_prev = jnp.where(first_in_seq, 2 * BLOCK, 0)
    tables = [jnp.broadcast_to(tbl_ref[h:h + 1, :], (SUBLANES, LANES)) for h in range(A_HEADS)]
    ones = jnp.ones((2 * BLOCK, LANES), BF16)
    group_heads = A_HEADS // A_KV_HEADS
    lane = lax.broadcasted_iota(jnp.int32, (1, LANES), 1)

    @pl.when(jnp.logical_and(pl.program_id(0) == 0, first_in_seq))
    def _():
        shared_ref[0] = 0

    def fill_shift_invariant():
        back = jnp.broadcast_to((BLOCK - lane) & (BLOCK - 1), (SUBLANES, LANES))
        idx = _t5_bucket(back)
        for h in range(A_HEADS):
            base = jnp.take_along_axis(tables[h], idx, axis=1)
            base = jnp.broadcast_to(base[0:1, :], (BLOCK, BLOCK))
            bias_ref[h] = pltpu.roll(base, 0, 1, stride=1, stride_axis=0)

    def fill_general(pos_q, pos_cur, pos_prev):
        bucket = jnp.where(cur_ok, _t5_bucket(pos_q - pos_cur), _t5_bucket(pos_q - pos_prev))
        for c in range(BLOCK // SUBLANES):
            idx = bucket[c * SUBLANES:(c + 1) * SUBLANES, :]
            for h in range(A_HEADS):
                bias_ref[h, c * SUBLANES:(c + 1) * SUBLANES, :] = jnp.take_along_axis(tables[h], idx, axis=1)

    for r in range(A_ROWS // BLOCK):
        rows = slice(r * BLOCK, (r + 1) * BLOCK)
        pos_cur = pkc_ref[r]
        if r == 0:
            k_prev, v_prev, pos_prev = kp_ref[...], vp_ref[...], pkp_ref[0]
            prev_ok = col > row + no_prev
        else:
            before = slice((r - 1) * BLOCK, r * BLOCK)
            k_prev, v_prev, pos_prev = kc_ref[before, :], vc_ref[before, :], pkc_ref[r - 1]
            prev_ok = tri_prev
        k_band = jnp.concatenate([k_prev, kc_ref[rows, :]], axis=0)
        v_band = jnp.concatenate([v_prev, vc_ref[rows, :]], axis=0)

        start = jnp.min(pos_cur, axis=-1, keepdims=True)
        off_cur = jnp.sum(jnp.where(pos_cur - start == lane, 0, 1))
        off_prev = jnp.sum(jnp.where(pos_prev - start == lane - BLOCK, 0, 1))
        if r == 0:
            off_prev = jnp.where(first_in_seq, 0, off_prev)
        consecutive = (off_cur + off_prev) == 0

        @pl.when(jnp.logical_and(consecutive, shared_ref[0] == 0))
        def _():
            fill_shift_invariant()
            shared_ref[0] = 1

        @pl.when(jnp.logical_not(consecutive))
        def _():
            fill_general(pq_ref[rows, :], pos_cur, pos_prev)
            shared_ref[0] = 0

        bias = [bias_ref[h] for h in range(A_HEADS)]
        for kv in range(A_KV_HEADS):
            pairs = range(kv * group_heads // 2, (kv + 1) * group_heads // 2)
            q_stack = jnp.concatenate([q_ref[rows, p * LANES:(p + 1) * LANES] for p in pairs], axis=0)
            halves = []
            for parity in range(2):
                group = slice((2 * kv + parity) * LANES, (2 * kv + parity + 1) * LANES)
                s_all = _dot_nt(q_stack, k_band[:, group])
                probs, sink_terms = [], []
                for i, p in enumerate(pairs):
                    head = 2 * p + parity
                    s = s_all[i * BLOCK:(i + 1) * BLOCK, :]
                    s_prev = jnp.where(prev_ok, s[:, :BLOCK] + bias[head], NEG)
                    s_cur = jnp.where(cur_ok, s[:, BLOCK:] + bias[head], NEG)
                    sink = sink_ref[head]
                    m = jnp.maximum(jnp.max(jnp.maximum(s_prev, s_cur), axis=-1, keepdims=True), sink)
                    probs.append(jnp.concatenate([jnp.exp(s_prev - m), jnp.exp(s_cur - m)], axis=1).astype(BF16))
                    sink_terms.append(jnp.exp(sink - m))
                v_ext = jnp.concatenate([v_band[:, group], ones], axis=1)
                out = _dot(jnp.concatenate(probs, axis=0), v_ext)
                scaled = []
                for i in range(len(pairs)):
                    o_i = out[i * BLOCK:(i + 1) * BLOCK, :]
                    scaled.append(o_i[:, :LANES] * (1.0 / (o_i[:, LANES:] + sink_terms[i])))
                halves.append(scaled)
            for i, p in enumerate(pairs):
                o_ref[rows, p * LANES:(p + 1) * LANES] = (halves[0][i] + halves[1][i]).astype(BF16)


def _a_attn(q, k4, v4, pos_col, pos_row, table, sinks):
    steps = SEQ // A_ROWS
    blocks_per_step = A_ROWS // BLOCK
    blocks_per_seq = SEQ // BLOCK

    def cur(b, i):
        return (b * steps + i, 0)

    def prev(b, i):
        return (b * blocks_per_seq + jnp.maximum(i * blocks_per_step - 1, 0), 0)

    def cur3(b, i):
        return (b * steps + i, 0, 0)

    def prev3(b, i):
        return (b * blocks_per_seq + jnp.maximum(i * blocks_per_step - 1, 0), 0, 0)

    return pl.pallas_call(
        _a_attn_kernel,
        grid=(BATCH, steps),
        in_specs=[
            pl.BlockSpec(memory_space=pltpu.SMEM),
            pl.BlockSpec((A_ROWS, A_QW), cur),
            pl.BlockSpec((A_ROWS, A_KW), cur),
            pl.BlockSpec((BLOCK, A_KW), prev),
            pl.BlockSpec((A_ROWS, A_KW), cur),
            pl.BlockSpec((BLOCK, A_KW), prev),
            pl.BlockSpec((A_ROWS, 1), cur),
            pl.BlockSpec((blocks_per_step, 1, BLOCK), cur3),
            pl.BlockSpec((1, 1, BLOCK), prev3),
            _const_spec((A_HEADS, LANES)),
        ],
        out_specs=pl.BlockSpec((A_ROWS, A_QW), cur),
        out_shape=jax.ShapeDtypeStruct((TOKENS, A_QW), BF16),
        scratch_shapes=[pltpu.VMEM((A_HEADS, BLOCK, BLOCK), F32), pltpu.SMEM((1,), jnp.int32)],
        compiler_params=pltpu.CompilerParams(dimension_semantics=("arbitrary", "arbitrary"),
                                             vmem_limit_bytes=VMEM_LIMIT),
        name="a_attn",
    )(sinks, q, k4, k4, v4, v4, pos_col, pos_row, pos_row, table)


B_TM = 256
B_CW = Q_LORA + KV_LORA + 2 * LANES
B_HW = B_HEADS * LANES
B_TQ = 512
B_TK = 512


def _b_proj_kernel(x_ref, pos_ref, g_ref, win_ref, qn_ref, kvn_ref, wq_ref, wqs_ref, wk_ref, wv_ref,
                   qg_ref, qgs_ref, kg_ref, krg_ref, krgs_ref, freq_ref, q_ref, k_ref, v_ref):
    h = _rms_rows(x_ref[...], g_ref[...]).astype(BF16)
    c = _dot(h, win_ref[...])
    cq = _rms_rows(c[:, :Q_LORA], qn_ref[...]).astype(BF16)
    ckv = _rms_rows(c[:, Q_LORA:Q_LORA + KV_LORA], kvn_ref[...]).astype(BF16)
    kr = c[:, Q_LORA + KV_LORA:Q_LORA + KV_LORA + LANES]
    kr_partner = c[:, Q_LORA + KV_LORA + LANES:]
    q = _dot(cq, wq_ref[...])
    q_partner = _dot(cq, wqs_ref[...])
    kn = _dot(ckv, wk_ref[...])
    v_ref[...] = _dot(ckv, wv_ref[...]).astype(BF16)

    lane = lax.broadcasted_iota(jnp.int32, (1, LANES), 1)
    ang = pos_ref[...].astype(F32) * freq_ref[...]
    cos = jnp.cos(ang)
    sin = jnp.sin(ang)
    half = QK_ROPE // 2
    sin_signed = jnp.where(lane < QK_NOPE + half, -sin, sin)

    k_rope = kr * krg_ref[...] * cos + kr_partner * krgs_ref[...] * sin_signed
    ss_rope = jnp.sum(kr * kr, axis=-1, keepdims=True)
    scale = B_DQK ** -0.5
    q_cos = qg_ref[...] * cos
    q_sin = qgs_ref[...] * sin_signed
    for hd in range(B_HEADS):
        g = slice(hd * LANES, (hd + 1) * LANES)
        qh = q[:, g]
        r = lax.rsqrt(jnp.sum(qh * qh, axis=-1, keepdims=True) * (1.0 / B_DQK) + EPS)
        q_rot = qh * q_cos + q_partner[:, g] * q_sin
        q_ref[:, g] = (q_rot * (r * scale)).astype(BF16)
        kh = kn[:, g]
        rk = lax.rsqrt((jnp.sum(kh * kh, axis=-1, keepdims=True) + ss_rope) * (1.0 / B_DQK) + EPS)
        k_ref[:, g] = ((kh * kg_ref[...] + k_rope) * rk).astype(BF16)


def _b_proj(x, pos_col, gain, w_in, qn, kvn, wq, wqs, wk, wv, qg, qgs, kg, krg, krgs, freq):
    tm = B_TM
    lane_vec = _const_spec((1, LANES))
    out_spec = pl.BlockSpec((tm, B_HW), lambda i: (i, 0))
    out_sds = jax.ShapeDtypeStruct((TOKENS, B_HW), BF16)
    return pl.pallas_call(
        _b_proj_kernel,
        grid=(TOKENS // tm,),
        in_specs=[pl.BlockSpec((tm, D_MODEL), lambda i: (i, 0)), pl.BlockSpec((tm, 1), lambda i: (i, 0)),
                  _const_spec((1, D_MODEL)), _const_spec((D_MODEL, B_CW)), _const_spec((1, Q_LORA)),
                  _const_spec((1, KV_LORA)), _const_spec((Q_LORA, B_HW)), _const_spec((Q_LORA, B_HW)),
                  _const_spec((KV_LORA, B_HW)), _const_spec((KV_LORA, B_HW)),
                  lane_vec, lane_vec, lane_vec, lane_vec, lane_vec, lane_vec],
        out_specs=[out_spec, out_spec, out_spec],
        out_shape=[out_sds, out_sds, out_sds],
        compiler_params=pltpu.CompilerParams(vmem_limit_bytes=VMEM_LIMIT),
        name="b_proj",
    )(x, pos_col, gain, w_in, qn, kvn, wq, wqs, wk, wv, qg, qgs, kg, krg, krgs, freq)


def _b_attn_kernel(q_ref, k_ref, v_ref, o_ref):
    qi = pl.program_id(2)
    row = lax.broadcasted_iota(jnp.int32, (B_TQ, B_TK), 0)
    col = lax.broadcasted_iota(jnp.int32, (B_TQ, B_TK), 1)
    diag_ok = col <= row
    groups = [slice(parity * LANES, (parity + 1) * LANES) for parity in range(2)]
    ones = jnp.ones((B_TK, LANES), BF16)

    def tile(j, carry, masked):
        keys = slice(j * B_TK, (j + 1) * B_TK)
        new = []
        for g, (m, acc) in zip(groups, carry):
            s = _dot_nt(q_ref[:, g], k_ref[keys, g])
            if masked:
                s = jnp.where(diag_ok, s, NEG)
            m_new = jnp.maximum(m, jnp.max(s, axis=-1, keepdims=True))
            alpha = jnp.exp(m - m_new)
            e = jnp.exp(s - m_new).astype(BF16)
            v_ext = jnp.concatenate([v_ref[keys, g], ones], axis=1)
            new.append((m_new, alpha * acc + _dot(e, v_ext)))
        return tuple(new)

    def attend(n_full):
        init = (jnp.full((B_TQ, 1), NEG, F32), jnp.zeros((B_TQ, 2 * LANES), F32))
        carry = (init, init)
        for j in range(n_full + 1):
            carry = tile(j, carry, masked=(j == n_full))
        (_, acc0), (_, acc1) = carry
        o_ref[...] = (acc0[:, :LANES] * (1.0 / acc0[:, LANES:])
                      + acc1[:, :LANES] * (1.0 / acc1[:, LANES:])).astype(BF16)

    for n_full in range(SEQ // B_TQ):
        pl.when(qi == n_full)(functools.partial(attend, n_full))


def _b_attn(q, k, v):
    nq = SEQ // B_TQ
    pairs = B_HEADS // 2
    return pl.pallas_call(
        _b_attn_kernel,
        grid=(BATCH, pairs, nq),
        in_specs=[pl.BlockSpec((B_TQ, 2 * LANES), lambda b, p, i: (b * nq + i, p)),
                  pl.BlockSpec((SEQ, 2 * LANES), lambda b, p, i: (b, p)),
                  pl.BlockSpec((SEQ, 2 * LANES), lambda b, p, i: (b, p))],
        out_specs=pl.BlockSpec((B_TQ, LANES), lambda b, p, i: (b * nq + i, p)),
        out_shape=jax.ShapeDtypeStruct((TOKENS, B_HEADS * V_DIM), BF16),
        compiler_params=pltpu.CompilerParams(vmem_limit_bytes=VMEM_LIMIT),
        name="b_attn",
    )(q, k, v)


def _lohi(cols):
    z = jnp.zeros_like(cols)
    return jnp.concatenate([cols, z, z, cols], axis=1)


def _prep_a(w_in, q_gain, k_gain):
    wq = w_in[:, :A_QW]
    k0 = w_in[:, A_QW:A_QW + A_HEAD_DIM]
    k1 = w_in[:, A_QW + A_HEAD_DIM:A_QW + 2 * A_HEAD_DIM]
    v0 = w_in[:, A_QW + 2 * A_HEAD_DIM:A_QW + 3 * A_HEAD_DIM]
    v1 = w_in[:, A_QW + 3 * A_HEAD_DIM:]
    w = jnp.concatenate([wq, _lohi(k0), _lohi(k1), _lohi(v0), _lohi(v1)], axis=1).astype(BF16)
    qg = jnp.concatenate([q_gain, q_gain])[None, :]
    z = jnp.zeros_like(k_gain)
    kg = jnp.concatenate([k_gain, z, z, k_gain, k_gain, z, z, k_gain])[None, :]
    return w, qg, kg


def _head_groups(w, per_head, src_lo, src_hi, dst_lo):
    rows = w.shape[0]
    w3 = w.reshape(rows, B_HEADS, per_head)[:, :, src_lo:src_hi]
    out = jnp.zeros((rows, B_HEADS, LANES), w.dtype)
    out = out.at[:, :, dst_lo:dst_lo + (src_hi - src_lo)].set(w3)
    return out


def _prep_b(w_in, w_uq, w_ukv, q_gain, k_gain):
    half = QK_ROPE // 2
    t1 = slice(QK_NOPE, QK_NOPE + half)
    t2 = slice(QK_NOPE + half, B_DQK)
    rows = w_in.shape[0]
    rope_cols = w_in[:, Q_LORA + KV_LORA:]
    zeros = lambda n: jnp.zeros((rows, n), w_in.dtype)
    rope_group = jnp.concatenate([zeros(QK_NOPE), rope_cols, zeros(LANES - B_DQK)], axis=1)
    partner_group = jnp.concatenate(
        [zeros(QK_NOPE), rope_cols[:, half:], rope_cols[:, :half], zeros(LANES - B_DQK)], axis=1)
    win = jnp.concatenate([w_in[:, :Q_LORA + KV_LORA], rope_group, partner_group], axis=1).astype(BF16)

    wq = _head_groups(w_uq, B_DQK, 0, B_DQK, 0)
    wqs = (_head_groups(w_uq, B_DQK, t2.start, t2.stop, t1.start)
           + _head_groups(w_uq, B_DQK, t1.start, t1.stop, t2.start))
    wk = _head_groups(w_ukv, QK_NOPE + V_DIM, 0, QK_NOPE, 0)
    v_lo = _head_groups(w_ukv, QK_NOPE + V_DIM, QK_NOPE, QK_NOPE + V_DIM, 0)
    v_hi = _head_groups(w_ukv, QK_NOPE + V_DIM, QK_NOPE, QK_NOPE + V_DIM, HALF)
    even = (jnp.arange(B_HEADS) % 2 == 0)[None, :, None]
    wv = jnp.where(even, v_lo, v_hi)
    flat = lambda a: a.reshape(a.shape[0], B_HW).astype(BF16)

    def lane_vec(pieces):
        out = jnp.zeros((LANES,), F32)
        for lo, vals in pieces:
            out = out.at[lo:lo + vals.shape[0]].set(vals)
        return out[None, :]

    qg = lane_vec([(0, q_gain)])
    qgs = lane_vec([(t1.start, q_gain[t2]), (t2.start, q_gain[t1])])
    kg = lane_vec([(0, k_gain[:QK_NOPE])])
    krg = lane_vec([(QK_NOPE, k_gain[QK_NOPE:])])
    krgs = lane_vec([(t1.start, k_gain[t2]), (t2.start, k_gain[t1])])
    return win, flat(wq), flat(wqs), flat(wk), flat(wv), qg, qgs, kg, krg, krgs


def _rope_freq():
    inv = ROPE_BASE ** (-np.arange(0, QK_ROPE, 2, dtype=np.float32) / QK_ROPE)
    out = np.zeros((1, LANES), np.float32)
    half = QK_ROPE // 2
    out[0, QK_NOPE:QK_NOPE + half] = inv
    out[0, QK_NOPE + half:B_DQK] = inv
    return jnp.asarray(out)


def kernel(x, positions, rel_bias, ffn_norm1, ffn1_wg, ffn1_wu, ffn1_wd, mix_norm, ffn_norm2, ffn2_wg,
           ffn2_wu, ffn2_wd, a_w_in, a_q_gain, a_k_gain, a_sinks, a_w_out, b_w_in, b_q_norm, b_kv_norm,
           b_w_uq, b_w_ukv, b_q_gain, b_k_gain, b_w_out):
    assert x.shape == (BATCH, SEQ, D_MODEL) and positions.shape == (BATCH, SEQ)
    xt = x.reshape(TOKENS, D_MODEL)
    pos_col = positions.reshape(TOKENS, 1)
    pos_row = positions.reshape(TOKENS // BLOCK, 1, BLOCK)
    table = jnp.zeros((A_HEADS, LANES), F32).at[:, :NUM_BUCKETS].set(rel_bias.T)
    bf = lambda w: w.astype(BF16)
    row = lambda v: v[None, :]
    ffn1 = (ffn_norm1[:, None, :], bf(ffn1_wg), bf(ffn1_wu), bf(ffn1_wd))
    ffn2 = (ffn_norm2[:, None, :], bf(ffn2_wg), bf(ffn2_wu), bf(ffn2_wd))

    for i in range(DEPTH):
        xt = _ffn(xt, i, *ffn1)
        j = i // N_MIXERS
        if i % N_MIXERS == 0:
            w, qg, kg = _prep_a(a_w_in[j], a_q_gain[j], a_k_gain[j])
            q, k4, v4 = _a_qkv(xt, row(mix_norm[i]), w, qg, kg)
            attn = _a_attn(q, k4, v4, pos_col, pos_row, table, a_sinks[j])
            w_out = bf(a_w_out[j])
        else:
            prep = _prep_b(b_w_in[j], b_w_uq[j], b_w_ukv[j], b_q_gain[j], b_k_gain[j])
            win, wq, wqs, wk, wv, qg, qgs, kg, krg, krgs = prep
            q, k, v = _b_proj(xt, pos_col, row(mix_norm[i]), win, row(b_q_norm[j]), row(b_kv_norm[j]),
                              wq, wqs, wk, wv, qg, qgs, kg, krg, krgs, _rope_freq())
            attn = _b_attn(q, k, v)
            w_out = bf(b_w_out[j])
        xt = _ffn(xt, i, *ffn2, attn, w_out)
    return xt.reshape(BATCH, SEQ, D_MODEL)
```

```python
import functools
import math

import numpy as np
import jax
import jax.numpy as jnp
from jax import lax
from jax.experimental import pallas as pl
from jax.experimental.pallas import tpu as pltpu

D_MODEL = 1024
BATCH = 8
SEQ = 2048
DEPTH = 2
N_MIXERS = 2
A_HEADS = 16
A_KV_HEADS = 2
A_HEAD_DIM = 64
WINDOW = 128
BLOCK = 128
NUM_BUCKETS = 32
MAX_DISTANCE = 128
B_HEADS = 16
Q_LORA = 256
KV_LORA = 128
QK_NOPE = 64
QK_ROPE = 32
V_DIM = 64
ROPE_BASE = 10000.0
D_FF = 2816
EPS = 1e-6
NEG = -1e30

TOKENS = BATCH * SEQ
LANES = 128
HALF = LANES // 2
B_DQK = QK_NOPE + QK_ROPE
VMEM_LIMIT = 56 * 1024 * 1024

F32 = jnp.float32
BF16 = jnp.bfloat16


def _rms_rows(x, gain):
    return x * lax.rsqrt(jnp.mean(x * x, axis=-1, keepdims=True) + EPS) * gain


def _dot(a, b):
    return jnp.dot(a, b, preferred_element_type=F32)


def _dot_nt(a, b):
    return lax.dot_general(a, b, (((1,), (1,)), ((), ())), preferred_element_type=F32)


def _const_spec(shape):
    nd = len(shape)
    return pl.BlockSpec(shape, lambda *_: (0,) * nd, pipeline_mode=pl.Buffered(1))


FFN_TM = 512


def _ffn_body(x, g_ref, wg_ref, wu_ref, wd_ref, o_ref):
    h = _rms_rows(x, g_ref[...]).astype(BF16)
    gate = _dot(h, wg_ref[...])
    up = _dot(h, wu_ref[...])
    act = (gate * jax.nn.sigmoid(gate) * up).astype(BF16)
    o_ref[...] = x + 0.5 * _dot(act, wd_ref[...])


def _ffn_kernel(x_ref, g_ref, wg_ref, wu_ref, wd_ref, o_ref):
    _ffn_body(x_ref[...], g_ref, wg_ref, wu_ref, wd_ref, o_ref)


def _proj_ffn_kernel(x_ref, a_ref, wo_ref, g_ref, wg_ref, wu_ref, wd_ref, o_ref):
    x = x_ref[...] + _dot(a_ref[...], wo_ref[...])
    _ffn_body(x, g_ref, wg_ref, wu_ref, wd_ref, o_ref)


def _layer_spec(layer, shape):
    nd = len(shape)
    return pl.BlockSpec((None,) + shape, lambda *_: (layer,) + (0,) * nd, pipeline_mode=pl.Buffered(1))


def _ffn(x, layer, gain, wg, wu, wd, attn=None, w_out=None):
    tm = FFN_TM
    row_spec = pl.BlockSpec((tm, D_MODEL), lambda i: (i, 0))
    w_specs = [_layer_spec(layer, (1, D_MODEL)), _layer_spec(layer, (D_MODEL, D_FF)),
               _layer_spec(layer, (D_MODEL, D_FF)), _layer_spec(layer, (D_FF, D_MODEL))]
    if attn is None:
        kern, in_specs, args = _ffn_kernel, [row_spec] + w_specs, (x, gain, wg, wu, wd)
    else:
        kern = _proj_ffn_kernel
        in_specs = [row_spec, row_spec, _const_spec((D_MODEL, D_MODEL))] + w_specs
        args = (x, attn, w_out, gain, wg, wu, wd)
    return pl.pallas_call(
        kern,
        grid=(TOKENS // tm,),
        in_specs=in_specs,
        out_specs=row_spec,
        out_shape=jax.ShapeDtypeStruct((TOKENS, D_MODEL), F32),
        compiler_params=pltpu.CompilerParams(vmem_limit_bytes=VMEM_LIMIT),
        name="ffn" if attn is None else "proj_ffn",
    )(*args)


A_TM = 512
A_QW = A_HEADS * A_HEAD_DIM
A_KW = 4 * LANES
A_PAIRS = A_HEADS // 2
A_ROWS = 256


def _a_qkv_kernel(x_ref, g_ref, w_ref, qg_ref, kg_ref, q_ref, k_ref, v_ref):
    h = _rms_rows(x_ref[...], g_ref[...]).astype(BF16)
    qkv = _dot(h, w_ref[...])
    lane = lax.broadcasted_iota(jnp.int32, (1, LANES), 1)
    low = lane < HALF
    scale = A_HEAD_DIM ** -0.5
    for p in range(A_PAIRS):
        qp = qkv[:, p * LANES:(p + 1) * LANES]
        sq = qp * qp
        ms_lo = jnp.sum(jnp.where(low, sq, 0.0), axis=-1, keepdims=True) * (1.0 / A_HEAD_DIM)
        ms_hi = jnp.sum(jnp.where(low, 0.0, sq), axis=-1, keepdims=True) * (1.0 / A_HEAD_DIM)
        r = jnp.where(low, lax.rsqrt(ms_lo + EPS), lax.rsqrt(ms_hi + EPS))
        q_ref[:, p * LANES:(p + 1) * LANES] = (qp * r * qg_ref[...] * scale).astype(BF16)
    for j in range(A_KW // LANES):
        kj = qkv[:, A_QW + j * LANES:A_QW + (j + 1) * LANES]
        ms = jnp.sum(kj * kj, axis=-1, keepdims=True) * (1.0 / A_HEAD_DIM)
        k_ref[:, j * LANES:(j + 1) * LANES] = (
            kj * lax.rsqrt(ms + EPS) * kg_ref[:, j * LANES:(j + 1) * LANES]).astype(BF16)
    v_ref[...] = qkv[:, A_QW + A_KW:].astype(BF16)


def _a_qkv(x, gain, w, qg, kg):
    tm = A_TM
    width = A_QW + 2 * A_KW
    return pl.pallas_call(
        _a_qkv_kernel,
        grid=(TOKENS // tm,),
        in_specs=[pl.BlockSpec((tm, D_MODEL), lambda i: (i, 0)), _const_spec((1, D_MODEL)),
                  _const_spec((D_MODEL, width)), _const_spec((1, LANES)), _const_spec((1, A_KW))],
        out_specs=[pl.BlockSpec((tm, A_QW), lambda i: (i, 0)), pl.BlockSpec((tm, A_KW), lambda i: (i, 0)),
                   pl.BlockSpec((tm, A_KW), lambda i: (i, 0))],
        out_shape=[jax.ShapeDtypeStruct((TOKENS, A_QW), BF16), jax.ShapeDtypeStruct((TOKENS, A_KW), BF16),
                   jax.ShapeDtypeStruct((TOKENS, A_KW), BF16)],
        compiler_params=pltpu.CompilerParams(vmem_limit_bytes=VMEM_LIMIT),
        name="a_qkv",
    )(x, gain, w, qg, kg)


def _t5_bucket(dist):
    n = jnp.maximum(dist, 0)
    max_exact = NUM_BUCKETS // 2
    large = max_exact + (jnp.log(jnp.maximum(n, 1).astype(F32) / max_exact)
                         / math.log(MAX_DISTANCE / max_exact)
                         * (NUM_BUCKETS - max_exact)).astype(jnp.int32)
    large = jnp.minimum(large, NUM_BUCKETS - 1)
    return jnp.where(n < max_exact, n, large)


SUBLANES = 8


def _a_attn_kernel(sink_ref, q_ref, kc_ref, kp_ref, vc_ref, vp_ref, pq_ref, pkc_ref, pkp_ref, tbl_ref, o_ref,
                   bias_ref, shared_ref):
    row = lax.broadcasted_iota(jnp.int32, (BLOCK, BLOCK), 0)
    col = lax.broadcasted_iota(jnp.int32, (BLOCK, BLOCK), 1)
    cur_ok = col <= row
    tri_prev = col > row
    first_in_seq = pl.program_id(1) == 0
    no_prev = jnp.where(first_in_seq, 2 * BLOCK, 0)
    tables = [jnp.broadcast_to(tbl_ref[h:h + 1, :], (SUBLANES, LANES)) for h in range(A_HEADS)]
    ones = jnp.ones((2 * BLOCK, LANES), BF16)
    group_heads = A_HEADS // A_KV_HEADS
    lane = lax.broadcasted_iota(jnp.int32, (1, LANES), 1)

    @pl.when(jnp.logical_and(pl.program_id(0) == 0, first_in_seq))
    def _():
        shared_ref[0] = 0

    def fill_shift_invariant():
        back = jnp.broadcast_to((BLOCK - lane) & (BLOCK - 1), (SUBLANES, LANES))
        idx = _t5_bucket(back)
        for h in range(A_HEADS):
            base = jnp.take_along_axis(tables[h], idx, axis=1)
            base = jnp.broadcast_to(base[0:1, :], (BLOCK, BLOCK))
            bias_ref[h] = pltpu.roll(base, 0, 1, stride=1, stride_axis=0)

    def fill_general(pos_q, pos_cur, pos_prev):
        bucket = jnp.where(cur_ok, _t5_bucket(pos_q - pos_cur), _t5_bucket(pos_q - pos_prev))
        for c in range(BLOCK // SUBLANES):
            idx = bucket[c * SUBLANES:(c + 1) * SUBLANES, :]
            for h in range(A_HEADS):
                bias_ref[h, c * SUBLANES:(c + 1) * SUBLANES, :] = jnp.take_along_axis(tables[h], idx, axis=1)

    for r in range(A_ROWS // BLOCK):
        rows = slice(r * BLOCK, (r + 1) * BLOCK)
        pos_cur = pkc_ref[r]
        if r == 0:
            k_prev, v_prev, pos_prev = kp_ref[...], vp_ref[...], pkp_ref[0]
            prev_ok = col > row + no_prev
        else:
            before = slice((r - 1) * BLOCK, r * BLOCK)
            k_prev, v_prev, pos_prev = kc_ref[before, :], vc_ref[before, :], pkc_ref[r - 1]
            prev_ok = tri_prev
        k_band = jnp.concatenate([k_prev, kc_ref[rows, :]], axis=0)
        v_band = jnp.concatenate([v_prev, vc_ref[rows, :]], axis=0)

        start = jnp.min(pos_cur, axis=-1, keepdims=True)
        off_cur = jnp.sum(jnp.where(pos_cur - start == lane, 0, 1))
        off_prev = jnp.sum(jnp.where(pos_prev - start == lane - BLOCK, 0, 1))
        if r == 0:
            off_prev = jnp.where(first_in_seq, 0, off_prev)
        consecutive = (off_cur + off_prev) == 0

        @pl.when(jnp.logical_and(consecutive, shared_ref[0] == 0))
        def _():
            fill_shift_invariant()
            shared_ref[0] = 1

        @pl.when(jnp.logical_not(consecutive))
        def _():
            fill_general(pq_ref[rows, :], pos_cur, pos_prev)
            shared_ref[0] = 0

        bias = [bias_ref[h] for h in range(A_HEADS)]
        for kv in range(A_KV_HEADS):
            pairs = range(kv * group_heads // 2, (kv + 1) * group_heads // 2)
            q_stack = jnp.concatenate([q_ref[rows, p * LANES:(p + 1) * LANES] for p in pairs], axis=0)
            halves = []
            for parity in range(2):
                group = slice((2 * kv + parity) * LANES, (2 * kv + parity + 1) * LANES)
                s_all = _dot_nt(q_stack, k_band[:, group])
                probs, sink_terms = [], []
                for i, p in enumerate(pairs):
                    head = 2 * p + parity
                    s = s_all[i * BLOCK:(i + 1) * BLOCK, :]
                    s_prev = jnp.where(prev_ok, s[:, :BLOCK] + bias[head], NEG)
                    s_cur = jnp.where(cur_ok, s[:, BLOCK:] + bias[head], NEG)
                    sink = sink_ref[head]
                    m = jnp.maximum(jnp.max(jnp.maximum(s_prev, s_cur), axis=-1, keepdims=True), sink)
                    probs.append(jnp.concatenate([jnp.exp(s_prev - m), jnp.exp(s_cur - m)], axis=1).astype(BF16))
                    sink_terms.append(jnp.exp(sink - m))
                v_ext = jnp.concatenate([v_band[:, group], ones], axis=1)
                out = _dot(jnp.concatenate(probs, axis=0), v_ext)
                scaled = []
                for i in range(len(pairs)):
                    o_i = out[i * BLOCK:(i + 1) * BLOCK, :]
                    scaled.append(o_i[:, :LANES] * (1.0 / (o_i[:, LANES:] + sink_terms[i])))
                halves.append(scaled)
            for i, p in enumerate(pairs):
                o_ref[rows, p * LANES:(p + 1) * LANES] = (halves[0][i] + halves[1][i]).astype(BF16)


def _a_attn(q, k4, v4, pos_col, pos_row, table, sinks):
    steps = SEQ // A_ROWS
    blocks_per_step = A_ROWS // BLOCK
    blocks_per_seq = SEQ // BLOCK

    def cur(b, i):
        return (b * steps + i, 0)

    def prev(b, i):
        return (b * blocks_per_seq + jnp.maximum(i * blocks_per_step - 1, 0), 0)

    def cur3(b, i):
        return (b * steps + i, 0, 0)

    def prev3(b, i):
        return (b * blocks_per_seq + jnp.maximum(i * blocks_per_step - 1, 0), 0, 0)

    return pl.pallas_call(
        _a_attn_kernel,
        grid=(BATCH, steps),
        in_specs=[
            pl.BlockSpec(memory_space=pltpu.SMEM),
            pl.BlockSpec((A_ROWS, A_QW), cur),
            pl.BlockSpec((A_ROWS, A_KW), cur),
            pl.BlockSpec((BLOCK, A_KW), prev),
            pl.BlockSpec((A_ROWS, A_KW), cur),
            pl.BlockSpec((BLOCK, A_KW), prev),
            pl.BlockSpec((A_ROWS, 1), cur),
            pl.BlockSpec((blocks_per_step, 1, BLOCK), cur3),
            pl.BlockSpec((1, 1, BLOCK), prev3),
            _const_spec((A_HEADS, LANES)),
        ],
        out_specs=pl.BlockSpec((A_ROWS, A_QW), cur),
        out_shape=jax.ShapeDtypeStruct((TOKENS, A_QW), BF16),
        scratch_shapes=[pltpu.VMEM((A_HEADS, BLOCK, BLOCK), F32), pltpu.SMEM((1,), jnp.int32)],
        compiler_params=pltpu.CompilerParams(dimension_semantics=("arbitrary", "arbitrary"),
                                             vmem_limit_bytes=VMEM_LIMIT),
        name="a_attn",
    )(sinks, q, k4, k4, v4, v4, pos_col, pos_row, pos_row, table)


B_TM = 256
B_CW = Q_LORA + KV_LORA + 2 * LANES
B_HW = B_HEADS * LANES
B_VW = B_HEADS * V_DIM
B_TQ = 512
B_TK = 512


def _b_proj_kernel(x_ref, pos_ref, g_ref, win_ref, qn_ref, kvn_ref, wq_ref, wqs_ref, wk_ref, wvt_ref,
                   qg_ref, qgs_ref, kg_ref, krg_ref, krgs_ref, freq_ref, q_ref, k_ref, vt_ref):
    h = _rms_rows(x_ref[...], g_ref[...]).astype(BF16)
    c = _dot(h, win_ref[...])
    cq = _rms_rows(c[:, :Q_LORA], qn_ref[...]).astype(BF16)
    ckv_f32 = _rms_rows(c[:, Q_LORA:Q_LORA + KV_LORA], kvn_ref[...])
    ckv = ckv_f32.astype(BF16)
    kr = c[:, Q_LORA + KV_LORA:Q_LORA + KV_LORA + LANES]
    kr_partner = c[:, Q_LORA + KV_LORA + LANES:]
    q = _dot(cq, wq_ref[...])
    q_partner = _dot(cq, wqs_ref[...])
    kn = _dot(ckv, wk_ref[...])
    vt_ref[...] = _dot(wvt_ref[...], ckv_f32.T.astype(BF16)).astype(BF16)

    lane = lax.broadcasted_iota(jnp.int32, (1, LANES), 1)
    ang = pos_ref[...].astype(F32) * freq_ref[...]
    cos = jnp.cos(ang)
    sin = jnp.sin(ang)
    half = QK_ROPE // 2
    sin_signed = jnp.where(lane < QK_NOPE + half, -sin, sin)

    k_rope = kr * krg_ref[...] * cos + kr_partner * krgs_ref[...] * sin_signed
    ss_rope = jnp.sum(kr * kr, axis=-1, keepdims=True)
    scale = B_DQK ** -0.5
    q_cos = qg_ref[...] * cos
    q_sin = qgs_ref[...] * sin_signed
    for hd in range(B_HEADS):
        g = slice(hd * LANES, (hd + 1) * LANES)
        qh = q[:, g]
        r = lax.rsqrt(jnp.sum(qh * qh, axis=-1, keepdims=True) * (1.0 / B_DQK) + EPS)
        q_rot = qh * q_cos + q_partner[:, g] * q_sin
        q_ref[:, g] = (q_rot * (r * scale)).astype(BF16)
        kh = kn[:, g]
        rk = lax.rsqrt((jnp.sum(kh * kh, axis=-1, keepdims=True) + ss_rope) * (1.0 / B_DQK) + EPS)
        k_ref[:, g] = ((kh * kg_ref[...] + k_rope) * rk).astype(BF16)


def _b_proj(x, pos_col, gain, w_in, qn, kvn, wq, wqs, wk, wvt, qg, qgs, kg, krg, krgs, freq):
    tm = B_TM
    tiles_per_seq = SEQ // tm
    lane_vec = _const_spec((1, LANES))
    out_spec = pl.BlockSpec((tm, B_HW), lambda i: (i, 0))
    out_sds = jax.ShapeDtypeStruct((TOKENS, B_HW), BF16)
    vt_spec = pl.BlockSpec((None, B_VW, tm), lambda i: (i // tiles_per_seq, 0, i % tiles_per_seq))
    return pl.pallas_call(
        _b_proj_kernel,
        grid=(TOKENS // tm,),
        in_specs=[pl.BlockSpec((tm, D_MODEL), lambda i: (i, 0)), pl.BlockSpec((tm, 1), lambda i: (i, 0)),
                  _const_spec((1, D_MODEL)), _const_spec((D_MODEL, B_CW)), _const_spec((1, Q_LORA)),
                  _const_spec((1, KV_LORA)), _const_spec((Q_LORA, B_HW)), _const_spec((Q_LORA, B_HW)),
                  _const_spec((KV_LORA, B_HW)), _const_spec((B_VW, KV_LORA)),
                  lane_vec, lane_vec, lane_vec, lane_vec, lane_vec, lane_vec],
        out_specs=[out_spec, out_spec, vt_spec],
        out_shape=[out_sds, out_sds, jax.ShapeDtypeStruct((BATCH, B_VW, SEQ), BF16)],
        compiler_params=pltpu.CompilerParams(vmem_limit_bytes=VMEM_LIMIT),
        name="b_proj",
    )(x, pos_col, gain, w_in, qn, kvn, wq, wqs, wk, wvt, qg, qgs, kg, krg, krgs, freq)


B_LOOKAHEAD = 2


def _b_attn_kernel(q_ref, k_ref, vt_ref, o_ref):
    key = lax.broadcasted_iota(jnp.int32, (B_TK, B_TQ), 0)
    query = lax.broadcasted_iota(jnp.int32, (B_TK, B_TQ), 1)
    groups = [slice(parity * LANES, (parity + 1) * LANES) for parity in range(2)]
    ones = jnp.ones((V_DIM, B_TK), BF16)
    n_query_tiles = SEQ // B_TQ

    def n_key_tiles(qt):
        return (qt + 1) * B_TQ // B_TK

    def scores(qt, hd, j):
        return _dot_nt(k_ref[j * B_TK:(j + 1) * B_TK, groups[hd]], q_ref[qt * B_TQ:(qt + 1) * B_TQ, groups[hd]])

    def accumulate(qt, hd, j, s, carry):
        keys = slice(j * B_TK, (j + 1) * B_TK)
        m, acc = carry
        if (j + 1) * B_TK - 1 > qt * B_TQ:
            s = jnp.where(key + (j * B_TK - qt * B_TQ) <= query, s, NEG)
        m_new = jnp.maximum(m, jnp.max(s, axis=0, keepdims=True))
        alpha = jnp.exp(m - m_new)
        e = jnp.exp(s - m_new).astype(BF16)
        vt_ext = jnp.concatenate([vt_ref[hd * V_DIM:(hd + 1) * V_DIM, keys], ones], axis=0)
        return m_new, alpha * acc + _dot(vt_ext, e)

    units = [(qt, hd, j) for j in range(n_key_tiles(n_query_tiles - 1))
             for qt in reversed(range(n_query_tiles)) for hd in range(2) if j < n_key_tiles(qt)]
    init = (jnp.full((1, B_TQ), NEG, F32), jnp.zeros((2 * V_DIM, B_TQ), F32))
    carry = {(qt, hd): init for qt in range(n_query_tiles) for hd in range(2)}
    raw = {u: scores(*u) for u in units[:B_LOOKAHEAD]}
    for i, (qt, hd, j) in enumerate(units):
        if i + B_LOOKAHEAD < len(units):
            ahead = units[i + B_LOOKAHEAD]
            raw[ahead] = scores(*ahead)
        carry[(qt, hd)] = accumulate(qt, hd, j, raw.pop((qt, hd, j)), carry[(qt, hd)])
        if hd == 1 and j == n_key_tiles(qt) - 1:
            outs = [acc[:V_DIM, :] * (1.0 / acc[V_DIM:V_DIM + 1, :])
                    for _, acc in (carry.pop((qt, 0)), carry.pop((qt, 1)))]
            o_ref[qt * B_TQ:(qt + 1) * B_TQ, :] = jnp.concatenate(outs, axis=0).T.astype(BF16)


def _b_attn(q, k, vt):
    pairs = B_HEADS // 2
    return pl.pallas_call(
        _b_attn_kernel,
        grid=(BATCH, pairs),
        in_specs=[pl.BlockSpec((SEQ, 2 * LANES), lambda b, p: (b, p)),
                  pl.BlockSpec((SEQ, 2 * LANES), lambda b, p: (b, p)),
                  pl.BlockSpec((None, 2 * V_DIM, SEQ), lambda b, p: (b, p, 0))],
        out_specs=pl.BlockSpec((SEQ, 2 * V_DIM), lambda b, p: (b, p)),
        out_shape=jax.ShapeDtypeStruct((TOKENS, B_VW), BF16),
        compiler_params=pltpu.CompilerParams(vmem_limit_bytes=VMEM_LIMIT),
        name="b_attn",
    )(q, k, vt)


def _lohi(cols):
    z = jnp.zeros_like(cols)
    return jnp.concatenate([cols, z, z, cols], axis=1)


def _prep_a(w_in, q_gain, k_gain):
    wq = w_in[:, :A_QW]
    k0 = w_in[:, A_QW:A_QW + A_HEAD_DIM]
    k1 = w_in[:, A_QW + A_HEAD_DIM:A_QW + 2 * A_HEAD_DIM]
    v0 = w_in[:, A_QW + 2 * A_HEAD_DIM:A_QW + 3 * A_HEAD_DIM]
    v1 = w_in[:, A_QW + 3 * A_HEAD_DIM:]
    w = jnp.concatenate([wq, _lohi(k0), _lohi(k1), _lohi(v0), _lohi(v1)], axis=1).astype(BF16)
    qg = jnp.concatenate([q_gain, q_gain])[None, :]
    z = jnp.zeros_like(k_gain)
    kg = jnp.concatenate([k_gain, z, z, k_gain, k_gain, z, z, k_gain])[None, :]
    return w, qg, kg


def _head_groups(w, per_head, src_lo, src_hi, dst_lo):
    rows = w.shape[0]
    w3 = w.reshape(rows, B_HEADS, per_head)[:, :, src_lo:src_hi]
    out = jnp.zeros((rows, B_HEADS, LANES), w.dtype)
    out = out.at[:, :, dst_lo:dst_lo + (src_hi - src_lo)].set(w3)
    return out


def _prep_b(w_in, w_uq, w_ukv, q_gain, k_gain):
    half = QK_ROPE // 2
    t1 = slice(QK_NOPE, QK_NOPE + half)
    t2 = slice(QK_NOPE + half, B_DQK)
    rows = w_in.shape[0]
    rope_cols = w_in[:, Q_LORA + KV_LORA:]
    zeros = lambda n: jnp.zeros((rows, n), w_in.dtype)
    rope_group = jnp.concatenate([zeros(QK_NOPE), rope_cols, zeros(LANES - B_DQK)], axis=1)
    partner_group = jnp.concatenate(
        [zeros(QK_NOPE), rope_cols[:, half:], rope_cols[:, :half], zeros(LANES - B_DQK)], axis=1)
    win = jnp.concatenate([w_in[:, :Q_LORA + KV_LORA], rope_group, partner_group], axis=1).astype(BF16)

    wq = _head_groups(w_uq, B_DQK, 0, B_DQK, 0)
    wqs = (_head_groups(w_uq, B_DQK, t2.start, t2.stop, t1.start)
           + _head_groups(w_uq, B_DQK, t1.start, t1.stop, t2.start))
    wk = _head_groups(w_ukv, QK_NOPE + V_DIM, 0, QK_NOPE, 0)
    wvt = w_ukv.reshape(KV_LORA, B_HEADS, QK_NOPE + V_DIM)[:, :, QK_NOPE:].reshape(KV_LORA, B_VW).T.astype(BF16)
    flat = lambda a: a.reshape(a.shape[0], B_HW).astype(BF16)

    def lane_vec(pieces):
        out = jnp.zeros((LANES,), F32)
        for lo, vals in pieces:
            out = out.at[lo:lo + vals.shape[0]].set(vals)
        return out[None, :]

    qg = lane_vec([(0, q_gain)])
    qgs = lane_vec([(t1.start, q_gain[t2]), (t2.start, q_gain[t1])])
    kg = lane_vec([(0, k_gain[:QK_NOPE])])
    krg = lane_vec([(QK_NOPE, k_gain[QK_NOPE:])])
    krgs = lane_vec([(t1.start, k_gain[t2]), (t2.start, k_gain[t1])])
    return win, flat(wq), flat(wqs), flat(wk), wvt, qg, qgs, kg, krg, krgs


def _rope_freq():
    inv = ROPE_BASE ** (-np.arange(0, QK_ROPE, 2, dtype=np.float32) / QK_ROPE)
    out = np.zeros((1, LANES), np.float32)
    half = QK_ROPE // 2
    out[0, QK_NOPE:QK_NOPE + half] = inv
    out[0, QK_NOPE + half:B_DQK] = inv
    return jnp.asarray(out)


def kernel(x, positions, rel_bias, ffn_norm1, ffn1_wg, ffn1_wu, ffn1_wd, mix_norm, ffn_norm2, ffn2_wg,
           ffn2_wu, ffn2_wd, a_w_in, a_q_gain, a_k_gain, a_sinks, a_w_out, b_w_in, b_q_norm, b_kv_norm,
           b_w_uq, b_w_ukv, b_q_gain, b_k_gain, b_w_out):
    assert x.shape == (BATCH, SEQ, D_MODEL) and positions.shape == (BATCH, SEQ)
    xt = x.reshape(TOKENS, D_MODEL)
    pos_col = positions.reshape(TOKENS, 1)
    pos_row = positions.reshape(TOKENS // BLOCK, 1, BLOCK)
    table = jnp.zeros((A_HEADS, LANES), F32).at[:, :NUM_BUCKETS].set(rel_bias.T)
    bf = lambda w: w.astype(BF16)
    row = lambda v: v[None, :]
    ffn1 = (ffn_norm1[:, None, :], bf(ffn1_wg), bf(ffn1_wu), bf(ffn1_wd))
    ffn2 = (ffn_norm2[:, None, :], bf(ffn2_wg), bf(ffn2_wu), bf(ffn2_wd))

    for i in range(DEPTH):
        xt = _ffn(xt, i, *ffn1)
        j = i // N_MIXERS
        if i % N_MIXERS == 0:
            w, qg, kg = _prep_a(a_w_in[j], a_q_gain[j], a_k_gain[j])
            q, k4, v4 = _a_qkv(xt, row(mix_norm[i]), w, qg, kg)
            attn = _a_attn(q, k4, v4, pos_col, pos_row, table, a_sinks[j])
            w_out = bf(a_w_out[j])
        else:
            prep = _prep_b(b_w_in[j], b_w_uq[j], b_w_ukv[j], b_q_gain[j], b_k_gain[j])
            win, wq, wqs, wk, wvt, qg, qgs, kg, krg, krgs = prep
            q, k, vt = _b_proj(xt, pos_col, row(mix_norm[i]), win, row(b_q_norm[j]), row(b_kv_norm[j]),
                              wq, wqs, wk, wvt, qg, qgs, kg, krg, krgs, _rope_freq())
            attn = _b_attn(q, k, vt)
            w_out = bf(b_w_out[j])
        xt = _ffn(xt, i, *ffn2, attn, w_out)
    return xt.reshape(BATCH, SEQ, D_MODEL)
```

```python
import functools
import math

import numpy as np
import jax
import jax.numpy as jnp
from jax import lax
from jax.experimental import pallas as pl
from jax.experimental.pallas import tpu as pltpu

D_MODEL = 1024
BATCH = 8
SEQ = 2048
DEPTH = 2
N_MIXERS = 2
A_HEADS = 16
A_KV_HEADS = 2
A_HEAD_DIM = 64
WINDOW = 128
BLOCK = 128
NUM_BUCKETS = 32
MAX_DISTANCE = 128
B_HEADS = 16
Q_LORA = 256
KV_LORA = 128
QK_NOPE = 64
QK_ROPE = 32
V_DIM = 64
ROPE_BASE = 10000.0
D_FF = 2816
EPS = 1e-6
NEG = -1e30

TOKENS = BATCH * SEQ
LANES = 128
HALF = LANES // 2
B_DQK = QK_NOPE + QK_ROPE
VMEM_LIMIT = 56 * 1024 * 1024

LOG2E = math.log2(math.e)

F32 = jnp.float32
BF16 = jnp.bfloat16


def _rms_rows(x, gain):
    return x * lax.rsqrt(jnp.mean(x * x, axis=-1, keepdims=True) + EPS) * gain


def _dot(a, b):
    return jnp.dot(a, b, preferred_element_type=F32)


def _dot_nt(a, b):
    return lax.dot_general(a, b, (((1,), (1,)), ((), ())), preferred_element_type=F32)


def _const_spec(shape):
    nd = len(shape)
    return pl.BlockSpec(shape, lambda *_: (0,) * nd, pipeline_mode=pl.Buffered(1))


FFN_TM = 512


def _ffn_body(x, g_ref, wg_ref, wu_ref, wd_ref, o_ref):
    h = _rms_rows(x, g_ref[...]).astype(BF16)
    gate = _dot(h, wg_ref[...])
    up = _dot(h, wu_ref[...])
    act = (gate * jax.nn.sigmoid(gate) * up).astype(BF16)
    o_ref[...] = x + 0.5 * _dot(act, wd_ref[...])


def _ffn_kernel(x_ref, g_ref, wg_ref, wu_ref, wd_ref, o_ref):
    _ffn_body(x_ref[...], g_ref, wg_ref, wu_ref, wd_ref, o_ref)


def _proj_ffn_kernel(x_ref, a_ref, wo_ref, g_ref, wg_ref, wu_ref, wd_ref, o_ref):
    x = x_ref[...] + _dot(a_ref[...], wo_ref[...])
    _ffn_body(x, g_ref, wg_ref, wu_ref, wd_ref, o_ref)


def _layer_spec(layer, shape):
    nd = len(shape)
    return pl.BlockSpec((None,) + shape, lambda *_: (layer,) + (0,) * nd, pipeline_mode=pl.Buffered(1))


def _ffn(x, layer, gain, wg, wu, wd, attn=None, w_out=None):
    tm = FFN_TM
    row_spec = pl.BlockSpec((tm, D_MODEL), lambda i: (i, 0))
    w_specs = [_layer_spec(layer, (1, D_MODEL)), _layer_spec(layer, (D_MODEL, D_FF)),
               _layer_spec(layer, (D_MODEL, D_FF)), _layer_spec(layer, (D_FF, D_MODEL))]
    if attn is None:
        kern, in_specs, args = _ffn_kernel, [row_spec] + w_specs, (x, gain, wg, wu, wd)
    else:
        kern = _proj_ffn_kernel
        in_specs = [row_spec, row_spec, _const_spec((D_MODEL, D_MODEL))] + w_specs
        args = (x, attn, w_out, gain, wg, wu, wd)
    return pl.pallas_call(
        kern,
        grid=(TOKENS // tm,),
        in_specs=in_specs,
        out_specs=row_spec,
        out_shape=jax.ShapeDtypeStruct((TOKENS, D_MODEL), F32),
        compiler_params=pltpu.CompilerParams(vmem_limit_bytes=VMEM_LIMIT),
        name="ffn" if attn is None else "proj_ffn",
    )(*args)


A_TM = 512
A_QW = A_HEADS * A_HEAD_DIM
A_KW = 4 * LANES
A_PAIRS = A_HEADS // 2
A_ROWS = 256


def _a_qkv_kernel(x_ref, g_ref, w_ref, qg_ref, kg_ref, q_ref, k_ref, v_ref):
    h = _rms_rows(x_ref[...], g_ref[...]).astype(BF16)
    qkv = _dot(h, w_ref[...])
    lane = lax.broadcasted_iota(jnp.int32, (1, LANES), 1)
    low = lane < HALF
    scale = A_HEAD_DIM ** -0.5 * LOG2E
    for p in range(A_PAIRS):
        qp = qkv[:, p * LANES:(p + 1) * LANES]
        sq = qp * qp
        ms_lo = jnp.sum(jnp.where(low, sq, 0.0), axis=-1, keepdims=True) * (1.0 / A_HEAD_DIM)
        ms_hi = jnp.sum(jnp.where(low, 0.0, sq), axis=-1, keepdims=True) * (1.0 / A_HEAD_DIM)
        r = jnp.where(low, lax.rsqrt(ms_lo + EPS), lax.rsqrt(ms_hi + EPS))
        q_ref[:, p * LANES:(p + 1) * LANES] = (qp * r * qg_ref[...] * scale).astype(BF16)
    for j in range(A_KW // LANES):
        kj = qkv[:, A_QW + j * LANES:A_QW + (j + 1) * LANES]
        ms = jnp.sum(kj * kj, axis=-1, keepdims=True) * (1.0 / A_HEAD_DIM)
        k_ref[:, j * LANES:(j + 1) * LANES] = (
            kj * lax.rsqrt(ms + EPS) * kg_ref[:, j * LANES:(j + 1) * LANES]).astype(BF16)
    v_ref[...] = qkv[:, A_QW + A_KW:].astype(BF16)


def _a_qkv(x, gain, w, qg, kg):
    tm = A_TM
    width = A_QW + 2 * A_KW
    return pl.pallas_call(
        _a_qkv_kernel,
        grid=(TOKENS // tm,),
        in_specs=[pl.BlockSpec((tm, D_MODEL), lambda i: (i, 0)), _const_spec((1, D_MODEL)),
                  _const_spec((D_MODEL, width)), _const_spec((1, LANES)), _const_spec((1, A_KW))],
        out_specs=[pl.BlockSpec((tm, A_QW), lambda i: (i, 0)), pl.BlockSpec((tm, A_KW), lambda i: (i, 0)),
                   pl.BlockSpec((tm, A_KW), lambda i: (i, 0))],
        out_shape=[jax.ShapeDtypeStruct((TOKENS, A_QW), BF16), jax.ShapeDtypeStruct((TOKENS, A_KW), BF16),
                   jax.ShapeDtypeStruct((TOKENS, A_KW), BF16)],
        compiler_params=pltpu.CompilerParams(vmem_limit_bytes=VMEM_LIMIT),
        name="a_qkv",
    )(x, gain, w, qg, kg)


def _t5_bucket(dist):
    n = jnp.maximum(dist, 0)
    max_exact = NUM_BUCKETS // 2
    large = max_exact + (jnp.log(jnp.maximum(n, 1).astype(F32) / max_exact)
                         / math.log(MAX_DISTANCE / max_exact)
                         * (NUM_BUCKETS - max_exact)).astype(jnp.int32)
    large = jnp.minimum(large, NUM_BUCKETS - 1)
    return jnp.where(n < max_exact, n, large)


SUBLANES = 8


def _a_attn_kernel(sink_ref, q_ref, kc_ref, kp_ref, vc_ref, vp_ref, pq_ref, pkc_ref, pkp_ref, tbl_ref, o_ref,
                   bias_ref, shared_ref):
    row = lax.broadcasted_iota(jnp.int32, (BLOCK, BLOCK), 0)
    col = lax.broadcasted_iota(jnp.int32, (BLOCK, BLOCK), 1)
    cur_ok = col <= row
    first_in_seq = pl.program_id(1) == 0
    no_prev = jnp.where(first_in_seq, NEG, 0.0)
    tables = [jnp.broadcast_to(tbl_ref[h:h + 1, :] * LOG2E, (SUBLANES, LANES)) for h in range(A_HEADS)]
    ones = jnp.ones((2 * BLOCK, LANES), BF16)
    group_heads = A_HEADS // A_KV_HEADS
    lane = lax.broadcasted_iota(jnp.int32, (1, LANES), 1)

    @pl.when(jnp.logical_and(pl.program_id(0) == 0, first_in_seq))
    def _():
        shared_ref[0] = 0

    def fill_shift_invariant():
        back = jnp.broadcast_to((BLOCK - lane) & (BLOCK - 1), (SUBLANES, LANES))
        idx = _t5_bucket(back)
        for h in range(A_HEADS):
            base = jnp.take_along_axis(tables[h], idx, axis=1)
            base = jnp.broadcast_to(base[0:1, :], (BLOCK, BLOCK))
            tile = pltpu.roll(base, 0, 1, stride=1, stride_axis=0)
            bias_ref[h] = jnp.where(cur_ok, tile, NEG)
            bias_ref[A_HEADS + h] = jnp.where(cur_ok, NEG, tile)

    def fill_general(pos_q, pos_cur, pos_prev):
        bucket = jnp.where(cur_ok, _t5_bucket(pos_q - pos_cur), _t5_bucket(pos_q - pos_prev))
        for c in range(BLOCK // SUBLANES):
            chunk = slice(c * SUBLANES, (c + 1) * SUBLANES)
            for h in range(A_HEADS):
                piece = jnp.take_along_axis(tables[h], bucket[chunk, :], axis=1)
                bias_ref[h, chunk, :] = jnp.where(cur_ok[chunk, :], piece, NEG)
                bias_ref[A_HEADS + h, chunk, :] = jnp.where(cur_ok[chunk, :], NEG, piece)

    blocks = range(A_ROWS // BLOCK)

    def rows_of(r):
        return slice(r * BLOCK, (r + 1) * BLOCK)

    def pairs_of(kv):
        return range(kv * group_heads // 2, (kv + 1) * group_heads // 2)

    def band(r):
        if r == 0:
            k_prev, v_prev, pos_prev = kp_ref[...], vp_ref[...], pkp_ref[0]
        else:
            k_prev, v_prev, pos_prev = kc_ref[rows_of(r - 1), :], vc_ref[rows_of(r - 1), :], pkc_ref[r - 1]
        k_band = jnp.concatenate([k_prev, kc_ref[rows_of(r), :]], axis=0)
        v_band = jnp.concatenate([v_prev, vc_ref[rows_of(r), :]], axis=0)
        return k_band, v_band, pos_prev

    def scores(r, kv, parity, bands):
        group = slice((2 * kv + parity) * LANES, (2 * kv + parity + 1) * LANES)
        q_stack = jnp.concatenate([q_ref[rows_of(r), p * LANES:(p + 1) * LANES] for p in pairs_of(kv)], axis=0)
        return _dot_nt(q_stack, bands[r][0][:, group])

    def finish(r, kv, parity, s_all, bias, bands):
        v_band = bands[r][1]
        group = slice((2 * kv + parity) * LANES, (2 * kv + parity + 1) * LANES)
        probs, sink_terms = [], []
        for i, p in enumerate(pairs_of(kv)):
            head = 2 * p + parity
            s = s_all[i * BLOCK:(i + 1) * BLOCK, :]
            s_prev = s[:, :BLOCK] + bias[A_HEADS + head]
            if r == 0:
                s_prev = s_prev + no_prev
            s_cur = s[:, BLOCK:] + bias[head]
            sink = sink_ref[head] * LOG2E
            m = jnp.maximum(jnp.max(jnp.maximum(s_prev, s_cur), axis=-1, keepdims=True), sink)
            probs.append(jnp.concatenate([jnp.exp2(s_prev - m), jnp.exp2(s_cur - m)], axis=1).astype(BF16))
            sink_terms.append(jnp.exp2(sink - m))
        v_ext = jnp.concatenate([v_band[:, group], ones], axis=1)
        out = _dot(jnp.concatenate(probs, axis=0), v_ext)
        scaled = []
        for i in range(len(probs)):
            o_i = out[i * BLOCK:(i + 1) * BLOCK, :]
            scaled.append(o_i[:, :LANES] * (1.0 / (o_i[:, LANES:] + sink_terms[i])))
        return scaled

    def attend(which_blocks):
        bias = [bias_ref[h] for h in range(2 * A_HEADS)]
        bands = {r: band(r) for r in which_blocks}
        units = [(r, kv, parity) for r in which_blocks for kv in range(A_KV_HEADS) for parity in range(2)]
        raw = {units[0]: scores(*units[0], bands)}
        done = {}
        for i, unit in enumerate(units):
            if i + 1 < len(units):
                raw[units[i + 1]] = scores(*units[i + 1], bands)
            done[unit] = finish(*unit, raw.pop(unit), bias, bands)
            r, kv, parity = unit
            if parity == 1:
                low, high = done.pop((r, kv, 0)), done.pop((r, kv, 1))
                for i_pair, p in enumerate(pairs_of(kv)):
                    o_ref[rows_of(r), p * LANES:(p + 1) * LANES] = (low[i_pair] + high[i_pair]).astype(BF16)

    @pl.when(shared_ref[0] == 0)
    def _():
        fill_shift_invariant()
        shared_ref[0] = 1

    attend(blocks)

    off = jnp.zeros((1, LANES), jnp.int32)
    for r in blocks:
        pos_cur, pos_prev = pkc_ref[r], band(r)[2]
        start = jnp.min(pos_cur, axis=-1, keepdims=True)
        off_prev = jnp.where(pos_prev - start == lane - BLOCK, 0, 1)
        if r == 0:
            off_prev = off_prev * jnp.where(first_in_seq, 0, 1)
        off = off + jnp.where(pos_cur - start == lane, 0, 1) + off_prev

    @pl.when(jnp.sum(off) != 0)
    def _():
        for r in blocks:
            fill_general(pq_ref[rows_of(r), :], pkc_ref[r], band(r)[2])
            attend([r])
        shared_ref[0] = 0


def _a_attn(q, k4, v4, pos_col, pos_row, table, sinks):
    steps = SEQ // A_ROWS
    blocks_per_step = A_ROWS // BLOCK
    blocks_per_seq = SEQ // BLOCK

    def cur(b, i):
        return (b * steps + i, 0)

    def prev(b, i):
        return (b * blocks_per_seq + jnp.maximum(i * blocks_per_step - 1, 0), 0)

    def cur3(b, i):
        return (b * steps + i, 0, 0)

    def prev3(b, i):
        return (b * blocks_per_seq + jnp.maximum(i * blocks_per_step - 1, 0), 0, 0)

    return pl.pallas_call(
        _a_attn_kernel,
        grid=(BATCH, steps),
        in_specs=[
            pl.BlockSpec(memory_space=pltpu.SMEM),
            pl.BlockSpec((A_ROWS, A_QW), cur),
            pl.BlockSpec((A_ROWS, A_KW), cur),
            pl.BlockSpec((BLOCK, A_KW), prev),
            pl.BlockSpec((A_ROWS, A_KW), cur),
            pl.BlockSpec((BLOCK, A_KW), prev),
            pl.BlockSpec((A_ROWS, 1), cur),
            pl.BlockSpec((blocks_per_step, 1, BLOCK), cur3),
            pl.BlockSpec((1, 1, BLOCK), prev3),
            _const_spec((A_HEADS, LANES)),
        ],
        out_specs=pl.BlockSpec((A_ROWS, A_QW), cur),
        out_shape=jax.ShapeDtypeStruct((TOKENS, A_QW), BF16),
        scratch_shapes=[pltpu.VMEM((2 * A_HEADS, BLOCK, BLOCK), F32), pltpu.SMEM((1,), jnp.int32)],
        compiler_params=pltpu.CompilerParams(dimension_semantics=("arbitrary", "arbitrary"),
                                             vmem_limit_bytes=VMEM_LIMIT),
        name="a_attn",
    )(sinks, q, k4, k4, v4, v4, pos_col, pos_row, pos_row, table)


B_TM = 256
B_CW = Q_LORA + KV_LORA + 2 * LANES
B_HW = B_HEADS * LANES
B_VW = B_HEADS * V_DIM
B_TQ = 512
B_TK = 512


def _b_proj_kernel(x_ref, pos_ref, g_ref, win_ref, qn_ref, kvn_ref, wq_ref, wqs_ref, wk_ref, wvt_ref,
                   qg_ref, qgs_ref, kg_ref, krg_ref, krgs_ref, freq_ref, q_ref, k_ref, vt_ref):
    h = _rms_rows(x_ref[...], g_ref[...]).astype(BF16)
    c = _dot(h, win_ref[...])
    cq = _rms_rows(c[:, :Q_LORA], qn_ref[...]).astype(BF16)
    ckv_f32 = _rms_rows(c[:, Q_LORA:Q_LORA + KV_LORA], kvn_ref[...])
    ckv = ckv_f32.astype(BF16)
    kr = c[:, Q_LORA + KV_LORA:Q_LORA + KV_LORA + LANES]
    kr_partner = c[:, Q_LORA + KV_LORA + LANES:]
    q = _dot(cq, wq_ref[...])
    q_partner = _dot(cq, wqs_ref[...])
    kn = _dot(ckv, wk_ref[...])
    vt_ref[...] = _dot(wvt_ref[...], ckv_f32.T.astype(BF16)).astype(BF16)

    lane = lax.broadcasted_iota(jnp.int32, (1, LANES), 1)
    ang = pos_ref[...].astype(F32) * freq_ref[...]
    cos = jnp.cos(ang)
    sin = jnp.sin(ang)
    half = QK_ROPE // 2
    sin_signed = jnp.where(lane < QK_NOPE + half, -sin, sin)

    k_rope = kr * krg_ref[...] * cos + kr_partner * krgs_ref[...] * sin_signed
    ss_rope = jnp.sum(kr * kr, axis=-1, keepdims=True)
    scale = B_DQK ** -0.5 * LOG2E
    q_cos = qg_ref[...] * cos
    q_sin = qgs_ref[...] * sin_signed
    for hd in range(B_HEADS):
        g = slice(hd * LANES, (hd + 1) * LANES)
        qh = q[:, g]
        r = lax.rsqrt(jnp.sum(qh * qh, axis=-1, keepdims=True) * (1.0 / B_DQK) + EPS)
        q_rot = qh * q_cos + q_partner[:, g] * q_sin
        q_ref[:, g] = (q_rot * (r * scale)).astype(BF16)
        kh = kn[:, g]
        rk = lax.rsqrt((jnp.sum(kh * kh, axis=-1, keepdims=True) + ss_rope) * (1.0 / B_DQK) + EPS)
        k_ref[:, g] = ((kh * kg_ref[...] + k_rope) * rk).astype(BF16)


def _b_proj(x, pos_col, gain, w_in, qn, kvn, wq, wqs, wk, wvt, qg, qgs, kg, krg, krgs, freq):
    tm = B_TM
    tiles_per_seq = SEQ // tm
    lane_vec = _const_spec((1, LANES))
    out_spec = pl.BlockSpec((tm, B_HW), lambda i: (i, 0))
    out_sds = jax.ShapeDtypeStruct((TOKENS, B_HW), BF16)
    vt_spec = pl.BlockSpec((None, B_VW, tm), lambda i: (i // tiles_per_seq, 0, i % tiles_per_seq))
    return pl.pallas_call(
        _b_proj_kernel,
        grid=(TOKENS // tm,),
        in_specs=[pl.BlockSpec((tm, D_MODEL), lambda i: (i, 0)), pl.BlockSpec((tm, 1), lambda i: (i, 0)),
                  _const_spec((1, D_MODEL)), _const_spec((D_MODEL, B_CW)), _const_spec((1, Q_LORA)),
                  _const_spec((1, KV_LORA)), _const_spec((Q_LORA, B_HW)), _const_spec((Q_LORA, B_HW)),
                  _const_spec((KV_LORA, B_HW)), _const_spec((B_VW, KV_LORA)),
                  lane_vec, lane_vec, lane_vec, lane_vec, lane_vec, lane_vec],
        out_specs=[out_spec, out_spec, vt_spec],
        out_shape=[out_sds, out_sds, jax.ShapeDtypeStruct((BATCH, B_VW, SEQ), BF16)],
        compiler_params=pltpu.CompilerParams(vmem_limit_bytes=VMEM_LIMIT),
        name="b_proj",
    )(x, pos_col, gain, w_in, qn, kvn, wq, wqs, wk, wvt, qg, qgs, kg, krg, krgs, freq)


B_LOOKAHEAD = 2


def _b_attn_kernel(q_ref, k_ref, vt_ref, o_ref):
    key = lax.broadcasted_iota(jnp.int32, (B_TK, B_TQ), 0)
    query = lax.broadcasted_iota(jnp.int32, (B_TK, B_TQ), 1)
    groups = [slice(parity * LANES, (parity + 1) * LANES) for parity in range(2)]
    ones = jnp.ones((V_DIM, B_TK), BF16)
    n_query_tiles = SEQ // B_TQ

    def n_key_tiles(qt):
        return (qt + 1) * B_TQ // B_TK

    def scores(qt, hd, j):
        return _dot_nt(k_ref[j * B_TK:(j + 1) * B_TK, groups[hd]], q_ref[qt * B_TQ:(qt + 1) * B_TQ, groups[hd]])

    def accumulate(qt, hd, j, s, carry):
        keys = slice(j * B_TK, (j + 1) * B_TK)
        m, acc = carry
        if (j + 1) * B_TK - 1 > qt * B_TQ:
            s = jnp.where(key + (j * B_TK - qt * B_TQ) <= query, s, NEG)
        m_new = jnp.maximum(m, jnp.max(s, axis=0, keepdims=True))
        alpha = jnp.exp2(m - m_new)
        e = jnp.exp2(s - m_new).astype(BF16)
        vt_ext = jnp.concatenate([vt_ref[hd * V_DIM:(hd + 1) * V_DIM, keys], ones], axis=0)
        return m_new, alpha * acc + _dot(vt_ext, e)

    units = [(qt, hd, j) for j in range(n_key_tiles(n_query_tiles - 1))
             for qt in reversed(range(n_query_tiles)) for hd in range(2) if j < n_key_tiles(qt)]
    init = (jnp.full((1, B_TQ), NEG, F32), jnp.zeros((2 * V_DIM, B_TQ), F32))
    carry = {(qt, hd): init for qt in range(n_query_tiles) for hd in range(2)}
    raw = {u: scores(*u) for u in units[:B_LOOKAHEAD]}
    for i, (qt, hd, j) in enumerate(units):
        if i + B_LOOKAHEAD < len(units):
            ahead = units[i + B_LOOKAHEAD]
            raw[ahead] = scores(*ahead)
        carry[(qt, hd)] = accumulate(qt, hd, j, raw.pop((qt, hd, j)), carry[(qt, hd)])
        if hd == 1 and j == n_key_tiles(qt) - 1:
            outs = [acc[:V_DIM, :] * (1.0 / acc[V_DIM:V_DIM + 1, :])
                    for _, acc in (carry.pop((qt, 0)), carry.pop((qt, 1)))]
            o_ref[qt * B_TQ:(qt + 1) * B_TQ, :] = jnp.concatenate(outs, axis=0).T.astype(BF16)


def _b_attn(q, k, vt):
    pairs = B_HEADS // 2
    return pl.pallas_call(
        _b_attn_kernel,
        grid=(BATCH, pairs),
        in_specs=[pl.BlockSpec((SEQ, 2 * LANES), lambda b, p: (b, p)),
                  pl.BlockSpec((SEQ, 2 * LANES), lambda b, p: (b, p)),
                  pl.BlockSpec((None, 2 * V_DIM, SEQ), lambda b, p: (b, p, 0))],
        out_specs=pl.BlockSpec((SEQ, 2 * V_DIM), lambda b, p: (b, p)),
        out_shape=jax.ShapeDtypeStruct((TOKENS, B_VW), BF16),
        compiler_params=pltpu.CompilerParams(vmem_limit_bytes=VMEM_LIMIT),
        name="b_attn",
    )(q, k, vt)


def _lohi(cols):
    z = jnp.zeros_like(cols)
    return jnp.concatenate([cols, z, z, cols], axis=1)


def _prep_a(w_in, q_gain, k_gain):
    wq = w_in[:, :A_QW]
    k0 = w_in[:, A_QW:A_QW + A_HEAD_DIM]
    k1 = w_in[:, A_QW + A_HEAD_DIM:A_QW + 2 * A_HEAD_DIM]
    v0 = w_in[:, A_QW + 2 * A_HEAD_DIM:A_QW + 3 * A_HEAD_DIM]
    v1 = w_in[:, A_QW + 3 * A_HEAD_DIM:]
    w = jnp.concatenate([wq, _lohi(k0), _lohi(k1), _lohi(v0), _lohi(v1)], axis=1).astype(BF16)
    qg = jnp.concatenate([q_gain, q_gain])[None, :]
    z = jnp.zeros_like(k_gain)
    kg = jnp.concatenate([k_gain, z, z, k_gain, k_gain, z, z, k_gain])[None, :]
    return w, qg, kg


def _head_groups(w, per_head, src_lo, src_hi, dst_lo):
    rows = w.shape[0]
    w3 = w.reshape(rows, B_HEADS, per_head)[:, :, src_lo:src_hi]
    out = jnp.zeros((rows, B_HEADS, LANES), w.dtype)
    out = out.at[:, :, dst_lo:dst_lo + (src_hi - src_lo)].set(w3)
    return out


def _prep_b(w_in, w_uq, w_ukv, q_gain, k_gain):
    half = QK_ROPE // 2
    t1 = slice(QK_NOPE, QK_NOPE + half)
    t2 = slice(QK_NOPE + half, B_DQK)
    rows = w_in.shape[0]
    rope_cols = w_in[:, Q_LORA + KV_LORA:]
    zeros = lambda n: jnp.zeros((rows, n), w_in.dtype)
    rope_group = jnp.concatenate([zeros(QK_NOPE), rope_cols, zeros(LANES - B_DQK)], axis=1)
    partner_group = jnp.concatenate(
        [zeros(QK_NOPE), rope_cols[:, half:], rope_cols[:, :half], zeros(LANES - B_DQK)], axis=1)
    win = jnp.concatenate([w_in[:, :Q_LORA + KV_LORA], rope_group, partner_group], axis=1).astype(BF16)

    wq = _head_groups(w_uq, B_DQK, 0, B_DQK, 0)
    wqs = (_head_groups(w_uq, B_DQK, t2.start, t2.stop, t1.start)
           + _head_groups(w_uq, B_DQK, t1.start, t1.stop, t2.start))
    wk = _head_groups(w_ukv, QK_NOPE + V_DIM, 0, QK_NOPE, 0)
    wvt = w_ukv.reshape(KV_LORA, B_HEADS, QK_NOPE + V_DIM)[:, :, QK_NOPE:].reshape(KV_LORA, B_VW).T.astype(BF16)
    flat = lambda a: a.reshape(a.shape[0], B_HW).astype(BF16)

    def lane_vec(pieces):
        out = jnp.zeros((LANES,), F32)
        for lo, vals in pieces:
            out = out.at[lo:lo + vals.shape[0]].set(vals)
        return out[None, :]

    qg = lane_vec([(0, q_gain)])
    qgs = lane_vec([(t1.start, q_gain[t2]), (t2.start, q_gain[t1])])
    kg = lane_vec([(0, k_gain[:QK_NOPE])])
    krg = lane_vec([(QK_NOPE, k_gain[QK_NOPE:])])
    krgs = lane_vec([(t1.start, k_gain[t2]), (t2.start, k_gain[t1])])
    return win, flat(wq), flat(wqs), flat(wk), wvt, qg, qgs, kg, krg, krgs


def _rope_freq():
    inv = ROPE_BASE ** (-np.arange(0, QK_ROPE, 2, dtype=np.float32) / QK_ROPE)
    out = np.zeros((1, LANES), np.float32)
    half = QK_ROPE // 2
    out[0, QK_NOPE:QK_NOPE + half] = inv
    out[0, QK_NOPE + half:B_DQK] = inv
    return jnp.asarray(out)


def kernel(x, positions, rel_bias, ffn_norm1, ffn1_wg, ffn1_wu, ffn1_wd, mix_norm, ffn_norm2, ffn2_wg,
           ffn2_wu, ffn2_wd, a_w_in, a_q_gain, a_k_gain, a_sinks, a_w_out, b_w_in, b_q_norm, b_kv_norm,
           b_w_uq, b_w_ukv, b_q_gain, b_k_gain, b_w_out):
    assert x.shape == (BATCH, SEQ, D_MODEL) and positions.shape == (BATCH, SEQ)
    xt = x.reshape(TOKENS, D_MODEL)
    pos_col = positions.reshape(TOKENS, 1)
    pos_row = positions.reshape(TOKENS // BLOCK, 1, BLOCK)
    table = jnp.zeros((A_HEADS, LANES), F32).at[:, :NUM_BUCKETS].set(rel_bias.T)
    bf = lambda w: w.astype(BF16)
    row = lambda v: v[None, :]
    ffn1 = (ffn_norm1[:, None, :], bf(ffn1_wg), bf(ffn1_wu), bf(ffn1_wd))
    ffn2 = (ffn_norm2[:, None, :], bf(ffn2_wg), bf(ffn2_wu), bf(ffn2_wd))

    for i in range(DEPTH):
        xt = _ffn(xt, i, *ffn1)
        j = i // N_MIXERS
        if i % N_MIXERS == 0:
            w, qg, kg = _prep_a(a_w_in[j], a_q_gain[j], a_k_gain[j])
            q, k4, v4 = _a_qkv(xt, row(mix_norm[i]), w, qg, kg)
            attn = _a_attn(q, k4, v4, pos_col, pos_row, table, a_sinks[j])
            w_out = bf(a_w_out[j])
        else:
            prep = _prep_b(b_w_in[j], b_w_uq[j], b_w_ukv[j], b_q_gain[j], b_k_gain[j])
            win, wq, wqs, wk, wvt, qg, qgs, kg, krg, krgs = prep
            q, k, vt = _b_proj(xt, pos_col, row(mix_norm[i]), win, row(b_q_norm[j]), row(b_kv_norm[j]),
                              wq, wqs, wk, wvt, qg, qgs, kg, krg, krgs, _rope_freq())
            attn = _b_attn(q, k, vt)
            w_out = bf(b_w_out[j])
        xt = _ffn(xt, i, *ffn2, attn, w_out)
    return xt.reshape(BATCH, SEQ, D_MODEL)
```

```python
import functools
import math

import numpy as np
import jax
import jax.numpy as jnp
from jax import lax
from jax.experimental import pallas as pl
from jax.experimental.pallas import tpu as pltpu

D_MODEL = 1024
BATCH = 8
SEQ = 2048
DEPTH = 2
N_MIXERS = 2
A_HEADS = 16
A_KV_HEADS = 2
A_HEAD_DIM = 64
WINDOW = 128
BLOCK = 128
NUM_BUCKETS = 32
MAX_DISTANCE = 128
B_HEADS = 16
Q_LORA = 256
KV_LORA = 128
QK_NOPE = 64
QK_ROPE = 32
V_DIM = 64
ROPE_BASE = 10000.0
D_FF = 2816
EPS = 1e-6
NEG = -1e30

TOKENS = BATCH * SEQ
LANES = 128
HALF = LANES // 2
B_DQK = QK_NOPE + QK_ROPE
VMEM_LIMIT = 56 * 1024 * 1024

LOG2E = math.log2(math.e)

F32 = jnp.float32
BF16 = jnp.bfloat16


def _rms_rows(x, gain):
    return x * lax.rsqrt(jnp.mean(x * x, axis=-1, keepdims=True) + EPS) * gain


def _dot(a, b):
    return jnp.dot(a, b, preferred_element_type=F32)


def _dot_nt(a, b):
    return lax.dot_general(a, b, (((1,), (1,)), ((), ())), preferred_element_type=F32)


def _const_spec(shape):
    nd = len(shape)
    return pl.BlockSpec(shape, lambda *_: (0,) * nd, pipeline_mode=pl.Buffered(1))


FFN_TM = 1024
MXU_WIDTH = 256
FFN_CHUNKS = ((0, 5 * MXU_WIDTH), (5 * MXU_WIDTH, D_FF))


def _ffn_body(x, g_ref, wg_ref, wu_ref, wd_ref, o_ref):
    h = _rms_rows(x, g_ref[...]).astype(BF16)
    down = None
    for lo, hi in FFN_CHUNKS:
        gate = _dot(h, wg_ref[:, lo:hi])
        up = _dot(h, wu_ref[:, lo:hi])
        act = (gate * jax.nn.sigmoid(gate) * up).astype(BF16)
        part = _dot(act, wd_ref[lo:hi, :])
        down = part if down is None else down + part
    o_ref[...] = x + 0.5 * down


def _ffn_kernel(x_ref, g_ref, wg_ref, wu_ref, wd_ref, o_ref):
    _ffn_body(x_ref[...], g_ref, wg_ref, wu_ref, wd_ref, o_ref)


def _proj_ffn_kernel(x_ref, a_ref, wo_ref, g_ref, wg_ref, wu_ref, wd_ref, o_ref):
    x = x_ref[...] + _dot(a_ref[...], wo_ref[...])
    _ffn_body(x, g_ref, wg_ref, wu_ref, wd_ref, o_ref)


def _layer_spec(layer, shape):
    nd = len(shape)
    return pl.BlockSpec((None,) + shape, lambda *_: (layer,) + (0,) * nd, pipeline_mode=pl.Buffered(1))


def _ffn(x, layer, gain, wg, wu, wd, attn=None, w_out=None):
    tm = FFN_TM
    row_spec = pl.BlockSpec((tm, D_MODEL), lambda i: (i, 0))
    w_specs = [_layer_spec(layer, (1, D_MODEL)), _layer_spec(layer, (D_MODEL, D_FF)),
               _layer_spec(layer, (D_MODEL, D_FF)), _layer_spec(layer, (D_FF, D_MODEL))]
    if attn is None:
        kern, in_specs, args = _ffn_kernel, [row_spec] + w_specs, (x, gain, wg, wu, wd)
    else:
        kern = _proj_ffn_kernel
        in_specs = [row_spec, row_spec, _const_spec((D_MODEL, D_MODEL))] + w_specs
        args = (x, attn, w_out, gain, wg, wu, wd)
    return pl.pallas_call(
        kern,
        grid=(TOKENS // tm,),
        in_specs=in_specs,
        out_specs=row_spec,
        out_shape=jax.ShapeDtypeStruct((TOKENS, D_MODEL), F32),
        compiler_params=pltpu.CompilerParams(vmem_limit_bytes=VMEM_LIMIT),
        name="ffn" if attn is None else "proj_ffn",
    )(*args)


A_TM = 512
A_QW = A_HEADS * A_HEAD_DIM
A_KW = 4 * LANES
A_PAIRS = A_HEADS // 2
A_ROWS = 256


def _a_qkv_kernel(x_ref, g_ref, w_ref, qg_ref, kg_ref, q_ref, k_ref, v_ref):
    h = _rms_rows(x_ref[...], g_ref[...]).astype(BF16)
    qkv = _dot(h, w_ref[...])
    lane = lax.broadcasted_iota(jnp.int32, (1, LANES), 1)
    low = lane < HALF
    q_table = qg_ref[...] * (math.sqrt(A_HEAD_DIM) * A_HEAD_DIM ** -0.5 * LOG2E)
    for p in range(A_PAIRS):
        qp = qkv[:, p * LANES:(p + 1) * LANES]
        sq = qp * qp
        ss_lo = jnp.sum(jnp.where(low, sq, 0.0), axis=-1, keepdims=True)
        ss_hi = jnp.sum(jnp.where(low, 0.0, sq), axis=-1, keepdims=True)
        r = jnp.where(low, lax.rsqrt(ss_lo + A_HEAD_DIM * EPS), lax.rsqrt(ss_hi + A_HEAD_DIM * EPS))
        q_ref[:, p * LANES:(p + 1) * LANES] = (qp * q_table * r).astype(BF16)
    for j in range(A_KW // LANES):
        kj = qkv[:, A_QW + j * LANES:A_QW + (j + 1) * LANES]
        ms = jnp.sum(kj * kj, axis=-1, keepdims=True) * (1.0 / A_HEAD_DIM)
        k_ref[:, j * LANES:(j + 1) * LANES] = (
            kj * lax.rsqrt(ms + EPS) * kg_ref[:, j * LANES:(j + 1) * LANES]).astype(BF16)
    v_ref[...] = qkv[:, A_QW + A_KW:].astype(BF16)


def _a_qkv(x, gain, w, qg, kg):
    tm = A_TM
    width = A_QW + 2 * A_KW
    return pl.pallas_call(
        _a_qkv_kernel,
        grid=(TOKENS // tm,),
        in_specs=[pl.BlockSpec((tm, D_MODEL), lambda i: (i, 0)), _const_spec((1, D_MODEL)),
                  _const_spec((D_MODEL, width)), _const_spec((1, LANES)), _const_spec((1, A_KW))],
        out_specs=[pl.BlockSpec((tm, A_QW), lambda i: (i, 0)), pl.BlockSpec((tm, A_KW), lambda i: (i, 0)),
                   pl.BlockSpec((tm, A_KW), lambda i: (i, 0))],
        out_shape=[jax.ShapeDtypeStruct((TOKENS, A_QW), BF16), jax.ShapeDtypeStruct((TOKENS, A_KW), BF16),
                   jax.ShapeDtypeStruct((TOKENS, A_KW), BF16)],
        compiler_params=pltpu.CompilerParams(vmem_limit_bytes=VMEM_LIMIT),
        name="a_qkv",
    )(x, gain, w, qg, kg)


def _t5_bucket(dist):
    n = jnp.maximum(dist, 0)
    max_exact = NUM_BUCKETS // 2
    large = max_exact + (jnp.log(jnp.maximum(n, 1).astype(F32) / max_exact)
                         / math.log(MAX_DISTANCE / max_exact)
                         * (NUM_BUCKETS - max_exact)).astype(jnp.int32)
    large = jnp.minimum(large, NUM_BUCKETS - 1)
    return jnp.where(n < max_exact, n, large)


SUBLANES = 8


def _a_attn_kernel(sink_ref, q_ref, kc_ref, kp_ref, vc_ref, vp_ref, pq_ref, pkc_ref, pkp_ref, tbl_ref, o_ref,
                   bias_ref, shared_ref):
    row = lax.broadcasted_iota(jnp.int32, (BLOCK, BLOCK), 0)
    col = lax.broadcasted_iota(jnp.int32, (BLOCK, BLOCK), 1)
    cur_ok = col <= row
    first_in_seq = pl.program_id(1) == 0
    no_prev = jnp.where(first_in_seq, NEG, 0.0)
    tables = [jnp.broadcast_to(tbl_ref[h:h + 1, :] * LOG2E, (SUBLANES, LANES)) for h in range(A_HEADS)]
    ones = jnp.ones((2 * BLOCK, LANES), BF16)
    group_heads = A_HEADS // A_KV_HEADS
    lane = lax.broadcasted_iota(jnp.int32, (1, LANES), 1)

    @pl.when(jnp.logical_and(pl.program_id(0) == 0, first_in_seq))
    def _():
        shared_ref[0] = 0

    def fill_shift_invariant():
        back = jnp.broadcast_to((BLOCK - lane) & (BLOCK - 1), (SUBLANES, LANES))
        idx = _t5_bucket(back)
        for h in range(A_HEADS):
            base = jnp.take_along_axis(tables[h], idx, axis=1)
            base = jnp.broadcast_to(base[0:1, :], (BLOCK, BLOCK))
            tile = pltpu.roll(base, 0, 1, stride=1, stride_axis=0)
            bias_ref[h] = jnp.where(cur_ok, tile, NEG)
            bias_ref[A_HEADS + h] = jnp.where(cur_ok, NEG, tile)

    def fill_general(pos_q, pos_cur, pos_prev):
        bucket = jnp.where(cur_ok, _t5_bucket(pos_q - pos_cur), _t5_bucket(pos_q - pos_prev))
        for c in range(BLOCK // SUBLANES):
            chunk = slice(c * SUBLANES, (c + 1) * SUBLANES)
            for h in range(A_HEADS):
                piece = jnp.take_along_axis(tables[h], bucket[chunk, :], axis=1)
                bias_ref[h, chunk, :] = jnp.where(cur_ok[chunk, :], piece, NEG)
                bias_ref[A_HEADS + h, chunk, :] = jnp.where(cur_ok[chunk, :], NEG, piece)

    blocks = range(A_ROWS // BLOCK)

    def rows_of(r):
        return slice(r * BLOCK, (r + 1) * BLOCK)

    def pairs_of(kv):
        return range(kv * group_heads // 2, (kv + 1) * group_heads // 2)

    def band(r):
        if r == 0:
            k_prev, v_prev, pos_prev = kp_ref[...], vp_ref[...], pkp_ref[0]
        else:
            k_prev, v_prev, pos_prev = kc_ref[rows_of(r - 1), :], vc_ref[rows_of(r - 1), :], pkc_ref[r - 1]
        k_band = jnp.concatenate([k_prev, kc_ref[rows_of(r), :]], axis=0)
        v_band = jnp.concatenate([v_prev, vc_ref[rows_of(r), :]], axis=0)
        return k_band, v_band, pos_prev

    def scores(r, kv, parity, bands):
        group = slice((2 * kv + parity) * LANES, (2 * kv + parity + 1) * LANES)
        q_stack = jnp.concatenate([q_ref[rows_of(r), p * LANES:(p + 1) * LANES] for p in pairs_of(kv)], axis=0)
        return _dot_nt(q_stack, bands[r][0][:, group])

    def finish(r, kv, parity, s_all, bias, bands):
        v_band = bands[r][1]
        group = slice((2 * kv + parity) * LANES, (2 * kv + parity + 1) * LANES)
        probs, sink_terms = [], []
        for i, p in enumerate(pairs_of(kv)):
            head = 2 * p + parity
            s = s_all[i * BLOCK:(i + 1) * BLOCK, :]
            s_prev = s[:, :BLOCK] + bias[A_HEADS + head]
            if r == 0:
                s_prev = s_prev + no_prev
            s_cur = s[:, BLOCK:] + bias[head]
            sink = sink_ref[head] * LOG2E
            m = jnp.maximum(jnp.max(jnp.maximum(s_prev, s_cur), axis=-1, keepdims=True), sink)
            probs.append(jnp.concatenate([jnp.exp2(s_prev - m), jnp.exp2(s_cur - m)], axis=1).astype(BF16))
            sink_terms.append(jnp.exp2(sink - m))
        v_ext = jnp.concatenate([v_band[:, group], ones], axis=1)
        out = _dot(jnp.concatenate(probs, axis=0), v_ext)
        scaled = []
        for i in range(len(probs)):
            o_i = out[i * BLOCK:(i + 1) * BLOCK, :]
            scaled.append(o_i[:, :LANES] * (1.0 / (o_i[:, LANES:] + sink_terms[i])))
        return scaled

    def attend(which_blocks):
        bias = [bias_ref[h] for h in range(2 * A_HEADS)]
        bands = {r: band(r) for r in which_blocks}
        units = [(r, kv, parity) for r in which_blocks for kv in range(A_KV_HEADS) for parity in range(2)]
        raw = {units[0]: scores(*units[0], bands)}
        done = {}
        for i, unit in enumerate(units):
            if i + 1 < len(units):
                raw[units[i + 1]] = scores(*units[i + 1], bands)
            done[unit] = finish(*unit, raw.pop(unit), bias, bands)
            r, kv, parity = unit
            if parity == 1:
                low, high = done.pop((r, kv, 0)), done.pop((r, kv, 1))
                for i_pair, p in enumerate(pairs_of(kv)):
                    o_ref[rows_of(r), p * LANES:(p + 1) * LANES] = (low[i_pair] + high[i_pair]).astype(BF16)

    @pl.when(shared_ref[0] == 0)
    def _():
        fill_shift_invariant()
        shared_ref[0] = 1

    attend(blocks)

    off = jnp.zeros((1, LANES), jnp.int32)
    for r in blocks:
        pos_cur, pos_prev = pkc_ref[r], band(r)[2]
        start = jnp.min(pos_cur, axis=-1, keepdims=True)
        off_prev = jnp.where(pos_prev - start == lane - BLOCK, 0, 1)
        if r == 0:
            off_prev = off_prev * jnp.where(first_in_seq, 0, 1)
        off = off + jnp.where(pos_cur - start == lane, 0, 1) + off_prev

    @pl.when(jnp.sum(off) != 0)
    def _():
        for r in blocks:
            fill_general(pq_ref[rows_of(r), :], pkc_ref[r], band(r)[2])
            attend([r])
        shared_ref[0] = 0


def _a_attn(q, k4, v4, pos_col, pos_row, table, sinks):
    steps = SEQ // A_ROWS
    blocks_per_step = A_ROWS // BLOCK
    blocks_per_seq = SEQ // BLOCK

    def cur(b, i):
        return (b * steps + i, 0)

    def prev(b, i):
        return (b * blocks_per_seq + jnp.maximum(i * blocks_per_step - 1, 0), 0)

    def cur3(b, i):
        return (b * steps + i, 0, 0)

    def prev3(b, i):
        return (b * blocks_per_seq + jnp.maximum(i * blocks_per_step - 1, 0), 0, 0)

    return pl.pallas_call(
        _a_attn_kernel,
        grid=(BATCH, steps),
        in_specs=[
            pl.BlockSpec(memory_space=pltpu.SMEM),
            pl.BlockSpec((A_ROWS, A_QW), cur),
            pl.BlockSpec((A_ROWS, A_KW), cur),
            pl.BlockSpec((BLOCK, A_KW), prev),
            pl.BlockSpec((A_ROWS, A_KW), cur),
            pl.BlockSpec((BLOCK, A_KW), prev),
            pl.BlockSpec((A_ROWS, 1), cur),
            pl.BlockSpec((blocks_per_step, 1, BLOCK), cur3),
            pl.BlockSpec((1, 1, BLOCK), prev3),
            _const_spec((A_HEADS, LANES)),
        ],
        out_specs=pl.BlockSpec((A_ROWS, A_QW), cur),
        out_shape=jax.ShapeDtypeStruct((TOKENS, A_QW), BF16),
        scratch_shapes=[pltpu.VMEM((2 * A_HEADS, BLOCK, BLOCK), F32), pltpu.SMEM((1,), jnp.int32)],
        compiler_params=pltpu.CompilerParams(dimension_semantics=("arbitrary", "arbitrary"),
                                             vmem_limit_bytes=VMEM_LIMIT),
        name="a_attn",
    )(sinks, q, k4, k4, v4, v4, pos_col, pos_row, pos_row, table)


B_TM = 256
B_CW = Q_LORA + KV_LORA + 2 * LANES
B_HW = B_HEADS * LANES
B_VW = B_HEADS * V_DIM
B_TQ = 512
B_TK = 512


def _b_proj_kernel(x_ref, pos_ref, g_ref, win_ref, qn_ref, kvn_ref, wq_ref, wqs_ref, wk_ref, wvt_ref,
                   qg_ref, qgs_ref, kg_ref, krg_ref, krgs_ref, freq_ref, q_ref, k_ref, vt_ref):
    h = _rms_rows(x_ref[...], g_ref[...]).astype(BF16)
    c = _dot(h, win_ref[...])
    cq = _rms_rows(c[:, :Q_LORA], qn_ref[...]).astype(BF16)
    ckv_f32 = _rms_rows(c[:, Q_LORA:Q_LORA + KV_LORA], kvn_ref[...])
    ckv = ckv_f32.astype(BF16)
    kr = c[:, Q_LORA + KV_LORA:Q_LORA + KV_LORA + LANES]
    kr_partner = c[:, Q_LORA + KV_LORA + LANES:]
    q = _dot(cq, wq_ref[...])
    q_partner = _dot(cq, wqs_ref[...])
    kn = _dot(ckv, wk_ref[...])
    vt_ref[...] = _dot(wvt_ref[...], ckv_f32.T.astype(BF16)).astype(BF16)

    lane = lax.broadcasted_iota(jnp.int32, (1, LANES), 1)
    ang = pos_ref[...].astype(F32) * freq_ref[...]
    cos = jnp.cos(ang)
    sin = jnp.sin(ang)
    half = QK_ROPE // 2
    sin_signed = jnp.where(lane < QK_NOPE + half, -sin, sin)

    root_d = math.sqrt(B_DQK)
    q_const = root_d * B_DQK ** -0.5 * LOG2E
    k_rope = (kr * krg_ref[...] * cos + kr_partner * krgs_ref[...] * sin_signed) * root_d
    k_gain = kg_ref[...] * root_d
    ss_rope = jnp.sum(kr * kr, axis=-1, keepdims=True) + B_DQK * EPS
    q_cos = qg_ref[...] * cos * q_const
    q_sin = qgs_ref[...] * sin_signed * q_const
    for hd in range(B_HEADS):
        g = slice(hd * LANES, (hd + 1) * LANES)
        qh = q[:, g]
        r = lax.rsqrt(jnp.sum(qh * qh, axis=-1, keepdims=True) + B_DQK * EPS)
        q_ref[:, g] = ((qh * q_cos + q_partner[:, g] * q_sin) * r).astype(BF16)
        kh = kn[:, g]
        rk = lax.rsqrt(jnp.sum(kh * kh, axis=-1, keepdims=True) + ss_rope)
        k_ref[:, g] = ((kh * k_gain + k_rope) * rk).astype(BF16)


def _b_proj(x, pos_col, gain, w_in, qn, kvn, wq, wqs, wk, wvt, qg, qgs, kg, krg, krgs, freq):
    tm = B_TM
    tiles_per_seq = SEQ // tm
    lane_vec = _const_spec((1, LANES))
    out_spec = pl.BlockSpec((tm, B_HW), lambda i: (i, 0))
    out_sds = jax.ShapeDtypeStruct((TOKENS, B_HW), BF16)
    vt_spec = pl.BlockSpec((None, B_VW, tm), lambda i: (i // tiles_per_seq, 0, i % tiles_per_seq))
    return pl.pallas_call(
        _b_proj_kernel,
        grid=(TOKENS // tm,),
        in_specs=[pl.BlockSpec((tm, D_MODEL), lambda i: (i, 0)), pl.BlockSpec((tm, 1), lambda i: (i, 0)),
                  _const_spec((1, D_MODEL)), _const_spec((D_MODEL, B_CW)), _const_spec((1, Q_LORA)),
                  _const_spec((1, KV_LORA)), _const_spec((Q_LORA, B_HW)), _const_spec((Q_LORA, B_HW)),
                  _const_spec((KV_LORA, B_HW)), _const_spec((B_VW, KV_LORA)),
                  lane_vec, lane_vec, lane_vec, lane_vec, lane_vec, lane_vec],
        out_specs=[out_spec, out_spec, vt_spec],
        out_shape=[out_sds, out_sds, jax.ShapeDtypeStruct((BATCH, B_VW, SEQ), BF16)],
        compiler_params=pltpu.CompilerParams(vmem_limit_bytes=VMEM_LIMIT),
        name="b_proj",
    )(x, pos_col, gain, w_in, qn, kvn, wq, wqs, wk, wvt, qg, qgs, kg, krg, krgs, freq)


B_LOOKAHEAD = 2


def _b_attn_kernel(q_ref, k_ref, vt_ref, o_ref):
    key = lax.broadcasted_iota(jnp.int32, (B_TK, B_TQ), 0)
    query = lax.broadcasted_iota(jnp.int32, (B_TK, B_TQ), 1)
    groups = [slice(parity * LANES, (parity + 1) * LANES) for parity in range(2)]
    ones = jnp.ones((V_DIM, B_TK), BF16)
    n_query_tiles = SEQ // B_TQ

    def n_key_tiles(qt):
        return (qt + 1) * B_TQ // B_TK

    def scores(qt, hd, j):
        return _dot_nt(k_ref[j * B_TK:(j + 1) * B_TK, groups[hd]], q_ref[qt * B_TQ:(qt + 1) * B_TQ, groups[hd]])

    def accumulate(qt, hd, j, s, carry):
        keys = slice(j * B_TK, (j + 1) * B_TK)
        m, acc = carry
        if (j + 1) * B_TK - 1 > qt * B_TQ:
            s = jnp.where(key + (j * B_TK - qt * B_TQ) <= query, s, NEG)
        m_new = jnp.maximum(m, jnp.max(s, axis=0, keepdims=True))
        alpha = jnp.exp2(m - m_new)
        e = jnp.exp2(s - m_new).astype(BF16)
        vt_ext = jnp.concatenate([vt_ref[hd * V_DIM:(hd + 1) * V_DIM, keys], ones], axis=0)
        return m_new, alpha * acc + _dot(vt_ext, e)

    units = [(qt, hd, j) for j in range(n_key_tiles(n_query_tiles - 1))
             for qt in reversed(range(n_query_tiles)) for hd in range(2) if j < n_key_tiles(qt)]
    init = (jnp.full((1, B_TQ), NEG, F32), jnp.zeros((2 * V_DIM, B_TQ), F32))
    carry = {(qt, hd): init for qt in range(n_query_tiles) for hd in range(2)}
    raw = {u: scores(*u) for u in units[:B_LOOKAHEAD]}
    for i, (qt, hd, j) in enumerate(units):
        if i + B_LOOKAHEAD < len(units):
            ahead = units[i + B_LOOKAHEAD]
            raw[ahead] = scores(*ahead)
        carry[(qt, hd)] = accumulate(qt, hd, j, raw.pop((qt, hd, j)), carry[(qt, hd)])
        if hd == 1 and j == n_key_tiles(qt) - 1:
            outs = [acc[:V_DIM, :] * (1.0 / acc[V_DIM:V_DIM + 1, :])
                    for _, acc in (carry.pop((qt, 0)), carry.pop((qt, 1)))]
            o_ref[qt * B_TQ:(qt + 1) * B_TQ, :] = jnp.concatenate(outs, axis=0).T.astype(BF16)


def _b_attn(q, k, vt):
    pairs = B_HEADS // 2
    return pl.pallas_call(
        _b_attn_kernel,
        grid=(BATCH, pairs),
        in_specs=[pl.BlockSpec((SEQ, 2 * LANES), lambda b, p: (b, p)),
                  pl.BlockSpec((SEQ, 2 * LANES), lambda b, p: (b, p)),
                  pl.BlockSpec((None, 2 * V_DIM, SEQ), lambda b, p: (b, p, 0))],
        out_specs=pl.BlockSpec((SEQ, 2 * V_DIM), lambda b, p: (b, p)),
        out_shape=jax.ShapeDtypeStruct((TOKENS, B_VW), BF16),
        compiler_params=pltpu.CompilerParams(vmem_limit_bytes=VMEM_LIMIT),
        name="b_attn",
    )(q, k, vt)


def _lohi(cols):
    z = jnp.zeros_like(cols)
    return jnp.concatenate([cols, z, z, cols], axis=1)


def _prep_a(w_in, q_gain, k_gain):
    wq = w_in[:, :A_QW]
    k0 = w_in[:, A_QW:A_QW + A_HEAD_DIM]
    k1 = w_in[:, A_QW + A_HEAD_DIM:A_QW + 2 * A_HEAD_DIM]
    v0 = w_in[:, A_QW + 2 * A_HEAD_DIM:A_QW + 3 * A_HEAD_DIM]
    v1 = w_in[:, A_QW + 3 * A_HEAD_DIM:]
    w = jnp.concatenate([wq, _lohi(k0), _lohi(k1), _lohi(v0), _lohi(v1)], axis=1).astype(BF16)
    qg = jnp.concatenate([q_gain, q_gain])[None, :]
    z = jnp.zeros_like(k_gain)
    kg = jnp.concatenate([k_gain, z, z, k_gain, k_gain, z, z, k_gain])[None, :]
    return w, qg, kg


def _head_groups(w, per_head, src_lo, src_hi, dst_lo):
    rows = w.shape[0]
    w3 = w.reshape(rows, B_HEADS, per_head)[:, :, src_lo:src_hi]
    out = jnp.zeros((rows, B_HEADS, LANES), w.dtype)
    out = out.at[:, :, dst_lo:dst_lo + (src_hi - src_lo)].set(w3)
    return out


def _prep_b(w_in, w_uq, w_ukv, q_gain, k_gain):
    half = QK_ROPE // 2
    t1 = slice(QK_NOPE, QK_NOPE + half)
    t2 = slice(QK_NOPE + half, B_DQK)
    rows = w_in.shape[0]
    rope_cols = w_in[:, Q_LORA + KV_LORA:]
    zeros = lambda n: jnp.zeros((rows, n), w_in.dtype)
    rope_group = jnp.concatenate([zeros(QK_NOPE), rope_cols, zeros(LANES - B_DQK)], axis=1)
    partner_group = jnp.concatenate(
        [zeros(QK_NOPE), rope_cols[:, half:], rope_cols[:, :half], zeros(LANES - B_DQK)], axis=1)
    win = jnp.concatenate([w_in[:, :Q_LORA + KV_LORA], rope_group, partner_group], axis=1).astype(BF16)

    wq = _head_groups(w_uq, B_DQK, 0, B_DQK, 0)
    wqs = (_head_groups(w_uq, B_DQK, t2.start, t2.stop, t1.start)
           + _head_groups(w_uq, B_DQK, t1.start, t1.stop, t2.start))
    wk = _head_groups(w_ukv, QK_NOPE + V_DIM, 0, QK_NOPE, 0)
    wvt = w_ukv.reshape(KV_LORA, B_HEADS, QK_NOPE + V_DIM)[:, :, QK_NOPE:].reshape(KV_LORA, B_VW).T.astype(BF16)
    flat = lambda a: a.reshape(a.shape[0], B_HW).astype(BF16)

    def lane_vec(pieces):
        out = jnp.zeros((LANES,), F32)
        for lo, vals in pieces:
            out = out.at[lo:lo + vals.shape[0]].set(vals)
        return out[None, :]

    qg = lane_vec([(0, q_gain)])
    qgs = lane_vec([(t1.start, q_gain[t2]), (t2.start, q_gain[t1])])
    kg = lane_vec([(0, k_gain[:QK_NOPE])])
    krg = lane_vec([(QK_NOPE, k_gain[QK_NOPE:])])
    krgs = lane_vec([(t1.start, k_gain[t2]), (t2.start, k_gain[t1])])
    return win, flat(wq), flat(wqs), flat(wk), wvt, qg, qgs, kg, krg, krgs


def _rope_freq():
    inv = ROPE_BASE ** (-np.arange(0, QK_ROPE, 2, dtype=np.float32) / QK_ROPE)
    out = np.zeros((1, LANES), np.float32)
    half = QK_ROPE // 2
    out[0, QK_NOPE:QK_NOPE + half] = inv
    out[0, QK_NOPE + half:B_DQK] = inv
    return jnp.asarray(out)


def kernel(x, positions, rel_bias, ffn_norm1, ffn1_wg, ffn1_wu, ffn1_wd, mix_norm, ffn_norm2, ffn2_wg,
           ffn2_wu, ffn2_wd, a_w_in, a_q_gain, a_k_gain, a_sinks, a_w_out, b_w_in, b_q_norm, b_kv_norm,
           b_w_uq, b_w_ukv, b_q_gain, b_k_gain, b_w_out):
    assert x.shape == (BATCH, SEQ, D_MODEL) and positions.shape == (BATCH, SEQ)
    xt = x.reshape(TOKENS, D_MODEL)
    pos_col = positions.reshape(TOKENS, 1)
    pos_row = positions.reshape(TOKENS // BLOCK, 1, BLOCK)
    table = jnp.zeros((A_HEADS, LANES), F32).at[:, :NUM_BUCKETS].set(rel_bias.T)
    bf = lambda w: w.astype(BF16)
    row = lambda v: v[None, :]
    ffn1 = (ffn_norm1[:, None, :], bf(ffn1_wg), bf(ffn1_wu), bf(ffn1_wd))
    ffn2 = (ffn_norm2[:, None, :], bf(ffn2_wg), bf(ffn2_wu), bf(ffn2_wd))

    for i in range(DEPTH):
        xt = _ffn(xt, i, *ffn1)
        j = i // N_MIXERS
        if i % N_MIXERS == 0:
            w, qg, kg = _prep_a(a_w_in[j], a_q_gain[j], a_k_gain[j])
            q, k4, v4 = _a_qkv(xt, row(mix_norm[i]), w, qg, kg)
            attn = _a_attn(q, k4, v4, pos_col, pos_row, table, a_sinks[j])
            w_out = bf(a_w_out[j])
        else:
            prep = _prep_b(b_w_in[j], b_w_uq[j], b_w_ukv[j], b_q_gain[j], b_k_gain[j])
            win, wq, wqs, wk, wvt, qg, qgs, kg, krg, krgs = prep
            q, k, vt = _b_proj(xt, pos_col, row(mix_norm[i]), win, row(b_q_norm[j]), row(b_kv_norm[j]),
                              wq, wqs, wk, wvt, qg, qgs, kg, krg, krgs, _rope_freq())
            attn = _b_attn(q, k, vt)
            w_out = bf(b_w_out[j])
        xt = _ffn(xt, i, *ffn2, attn, w_out)
    return xt.reshape(BATCH, SEQ, D_MODEL)
```

```python
import functools
import math

import numpy as np
import jax
import jax.numpy as jnp
from jax import lax
from jax.experimental import pallas as pl
from jax.experimental.pallas import tpu as pltpu

D_MODEL = 1024
BATCH = 8
SEQ = 2048
DEPTH = 2
N_MIXERS = 2
A_HEADS = 16
A_KV_HEADS = 2
A_HEAD_DIM = 64
WINDOW = 128
BLOCK = 128
NUM_BUCKETS = 32
MAX_DISTANCE = 128
B_HEADS = 16
Q_LORA = 256
KV_LORA = 128
QK_NOPE = 64
QK_ROPE = 32
V_DIM = 64
ROPE_BASE = 10000.0
D_FF = 2816
EPS = 1e-6
NEG = -1e30

TOKENS = BATCH * SEQ
LANES = 128
HALF = LANES // 2
B_DQK = QK_NOPE + QK_ROPE
VMEM_LIMIT = 60 * 1024 * 1024

LOG2E = math.log2(math.e)

F32 = jnp.float32
BF16 = jnp.bfloat16


def _rms_rows(x, gain):
    return x * lax.rsqrt(jnp.mean(x * x, axis=-1, keepdims=True) + EPS) * gain


def _dot(a, b):
    return jnp.dot(a, b, preferred_element_type=F32)


def _dot_nt(a, b):
    return lax.dot_general(a, b, (((1,), (1,)), ((), ())), preferred_element_type=F32)


def _const_spec(shape):
    nd = len(shape)
    return pl.BlockSpec(shape, lambda *_: (0,) * nd, pipeline_mode=pl.Buffered(1))


FFN_TM = 1024
MXU_WIDTH = 256
FFN_CHUNKS = ((0, 4 * MXU_WIDTH), (4 * MXU_WIDTH, 8 * MXU_WIDTH), (8 * MXU_WIDTH, D_FF))


def _ffn_body(x, g_ref, wg_ref, wu_ref, wd_ref, o_ref):
    h = _rms_rows(x, g_ref[...]).astype(BF16)
    down = None
    for lo, hi in FFN_CHUNKS:
        gate = _dot(h, wg_ref[:, lo:hi])
        up = _dot(h, wu_ref[:, lo:hi])
        act = (gate * jax.nn.sigmoid(gate) * up).astype(BF16)
        part = _dot(act, wd_ref[lo:hi, :])
        down = part if down is None else down + part
    o_ref[...] = x + 0.5 * down


def _ffn_kernel(*refs, project, cast_ahead):
    refs = list(refs)
    x = refs.pop(0)[...]
    if project:
        a_ref, wo_ref = refs.pop(0), refs.pop(0)
        x = x + _dot(a_ref[...], wo_ref[...])
    g_ref, wg_ref, wu_ref, wd_ref = refs[:4]
    refs = refs[4:]
    if cast_ahead:
        for src, dst in zip(refs[:3], refs[4:7]):
            dst[...] = src[...].astype(BF16)
        refs = refs[3:]
    _ffn_body(x, g_ref, wg_ref, wu_ref, wd_ref, refs[0])


def _ffn(x, gain, layer, wg, wu, wd, attn=None, w_out=None, cast_ahead=None):
    tm = FFN_TM
    steps = TOKENS // tm
    row_spec = pl.BlockSpec((tm, D_MODEL), lambda i: (i, 0))
    in_specs, args = [row_spec], [x]
    if attn is not None:
        in_specs += [row_spec, _const_spec((D_MODEL, D_MODEL))]
        args += [attn, w_out]
    in_specs += [pl.BlockSpec((None, 1, D_MODEL), lambda i: (layer, 0, 0), pipeline_mode=pl.Buffered(1)),
                 _const_spec((D_MODEL, D_FF)), _const_spec((D_MODEL, D_FF)), _const_spec((D_FF, D_MODEL))]
    args += [gain, wg, wu, wd]
    out_specs = [row_spec]
    out_shape = [jax.ShapeDtypeStruct((TOKENS, D_MODEL), F32)]
    if cast_ahead is not None:
        next_layer, *stacks = cast_ahead
        for w in stacks:
            rows, cols = w.shape[1] // steps, w.shape[2]
            in_specs.append(pl.BlockSpec((None, rows, cols), lambda i: (next_layer, i, 0)))
            out_specs.append(pl.BlockSpec((rows, cols), lambda i: (i, 0)))
            out_shape.append(jax.ShapeDtypeStruct(w.shape[1:], BF16))
        args += stacks
    outs = pl.pallas_call(
        functools.partial(_ffn_kernel, project=attn is not None, cast_ahead=cast_ahead is not None),
        grid=(steps,),
        in_specs=in_specs,
        out_specs=out_specs,
        out_shape=out_shape,
        compiler_params=pltpu.CompilerParams(vmem_limit_bytes=VMEM_LIMIT),
        name="ffn" if attn is None else "proj_ffn",
    )(*args)
    return outs[0], tuple(outs[1:])


A_TM = 512
A_QW = A_HEADS * A_HEAD_DIM
A_KW = 4 * LANES
A_PAIRS = A_HEADS // 2
A_ROWS = 256


def _a_qkv_kernel(x_ref, g_ref, w_ref, qg_ref, kg_ref, q_ref, k_ref, v_ref):
    h = _rms_rows(x_ref[...], g_ref[...]).astype(BF16)
    qkv = _dot(h, w_ref[...])
    lane = lax.broadcasted_iota(jnp.int32, (1, LANES), 1)
    low = lane < HALF
    q_table = qg_ref[...] * (math.sqrt(A_HEAD_DIM) * A_HEAD_DIM ** -0.5 * LOG2E)
    for p in range(A_PAIRS):
        qp = qkv[:, p * LANES:(p + 1) * LANES]
        sq = qp * qp
        ss_lo = jnp.sum(jnp.where(low, sq, 0.0), axis=-1, keepdims=True)
        ss_hi = jnp.sum(jnp.where(low, 0.0, sq), axis=-1, keepdims=True)
        r = jnp.where(low, lax.rsqrt(ss_lo + A_HEAD_DIM * EPS), lax.rsqrt(ss_hi + A_HEAD_DIM * EPS))
        q_ref[:, p * LANES:(p + 1) * LANES] = (qp * q_table * r).astype(BF16)
    for j in range(A_KW // LANES):
        kj = qkv[:, A_QW + j * LANES:A_QW + (j + 1) * LANES]
        ms = jnp.sum(kj * kj, axis=-1, keepdims=True) * (1.0 / A_HEAD_DIM)
        k_ref[:, j * LANES:(j + 1) * LANES] = (
            kj * lax.rsqrt(ms + EPS) * kg_ref[:, j * LANES:(j + 1) * LANES]).astype(BF16)
    v_ref[...] = qkv[:, A_QW + A_KW:].astype(BF16)


def _a_qkv(x, gain, w, qg, kg):
    tm = A_TM
    width = A_QW + 2 * A_KW
    return pl.pallas_call(
        _a_qkv_kernel,
        grid=(TOKENS // tm,),
        in_specs=[pl.BlockSpec((tm, D_MODEL), lambda i: (i, 0)), _const_spec((1, D_MODEL)),
                  _const_spec((D_MODEL, width)), _const_spec((1, LANES)), _const_spec((1, A_KW))],
        out_specs=[pl.BlockSpec((tm, A_QW), lambda i: (i, 0)), pl.BlockSpec((tm, A_KW), lambda i: (i, 0)),
                   pl.BlockSpec((tm, A_KW), lambda i: (i, 0))],
        out_shape=[jax.ShapeDtypeStruct((TOKENS, A_QW), BF16), jax.ShapeDtypeStruct((TOKENS, A_KW), BF16),
                   jax.ShapeDtypeStruct((TOKENS, A_KW), BF16)],
        compiler_params=pltpu.CompilerParams(vmem_limit_bytes=VMEM_LIMIT),
        name="a_qkv",
    )(x, gain, w, qg, kg)


def _t5_bucket(dist):
    n = jnp.maximum(dist, 0)
    max_exact = NUM_BUCKETS // 2
    large = max_exact + (jnp.log(jnp.maximum(n, 1).astype(F32) / max_exact)
                         / math.log(MAX_DISTANCE / max_exact)
                         * (NUM_BUCKETS - max_exact)).astype(jnp.int32)
    large = jnp.minimum(large, NUM_BUCKETS - 1)
    return jnp.where(n < max_exact, n, large)


SUBLANES = 8


def _a_attn_kernel(sink_ref, q_ref, kc_ref, kp_ref, vc_ref, vp_ref, pq_ref, pkc_ref, pkp_ref, tbl_ref, o_ref,
                   bias_ref, shared_ref):
    row = lax.broadcasted_iota(jnp.int32, (BLOCK, BLOCK), 0)
    col = lax.broadcasted_iota(jnp.int32, (BLOCK, BLOCK), 1)
    cur_ok = col <= row
    first_in_seq = pl.program_id(1) == 0
    no_prev = jnp.where(first_in_seq, NEG, 0.0)
    tables = [jnp.broadcast_to(tbl_ref[h:h + 1, :] * LOG2E, (SUBLANES, LANES)) for h in range(A_HEADS)]
    ones = jnp.ones((2 * BLOCK, LANES), BF16)
    group_heads = A_HEADS // A_KV_HEADS
    lane = lax.broadcasted_iota(jnp.int32, (1, LANES), 1)

    @pl.when(jnp.logical_and(pl.program_id(0) == 0, first_in_seq))
    def _():
        shared_ref[0] = 0

    def fill_shift_invariant():
        back = jnp.broadcast_to((BLOCK - lane) & (BLOCK - 1), (SUBLANES, LANES))
        idx = _t5_bucket(back)
        for h in range(A_HEADS):
            base = jnp.take_along_axis(tables[h], idx, axis=1)
            base = jnp.broadcast_to(base[0:1, :], (BLOCK, BLOCK))
            tile = pltpu.roll(base, 0, 1, stride=1, stride_axis=0)
            bias_ref[h] = jnp.where(cur_ok, tile, NEG)
            bias_ref[A_HEADS + h] = jnp.where(cur_ok, NEG, tile)

    def fill_general(pos_q, pos_cur, pos_prev):
        bucket = jnp.where(cur_ok, _t5_bucket(pos_q - pos_cur), _t5_bucket(pos_q - pos_prev))
        for c in range(BLOCK // SUBLANES):
            chunk = slice(c * SUBLANES, (c + 1) * SUBLANES)
            for h in range(A_HEADS):
                piece = jnp.take_along_axis(tables[h], bucket[chunk, :], axis=1)
                bias_ref[h, chunk, :] = jnp.where(cur_ok[chunk, :], piece, NEG)
                bias_ref[A_HEADS + h, chunk, :] = jnp.where(cur_ok[chunk, :], NEG, piece)

    blocks = range(A_ROWS // BLOCK)

    def rows_of(r):
        return slice(r * BLOCK, (r + 1) * BLOCK)

    def pairs_of(kv):
        return range(kv * group_heads // 2, (kv + 1) * group_heads // 2)

    def band(r):
        if r == 0:
            k_prev, v_prev, pos_prev = kp_ref[...], vp_ref[...], pkp_ref[0]
        else:
            k_prev, v_prev, pos_prev = kc_ref[rows_of(r - 1), :], vc_ref[rows_of(r - 1), :], pkc_ref[r - 1]
        k_band = jnp.concatenate([k_prev, kc_ref[rows_of(r), :]], axis=0)
        v_band = jnp.concatenate([v_prev, vc_ref[rows_of(r), :]], axis=0)
        return k_band, v_band, pos_prev

    def scores(r, kv, parity, bands):
        group = slice((2 * kv + parity) * LANES, (2 * kv + parity + 1) * LANES)
        q_stack = jnp.concatenate([q_ref[rows_of(r), p * LANES:(p + 1) * LANES] for p in pairs_of(kv)], axis=0)
        return _dot_nt(q_stack, bands[r][0][:, group])

    def finish(r, kv, parity, s_all, bias, bands):
        v_band = bands[r][1]
        group = slice((2 * kv + parity) * LANES, (2 * kv + parity + 1) * LANES)
        probs, sink_terms = [], []
        for i, p in enumerate(pairs_of(kv)):
            head = 2 * p + parity
            s = s_all[i * BLOCK:(i + 1) * BLOCK, :]
            s_prev = s[:, :BLOCK] + bias[A_HEADS + head]
            if r == 0:
                s_prev = s_prev + no_prev
            s_cur = s[:, BLOCK:] + bias[head]
            sink = sink_ref[head] * LOG2E
            m = jnp.maximum(jnp.max(jnp.maximum(s_prev, s_cur), axis=-1, keepdims=True), sink)
            probs.append(jnp.concatenate([jnp.exp2(s_prev - m), jnp.exp2(s_cur - m)], axis=1).astype(BF16))
            sink_terms.append(jnp.exp2(sink - m))
        v_ext = jnp.concatenate([v_band[:, group], ones], axis=1)
        out = _dot(jnp.concatenate(probs, axis=0), v_ext)
        scaled = []
        for i in range(len(probs)):
            o_i = out[i * BLOCK:(i + 1) * BLOCK, :]
            scaled.append(o_i[:, :LANES] * (1.0 / (o_i[:, LANES:] + sink_terms[i])))
        return scaled

    def attend(which_blocks):
        bias = [bias_ref[h] for h in range(2 * A_HEADS)]
        bands = {r: band(r) for r in which_blocks}
        units = [(r, kv, parity) for r in which_blocks for kv in range(A_KV_HEADS) for parity in range(2)]
        raw = {units[0]: scores(*units[0], bands)}
        done = {}
        for i, unit in enumerate(units):
            if i + 1 < len(units):
                raw[units[i + 1]] = scores(*units[i + 1], bands)
            done[unit] = finish(*unit, raw.pop(unit), bias, bands)
            r, kv, parity = unit
            if parity == 1:
                low, high = done.pop((r, kv, 0)), done.pop((r, kv, 1))
                for i_pair, p in enumerate(pairs_of(kv)):
                    o_ref[rows_of(r), p * LANES:(p + 1) * LANES] = (low[i_pair] + high[i_pair]).astype(BF16)

    @pl.when(shared_ref[0] == 0)
    def _():
        fill_shift_invariant()
        shared_ref[0] = 1

    attend(blocks)

    off = jnp.zeros((1, LANES), jnp.int32)
    for r in blocks:
        pos_cur, pos_prev = pkc_ref[r], band(r)[2]
        start = jnp.min(pos_cur, axis=-1, keepdims=True)
        off_prev = jnp.where(pos_prev - start == lane - BLOCK, 0, 1)
        if r == 0:
            off_prev = off_prev * jnp.where(first_in_seq, 0, 1)
        off = off + jnp.where(pos_cur - start == lane, 0, 1) + off_prev

    @pl.when(jnp.sum(off) != 0)
    def _():
        for r in blocks:
            fill_general(pq_ref[rows_of(r), :], pkc_ref[r], band(r)[2])
            attend([r])
        shared_ref[0] = 0


def _a_attn(q, k4, v4, pos_col, pos_row, table, sinks):
    steps = SEQ // A_ROWS
    blocks_per_step = A_ROWS // BLOCK
    blocks_per_seq = SEQ // BLOCK

    def cur(b, i):
        return (b * steps + i, 0)

    def prev(b, i):
        return (b * blocks_per_seq + jnp.maximum(i * blocks_per_step - 1, 0), 0)

    def cur3(b, i):
        return (b * steps + i, 0, 0)

    def prev3(b, i):
        return (b * blocks_per_seq + jnp.maximum(i * blocks_per_step - 1, 0), 0, 0)

    return pl.pallas_call(
        _a_attn_kernel,
        grid=(BATCH, steps),
        in_specs=[
            pl.BlockSpec(memory_space=pltpu.SMEM),
            pl.BlockSpec((A_ROWS, A_QW), cur),
            pl.BlockSpec((A_ROWS, A_KW), cur),
            pl.BlockSpec((BLOCK, A_KW), prev),
            pl.BlockSpec((A_ROWS, A_KW), cur),
            pl.BlockSpec((BLOCK, A_KW), prev),
            pl.BlockSpec((A_ROWS, 1), cur),
            pl.BlockSpec((blocks_per_step, 1, BLOCK), cur3),
            pl.BlockSpec((1, 1, BLOCK), prev3),
            _const_spec((A_HEADS, LANES)),
        ],
        out_specs=pl.BlockSpec((A_ROWS, A_QW), cur),
        out_shape=jax.ShapeDtypeStruct((TOKENS, A_QW), BF16),
        scratch_shapes=[pltpu.VMEM((2 * A_HEADS, BLOCK, BLOCK), F32), pltpu.SMEM((1,), jnp.int32)],
        compiler_params=pltpu.CompilerParams(dimension_semantics=("arbitrary", "arbitrary"),
                                             vmem_limit_bytes=VMEM_LIMIT),
        name="a_attn",
    )(sinks, q, k4, k4, v4, v4, pos_col, pos_row, pos_row, table)


B_TM = 256
B_CW = Q_LORA + KV_LORA + 2 * LANES
B_HW = B_HEADS * LANES
B_VW = B_HEADS * V_DIM
B_TQ = 512
B_TK = 512


def _b_proj_kernel(x_ref, pos_ref, g_ref, win_ref, qn_ref, kvn_ref, wq_ref, wqs_ref, wk_ref, wvt_ref,
                   qg_ref, qgs_ref, kg_ref, krg_ref, krgs_ref, freq_ref, q_ref, k_ref, vt_ref):
    h = _rms_rows(x_ref[...], g_ref[...]).astype(BF16)
    c = _dot(h, win_ref[...])
    cq = _rms_rows(c[:, :Q_LORA], qn_ref[...]).astype(BF16)
    ckv_f32 = _rms_rows(c[:, Q_LORA:Q_LORA + KV_LORA], kvn_ref[...])
    ckv = ckv_f32.astype(BF16)
    kr = c[:, Q_LORA + KV_LORA:Q_LORA + KV_LORA + LANES]
    kr_partner = c[:, Q_LORA + KV_LORA + LANES:]
    q = _dot(cq, wq_ref[...])
    q_partner = _dot(cq, wqs_ref[...])
    kn = _dot(ckv, wk_ref[...])
    vt_ref[...] = _dot(wvt_ref[...], ckv_f32.T.astype(BF16)).astype(BF16)

    lane = lax.broadcasted_iota(jnp.int32, (1, LANES), 1)
    ang = pos_ref[...].astype(F32) * freq_ref[...]
    cos = jnp.cos(ang)
    sin = jnp.sin(ang)
    half = QK_ROPE // 2
    sin_signed = jnp.where(lane < QK_NOPE + half, -sin, sin)

    root_d = math.sqrt(B_DQK)
    q_const = root_d * B_DQK ** -0.5 * LOG2E
    k_rope = (kr * krg_ref[...] * cos + kr_partner * krgs_ref[...] * sin_signed) * root_d
    k_gain = kg_ref[...] * root_d
    ss_rope = jnp.sum(kr * kr, axis=-1, keepdims=True) + B_DQK * EPS
    q_cos = qg_ref[...] * cos * q_const
    q_sin = qgs_ref[...] * sin_signed * q_const
    for hd in range(B_HEADS):
        g = slice(hd * LANES, (hd + 1) * LANES)
        qh = q[:, g]
        r = lax.rsqrt(jnp.sum(qh * qh, axis=-1, keepdims=True) + B_DQK * EPS)
        q_ref[:, g] = ((qh * q_cos + q_partner[:, g] * q_sin) * r).astype(BF16)
        kh = kn[:, g]
        rk = lax.rsqrt(jnp.sum(kh * kh, axis=-1, keepdims=True) + ss_rope)
        k_ref[:, g] = ((kh * k_gain + k_rope) * rk).astype(BF16)


def _b_proj(x, pos_col, gain, w_in, qn, kvn, wq, wqs, wk, wvt, qg, qgs, kg, krg, krgs, freq):
    tm = B_TM
    tiles_per_seq = SEQ // tm
    lane_vec = _const_spec((1, LANES))
    out_spec = pl.BlockSpec((tm, B_HW), lambda i: (i, 0))
    out_sds = jax.ShapeDtypeStruct((TOKENS, B_HW), BF16)
    vt_spec = pl.BlockSpec((None, B_VW, tm), lambda i: (i // tiles_per_seq, 0, i % tiles_per_seq))
    return pl.pallas_call(
        _b_proj_kernel,
        grid=(TOKENS // tm,),
        in_specs=[pl.BlockSpec((tm, D_MODEL), lambda i: (i, 0)), pl.BlockSpec((tm, 1), lambda i: (i, 0)),
                  _const_spec((1, D_MODEL)), _const_spec((D_MODEL, B_CW)), _const_spec((1, Q_LORA)),
                  _const_spec((1, KV_LORA)), _const_spec((Q_LORA, B_HW)), _const_spec((Q_LORA, B_HW)),
                  _const_spec((KV_LORA, B_HW)), _const_spec((B_VW, KV_LORA)),
                  lane_vec, lane_vec, lane_vec, lane_vec, lane_vec, lane_vec],
        out_specs=[out_spec, out_spec, vt_spec],
        out_shape=[out_sds, out_sds, jax.ShapeDtypeStruct((BATCH, B_VW, SEQ), BF16)],
        compiler_params=pltpu.CompilerParams(vmem_limit_bytes=VMEM_LIMIT),
        name="b_proj",
    )(x, pos_col, gain, w_in, qn, kvn, wq, wqs, wk, wvt, qg, qgs, kg, krg, krgs, freq)


B_LOOKAHEAD = 2


def _b_attn_kernel(q_ref, k_ref, vt_ref, o_ref):
    key = lax.broadcasted_iota(jnp.int32, (B_TK, B_TQ), 0)
    query = lax.broadcasted_iota(jnp.int32, (B_TK, B_TQ), 1)
    groups = [slice(parity * LANES, (parity + 1) * LANES) for parity in range(2)]
    ones = jnp.ones((V_DIM, B_TK), BF16)
    n_query_tiles = SEQ // B_TQ

    def n_key_tiles(qt):
        return (qt + 1) * B_TQ // B_TK

    def scores(qt, hd, j):
        return _dot_nt(k_ref[j * B_TK:(j + 1) * B_TK, groups[hd]], q_ref[qt * B_TQ:(qt + 1) * B_TQ, groups[hd]])

    def accumulate(qt, hd, j, s, carry):
        keys = slice(j * B_TK, (j + 1) * B_TK)
        m, acc = carry
        if (j + 1) * B_TK - 1 > qt * B_TQ:
            s = jnp.where(key + (j * B_TK - qt * B_TQ) <= query, s, NEG)
        m_new = jnp.maximum(m, jnp.max(s, axis=0, keepdims=True))
        alpha = jnp.exp2(m - m_new)
        e = jnp.exp2(s - m_new).astype(BF16)
        vt_ext = jnp.concatenate([vt_ref[hd * V_DIM:(hd + 1) * V_DIM, keys], ones], axis=0)
        return m_new, alpha * acc + _dot(vt_ext, e)

    units = [(qt, hd, j) for j in range(n_key_tiles(n_query_tiles - 1))
             for qt in reversed(range(n_query_tiles)) for hd in range(2) if j < n_key_tiles(qt)]
    init = (jnp.full((1, B_TQ), NEG, F32), jnp.zeros((2 * V_DIM, B_TQ), F32))
    carry = {(qt, hd): init for qt in range(n_query_tiles) for hd in range(2)}
    raw = {u: scores(*u) for u in units[:B_LOOKAHEAD]}
    for i, (qt, hd, j) in enumerate(units):
        if i + B_LOOKAHEAD < len(units):
            ahead = units[i + B_LOOKAHEAD]
            raw[ahead] = scores(*ahead)
        carry[(qt, hd)] = accumulate(qt, hd, j, raw.pop((qt, hd, j)), carry[(qt, hd)])
        if hd == 1 and j == n_key_tiles(qt) - 1:
            outs = [acc[:V_DIM, :] * (1.0 / acc[V_DIM:V_DIM + 1, :])
                    for _, acc in (carry.pop((qt, 0)), carry.pop((qt, 1)))]
            o_ref[qt * B_TQ:(qt + 1) * B_TQ, :] = jnp.concatenate(outs, axis=0).T.astype(BF16)


def _b_attn(q, k, vt):
    pairs = B_HEADS // 2
    return pl.pallas_call(
        _b_attn_kernel,
        grid=(BATCH, pairs),
        in_specs=[pl.BlockSpec((SEQ, 2 * LANES), lambda b, p: (b, p)),
                  pl.BlockSpec((SEQ, 2 * LANES), lambda b, p: (b, p)),
                  pl.BlockSpec((None, 2 * V_DIM, SEQ), lambda b, p: (b, p, 0))],
        out_specs=pl.BlockSpec((SEQ, 2 * V_DIM), lambda b, p: (b, p)),
        out_shape=jax.ShapeDtypeStruct((TOKENS, B_VW), BF16),
        compiler_params=pltpu.CompilerParams(vmem_limit_bytes=VMEM_LIMIT),
        name="b_attn",
    )(q, k, vt)


def _lohi(cols):
    z = jnp.zeros_like(cols)
    return jnp.concatenate([cols, z, z, cols], axis=1)


def _prep_a(w_in, q_gain, k_gain):
    wq = w_in[:, :A_QW]
    k0 = w_in[:, A_QW:A_QW + A_HEAD_DIM]
    k1 = w_in[:, A_QW + A_HEAD_DIM:A_QW + 2 * A_HEAD_DIM]
    v0 = w_in[:, A_QW + 2 * A_HEAD_DIM:A_QW + 3 * A_HEAD_DIM]
    v1 = w_in[:, A_QW + 3 * A_HEAD_DIM:]
    w = jnp.concatenate([wq, _lohi(k0), _lohi(k1), _lohi(v0), _lohi(v1)], axis=1).astype(BF16)
    qg = jnp.concatenate([q_gain, q_gain])[None, :]
    z = jnp.zeros_like(k_gain)
    kg = jnp.concatenate([k_gain, z, z, k_gain, k_gain, z, z, k_gain])[None, :]
    return w, qg, kg


def _head_groups(w, per_head, src_lo, src_hi, dst_lo):
    rows = w.shape[0]
    w3 = w.reshape(rows, B_HEADS, per_head)[:, :, src_lo:src_hi]
    out = jnp.zeros((rows, B_HEADS, LANES), w.dtype)
    out = out.at[:, :, dst_lo:dst_lo + (src_hi - src_lo)].set(w3)
    return out


def _prep_b(w_in, w_uq, w_ukv, q_gain, k_gain):
    half = QK_ROPE // 2
    t1 = slice(QK_NOPE, QK_NOPE + half)
    t2 = slice(QK_NOPE + half, B_DQK)
    rows = w_in.shape[0]
    rope_cols = w_in[:, Q_LORA + KV_LORA:]
    zeros = lambda n: jnp.zeros((rows, n), w_in.dtype)
    rope_group = jnp.concatenate([zeros(QK_NOPE), rope_cols, zeros(LANES - B_DQK)], axis=1)
    partner_group = jnp.concatenate(
        [zeros(QK_NOPE), rope_cols[:, half:], rope_cols[:, :half], zeros(LANES - B_DQK)], axis=1)
    win = jnp.concatenate([w_in[:, :Q_LORA + KV_LORA], rope_group, partner_group], axis=1).astype(BF16)

    wq = _head_groups(w_uq, B_DQK, 0, B_DQK, 0)
    wqs = (_head_groups(w_uq, B_DQK, t2.start, t2.stop, t1.start)
           + _head_groups(w_uq, B_DQK, t1.start, t1.stop, t2.start))
    wk = _head_groups(w_ukv, QK_NOPE + V_DIM, 0, QK_NOPE, 0)
    wvt = w_ukv.reshape(KV_LORA, B_HEADS, QK_NOPE + V_DIM)[:, :, QK_NOPE:].reshape(KV_LORA, B_VW).T.astype(BF16)
    flat = lambda a: a.reshape(a.shape[0], B_HW).astype(BF16)

    def lane_vec(pieces):
        out = jnp.zeros((LANES,), F32)
        for lo, vals in pieces:
            out = out.at[lo:lo + vals.shape[0]].set(vals)
        return out[None, :]

    qg = lane_vec([(0, q_gain)])
    qgs = lane_vec([(t1.start, q_gain[t2]), (t2.start, q_gain[t1])])
    kg = lane_vec([(0, k_gain[:QK_NOPE])])
    krg = lane_vec([(QK_NOPE, k_gain[QK_NOPE:])])
    krgs = lane_vec([(t1.start, k_gain[t2]), (t2.start, k_gain[t1])])
    return win, flat(wq), flat(wqs), flat(wk), wvt, qg, qgs, kg, krg, krgs


def _rope_freq():
    inv = ROPE_BASE ** (-np.arange(0, QK_ROPE, 2, dtype=np.float32) / QK_ROPE)
    out = np.zeros((1, LANES), np.float32)
    half = QK_ROPE // 2
    out[0, QK_NOPE:QK_NOPE + half] = inv
    out[0, QK_NOPE + half:B_DQK] = inv
    return jnp.asarray(out)


def kernel(x, positions, rel_bias, ffn_norm1, ffn1_wg, ffn1_wu, ffn1_wd, mix_norm, ffn_norm2, ffn2_wg,
           ffn2_wu, ffn2_wd, a_w_in, a_q_gain, a_k_gain, a_sinks, a_w_out, b_w_in, b_q_norm, b_kv_norm,
           b_w_uq, b_w_ukv, b_q_gain, b_k_gain, b_w_out):
    assert x.shape == (BATCH, SEQ, D_MODEL) and positions.shape == (BATCH, SEQ)
    xt = x.reshape(TOKENS, D_MODEL)
    pos_col = positions.reshape(TOKENS, 1)
    pos_row = positions.reshape(TOKENS // BLOCK, 1, BLOCK)
    table = jnp.zeros((A_HEADS, LANES), F32).at[:, :NUM_BUCKETS].set(rel_bias.T)
    bf = lambda w: w.astype(BF16)
    row = lambda v: v[None, :]
    gain1, gain2 = ffn_norm1[:, None, :], ffn_norm2[:, None, :]
    weights = (bf(ffn1_wg[0]), bf(ffn1_wu[0]), bf(ffn1_wd[0]))

    for i in range(DEPTH):
        xt, weights = _ffn(xt, gain1, i, *weights, cast_ahead=(i, ffn2_wg, ffn2_wu, ffn2_wd))
        j = i // N_MIXERS
        if i % N_MIXERS == 0:
            w, qg, kg = _prep_a(a_w_in[j], a_q_gain[j], a_k_gain[j])
            q, k4, v4 = _a_qkv(xt, row(mix_norm[i]), w, qg, kg)
            attn = _a_attn(q, k4, v4, pos_col, pos_row, table, a_sinks[j])
            w_out = bf(a_w_out[j])
        else:
            prep = _prep_b(b_w_in[j], b_w_uq[j], b_w_ukv[j], b_q_gain[j], b_k_gain[j])
            win, wq, wqs, wk, wvt, qg, qgs, kg, krg, krgs = prep
            q, k, vt = _b_proj(xt, pos_col, row(mix_norm[i]), win, row(b_q_norm[j]), row(b_kv_norm[j]),
                              wq, wqs, wk, wvt, qg, qgs, kg, krg, krgs, _rope_freq())
            attn = _b_attn(q, k, vt)
            w_out = bf(b_w_out[j])
        ahead = (i + 1, ffn1_wg, ffn1_wu, ffn1_wd) if i + 1 < DEPTH else None
        xt, weights = _ffn(xt, gain2, i, *weights, attn=attn, w_out=w_out, cast_ahead=ahead)
    return xt.reshape(BATCH, SEQ, D_MODEL)
```

```python
import functools
import math

import numpy as np
import jax
import jax.numpy as jnp
from jax import lax
from jax.experimental import pallas as pl
from jax.experimental.pallas import tpu as pltpu

D_MODEL = 1024
BATCH = 8
SEQ = 2048
DEPTH = 2
N_MIXERS = 2
A_HEADS = 16
A_KV_HEADS = 2
A_HEAD_DIM = 64
WINDOW = 128
BLOCK = 128
NUM_BUCKETS = 32
MAX_DISTANCE = 128
B_HEADS = 16
Q_LORA = 256
KV_LORA = 128
QK_NOPE = 64
QK_ROPE = 32
V_DIM = 64
ROPE_BASE = 10000.0
D_FF = 2816
EPS = 1e-6
NEG = -1e30

TOKENS = BATCH * SEQ
LANES = 128
HALF = LANES // 2
B_DQK = QK_NOPE + QK_ROPE
VMEM_LIMIT = 60 * 1024 * 1024

LOG2E = math.log2(math.e)

F32 = jnp.float32
BF16 = jnp.bfloat16


def _rms_rows(x, gain):
    return x * lax.rsqrt(jnp.mean(x * x, axis=-1, keepdims=True) + EPS) * gain


def _dot(a, b):
    return jnp.dot(a, b, preferred_element_type=F32)


def _dot_nt(a, b):
    return lax.dot_general(a, b, (((1,), (1,)), ((), ())), preferred_element_type=F32)


def _const_spec(shape):
    nd = len(shape)
    return pl.BlockSpec(shape, lambda *_: (0,) * nd, pipeline_mode=pl.Buffered(1))


FFN_TM = 1024
MXU_WIDTH = 256
FFN_CHUNKS = ((0, 4 * MXU_WIDTH), (4 * MXU_WIDTH, 8 * MXU_WIDTH), (8 * MXU_WIDTH, D_FF))


def _ffn_body(x, g_ref, wg_ref, wu_ref, wd_ref, o_ref):
    h = _rms_rows(x, g_ref[...]).astype(BF16)
    down = None
    for lo, hi in FFN_CHUNKS:
        gate = _dot(h, wg_ref[:, lo:hi])
        up = _dot(h, wu_ref[:, lo:hi])
        act = (gate * jax.nn.sigmoid(gate) * up).astype(BF16)
        part = _dot(act, wd_ref[lo:hi, :])
        down = part if down is None else down + part
    o_ref[...] = x + 0.5 * down


def _ffn_kernel(*refs, project, cast_ahead):
    refs = list(refs)
    x = refs.pop(0)[...]
    if project:
        a_ref, wo_ref = refs.pop(0), refs.pop(0)
        x = x + _dot(a_ref[...], wo_ref[...])
    g_ref, wg_ref, wu_ref, wd_ref = refs[:4]
    refs = refs[4:]
    if cast_ahead:
        for src, dst in zip(refs[:3], refs[4:7]):
            dst[...] = src[...].astype(BF16)
        refs = refs[3:]
    _ffn_body(x, g_ref, wg_ref, wu_ref, wd_ref, refs[0])


def _ffn(x, gain, layer, wg, wu, wd, attn=None, w_out=None, cast_ahead=None):
    tm = FFN_TM
    steps = TOKENS // tm
    row_spec = pl.BlockSpec((tm, D_MODEL), lambda i: (i, 0))
    in_specs, args = [row_spec], [x]
    if attn is not None:
        in_specs += [row_spec, _const_spec((D_MODEL, D_MODEL))]
        args += [attn, w_out]
    in_specs += [pl.BlockSpec((None, 1, D_MODEL), lambda i: (layer, 0, 0), pipeline_mode=pl.Buffered(1)),
                 _const_spec((D_MODEL, D_FF)), _const_spec((D_MODEL, D_FF)), _const_spec((D_FF, D_MODEL))]
    args += [gain, wg, wu, wd]
    out_specs = [row_spec]
    out_shape = [jax.ShapeDtypeStruct((TOKENS, D_MODEL), F32)]
    if cast_ahead is not None:
        next_layer, *stacks = cast_ahead
        for w in stacks:
            rows, cols = w.shape[1] // steps, w.shape[2]
            in_specs.append(pl.BlockSpec((None, rows, cols), lambda i: (next_layer, i, 0)))
            out_specs.append(pl.BlockSpec((rows, cols), lambda i: (i, 0)))
            out_shape.append(jax.ShapeDtypeStruct(w.shape[1:], BF16))
        args += stacks
    outs = pl.pallas_call(
        functools.partial(_ffn_kernel, project=attn is not None, cast_ahead=cast_ahead is not None),
        grid=(steps,),
        in_specs=in_specs,
        out_specs=out_specs,
        out_shape=out_shape,
        compiler_params=pltpu.CompilerParams(vmem_limit_bytes=VMEM_LIMIT),
        name="ffn" if attn is None else "proj_ffn",
    )(*args)
    return outs[0], tuple(outs[1:])


A_TM = 512
A_QW = A_HEADS * A_HEAD_DIM
A_KW = 4 * LANES
A_PAIRS = A_HEADS // 2
A_ROWS = 512


def _a_qkv_kernel(x_ref, g_ref, w_ref, qg_ref, kg_ref, q_ref, k_ref, v_ref):
    h = _rms_rows(x_ref[...], g_ref[...]).astype(BF16)
    qkv = _dot(h, w_ref[...])
    lane = lax.broadcasted_iota(jnp.int32, (1, LANES), 1)
    low = lane < HALF
    q_table = qg_ref[...] * (math.sqrt(A_HEAD_DIM) * A_HEAD_DIM ** -0.5 * LOG2E)
    for p in range(A_PAIRS):
        qp = qkv[:, p * LANES:(p + 1) * LANES]
        sq = qp * qp
        ss_lo = jnp.sum(jnp.where(low, sq, 0.0), axis=-1, keepdims=True)
        ss_hi = jnp.sum(jnp.where(low, 0.0, sq), axis=-1, keepdims=True)
        r = jnp.where(low, lax.rsqrt(ss_lo + A_HEAD_DIM * EPS), lax.rsqrt(ss_hi + A_HEAD_DIM * EPS))
        q_ref[:, p * LANES:(p + 1) * LANES] = (qp * q_table * r).astype(BF16)
    for j in range(A_KW // LANES):
        kj = qkv[:, A_QW + j * LANES:A_QW + (j + 1) * LANES]
        ms = jnp.sum(kj * kj, axis=-1, keepdims=True) * (1.0 / A_HEAD_DIM)
        k_ref[:, j * LANES:(j + 1) * LANES] = (
            kj * lax.rsqrt(ms + EPS) * kg_ref[:, j * LANES:(j + 1) * LANES]).astype(BF16)
    v_ref[...] = qkv[:, A_QW + A_KW:].astype(BF16)


def _a_qkv(x, gain, w, qg, kg):
    tm = A_TM
    width = A_QW + 2 * A_KW
    return pl.pallas_call(
        _a_qkv_kernel,
        grid=(TOKENS // tm,),
        in_specs=[pl.BlockSpec((tm, D_MODEL), lambda i: (i, 0)), _const_spec((1, D_MODEL)),
                  _const_spec((D_MODEL, width)), _const_spec((1, LANES)), _const_spec((1, A_KW))],
        out_specs=[pl.BlockSpec((tm, A_QW), lambda i: (i, 0)), pl.BlockSpec((tm, A_KW), lambda i: (i, 0)),
                   pl.BlockSpec((tm, A_KW), lambda i: (i, 0))],
        out_shape=[jax.ShapeDtypeStruct((TOKENS, A_QW), BF16), jax.ShapeDtypeStruct((TOKENS, A_KW), BF16),
                   jax.ShapeDtypeStruct((TOKENS, A_KW), BF16)],
        compiler_params=pltpu.CompilerParams(vmem_limit_bytes=VMEM_LIMIT),
        name="a_qkv",
    )(x, gain, w, qg, kg)


def _t5_bucket(dist):
    n = jnp.maximum(dist, 0)
    max_exact = NUM_BUCKETS // 2
    large = max_exact + (jnp.log(jnp.maximum(n, 1).astype(F32) / max_exact)
                         / math.log(MAX_DISTANCE / max_exact)
                         * (NUM_BUCKETS - max_exact)).astype(jnp.int32)
    large = jnp.minimum(large, NUM_BUCKETS - 1)
    return jnp.where(n < max_exact, n, large)


SUBLANES = 8


def _a_attn_kernel(sink_ref, q_ref, kc_ref, kp_ref, vc_ref, vp_ref, pq_ref, pkc_ref, pkp_ref, tbl_ref, o_ref,
                   bias_ref, shared_ref):
    row = lax.broadcasted_iota(jnp.int32, (BLOCK, BLOCK), 0)
    col = lax.broadcasted_iota(jnp.int32, (BLOCK, BLOCK), 1)
    cur_ok = col <= row
    first_in_seq = pl.program_id(1) == 0
    no_prev = jnp.where(first_in_seq, NEG, 0.0)
    tables = [jnp.broadcast_to(tbl_ref[h:h + 1, :] * LOG2E, (SUBLANES, LANES)) for h in range(A_HEADS)]
    ones = jnp.ones((2 * BLOCK, LANES), BF16)
    group_heads = A_HEADS // A_KV_HEADS
    lane = lax.broadcasted_iota(jnp.int32, (1, LANES), 1)

    @pl.when(jnp.logical_and(pl.program_id(0) == 0, first_in_seq))
    def _():
        shared_ref[0] = 0

    def fill_shift_invariant():
        back = jnp.broadcast_to((BLOCK - lane) & (BLOCK - 1), (SUBLANES, LANES))
        idx = _t5_bucket(back)
        for h in range(A_HEADS):
            base = jnp.take_along_axis(tables[h], idx, axis=1)
            base = jnp.broadcast_to(base[0:1, :], (BLOCK, BLOCK))
            tile = pltpu.roll(base, 0, 1, stride=1, stride_axis=0)
            bias_ref[h] = jnp.where(cur_ok, tile, NEG)
            bias_ref[A_HEADS + h] = jnp.where(cur_ok, NEG, tile)

    def fill_general(pos_q, pos_cur, pos_prev):
        bucket = jnp.where(cur_ok, _t5_bucket(pos_q - pos_cur), _t5_bucket(pos_q - pos_prev))
        for c in range(BLOCK // SUBLANES):
            chunk = slice(c * SUBLANES, (c + 1) * SUBLANES)
            for h in range(A_HEADS):
                piece = jnp.take_along_axis(tables[h], bucket[chunk, :], axis=1)
                bias_ref[h, chunk, :] = jnp.where(cur_ok[chunk, :], piece, NEG)
                bias_ref[A_HEADS + h, chunk, :] = jnp.where(cur_ok[chunk, :], NEG, piece)

    blocks = range(A_ROWS // BLOCK)

    def rows_of(r):
        return slice(r * BLOCK, (r + 1) * BLOCK)

    def pairs_of(kv):
        return range(kv * group_heads // 2, (kv + 1) * group_heads // 2)

    def band(r):
        if r == 0:
            k_prev, v_prev, pos_prev = kp_ref[...], vp_ref[...], pkp_ref[0]
        else:
            k_prev, v_prev, pos_prev = kc_ref[rows_of(r - 1), :], vc_ref[rows_of(r - 1), :], pkc_ref[r - 1]
        k_band = jnp.concatenate([k_prev, kc_ref[rows_of(r), :]], axis=0)
        v_band = jnp.concatenate([v_prev, vc_ref[rows_of(r), :]], axis=0)
        return k_band, v_band, pos_prev

    def scores(r, kv, parity, bands):
        group = slice((2 * kv + parity) * LANES, (2 * kv + parity + 1) * LANES)
        q_stack = jnp.concatenate([q_ref[rows_of(r), p * LANES:(p + 1) * LANES] for p in pairs_of(kv)], axis=0)
        return _dot_nt(q_stack, bands[r][0][:, group])

    def finish(r, kv, parity, s_all, bias, bands):
        v_band = bands[r][1]
        group = slice((2 * kv + parity) * LANES, (2 * kv + parity + 1) * LANES)
        probs, sink_terms = [], []
        for i, p in enumerate(pairs_of(kv)):
            head = 2 * p + parity
            s = s_all[i * BLOCK:(i + 1) * BLOCK, :]
            s_prev = s[:, :BLOCK] + bias[A_HEADS + head]
            if r == 0:
                s_prev = s_prev + no_prev
            s_cur = s[:, BLOCK:] + bias[head]
            sink = sink_ref[head] * LOG2E
            m = jnp.maximum(jnp.max(jnp.maximum(s_prev, s_cur), axis=-1, keepdims=True), sink)
            probs.append(jnp.concatenate([jnp.exp2(s_prev - m), jnp.exp2(s_cur - m)], axis=1).astype(BF16))
            sink_terms.append(jnp.exp2(sink - m))
        v_ext = jnp.concatenate([v_band[:, group], ones], axis=1)
        out = _dot(jnp.concatenate(probs, axis=0), v_ext)
        scaled = []
        for i in range(len(probs)):
            o_i = out[i * BLOCK:(i + 1) * BLOCK, :]
            scaled.append(o_i[:, :LANES] * (1.0 / (o_i[:, LANES:] + sink_terms[i])))
        return scaled

    def attend(which_blocks):
        bias = [bias_ref[h] for h in range(2 * A_HEADS)]
        bands = {r: band(r) for r in which_blocks}
        units = [(r, kv, parity) for r in which_blocks for kv in range(A_KV_HEADS) for parity in range(2)]
        raw = {units[0]: scores(*units[0], bands)}
        done = {}
        for i, unit in enumerate(units):
            if i + 1 < len(units):
                raw[units[i + 1]] = scores(*units[i + 1], bands)
            done[unit] = finish(*unit, raw.pop(unit), bias, bands)
            r, kv, parity = unit
            if parity == 1:
                low, high = done.pop((r, kv, 0)), done.pop((r, kv, 1))
                for i_pair, p in enumerate(pairs_of(kv)):
                    o_ref[rows_of(r), p * LANES:(p + 1) * LANES] = (low[i_pair] + high[i_pair]).astype(BF16)

    @pl.when(shared_ref[0] == 0)
    def _():
        fill_shift_invariant()
        shared_ref[0] = 1

    attend(blocks)

    off = jnp.zeros((1, LANES), jnp.int32)
    for r in blocks:
        pos_cur, pos_prev = pkc_ref[r], band(r)[2]
        start = jnp.min(pos_cur, axis=-1, keepdims=True)
        off_prev = jnp.where(pos_prev - start == lane - BLOCK, 0, 1)
        if r == 0:
            off_prev = off_prev * jnp.where(first_in_seq, 0, 1)
        off = off + jnp.where(pos_cur - start == lane, 0, 1) + off_prev

    @pl.when(jnp.sum(off) != 0)
    def _():
        for r in blocks:
            fill_general(pq_ref[rows_of(r), :], pkc_ref[r], band(r)[2])
            attend([r])
        shared_ref[0] = 0


def _a_attn(q, k4, v4, pos_col, pos_row, table, sinks):
    steps = SEQ // A_ROWS
    blocks_per_step = A_ROWS // BLOCK
    blocks_per_seq = SEQ // BLOCK

    def cur(b, i):
        return (b * steps + i, 0)

    def prev(b, i):
        return (b * blocks_per_seq + jnp.maximum(i * blocks_per_step - 1, 0), 0)

    def cur3(b, i):
        return (b * steps + i, 0, 0)

    def prev3(b, i):
        return (b * blocks_per_seq + jnp.maximum(i * blocks_per_step - 1, 0), 0, 0)

    return pl.pallas_call(
        _a_attn_kernel,
        grid=(BATCH, steps),
        in_specs=[
            pl.BlockSpec(memory_space=pltpu.SMEM),
            pl.BlockSpec((A_ROWS, A_QW), cur),
            pl.BlockSpec((A_ROWS, A_KW), cur),
            pl.BlockSpec((BLOCK, A_KW), prev),
            pl.BlockSpec((A_ROWS, A_KW), cur),
            pl.BlockSpec((BLOCK, A_KW), prev),
            pl.BlockSpec((A_ROWS, 1), cur),
            pl.BlockSpec((blocks_per_step, 1, BLOCK), cur3),
            pl.BlockSpec((1, 1, BLOCK), prev3),
            _const_spec((A_HEADS, LANES)),
        ],
        out_specs=pl.BlockSpec((A_ROWS, A_QW), cur),
        out_shape=jax.ShapeDtypeStruct((TOKENS, A_QW), BF16),
        scratch_shapes=[pltpu.VMEM((2 * A_HEADS, BLOCK, BLOCK), F32), pltpu.SMEM((1,), jnp.int32)],
        compiler_params=pltpu.CompilerParams(dimension_semantics=("arbitrary", "arbitrary"),
                                             vmem_limit_bytes=VMEM_LIMIT),
        name="a_attn",
    )(sinks, q, k4, k4, v4, v4, pos_col, pos_row, pos_row, table)


B_TM = 256
B_CW = Q_LORA + KV_LORA + 2 * LANES
B_HW = B_HEADS * LANES
B_VW = B_HEADS * V_DIM
B_TQ = 512
B_TK = 512


def _b_proj_kernel(x_ref, pos_ref, g_ref, win_ref, qn_ref, kvn_ref, wq_ref, wqs_ref, wk_ref, wvt_ref,
                   qg_ref, qgs_ref, kg_ref, krg_ref, krgs_ref, freq_ref, q_ref, k_ref, vt_ref):
    h = _rms_rows(x_ref[...], g_ref[...]).astype(BF16)
    c = _dot(h, win_ref[...])
    cq = _rms_rows(c[:, :Q_LORA], qn_ref[...]).astype(BF16)
    ckv_f32 = _rms_rows(c[:, Q_LORA:Q_LORA + KV_LORA], kvn_ref[...])
    ckv = ckv_f32.astype(BF16)
    kr = c[:, Q_LORA + KV_LORA:Q_LORA + KV_LORA + LANES]
    kr_partner = c[:, Q_LORA + KV_LORA + LANES:]
    q = _dot(cq, wq_ref[...])
    q_partner = _dot(cq, wqs_ref[...])
    kn = _dot(ckv, wk_ref[...])
    vt_ref[...] = _dot(wvt_ref[...], ckv_f32.T.astype(BF16)).astype(BF16)

    lane = lax.broadcasted_iota(jnp.int32, (1, LANES), 1)
    ang = pos_ref[...].astype(F32) * freq_ref[...]
    cos = jnp.cos(ang)
    sin = jnp.sin(ang)
    half = QK_ROPE // 2
    sin_signed = jnp.where(lane < QK_NOPE + half, -sin, sin)

    root_d = math.sqrt(B_DQK)
    q_const = root_d * B_DQK ** -0.5 * LOG2E
    k_rope = (kr * krg_ref[...] * cos + kr_partner * krgs_ref[...] * sin_signed) * root_d
    k_gain = kg_ref[...] * root_d
    ss_rope = jnp.sum(kr * kr, axis=-1, keepdims=True) + B_DQK * EPS
    q_cos = qg_ref[...] * cos * q_const
    q_sin = qgs_ref[...] * sin_signed * q_const
    for hd in range(B_HEADS):
        g = slice(hd * LANES, (hd + 1) * LANES)
        qh = q[:, g]
        r = lax.rsqrt(jnp.sum(qh * qh, axis=-1, keepdims=True) + B_DQK * EPS)
        q_ref[:, g] = ((qh * q_cos + q_partner[:, g] * q_sin) * r).astype(BF16)
        kh = kn[:, g]
        rk = lax.rsqrt(jnp.sum(kh * kh, axis=-1, keepdims=True) + ss_rope)
        k_ref[:, g] = ((kh * k_gain + k_rope) * rk).astype(BF16)


def _b_proj(x, pos_col, gain, w_in, qn, kvn, wq, wqs, wk, wvt, qg, qgs, kg, krg, krgs, freq):
    tm = B_TM
    tiles_per_seq = SEQ // tm
    lane_vec = _const_spec((1, LANES))
    out_spec = pl.BlockSpec((tm, B_HW), lambda i: (i, 0))
    out_sds = jax.ShapeDtypeStruct((TOKENS, B_HW), BF16)
    vt_spec = pl.BlockSpec((None, B_VW, tm), lambda i: (i // tiles_per_seq, 0, i % tiles_per_seq))
    return pl.pallas_call(
        _b_proj_kernel,
        grid=(TOKENS // tm,),
        in_specs=[pl.BlockSpec((tm, D_MODEL), lambda i: (i, 0)), pl.BlockSpec((tm, 1), lambda i: (i, 0)),
                  _const_spec((1, D_MODEL)), _const_spec((D_MODEL, B_CW)), _const_spec((1, Q_LORA)),
                  _const_spec((1, KV_LORA)), _const_spec((Q_LORA, B_HW)), _const_spec((Q_LORA, B_HW)),
                  _const_spec((KV_LORA, B_HW)), _const_spec((B_VW, KV_LORA)),
                  lane_vec, lane_vec, lane_vec, lane_vec, lane_vec, lane_vec],
        out_specs=[out_spec, out_spec, vt_spec],
        out_shape=[out_sds, out_sds, jax.ShapeDtypeStruct((BATCH, B_VW, SEQ), BF16)],
        compiler_params=pltpu.CompilerParams(vmem_limit_bytes=VMEM_LIMIT),
        name="b_proj",
    )(x, pos_col, gain, w_in, qn, kvn, wq, wqs, wk, wvt, qg, qgs, kg, krg, krgs, freq)


B_LOOKAHEAD = 3


def _b_attn_kernel(q_ref, k_ref, vt_ref, o_ref):
    groups = [slice(parity * LANES, (parity + 1) * LANES) for parity in range(2)]
    n_query_tiles = SEQ // B_TQ
    half = B_TQ // 2

    def chain(qt, hd):
        pieces = [(qt, hd, j * B_TK, B_TK, 0, B_TQ) for j in range(qt * B_TQ // B_TK)]
        return pieces + [(qt, hd, qt * B_TQ, half, 0, B_TQ), (qt, hd, qt * B_TQ + half, half, half, half)]

    def scores(qt, hd, key_lo, n_keys, q_lo, n_q):
        queries = slice(qt * B_TQ + q_lo, qt * B_TQ + q_lo + n_q)
        return _dot_nt(k_ref[key_lo:key_lo + n_keys, groups[hd]], q_ref[queries, groups[hd]])

    def accumulate(unit, s, carry):
        qt, hd, key_lo, n_keys, q_lo, n_q = unit
        m_all, acc_all = carry
        m, acc = m_all[:, q_lo:q_lo + n_q], acc_all[:, q_lo:q_lo + n_q]
        first_query = qt * B_TQ + q_lo
        if key_lo + n_keys - 1 > first_query:
            key = lax.broadcasted_iota(jnp.int32, (n_keys, n_q), 0)
            query = lax.broadcasted_iota(jnp.int32, (n_keys, n_q), 1)
            s = jnp.where(key + (key_lo - first_query) <= query, s, NEG)
        m_new = jnp.maximum(m, jnp.max(s, axis=0, keepdims=True))
        alpha = jnp.exp2(m - m_new)
        e = jnp.exp2(s - m_new).astype(BF16)
        vt_ext = jnp.concatenate([vt_ref[hd * V_DIM:(hd + 1) * V_DIM, key_lo:key_lo + n_keys],
                                  jnp.ones((V_DIM, n_keys), BF16)], axis=0)
        acc_new = alpha * acc + _dot(vt_ext, e)
        if q_lo:
            m_new = jnp.concatenate([m_all[:, :q_lo], m_new], axis=1)
            acc_new = jnp.concatenate([acc_all[:, :q_lo], acc_new], axis=1)
        return m_new, acc_new

    chains = [chain(qt, hd) for qt in reversed(range(n_query_tiles)) for hd in range(2)]
    units = [c[i] for i in range(len(chains[0])) for c in chains if i < len(c)]
    last = {c[-1] for c in chains}
    init = (jnp.full((1, B_TQ), NEG, F32), jnp.zeros((2 * V_DIM, B_TQ), F32))
    carry = {(qt, hd): init for qt in range(n_query_tiles) for hd in range(2)}
    raw = {u: scores(*u) for u in units[:B_LOOKAHEAD]}
    for i, unit in enumerate(units):
        qt, hd = unit[0], unit[1]
        if i + B_LOOKAHEAD < len(units):
            ahead = units[i + B_LOOKAHEAD]
            raw[ahead] = scores(*ahead)
        carry[(qt, hd)] = accumulate(unit, raw.pop(unit), carry[(qt, hd)])
        if hd == 1 and unit in last:
            outs = [acc[:V_DIM, :] * (1.0 / acc[V_DIM:V_DIM + 1, :])
                    for _, acc in (carry.pop((qt, 0)), carry.pop((qt, 1)))]
            o_ref[qt * B_TQ:(qt + 1) * B_TQ, :] = jnp.concatenate(outs, axis=0).T.astype(BF16)


def _b_attn(q, k, vt):
    pairs = B_HEADS // 2
    return pl.pallas_call(
        _b_attn_kernel,
        grid=(BATCH, pairs),
        in_specs=[pl.BlockSpec((SEQ, 2 * LANES), lambda b, p: (b, p)),
                  pl.BlockSpec((SEQ, 2 * LANES), lambda b, p: (b, p)),
                  pl.BlockSpec((None, 2 * V_DIM, SEQ), lambda b, p: (b, p, 0))],
        out_specs=pl.BlockSpec((SEQ, 2 * V_DIM), lambda b, p: (b, p)),
        out_shape=jax.ShapeDtypeStruct((TOKENS, B_VW), BF16),
        compiler_params=pltpu.CompilerParams(vmem_limit_bytes=VMEM_LIMIT),
        name="b_attn",
    )(q, k, vt)


def _lohi(cols):
    z = jnp.zeros_like(cols)
    return jnp.concatenate([cols, z, z, cols], axis=1)


def _prep_a(w_in, q_gain, k_gain):
    wq = w_in[:, :A_QW]
    k0 = w_in[:, A_QW:A_QW + A_HEAD_DIM]
    k1 = w_in[:, A_QW + A_HEAD_DIM:A_QW + 2 * A_HEAD_DIM]
    v0 = w_in[:, A_QW + 2 * A_HEAD_DIM:A_QW + 3 * A_HEAD_DIM]
    v1 = w_in[:, A_QW + 3 * A_HEAD_DIM:]
    w = jnp.concatenate([wq, _lohi(k0), _lohi(k1), _lohi(v0), _lohi(v1)], axis=1).astype(BF16)
    qg = jnp.concatenate([q_gain, q_gain])[None, :]
    z = jnp.zeros_like(k_gain)
    kg = jnp.concatenate([k_gain, z, z, k_gain, k_gain, z, z, k_gain])[None, :]
    return w, qg, kg


def _head_groups(w, per_head, src_lo, src_hi, dst_lo):
    rows = w.shape[0]
    w3 = w.reshape(rows, B_HEADS, per_head)[:, :, src_lo:src_hi]
    out = jnp.zeros((rows, B_HEADS, LANES), w.dtype)
    out = out.at[:, :, dst_lo:dst_lo + (src_hi - src_lo)].set(w3)
    return out


def _prep_b(w_in, w_uq, w_ukv, q_gain, k_gain):
    half = QK_ROPE // 2
    t1 = slice(QK_NOPE, QK_NOPE + half)
    t2 = slice(QK_NOPE + half, B_DQK)
    rows = w_in.shape[0]
    rope_cols = w_in[:, Q_LORA + KV_LORA:]
    zeros = lambda n: jnp.zeros((rows, n), w_in.dtype)
    rope_group = jnp.concatenate([zeros(QK_NOPE), rope_cols, zeros(LANES - B_DQK)], axis=1)
    partner_group = jnp.concatenate(
        [zeros(QK_NOPE), rope_cols[:, half:], rope_cols[:, :half], zeros(LANES - B_DQK)], axis=1)
    win = jnp.concatenate([w_in[:, :Q_LORA + KV_LORA], rope_group, partner_group], axis=1).astype(BF16)

    wq = _head_groups(w_uq, B_DQK, 0, B_DQK, 0)
    wqs = (_head_groups(w_uq, B_DQK, t2.start, t2.stop, t1.start)
           + _head_groups(w_uq, B_DQK, t1.start, t1.stop, t2.start))
    wk = _head_groups(w_ukv, QK_NOPE + V_DIM, 0, QK_NOPE, 0)
    wvt = w_ukv.reshape(KV_LORA, B_HEADS, QK_NOPE + V_DIM)[:, :, QK_NOPE:].reshape(KV_LORA, B_VW).T.astype(BF16)
    flat = lambda a: a.reshape(a.shape[0], B_HW).astype(BF16)

    def lane_vec(pieces):
        out = jnp.zeros((LANES,), F32)
        for lo, vals in pieces:
            out = out.at[lo:lo + vals.shape[0]].set(vals)
        return out[None, :]

    qg = lane_vec([(0, q_gain)])
    qgs = lane_vec([(t1.start, q_gain[t2]), (t2.start, q_gain[t1])])
    kg = lane_vec([(0, k_gain[:QK_NOPE])])
    krg = lane_vec([(QK_NOPE, k_gain[QK_NOPE:])])
    krgs = lane_vec([(t1.start, k_gain[t2]), (t2.start, k_gain[t1])])
    return win, flat(wq), flat(wqs), flat(wk), wvt, qg, qgs, kg, krg, krgs


def _rope_freq():
    inv = ROPE_BASE ** (-np.arange(0, QK_ROPE, 2, dtype=np.float32) / QK_ROPE)
    out = np.zeros((1, LANES), np.float32)
    half = QK_ROPE // 2
    out[0, QK_NOPE:QK_NOPE + half] = inv
    out[0, QK_NOPE + half:B_DQK] = inv
    return jnp.asarray(out)


def kernel(x, positions, rel_bias, ffn_norm1, ffn1_wg, ffn1_wu, ffn1_wd, mix_norm, ffn_norm2, ffn2_wg,
           ffn2_wu, ffn2_wd, a_w_in, a_q_gain, a_k_gain, a_sinks, a_w_out, b_w_in, b_q_norm, b_kv_norm,
           b_w_uq, b_w_ukv, b_q_gain, b_k_gain, b_w_out):
    assert x.shape == (BATCH, SEQ, D_MODEL) and positions.shape == (BATCH, SEQ)
    xt = x.reshape(TOKENS, D_MODEL)
    pos_col = positions.reshape(TOKENS, 1)
    pos_row = positions.reshape(TOKENS // BLOCK, 1, BLOCK)
    table = jnp.zeros((A_HEADS, LANES), F32).at[:, :NUM_BUCKETS].set(rel_bias.T)
    bf = lambda w: w.astype(BF16)
    row = lambda v: v[None, :]
    gain1, gain2 = ffn_norm1[:, None, :], ffn_norm2[:, None, :]
    weights = (bf(ffn1_wg[0]), bf(ffn1_wu[0]), bf(ffn1_wd[0]))

    for i in range(DEPTH):
        xt, weights = _ffn(xt, gain1, i, *weights, cast_ahead=(i, ffn2_wg, ffn2_wu, ffn2_wd))
        j = i // N_MIXERS
        if i % N_MIXERS == 0:
            w, qg, kg = _prep_a(a_w_in[j], a_q_gain[j], a_k_gain[j])
            q, k4, v4 = _a_qkv(xt, row(mix_norm[i]), w, qg, kg)
            attn = _a_attn(q, k4, v4, pos_col, pos_row, table, a_sinks[j])
            w_out = bf(a_w_out[j])
        else:
            prep = _prep_b(b_w_in[j], b_w_uq[j], b_w_ukv[j], b_q_gain[j], b_k_gain[j])
            win, wq, wqs, wk, wvt, qg, qgs, kg, krg, krgs = prep
            q, k, vt = _b_proj(xt, pos_col, row(mix_norm[i]), win, row(b_q_norm[j]), row(b_kv_norm[j]),
                              wq, wqs, wk, wvt, qg, qgs, kg, krg, krgs, _rope_freq())
            attn = _b_attn(q, k, vt)
            w_out = bf(b_w_out[j])
        ahead = (i + 1, ffn1_wg, ffn1_wu, ffn1_wd) if i + 1 < DEPTH else None
        xt, weights = _ffn(xt, gain2, i, *weights, attn=attn, w_out=w_out, cast_ahead=ahead)
    return xt.reshape(BATCH, SEQ, D_MODEL)
```

```python
import functools
import math

import numpy as np
import jax
import jax.numpy as jnp
from jax import lax
from jax.experimental import pallas as pl
from jax.experimental.pallas import tpu as pltpu

D_MODEL = 1024
BATCH = 8
SEQ = 2048
DEPTH = 2
N_MIXERS = 2
A_HEADS = 16
A_KV_HEADS = 2
A_HEAD_DIM = 64
WINDOW = 128
BLOCK = 128
NUM_BUCKETS = 32
MAX_DISTANCE = 128
B_HEADS = 16
Q_LORA = 256
KV_LORA = 128
QK_NOPE = 64
QK_ROPE = 32
V_DIM = 64
ROPE_BASE = 10000.0
D_FF = 2816
EPS = 1e-6
NEG = -1e30

TOKENS = BATCH * SEQ
LANES = 128
HALF = LANES // 2
B_DQK = QK_NOPE + QK_ROPE
VMEM_LIMIT = 60 * 1024 * 1024

LOG2E = math.log2(math.e)

F32 = jnp.float32
BF16 = jnp.bfloat16


def _rms_rows(x, gain):
    return x * lax.rsqrt(jnp.mean(x * x, axis=-1, keepdims=True) + EPS) * gain


def _dot(a, b):
    return jnp.dot(a, b, preferred_element_type=F32)


def _dot_nt(a, b):
    return lax.dot_general(a, b, (((1,), (1,)), ((), ())), preferred_element_type=F32)


def _const_spec(shape):
    nd = len(shape)
    return pl.BlockSpec(shape, lambda *_: (0,) * nd, pipeline_mode=pl.Buffered(1))


FFN_TM = 1024
MXU_WIDTH = 256
FFN_CHUNKS = ((0, 4 * MXU_WIDTH), (4 * MXU_WIDTH, 8 * MXU_WIDTH), (8 * MXU_WIDTH, D_FF))


def _ffn_body(x, g_ref, wgu_ref, wd_ref, o_ref):
    h = _rms_rows(x, g_ref[...]).astype(BF16)
    down = None
    for lo, hi in FFN_CHUNKS:
        gate_up = _dot(h, wgu_ref[:, 2 * lo:2 * hi])
        acts = []
        for c in range((hi - lo) // MXU_WIDTH):
            gate = gate_up[:, 2 * c * MXU_WIDTH:(2 * c + 1) * MXU_WIDTH]
            up = gate_up[:, (2 * c + 1) * MXU_WIDTH:(2 * c + 2) * MXU_WIDTH]
            acts.append((gate * jax.nn.sigmoid(gate) * up).astype(BF16))
        part = _dot(jnp.concatenate(acts, axis=1), wd_ref[lo:hi, :])
        down = part if down is None else down + part
    o_ref[...] = x + 0.5 * down


def _ffn_kernel(*refs, project, cast_ahead):
    refs = list(refs)
    x = refs.pop(0)[...]
    if project:
        a_ref, wo_ref = refs.pop(0), refs.pop(0)
        x = x + _dot(a_ref[...], wo_ref[...])
    g_ref, wgu_ref, wd_ref = refs[:3]
    refs = refs[3:]
    if cast_ahead:
        (wg_src, wu_src, wd_src), (wgu_dst, wd_dst) = refs[:3], refs[4:6]
        for c in range(D_FF // MXU_WIDTH):
            cols = slice(c * MXU_WIDTH, (c + 1) * MXU_WIDTH)
            wgu_dst[:, 2 * c * MXU_WIDTH:(2 * c + 1) * MXU_WIDTH] = wg_src[:, cols].astype(BF16)
            wgu_dst[:, (2 * c + 1) * MXU_WIDTH:(2 * c + 2) * MXU_WIDTH] = wu_src[:, cols].astype(BF16)
        wd_dst[...] = wd_src[...].astype(BF16)
        refs = refs[3:]
    _ffn_body(x, g_ref, wgu_ref, wd_ref, refs[0])


def _interleave_gate_up(wg, wu):
    blocks = D_FF // MXU_WIDTH
    both = jnp.stack([wg.reshape(D_MODEL, blocks, MXU_WIDTH), wu.reshape(D_MODEL, blocks, MXU_WIDTH)], axis=2)
    return both.reshape(D_MODEL, 2 * D_FF)


def _ffn(x, gain, layer, wgu, wd, attn=None, w_out=None, cast_ahead=None):
    tm = FFN_TM
    steps = TOKENS // tm
    row_spec = pl.BlockSpec((tm, D_MODEL), lambda i: (i, 0))
    in_specs, args = [row_spec], [x]
    if attn is not None:
        in_specs += [row_spec, _const_spec((D_MODEL, D_MODEL))]
        args += [attn, w_out]
    in_specs += [pl.BlockSpec((None, 1, D_MODEL), lambda i: (layer, 0, 0), pipeline_mode=pl.Buffered(1)),
                 _const_spec((D_MODEL, 2 * D_FF)), _const_spec((D_FF, D_MODEL))]
    args += [gain, wgu, wd]
    out_specs = [row_spec]
    out_shape = [jax.ShapeDtypeStruct((TOKENS, D_MODEL), F32)]
    if cast_ahead is not None:
        next_layer, *stacks = cast_ahead
        for w in stacks:
            in_specs.append(pl.BlockSpec((None, w.shape[1] // steps, w.shape[2]), lambda i: (next_layer, i, 0)))
        args += stacks
        out_specs += [pl.BlockSpec((D_MODEL // steps, 2 * D_FF), lambda i: (i, 0)),
                      pl.BlockSpec((D_FF // steps, D_MODEL), lambda i: (i, 0))]
        out_shape += [jax.ShapeDtypeStruct((D_MODEL, 2 * D_FF), BF16), jax.ShapeDtypeStruct((D_FF, D_MODEL), BF16)]
    outs = pl.pallas_call(
        functools.partial(_ffn_kernel, project=attn is not None, cast_ahead=cast_ahead is not None),
        grid=(steps,),
        in_specs=in_specs,
        out_specs=out_specs,
        out_shape=out_shape,
        compiler_params=pltpu.CompilerParams(vmem_limit_bytes=VMEM_LIMIT),
        name="ffn" if attn is None else "proj_ffn",
    )(*args)
    return outs[0], tuple(outs[1:])


A_TM = 512
A_QW = A_HEADS * A_HEAD_DIM
A_KW = 4 * LANES
A_PAIRS = A_HEADS // 2
A_ROWS = 512


def _a_qkv_kernel(x_ref, g_ref, w_ref, qg_ref, kg_ref, q_ref, k_ref, v_ref):
    h = _rms_rows(x_ref[...], g_ref[...]).astype(BF16)
    qkv = _dot(h, w_ref[...])
    lane = lax.broadcasted_iota(jnp.int32, (1, LANES), 1)
    low = lane < HALF
    q_table = qg_ref[...] * (math.sqrt(A_HEAD_DIM) * A_HEAD_DIM ** -0.5 * LOG2E)
    for p in range(A_PAIRS):
        qp = qkv[:, p * LANES:(p + 1) * LANES]
        sq = qp * qp
        ss_lo = jnp.sum(jnp.where(low, sq, 0.0), axis=-1, keepdims=True)
        ss_hi = jnp.sum(jnp.where(low, 0.0, sq), axis=-1, keepdims=True)
        r = jnp.where(low, lax.rsqrt(ss_lo + A_HEAD_DIM * EPS), lax.rsqrt(ss_hi + A_HEAD_DIM * EPS))
        q_ref[:, p * LANES:(p + 1) * LANES] = (qp * q_table * r).astype(BF16)
    for j in range(A_KW // LANES):
        kj = qkv[:, A_QW + j * LANES:A_QW + (j + 1) * LANES]
        ms = jnp.sum(kj * kj, axis=-1, keepdims=True) * (1.0 / A_HEAD_DIM)
        k_ref[:, j * LANES:(j + 1) * LANES] = (
            kj * lax.rsqrt(ms + EPS) * kg_ref[:, j * LANES:(j + 1) * LANES]).astype(BF16)
    v_ref[...] = qkv[:, A_QW + A_KW:].astype(BF16)


def _a_qkv(x, gain, w, qg, kg):
    tm = A_TM
    width = A_QW + 2 * A_KW
    return pl.pallas_call(
        _a_qkv_kernel,
        grid=(TOKENS // tm,),
        in_specs=[pl.BlockSpec((tm, D_MODEL), lambda i: (i, 0)), _const_spec((1, D_MODEL)),
                  _const_spec((D_MODEL, width)), _const_spec((1, LANES)), _const_spec((1, A_KW))],
        out_specs=[pl.BlockSpec((tm, A_QW), lambda i: (i, 0)), pl.BlockSpec((tm, A_KW), lambda i: (i, 0)),
                   pl.BlockSpec((tm, A_KW), lambda i: (i, 0))],
        out_shape=[jax.ShapeDtypeStruct((TOKENS, A_QW), BF16), jax.ShapeDtypeStruct((TOKENS, A_KW), BF16),
                   jax.ShapeDtypeStruct((TOKENS, A_KW), BF16)],
        compiler_params=pltpu.CompilerParams(vmem_limit_bytes=VMEM_LIMIT),
        name="a_qkv",
    )(x, gain, w, qg, kg)


def _t5_bucket(dist):
    n = jnp.maximum(dist, 0)
    max_exact = NUM_BUCKETS // 2
    large = max_exact + (jnp.log(jnp.maximum(n, 1).astype(F32) / max_exact)
                         / math.log(MAX_DISTANCE / max_exact)
                         * (NUM_BUCKETS - max_exact)).astype(jnp.int32)
    large = jnp.minimum(large, NUM_BUCKETS - 1)
    return jnp.where(n < max_exact, n, large)


SUBLANES = 8


def _a_attn_kernel(sink_ref, q_ref, kc_ref, kp_ref, vc_ref, vp_ref, pq_ref, pkc_ref, pkp_ref, tbl_ref, o_ref,
                   bias_ref, shared_ref):
    row = lax.broadcasted_iota(jnp.int32, (BLOCK, BLOCK), 0)
    col = lax.broadcasted_iota(jnp.int32, (BLOCK, BLOCK), 1)
    cur_ok = col <= row
    first_in_seq = pl.program_id(1) == 0
    no_prev = jnp.where(first_in_seq, NEG, 0.0)
    tables = [jnp.broadcast_to(tbl_ref[h:h + 1, :] * LOG2E, (SUBLANES, LANES)) for h in range(A_HEADS)]
    ones = jnp.ones((2 * BLOCK, LANES), BF16)
    group_heads = A_HEADS // A_KV_HEADS
    lane = lax.broadcasted_iota(jnp.int32, (1, LANES), 1)

    @pl.when(jnp.logical_and(pl.program_id(0) == 0, first_in_seq))
    def _():
        shared_ref[0] = 0

    def fill_shift_invariant():
        back = jnp.broadcast_to((BLOCK - lane) & (BLOCK - 1), (SUBLANES, LANES))
        idx = _t5_bucket(back)
        for h in range(A_HEADS):
            base = jnp.take_along_axis(tables[h], idx, axis=1)
            base = jnp.broadcast_to(base[0:1, :], (BLOCK, BLOCK))
            tile = pltpu.roll(base, 0, 1, stride=1, stride_axis=0)
            bias_ref[h] = jnp.where(cur_ok, tile, NEG)
            bias_ref[A_HEADS + h] = jnp.where(cur_ok, NEG, tile)

    def fill_general(pos_q, pos_cur, pos_prev):
        bucket = jnp.where(cur_ok, _t5_bucket(pos_q - pos_cur), _t5_bucket(pos_q - pos_prev))
        for c in range(BLOCK // SUBLANES):
            chunk = slice(c * SUBLANES, (c + 1) * SUBLANES)
            for h in range(A_HEADS):
                piece = jnp.take_along_axis(tables[h], bucket[chunk, :], axis=1)
                bias_ref[h, chunk, :] = jnp.where(cur_ok[chunk, :], piece, NEG)
                bias_ref[A_HEADS + h, chunk, :] = jnp.where(cur_ok[chunk, :], NEG, piece)

    blocks = range(A_ROWS // BLOCK)

    def rows_of(r):
        return slice(r * BLOCK, (r + 1) * BLOCK)

    def pairs_of(kv):
        return range(kv * group_heads // 2, (kv + 1) * group_heads // 2)

    def band(r):
        if r == 0:
            k_prev, v_prev, pos_prev = kp_ref[...], vp_ref[...], pkp_ref[0]
        else:
            k_prev, v_prev, pos_prev = kc_ref[rows_of(r - 1), :], vc_ref[rows_of(r - 1), :], pkc_ref[r - 1]
        k_band = jnp.concatenate([k_prev, kc_ref[rows_of(r), :]], axis=0)
        v_band = jnp.concatenate([v_prev, vc_ref[rows_of(r), :]], axis=0)
        return k_band, v_band, pos_prev

    def scores(r, kv, parity, bands):
        group = slice((2 * kv + parity) * LANES, (2 * kv + parity + 1) * LANES)
        q_stack = jnp.concatenate([q_ref[rows_of(r), p * LANES:(p + 1) * LANES] for p in pairs_of(kv)], axis=0)
        return _dot_nt(q_stack, bands[r][0][:, group])

    def finish(r, kv, parity, s_all, bias, bands):
        v_band = bands[r][1]
        group = slice((2 * kv + parity) * LANES, (2 * kv + parity + 1) * LANES)
        probs, sink_terms = [], []
        for i, p in enumerate(pairs_of(kv)):
            head = 2 * p + parity
            s = s_all[i * BLOCK:(i + 1) * BLOCK, :]
            s_prev = s[:, :BLOCK] + bias[A_HEADS + head]
            if r == 0:
                s_prev = s_prev + no_prev
            s_cur = s[:, BLOCK:] + bias[head]
            sink = sink_ref[head] * LOG2E
            m = jnp.maximum(jnp.max(jnp.maximum(s_prev, s_cur), axis=-1, keepdims=True), sink)
            probs.append(jnp.concatenate([jnp.exp2(s_prev - m), jnp.exp2(s_cur - m)], axis=1).astype(BF16))
            sink_terms.append(jnp.exp2(sink - m))
        v_ext = jnp.concatenate([v_band[:, group], ones], axis=1)
        out = _dot(jnp.concatenate(probs, axis=0), v_ext)
        scaled = []
        for i in range(len(probs)):
            o_i = out[i * BLOCK:(i + 1) * BLOCK, :]
            scaled.append(o_i[:, :LANES] * (1.0 / (o_i[:, LANES:] + sink_terms[i])))
        return scaled

    def attend(which_blocks):
        bias = [bias_ref[h] for h in range(2 * A_HEADS)]
        bands = {r: band(r) for r in which_blocks}
        units = [(r, kv, parity) for r in which_blocks for kv in range(A_KV_HEADS) for parity in range(2)]
        raw = {units[0]: scores(*units[0], bands)}
        done = {}
        for i, unit in enumerate(units):
            if i + 1 < len(units):
                raw[units[i + 1]] = scores(*units[i + 1], bands)
            done[unit] = finish(*unit, raw.pop(unit), bias, bands)
            r, kv, parity = unit
            if parity == 1:
                low, high = done.pop((r, kv, 0)), done.pop((r, kv, 1))
                for i_pair, p in enumerate(pairs_of(kv)):
                    o_ref[rows_of(r), p * LANES:(p + 1) * LANES] = (low[i_pair] + high[i_pair]).astype(BF16)

    @pl.when(shared_ref[0] == 0)
    def _():
        fill_shift_invariant()
        shared_ref[0] = 1

    attend(blocks)

    off = jnp.zeros((1, LANES), jnp.int32)
    for r in blocks:
        pos_cur, pos_prev = pkc_ref[r], band(r)[2]
        start = jnp.min(pos_cur, axis=-1, keepdims=True)
        off_prev = jnp.where(pos_prev - start == lane - BLOCK, 0, 1)
        if r == 0:
            off_prev = off_prev * jnp.where(first_in_seq, 0, 1)
        off = off + jnp.where(pos_cur - start == lane, 0, 1) + off_prev

    @pl.when(jnp.sum(off) != 0)
    def _():
        for r in blocks:
            fill_general(pq_ref[rows_of(r), :], pkc_ref[r], band(r)[2])
            attend([r])
        shared_ref[0] = 0


def _a_attn(q, k4, v4, pos_col, pos_row, table, sinks):
    steps = SEQ // A_ROWS
    blocks_per_step = A_ROWS // BLOCK
    blocks_per_seq = SEQ // BLOCK

    def cur(b, i):
        return (b * steps + i, 0)

    def prev(b, i):
        return (b * blocks_per_seq + jnp.maximum(i * blocks_per_step - 1, 0), 0)

    def cur3(b, i):
        return (b * steps + i, 0, 0)

    def prev3(b, i):
        return (b * blocks_per_seq + jnp.maximum(i * blocks_per_step - 1, 0), 0, 0)

    return pl.pallas_call(
        _a_attn_kernel,
        grid=(BATCH, steps),
        in_specs=[
            pl.BlockSpec(memory_space=pltpu.SMEM),
            pl.BlockSpec((A_ROWS, A_QW), cur),
            pl.BlockSpec((A_ROWS, A_KW), cur),
            pl.BlockSpec((BLOCK, A_KW), prev),
            pl.BlockSpec((A_ROWS, A_KW), cur),
            pl.BlockSpec((BLOCK, A_KW), prev),
            pl.BlockSpec((A_ROWS, 1), cur),
            pl.BlockSpec((blocks_per_step, 1, BLOCK), cur3),
            pl.BlockSpec((1, 1, BLOCK), prev3),
            _const_spec((A_HEADS, LANES)),
        ],
        out_specs=pl.BlockSpec((A_ROWS, A_QW), cur),
        out_shape=jax.ShapeDtypeStruct((TOKENS, A_QW), BF16),
        scratch_shapes=[pltpu.VMEM((2 * A_HEADS, BLOCK, BLOCK), F32), pltpu.SMEM((1,), jnp.int32)],
        compiler_params=pltpu.CompilerParams(dimension_semantics=("arbitrary", "arbitrary"),
                                             vmem_limit_bytes=VMEM_LIMIT),
        name="a_attn",
    )(sinks, q, k4, k4, v4, v4, pos_col, pos_row, pos_row, table)


B_TM = 256
B_CW = Q_LORA + KV_LORA + 2 * LANES
B_HW = B_HEADS * LANES
B_VW = B_HEADS * V_DIM
B_TQ = 512
B_TK = 512


def _b_proj_kernel(x_ref, pos_ref, g_ref, win_ref, qn_ref, kvn_ref, wq_ref, wqs_ref, wk_ref, wvt_ref,
                   qg_ref, qgs_ref, kg_ref, krg_ref, krgs_ref, freq_ref, q_ref, k_ref, vt_ref):
    h = _rms_rows(x_ref[...], g_ref[...]).astype(BF16)
    c = _dot(h, win_ref[...])
    cq = _rms_rows(c[:, :Q_LORA], qn_ref[...]).astype(BF16)
    ckv_f32 = _rms_rows(c[:, Q_LORA:Q_LORA + KV_LORA], kvn_ref[...])
    ckv = ckv_f32.astype(BF16)
    kr = c[:, Q_LORA + KV_LORA:Q_LORA + KV_LORA + LANES]
    kr_partner = c[:, Q_LORA + KV_LORA + LANES:]
    q = _dot(cq, wq_ref[...])
    q_partner = _dot(cq, wqs_ref[...])
    kn = _dot(ckv, wk_ref[...])
    vt_ref[...] = _dot(wvt_ref[...], ckv_f32.T.astype(BF16)).astype(BF16)

    lane = lax.broadcasted_iota(jnp.int32, (1, LANES), 1)
    ang = pos_ref[...].astype(F32) * freq_ref[...]
    cos = jnp.cos(ang)
    sin = jnp.sin(ang)
    half = QK_ROPE // 2
    sin_signed = jnp.where(lane < QK_NOPE + half, -sin, sin)

    root_d = math.sqrt(B_DQK)
    q_const = root_d * B_DQK ** -0.5 * LOG2E
    k_rope = (kr * krg_ref[...] * cos + kr_partner * krgs_ref[...] * sin_signed) * root_d
    k_gain = kg_ref[...] * root_d
    ss_rope = jnp.sum(kr * kr, axis=-1, keepdims=True) + B_DQK * EPS
    q_cos = qg_ref[...] * cos * q_const
    q_sin = qgs_ref[...] * sin_signed * q_const
    for hd in range(B_HEADS):
        g = slice(hd * LANES, (hd + 1) * LANES)
        qh = q[:, g]
        r = lax.rsqrt(jnp.sum(qh * qh, axis=-1, keepdims=True) + B_DQK * EPS)
        q_ref[:, g] = ((qh * q_cos + q_partner[:, g] * q_sin) * r).astype(BF16)
        kh = kn[:, g]
        rk = lax.rsqrt(jnp.sum(kh * kh, axis=-1, keepdims=True) + ss_rope)
        k_ref[:, g] = ((kh * k_gain + k_rope) * rk).astype(BF16)


def _b_proj(x, pos_col, gain, w_in, qn, kvn, wq, wqs, wk, wvt, qg, qgs, kg, krg, krgs, freq):
    tm = B_TM
    tiles_per_seq = SEQ // tm
    lane_vec = _const_spec((1, LANES))
    out_spec = pl.BlockSpec((tm, B_HW), lambda i: (i, 0))
    out_sds = jax.ShapeDtypeStruct((TOKENS, B_HW), BF16)
    vt_spec = pl.BlockSpec((None, B_VW, tm), lambda i: (i // tiles_per_seq, 0, i % tiles_per_seq))
    return pl.pallas_call(
        _b_proj_kernel,
        grid=(TOKENS // tm,),
        in_specs=[pl.BlockSpec((tm, D_MODEL), lambda i: (i, 0)), pl.BlockSpec((tm, 1), lambda i: (i, 0)),
                  _const_spec((1, D_MODEL)), _const_spec((D_MODEL, B_CW)), _const_spec((1, Q_LORA)),
                  _const_spec((1, KV_LORA)), _const_spec((Q_LORA, B_HW)), _const_spec((Q_LORA, B_HW)),
                  _const_spec((KV_LORA, B_HW)), _const_spec((B_VW, KV_LORA)),
                  lane_vec, lane_vec, lane_vec, lane_vec, lane_vec, lane_vec],
        out_specs=[out_spec, out_spec, vt_spec],
        out_shape=[out_sds, out_sds, jax.ShapeDtypeStruct((BATCH, B_VW, SEQ), BF16)],
        compiler_params=pltpu.CompilerParams(vmem_limit_bytes=VMEM_LIMIT),
        name="b_proj",
    )(x, pos_col, gain, w_in, qn, kvn, wq, wqs, wk, wvt, qg, qgs, kg, krg, krgs, freq)


B_LOOKAHEAD = 3


def _b_attn_kernel(q_ref, k_ref, vt_ref, o_ref):
    groups = [slice(parity * LANES, (parity + 1) * LANES) for parity in range(2)]
    n_query_tiles = SEQ // B_TQ
    half = B_TQ // 2

    def chain(qt, hd):
        pieces = [(qt, hd, j * B_TK, B_TK, 0, B_TQ) for j in range(qt * B_TQ // B_TK)]
        return pieces + [(qt, hd, qt * B_TQ, half, 0, B_TQ), (qt, hd, qt * B_TQ + half, half, half, half)]

    def scores(qt, hd, key_lo, n_keys, q_lo, n_q):
        queries = slice(qt * B_TQ + q_lo, qt * B_TQ + q_lo + n_q)
        return _dot_nt(k_ref[key_lo:key_lo + n_keys, groups[hd]], q_ref[queries, groups[hd]])

    def accumulate(unit, s, carry):
        qt, hd, key_lo, n_keys, q_lo, n_q = unit
        m_all, acc_all = carry
        m, acc = m_all[:, q_lo:q_lo + n_q], acc_all[:, q_lo:q_lo + n_q]
        first_query = qt * B_TQ + q_lo
        if key_lo + n_keys - 1 > first_query:
            key = lax.broadcasted_iota(jnp.int32, (n_keys, n_q), 0)
            query = lax.broadcasted_iota(jnp.int32, (n_keys, n_q), 1)
            s = jnp.where(key + (key_lo - first_query) <= query, s, NEG)
        m_new = jnp.maximum(m, jnp.max(s, axis=0, keepdims=True))
        alpha = jnp.exp2(m - m_new)
        e = jnp.exp2(s - m_new).astype(BF16)
        vt_ext = jnp.concatenate([vt_ref[hd * V_DIM:(hd + 1) * V_DIM, key_lo:key_lo + n_keys],
                                  jnp.ones((V_DIM, n_keys), BF16)], axis=0)
        acc_new = alpha * acc + _dot(vt_ext, e)
        if q_lo:
            m_new = jnp.concatenate([m_all[:, :q_lo], m_new], axis=1)
            acc_new = jnp.concatenate([acc_all[:, :q_lo], acc_new], axis=1)
        return m_new, acc_new

    chains = [chain(qt, hd) for qt in reversed(range(n_query_tiles)) for hd in range(2)]
    units = [c[i] for i in range(len(chains[0])) for c in chains if i < len(c)]
    last = {c[-1] for c in chains}
    init = (jnp.full((1, B_TQ), NEG, F32), jnp.zeros((2 * V_DIM, B_TQ), F32))
    carry = {(qt, hd): init for qt in range(n_query_tiles) for hd in range(2)}
    raw = {u: scores(*u) for u in units[:B_LOOKAHEAD]}
    for i, unit in enumerate(units):
        qt, hd = unit[0], unit[1]
        if i + B_LOOKAHEAD < len(units):
            ahead = units[i + B_LOOKAHEAD]
            raw[ahead] = scores(*ahead)
        carry[(qt, hd)] = accumulate(unit, raw.pop(unit), carry[(qt, hd)])
        if hd == 1 and unit in last:
            outs = [acc[:V_DIM, :] * (1.0 / acc[V_DIM:V_DIM + 1, :])
                    for _, acc in (carry.pop((qt, 0)), carry.pop((qt, 1)))]
            o_ref[qt * B_TQ:(qt + 1) * B_TQ, :] = jnp.concatenate(outs, axis=0).T.astype(BF16)


def _b_attn(q, k, vt):
    pairs = B_HEADS // 2
    return pl.pallas_call(
        _b_attn_kernel,
        grid=(BATCH, pairs),
        in_specs=[pl.BlockSpec((SEQ, 2 * LANES), lambda b, p: (b, p)),
                  pl.BlockSpec((SEQ, 2 * LANES), lambda b, p: (b, p)),
                  pl.BlockSpec((None, 2 * V_DIM, SEQ), lambda b, p: (b, p, 0))],
        out_specs=pl.BlockSpec((SEQ, 2 * V_DIM), lambda b, p: (b, p)),
        out_shape=jax.ShapeDtypeStruct((TOKENS, B_VW), BF16),
        compiler_params=pltpu.CompilerParams(vmem_limit_bytes=VMEM_LIMIT),
        name="b_attn",
    )(q, k, vt)


def _lohi(cols):
    z = jnp.zeros_like(cols)
    return jnp.concatenate([cols, z, z, cols], axis=1)


def _prep_a(w_in, q_gain, k_gain):
    wq = w_in[:, :A_QW]
    k0 = w_in[:, A_QW:A_QW + A_HEAD_DIM]
    k1 = w_in[:, A_QW + A_HEAD_DIM:A_QW + 2 * A_HEAD_DIM]
    v0 = w_in[:, A_QW + 2 * A_HEAD_DIM:A_QW + 3 * A_HEAD_DIM]
    v1 = w_in[:, A_QW + 3 * A_HEAD_DIM:]
    w = jnp.concatenate([wq, _lohi(k0), _lohi(k1), _lohi(v0), _lohi(v1)], axis=1).astype(BF16)
    qg = jnp.concatenate([q_gain, q_gain])[None, :]
    z = jnp.zeros_like(k_gain)
    kg = jnp.concatenate([k_gain, z, z, k_gain, k_gain, z, z, k_gain])[None, :]
    return w, qg, kg


def _head_groups(w, per_head, src_lo, src_hi, dst_lo):
    rows = w.shape[0]
    w3 = w.reshape(rows, B_HEADS, per_head)[:, :, src_lo:src_hi]
    out = jnp.zeros((rows, B_HEADS, LANES), w.dtype)
    out = out.at[:, :, dst_lo:dst_lo + (src_hi - src_lo)].set(w3)
    return out


def _prep_b(w_in, w_uq, w_ukv, q_gain, k_gain):
    half = QK_ROPE // 2
    t1 = slice(QK_NOPE, QK_NOPE + half)
    t2 = slice(QK_NOPE + half, B_DQK)
    rows = w_in.shape[0]
    rope_cols = w_in[:, Q_LORA + KV_LORA:]
    zeros = lambda n: jnp.zeros((rows, n), w_in.dtype)
    rope_group = jnp.concatenate([zeros(QK_NOPE), rope_cols, zeros(LANES - B_DQK)], axis=1)
    partner_group = jnp.concatenate(
        [zeros(QK_NOPE), rope_cols[:, half:], rope_cols[:, :half], zeros(LANES - B_DQK)], axis=1)
    win = jnp.concatenate([w_in[:, :Q_LORA + KV_LORA], rope_group, partner_group], axis=1).astype(BF16)

    wq = _head_groups(w_uq, B_DQK, 0, B_DQK, 0)
    wqs = (_head_groups(w_uq, B_DQK, t2.start, t2.stop, t1.start)
           + _head_groups(w_uq, B_DQK, t1.start, t1.stop, t2.start))
    wk = _head_groups(w_ukv, QK_NOPE + V_DIM, 0, QK_NOPE, 0)
    wvt = w_ukv.reshape(KV_LORA, B_HEADS, QK_NOPE + V_DIM)[:, :, QK_NOPE:].reshape(KV_LORA, B_VW).T.astype(BF16)
    flat = lambda a: a.reshape(a.shape[0], B_HW).astype(BF16)

    def lane_vec(pieces):
        out = jnp.zeros((LANES,), F32)
        for lo, vals in pieces:
            out = out.at[lo:lo + vals.shape[0]].set(vals)
        return out[None, :]

    qg = lane_vec([(0, q_gain)])
    qgs = lane_vec([(t1.start, q_gain[t2]), (t2.start, q_gain[t1])])
    kg = lane_vec([(0, k_gain[:QK_NOPE])])
    krg = lane_vec([(QK_NOPE, k_gain[QK_NOPE:])])
    krgs = lane_vec([(t1.start, k_gain[t2]), (t2.start, k_gain[t1])])
    return win, flat(wq), flat(wqs), flat(wk), wvt, qg, qgs, kg, krg, krgs


def _rope_freq():
    inv = ROPE_BASE ** (-np.arange(0, QK_ROPE, 2, dtype=np.float32) / QK_ROPE)
    out = np.zeros((1, LANES), np.float32)
    half = QK_ROPE // 2
    out[0, QK_NOPE:QK_NOPE + half] = inv
    out[0, QK_NOPE + half:B_DQK] = inv
    return jnp.asarray(out)


def kernel(x, positions, rel_bias, ffn_norm1, ffn1_wg, ffn1_wu, ffn1_wd, mix_norm, ffn_norm2, ffn2_wg,
           ffn2_wu, ffn2_wd, a_w_in, a_q_gain, a_k_gain, a_sinks, a_w_out, b_w_in, b_q_norm, b_kv_norm,
           b_w_uq, b_w_ukv, b_q_gain, b_k_gain, b_w_out):
    assert x.shape == (BATCH, SEQ, D_MODEL) and positions.shape == (BATCH, SEQ)
    xt = x.reshape(TOKENS, D_MODEL)
    pos_col = positions.reshape(TOKENS, 1)
    pos_row = positions.reshape(TOKENS // BLOCK, 1, BLOCK)
    table = jnp.zeros((A_HEADS, LANES), F32).at[:, :NUM_BUCKETS].set(rel_bias.T)
    bf = lambda w: w.astype(BF16)
    row = lambda v: v[None, :]
    gain1, gain2 = ffn_norm1[:, None, :], ffn_norm2[:, None, :]
    weights = (_interleave_gate_up(bf(ffn1_wg[0]), bf(ffn1_wu[0])), bf(ffn1_wd[0]))

    for i in range(DEPTH):
        xt, weights = _ffn(xt, gain1, i, *weights, cast_ahead=(i, ffn2_wg, ffn2_wu, ffn2_wd))
        j = i // N_MIXERS
        if i % N_MIXERS == 0:
            w, qg, kg = _prep_a(a_w_in[j], a_q_gain[j], a_k_gain[j])
            q, k4, v4 = _a_qkv(xt, row(mix_norm[i]), w, qg, kg)
            attn = _a_attn(q, k4, v4, pos_col, pos_row, table, a_sinks[j])
            w_out = bf(a_w_out[j])
        else:
            prep = _prep_b(b_w_in[j], b_w_uq[j], b_w_ukv[j], b_q_gain[j], b_k_gain[j])
            win, wq, wqs, wk, wvt, qg, qgs, kg, krg, krgs = prep
            q, k, vt = _b_proj(xt, pos_col, row(mix_norm[i]), win, row(b_q_norm[j]), row(b_kv_norm[j]),
                              wq, wqs, wk, wvt, qg, qgs, kg, krg, krgs, _rope_freq())
            attn = _b_attn(q, k, vt)
            w_out = bf(b_w_out[j])
        ahead = (i + 1, ffn1_wg, ffn1_wu, ffn1_wd) if i + 1 < DEPTH else None
        xt, weights = _ffn(xt, gain2, i, *weights, attn=attn, w_out=w_out, cast_ahead=ahead)
    return xt.reshape(BATCH, SEQ, D_MODEL)
```

```python
import functools
import math

import numpy as np
import jax
import jax.numpy as jnp
from jax import lax
from jax.experimental import pallas as pl
from jax.experimental.pallas import tpu as pltpu

D_MODEL = 1024
BATCH = 8
SEQ = 2048
DEPTH = 2
N_MIXERS = 2
A_HEADS = 16
A_KV_HEADS = 2
A_HEAD_DIM = 64
WINDOW = 128
BLOCK = 128
NUM_BUCKETS = 32
MAX_DISTANCE = 128
B_HEADS = 16
Q_LORA = 256
KV_LORA = 128
QK_NOPE = 64
QK_ROPE = 32
V_DIM = 64
ROPE_BASE = 10000.0
D_FF = 2816
EPS = 1e-6
NEG = -1e30

TOKENS = BATCH * SEQ
LANES = 128
HALF = LANES // 2
B_DQK = QK_NOPE + QK_ROPE
VMEM_LIMIT = 60 * 1024 * 1024

LOG2E = math.log2(math.e)

F32 = jnp.float32
BF16 = jnp.bfloat16


def _rms_rows(x, gain):
    return x * lax.rsqrt(jnp.mean(x * x, axis=-1, keepdims=True) + EPS) * gain


def _dot(a, b):
    return jnp.dot(a, b, preferred_element_type=F32)


def _dot_nt(a, b):
    return lax.dot_general(a, b, (((1,), (1,)), ((), ())), preferred_element_type=F32)


def _const_spec(shape):
    nd = len(shape)
    return pl.BlockSpec(shape, lambda *_: (0,) * nd, pipeline_mode=pl.Buffered(1))


FFN_TM = 1024
MXU_WIDTH = 256
FFN_CHUNKS = ((0, 4 * MXU_WIDTH), (4 * MXU_WIDTH, 8 * MXU_WIDTH), (8 * MXU_WIDTH, D_FF))


def _ffn_body(x, g_ref, wgu_ref, wd_ref, o_ref):
    h = _rms_rows(x, g_ref[...]).astype(BF16)
    down = None
    for lo, hi in FFN_CHUNKS:
        gate_up = _dot(h, wgu_ref[:, 2 * lo:2 * hi])
        acts = []
        for c in range((hi - lo) // MXU_WIDTH):
            gate = gate_up[:, 2 * c * MXU_WIDTH:(2 * c + 1) * MXU_WIDTH]
            up = gate_up[:, (2 * c + 1) * MXU_WIDTH:(2 * c + 2) * MXU_WIDTH]
            acts.append((gate * jax.nn.sigmoid(gate) * up).astype(BF16))
        part = _dot(jnp.concatenate(acts, axis=1), wd_ref[lo:hi, :])
        down = part if down is None else down + part
    o_ref[...] = x + 0.5 * down


def _ffn_kernel(*refs, project, cast_ahead):
    refs = list(refs)
    x = refs.pop(0)[...]
    if project:
        a_ref, wo_ref = refs.pop(0), refs.pop(0)
        x = x + _dot(a_ref[...], wo_ref[...])
    g_ref, wgu_ref, wd_ref = refs[:3]
    refs = refs[3:]
    if cast_ahead:
        _cast_ffn_weights(*refs[:3], *refs[4:6])
        refs = refs[3:]
    _ffn_body(x, g_ref, wgu_ref, wd_ref, refs[0])


def _cast_ffn_weights(wg_src, wu_src, wd_src, wgu_dst, wd_dst):
    for c in range(D_FF // MXU_WIDTH):
        cols = slice(c * MXU_WIDTH, (c + 1) * MXU_WIDTH)
        wgu_dst[:, 2 * c * MXU_WIDTH:(2 * c + 1) * MXU_WIDTH] = wg_src[:, cols].astype(BF16)
        wgu_dst[:, (2 * c + 1) * MXU_WIDTH:(2 * c + 2) * MXU_WIDTH] = wu_src[:, cols].astype(BF16)
    wd_dst[...] = wd_src[...].astype(BF16)


def _cast_specs(layer, stacks, steps):
    in_specs = [pl.BlockSpec((None, w.shape[1] // steps, w.shape[2]), lambda i: (layer, i, 0)) for w in stacks]
    out_specs = [pl.BlockSpec((D_MODEL // steps, 2 * D_FF), lambda i: (i, 0)),
                 pl.BlockSpec((D_FF // steps, D_MODEL), lambda i: (i, 0))]
    out_shape = [jax.ShapeDtypeStruct((D_MODEL, 2 * D_FF), BF16), jax.ShapeDtypeStruct((D_FF, D_MODEL), BF16)]
    return in_specs, out_specs, out_shape


def _first_ffn_weights(layer, wg, wu, wd):
    steps = 8
    in_specs, out_specs, out_shape = _cast_specs(layer, (wg, wu, wd), steps)
    return pl.pallas_call(_cast_ffn_weights, grid=(steps,), in_specs=in_specs, out_specs=out_specs,
                          out_shape=out_shape, name="cast_ffn_weights")(wg, wu, wd)


def _ffn(x, gain, layer, wgu, wd, attn=None, w_out=None, cast_ahead=None):
    tm = FFN_TM
    steps = TOKENS // tm
    row_spec = pl.BlockSpec((tm, D_MODEL), lambda i: (i, 0))
    in_specs, args = [row_spec], [x]
    if attn is not None:
        in_specs += [row_spec, _const_spec((D_MODEL, D_MODEL))]
        args += [attn, w_out]
    in_specs += [pl.BlockSpec((None, 1, D_MODEL), lambda i: (layer, 0, 0), pipeline_mode=pl.Buffered(1)),
                 _const_spec((D_MODEL, 2 * D_FF)), _const_spec((D_FF, D_MODEL))]
    args += [gain, wgu, wd]
    out_specs = [row_spec]
    out_shape = [jax.ShapeDtypeStruct((TOKENS, D_MODEL), F32)]
    if cast_ahead is not None:
        next_layer, *stacks = cast_ahead
        cast_in, cast_out, cast_shape = _cast_specs(next_layer, stacks, steps)
        in_specs += cast_in
        args += stacks
        out_specs += cast_out
        out_shape += cast_shape
    outs = pl.pallas_call(
        functools.partial(_ffn_kernel, project=attn is not None, cast_ahead=cast_ahead is not None),
        grid=(steps,),
        in_specs=in_specs,
        out_specs=out_specs,
        out_shape=out_shape,
        compiler_params=pltpu.CompilerParams(vmem_limit_bytes=VMEM_LIMIT),
        name="ffn" if attn is None else "proj_ffn",
    )(*args)
    return outs[0], tuple(outs[1:])


A_TM = 512
A_QW = A_HEADS * A_HEAD_DIM
A_KW = 4 * LANES
A_PAIRS = A_HEADS // 2
A_ROWS = 512


def _a_qkv_kernel(x_ref, g_ref, w_ref, qg_ref, kg_ref, q_ref, k_ref, v_ref):
    h = _rms_rows(x_ref[...], g_ref[...]).astype(BF16)
    qkv = _dot(h, w_ref[...])
    lane = lax.broadcasted_iota(jnp.int32, (1, LANES), 1)
    low = lane < HALF
    q_table = qg_ref[...] * (math.sqrt(A_HEAD_DIM) * A_HEAD_DIM ** -0.5 * LOG2E)
    for p in range(A_PAIRS):
        qp = qkv[:, p * LANES:(p + 1) * LANES]
        sq = qp * qp
        ss_lo = jnp.sum(jnp.where(low, sq, 0.0), axis=-1, keepdims=True)
        ss_hi = jnp.sum(jnp.where(low, 0.0, sq), axis=-1, keepdims=True)
        r = jnp.where(low, lax.rsqrt(ss_lo + A_HEAD_DIM * EPS), lax.rsqrt(ss_hi + A_HEAD_DIM * EPS))
        q_ref[:, p * LANES:(p + 1) * LANES] = (qp * q_table * r).astype(BF16)
    for j in range(A_KW // LANES):
        kj = qkv[:, A_QW + j * LANES:A_QW + (j + 1) * LANES]
        ms = jnp.sum(kj * kj, axis=-1, keepdims=True) * (1.0 / A_HEAD_DIM)
        k_ref[:, j * LANES:(j + 1) * LANES] = (
            kj * lax.rsqrt(ms + EPS) * kg_ref[:, j * LANES:(j + 1) * LANES]).astype(BF16)
    v_ref[...] = qkv[:, A_QW + A_KW:].astype(BF16)


def _a_qkv(x, gain, w, qg, kg):
    tm = A_TM
    width = A_QW + 2 * A_KW
    return pl.pallas_call(
        _a_qkv_kernel,
        grid=(TOKENS // tm,),
        in_specs=[pl.BlockSpec((tm, D_MODEL), lambda i: (i, 0)), _const_spec((1, D_MODEL)),
                  _const_spec((D_MODEL, width)), _const_spec((1, LANES)), _const_spec((1, A_KW))],
        out_specs=[pl.BlockSpec((tm, A_QW), lambda i: (i, 0)), pl.BlockSpec((tm, A_KW), lambda i: (i, 0)),
                   pl.BlockSpec((tm, A_KW), lambda i: (i, 0))],
        out_shape=[jax.ShapeDtypeStruct((TOKENS, A_QW), BF16), jax.ShapeDtypeStruct((TOKENS, A_KW), BF16),
                   jax.ShapeDtypeStruct((TOKENS, A_KW), BF16)],
        compiler_params=pltpu.CompilerParams(vmem_limit_bytes=VMEM_LIMIT),
        name="a_qkv",
    )(x, gain, w, qg, kg)


def _t5_bucket(dist):
    n = jnp.maximum(dist, 0)
    max_exact = NUM_BUCKETS // 2
    large = max_exact + (jnp.log(jnp.maximum(n, 1).astype(F32) / max_exact)
                         / math.log(MAX_DISTANCE / max_exact)
                         * (NUM_BUCKETS - max_exact)).astype(jnp.int32)
    large = jnp.minimum(large, NUM_BUCKETS - 1)
    return jnp.where(n < max_exact, n, large)


SUBLANES = 8


def _a_attn_kernel(sink_ref, q_ref, kc_ref, kp_ref, vc_ref, vp_ref, pq_ref, pkc_ref, pkp_ref, tbl_ref, o_ref,
                   bias_ref, shared_ref):
    row = lax.broadcasted_iota(jnp.int32, (BLOCK, BLOCK), 0)
    col = lax.broadcasted_iota(jnp.int32, (BLOCK, BLOCK), 1)
    cur_ok = col <= row
    first_in_seq = pl.program_id(1) == 0
    no_prev = jnp.where(first_in_seq, NEG, 0.0)
    tables = [jnp.broadcast_to(tbl_ref[h:h + 1, :] * LOG2E, (SUBLANES, LANES)) for h in range(A_HEADS)]
    ones = jnp.ones((2 * BLOCK, LANES), BF16)
    group_heads = A_HEADS // A_KV_HEADS
    lane = lax.broadcasted_iota(jnp.int32, (1, LANES), 1)

    @pl.when(jnp.logical_and(pl.program_id(0) == 0, first_in_seq))
    def _():
        shared_ref[0] = 0

    def fill_shift_invariant():
        back = jnp.broadcast_to((BLOCK - lane) & (BLOCK - 1), (SUBLANES, LANES))
        idx = _t5_bucket(back)
        for h in range(A_HEADS):
            base = jnp.take_along_axis(tables[h], idx, axis=1)
            base = jnp.broadcast_to(base[0:1, :], (BLOCK, BLOCK))
            tile = pltpu.roll(base, 0, 1, stride=1, stride_axis=0)
            bias_ref[h] = jnp.where(cur_ok, tile, NEG)
            bias_ref[A_HEADS + h] = jnp.where(cur_ok, NEG, tile)

    def fill_general(pos_q, pos_cur, pos_prev):
        bucket = jnp.where(cur_ok, _t5_bucket(pos_q - pos_cur), _t5_bucket(pos_q - pos_prev))
        for c in range(BLOCK // SUBLANES):
            chunk = slice(c * SUBLANES, (c + 1) * SUBLANES)
            for h in range(A_HEADS):
                piece = jnp.take_along_axis(tables[h], bucket[chunk, :], axis=1)
                bias_ref[h, chunk, :] = jnp.where(cur_ok[chunk, :], piece, NEG)
                bias_ref[A_HEADS + h, chunk, :] = jnp.where(cur_ok[chunk, :], NEG, piece)

    blocks = range(A_ROWS // BLOCK)

    def rows_of(r):
        return slice(r * BLOCK, (r + 1) * BLOCK)

    def pairs_of(kv):
        return range(kv * group_heads // 2, (kv + 1) * group_heads // 2)

    def band(r):
        if r == 0:
            k_prev, v_prev, pos_prev = kp_ref[...], vp_ref[...], pkp_ref[0]
        else:
            k_prev, v_prev, pos_prev = kc_ref[rows_of(r - 1), :], vc_ref[rows_of(r - 1), :], pkc_ref[r - 1]
        k_band = jnp.concatenate([k_prev, kc_ref[rows_of(r), :]], axis=0)
        v_band = jnp.concatenate([v_prev, vc_ref[rows_of(r), :]], axis=0)
        return k_band, v_band, pos_prev

    def scores(r, kv, parity, bands):
        group = slice((2 * kv + parity) * LANES, (2 * kv + parity + 1) * LANES)
        q_stack = jnp.concatenate([q_ref[rows_of(r), p * LANES:(p + 1) * LANES] for p in pairs_of(kv)], axis=0)
        return _dot_nt(q_stack, bands[r][0][:, group])

    def finish(r, kv, parity, s_all, bias, bands):
        v_band = bands[r][1]
        group = slice((2 * kv + parity) * LANES, (2 * kv + parity + 1) * LANES)
        probs, sink_terms = [], []
        for i, p in enumerate(pairs_of(kv)):
            head = 2 * p + parity
            s = s_all[i * BLOCK:(i + 1) * BLOCK, :]
            s_prev = s[:, :BLOCK] + bias[A_HEADS + head]
            if r == 0:
                s_prev = s_prev + no_prev
            s_cur = s[:, BLOCK:] + bias[head]
            sink = sink_ref[head] * LOG2E
            m = jnp.maximum(jnp.max(jnp.maximum(s_prev, s_cur), axis=-1, keepdims=True), sink)
            probs.append(jnp.concatenate([jnp.exp2(s_prev - m), jnp.exp2(s_cur - m)], axis=1).astype(BF16))
            sink_terms.append(jnp.exp2(sink - m))
        v_ext = jnp.concatenate([v_band[:, group], ones], axis=1)
        out = _dot(jnp.concatenate(probs, axis=0), v_ext)
        scaled = []
        for i in range(len(probs)):
            o_i = out[i * BLOCK:(i + 1) * BLOCK, :]
            scaled.append(o_i[:, :LANES] * (1.0 / (o_i[:, LANES:] + sink_terms[i])))
        return scaled

    def attend(which_blocks):
        bias = [bias_ref[h] for h in range(2 * A_HEADS)]
        bands = {r: band(r) for r in which_blocks}
        units = [(r, kv, parity) for r in which_blocks for kv in range(A_KV_HEADS) for parity in range(2)]
        raw = {units[0]: scores(*units[0], bands)}
        done = {}
        for i, unit in enumerate(units):
            if i + 1 < len(units):
                raw[units[i + 1]] = scores(*units[i + 1], bands)
            done[unit] = finish(*unit, raw.pop(unit), bias, bands)
            r, kv, parity = unit
            if parity == 1:
                low, high = done.pop((r, kv, 0)), done.pop((r, kv, 1))
                for i_pair, p in enumerate(pairs_of(kv)):
                    o_ref[rows_of(r), p * LANES:(p + 1) * LANES] = (low[i_pair] + high[i_pair]).astype(BF16)

    @pl.when(shared_ref[0] == 0)
    def _():
        fill_shift_invariant()
        shared_ref[0] = 1

    attend(blocks)

    off = jnp.zeros((1, LANES), jnp.int32)
    for r in blocks:
        pos_cur, pos_prev = pkc_ref[r], band(r)[2]
        start = jnp.min(pos_cur, axis=-1, keepdims=True)
        off_prev = jnp.where(pos_prev - start == lane - BLOCK, 0, 1)
        if r == 0:
            off_prev = off_prev * jnp.where(first_in_seq, 0, 1)
        off = off + jnp.where(pos_cur - start == lane, 0, 1) + off_prev

    @pl.when(jnp.sum(off) != 0)
    def _():
        for r in blocks:
            fill_general(pq_ref[rows_of(r), :], pkc_ref[r], band(r)[2])
            attend([r])
        shared_ref[0] = 0


def _a_attn(q, k4, v4, pos_col, pos_row, table, sinks):
    steps = SEQ // A_ROWS
    blocks_per_step = A_ROWS // BLOCK
    blocks_per_seq = SEQ // BLOCK

    def cur(b, i):
        return (b * steps + i, 0)

    def prev(b, i):
        return (b * blocks_per_seq + jnp.maximum(i * blocks_per_step - 1, 0), 0)

    def cur3(b, i):
        return (b * steps + i, 0, 0)

    def prev3(b, i):
        return (b * blocks_per_seq + jnp.maximum(i * blocks_per_step - 1, 0), 0, 0)

    return pl.pallas_call(
        _a_attn_kernel,
        grid=(BATCH, steps),
        in_specs=[
            pl.BlockSpec(memory_space=pltpu.SMEM),
            pl.BlockSpec((A_ROWS, A_QW), cur),
            pl.BlockSpec((A_ROWS, A_KW), cur),
            pl.BlockSpec((BLOCK, A_KW), prev),
            pl.BlockSpec((A_ROWS, A_KW), cur),
            pl.BlockSpec((BLOCK, A_KW), prev),
            pl.BlockSpec((A_ROWS, 1), cur),
            pl.BlockSpec((blocks_per_step, 1, BLOCK), cur3),
            pl.BlockSpec((1, 1, BLOCK), prev3),
            _const_spec((A_HEADS, LANES)),
        ],
        out_specs=pl.BlockSpec((A_ROWS, A_QW), cur),
        out_shape=jax.ShapeDtypeStruct((TOKENS, A_QW), BF16),
        scratch_shapes=[pltpu.VMEM((2 * A_HEADS, BLOCK, BLOCK), F32), pltpu.SMEM((1,), jnp.int32)],
        compiler_params=pltpu.CompilerParams(dimension_semantics=("arbitrary", "arbitrary"),
                                             vmem_limit_bytes=VMEM_LIMIT),
        name="a_attn",
    )(sinks, q, k4, k4, v4, v4, pos_col, pos_row, pos_row, table)


B_TM = 256
B_CW = Q_LORA + KV_LORA + 2 * LANES
B_HW = B_HEADS * LANES
B_VW = B_HEADS * V_DIM
B_TQ = 512
B_TK = 512


def _b_proj_kernel(x_ref, pos_ref, g_ref, win_ref, qn_ref, kvn_ref, wq_ref, wqs_ref, wk_ref, wvt_ref,
                   qg_ref, qgs_ref, kg_ref, krg_ref, krgs_ref, freq_ref, q_ref, k_ref, vt_ref):
    h = _rms_rows(x_ref[...], g_ref[...]).astype(BF16)
    c = _dot(h, win_ref[...])
    cq = _rms_rows(c[:, :Q_LORA], qn_ref[...]).astype(BF16)
    ckv_f32 = _rms_rows(c[:, Q_LORA:Q_LORA + KV_LORA], kvn_ref[...])
    ckv = ckv_f32.astype(BF16)
    kr = c[:, Q_LORA + KV_LORA:Q_LORA + KV_LORA + LANES]
    kr_partner = c[:, Q_LORA + KV_LORA + LANES:]
    q = _dot(cq, wq_ref[...])
    q_partner = _dot(cq, wqs_ref[...])
    kn = _dot(ckv, wk_ref[...])
    vt_ref[...] = _dot(wvt_ref[...], ckv_f32.T.astype(BF16)).astype(BF16)

    lane = lax.broadcasted_iota(jnp.int32, (1, LANES), 1)
    ang = pos_ref[...].astype(F32) * freq_ref[...]
    cos = jnp.cos(ang)
    sin = jnp.sin(ang)
    half = QK_ROPE // 2
    sin_signed = jnp.where(lane < QK_NOPE + half, -sin, sin)

    root_d = math.sqrt(B_DQK)
    q_const = root_d * B_DQK ** -0.5 * LOG2E
    k_rope = (kr * krg_ref[...] * cos + kr_partner * krgs_ref[...] * sin_signed) * root_d
    k_gain = kg_ref[...] * root_d
    ss_rope = jnp.sum(kr * kr, axis=-1, keepdims=True) + B_DQK * EPS
    q_cos = qg_ref[...] * cos * q_const
    q_sin = qgs_ref[...] * sin_signed * q_const
    for hd in range(B_HEADS):
        g = slice(hd * LANES, (hd + 1) * LANES)
        qh = q[:, g]
        r = lax.rsqrt(jnp.sum(qh * qh, axis=-1, keepdims=True) + B_DQK * EPS)
        q_ref[:, g] = ((qh * q_cos + q_partner[:, g] * q_sin) * r).astype(BF16)
        kh = kn[:, g]
        rk = lax.rsqrt(jnp.sum(kh * kh, axis=-1, keepdims=True) + ss_rope)
        k_ref[:, g] = ((kh * k_gain + k_rope) * rk).astype(BF16)


def _b_proj(x, pos_col, gain, w_in, qn, kvn, wq, wqs, wk, wvt, qg, qgs, kg, krg, krgs, freq):
    tm = B_TM
    tiles_per_seq = SEQ // tm
    lane_vec = _const_spec((1, LANES))
    out_spec = pl.BlockSpec((tm, B_HW), lambda i: (i, 0))
    out_sds = jax.ShapeDtypeStruct((TOKENS, B_HW), BF16)
    vt_spec = pl.BlockSpec((None, B_VW, tm), lambda i: (i // tiles_per_seq, 0, i % tiles_per_seq))
    return pl.pallas_call(
        _b_proj_kernel,
        grid=(TOKENS // tm,),
        in_specs=[pl.BlockSpec((tm, D_MODEL), lambda i: (i, 0)), pl.BlockSpec((tm, 1), lambda i: (i, 0)),
                  _const_spec((1, D_MODEL)), _const_spec((D_MODEL, B_CW)), _const_spec((1, Q_LORA)),
                  _const_spec((1, KV_LORA)), _const_spec((Q_LORA, B_HW)), _const_spec((Q_LORA, B_HW)),
                  _const_spec((KV_LORA, B_HW)), _const_spec((B_VW, KV_LORA)),
                  lane_vec, lane_vec, lane_vec, lane_vec, lane_vec, lane_vec],
        out_specs=[out_spec, out_spec, vt_spec],
        out_shape=[out_sds, out_sds, jax.ShapeDtypeStruct((BATCH, B_VW, SEQ), BF16)],
        compiler_params=pltpu.CompilerParams(vmem_limit_bytes=VMEM_LIMIT),
        name="b_proj",
    )(x, pos_col, gain, w_in, qn, kvn, wq, wqs, wk, wvt, qg, qgs, kg, krg, krgs, freq)


B_LOOKAHEAD = 3


def _b_attn_kernel(q_ref, k_ref, vt_ref, o_ref):
    groups = [slice(parity * LANES, (parity + 1) * LANES) for parity in range(2)]
    n_query_tiles = SEQ // B_TQ
    half = B_TQ // 2

    def chain(qt, hd):
        pieces = [(qt, hd, j * B_TK, B_TK, 0, B_TQ) for j in range(qt * B_TQ // B_TK)]
        return pieces + [(qt, hd, qt * B_TQ, half, 0, B_TQ), (qt, hd, qt * B_TQ + half, half, half, half)]

    def scores(qt, hd, key_lo, n_keys, q_lo, n_q):
        queries = slice(qt * B_TQ + q_lo, qt * B_TQ + q_lo + n_q)
        return _dot_nt(k_ref[key_lo:key_lo + n_keys, groups[hd]], q_ref[queries, groups[hd]])

    def accumulate(unit, s, carry):
        qt, hd, key_lo, n_keys, q_lo, n_q = unit
        m_all, acc_all = carry
        m, acc = m_all[:, q_lo:q_lo + n_q], acc_all[:, q_lo:q_lo + n_q]
        first_query = qt * B_TQ + q_lo
        if key_lo + n_keys - 1 > first_query:
            key = lax.broadcasted_iota(jnp.int32, (n_keys, n_q), 0)
            query = lax.broadcasted_iota(jnp.int32, (n_keys, n_q), 1)
            s = jnp.where(key + (key_lo - first_query) <= query, s, NEG)
        m_new = jnp.maximum(m, jnp.max(s, axis=0, keepdims=True))
        alpha = jnp.exp2(m - m_new)
        e = jnp.exp2(s - m_new).astype(BF16)
        vt_ext = jnp.concatenate([vt_ref[hd * V_DIM:(hd + 1) * V_DIM, key_lo:key_lo + n_keys],
                                  jnp.ones((V_DIM, n_keys), BF16)], axis=0)
        acc_new = alpha * acc + _dot(vt_ext, e)
        if q_lo:
            m_new = jnp.concatenate([m_all[:, :q_lo], m_new], axis=1)
            acc_new = jnp.concatenate([acc_all[:, :q_lo], acc_new], axis=1)
        return m_new, acc_new

    chains = [chain(qt, hd) for qt in reversed(range(n_query_tiles)) for hd in range(2)]
    units = [c[i] for i in range(len(chains[0])) for c in chains if i < len(c)]
    last = {c[-1] for c in chains}
    init = (jnp.full((1, B_TQ), NEG, F32), jnp.zeros((2 * V_DIM, B_TQ), F32))
    carry = {(qt, hd): init for qt in range(n_query_tiles) for hd in range(2)}
    raw = {u: scores(*u) for u in units[:B_LOOKAHEAD]}
    for i, unit in enumerate(units):
        qt, hd = unit[0], unit[1]
        if i + B_LOOKAHEAD < len(units):
            ahead = units[i + B_LOOKAHEAD]
            raw[ahead] = scores(*ahead)
        carry[(qt, hd)] = accumulate(unit, raw.pop(unit), carry[(qt, hd)])
        if hd == 1 and unit in last:
            outs = [acc[:V_DIM, :] * (1.0 / acc[V_DIM:V_DIM + 1, :])
                    for _, acc in (carry.pop((qt, 0)), carry.pop((qt, 1)))]
            o_ref[qt * B_TQ:(qt + 1) * B_TQ, :] = jnp.concatenate(outs, axis=0).T.astype(BF16)


def _b_attn(q, k, vt):
    pairs = B_HEADS // 2
    return pl.pallas_call(
        _b_attn_kernel,
        grid=(BATCH, pairs),
        in_specs=[pl.BlockSpec((SEQ, 2 * LANES), lambda b, p: (b, p)),
                  pl.BlockSpec((SEQ, 2 * LANES), lambda b, p: (b, p)),
                  pl.BlockSpec((None, 2 * V_DIM, SEQ), lambda b, p: (b, p, 0))],
        out_specs=pl.BlockSpec((SEQ, 2 * V_DIM), lambda b, p: (b, p)),
        out_shape=jax.ShapeDtypeStruct((TOKENS, B_VW), BF16),
        compiler_params=pltpu.CompilerParams(vmem_limit_bytes=VMEM_LIMIT),
        name="b_attn",
    )(q, k, vt)


def _lohi(cols):
    z = jnp.zeros_like(cols)
    return jnp.concatenate([cols, z, z, cols], axis=1)


def _prep_a(w_in, q_gain, k_gain):
    wq = w_in[:, :A_QW]
    k0 = w_in[:, A_QW:A_QW + A_HEAD_DIM]
    k1 = w_in[:, A_QW + A_HEAD_DIM:A_QW + 2 * A_HEAD_DIM]
    v0 = w_in[:, A_QW + 2 * A_HEAD_DIM:A_QW + 3 * A_HEAD_DIM]
    v1 = w_in[:, A_QW + 3 * A_HEAD_DIM:]
    w = jnp.concatenate([wq, _lohi(k0), _lohi(k1), _lohi(v0), _lohi(v1)], axis=1).astype(BF16)
    qg = jnp.concatenate([q_gain, q_gain])[None, :]
    z = jnp.zeros_like(k_gain)
    kg = jnp.concatenate([k_gain, z, z, k_gain, k_gain, z, z, k_gain])[None, :]
    return w, qg, kg


def _head_groups(w, per_head, src_lo, src_hi, dst_lo):
    rows = w.shape[0]
    w3 = w.reshape(rows, B_HEADS, per_head)[:, :, src_lo:src_hi]
    out = jnp.zeros((rows, B_HEADS, LANES), w.dtype)
    out = out.at[:, :, dst_lo:dst_lo + (src_hi - src_lo)].set(w3)
    return out


def _prep_b(w_in, w_uq, w_ukv, q_gain, k_gain):
    half = QK_ROPE // 2
    t1 = slice(QK_NOPE, QK_NOPE + half)
    t2 = slice(QK_NOPE + half, B_DQK)
    rows = w_in.shape[0]
    rope_cols = w_in[:, Q_LORA + KV_LORA:]
    zeros = lambda n: jnp.zeros((rows, n), w_in.dtype)
    rope_group = jnp.concatenate([zeros(QK_NOPE), rope_cols, zeros(LANES - B_DQK)], axis=1)
    partner_group = jnp.concatenate(
        [zeros(QK_NOPE), rope_cols[:, half:], rope_cols[:, :half], zeros(LANES - B_DQK)], axis=1)
    win = jnp.concatenate([w_in[:, :Q_LORA + KV_LORA], rope_group, partner_group], axis=1).astype(BF16)

    wq = _head_groups(w_uq, B_DQK, 0, B_DQK, 0)
    wqs = (_head_groups(w_uq, B_DQK, t2.start, t2.stop, t1.start)
           + _head_groups(w_uq, B_DQK, t1.start, t1.stop, t2.start))
    wk = _head_groups(w_ukv, QK_NOPE + V_DIM, 0, QK_NOPE, 0)
    wvt = w_ukv.reshape(KV_LORA, B_HEADS, QK_NOPE + V_DIM)[:, :, QK_NOPE:].reshape(KV_LORA, B_VW).T.astype(BF16)
    flat = lambda a: a.reshape(a.shape[0], B_HW).astype(BF16)

    def lane_vec(pieces):
        out = jnp.zeros((LANES,), F32)
        for lo, vals in pieces:
            out = out.at[lo:lo + vals.shape[0]].set(vals)
        return out[None, :]

    qg = lane_vec([(0, q_gain)])
    qgs = lane_vec([(t1.start, q_gain[t2]), (t2.start, q_gain[t1])])
    kg = lane_vec([(0, k_gain[:QK_NOPE])])
    krg = lane_vec([(QK_NOPE, k_gain[QK_NOPE:])])
    krgs = lane_vec([(t1.start, k_gain[t2]), (t2.start, k_gain[t1])])
    return win, flat(wq), flat(wqs), flat(wk), wvt, qg, qgs, kg, krg, krgs


def _rope_freq():
    inv = ROPE_BASE ** (-np.arange(0, QK_ROPE, 2, dtype=np.float32) / QK_ROPE)
    out = np.zeros((1, LANES), np.float32)
    half = QK_ROPE // 2
    out[0, QK_NOPE:QK_NOPE + half] = inv
    out[0, QK_NOPE + half:B_DQK] = inv
    return jnp.asarray(out)


def kernel(x, positions, rel_bias, ffn_norm1, ffn1_wg, ffn1_wu, ffn1_wd, mix_norm, ffn_norm2, ffn2_wg,
           ffn2_wu, ffn2_wd, a_w_in, a_q_gain, a_k_gain, a_sinks, a_w_out, b_w_in, b_q_norm, b_kv_norm,
           b_w_uq, b_w_ukv, b_q_gain, b_k_gain, b_w_out):
    assert x.shape == (BATCH, SEQ, D_MODEL) and positions.shape == (BATCH, SEQ)
    xt = x.reshape(TOKENS, D_MODEL)
    pos_col = positions.reshape(TOKENS, 1)
    pos_row = positions.reshape(TOKENS // BLOCK, 1, BLOCK)
    table = jnp.zeros((A_HEADS, LANES), F32).at[:, :NUM_BUCKETS].set(rel_bias.T)
    bf = lambda w: w.astype(BF16)
    row = lambda v: v[None, :]
    gain1, gain2 = ffn_norm1[:, None, :], ffn_norm2[:, None, :]
    weights = _first_ffn_weights(0, ffn1_wg, ffn1_wu, ffn1_wd)

    for i in range(DEPTH):
        xt, weights = _ffn(xt, gain1, i, *weights, cast_ahead=(i, ffn2_wg, ffn2_wu, ffn2_wd))
        j = i // N_MIXERS
        if i % N_MIXERS == 0:
            w, qg, kg = _prep_a(a_w_in[j], a_q_gain[j], a_k_gain[j])
            q, k4, v4 = _a_qkv(xt, row(mix_norm[i]), w, qg, kg)
            attn = _a_attn(q, k4, v4, pos_col, pos_row, table, a_sinks[j])
            w_out = bf(a_w_out[j])
        else:
            prep = _prep_b(b_w_in[j], b_w_uq[j], b_w_ukv[j], b_q_gain[j], b_k_gain[j])
            win, wq, wqs, wk, wvt, qg, qgs, kg, krg, krgs = prep
            q, k, vt = _b_proj(xt, pos_col, row(mix_norm[i]), win, row(b_q_norm[j]), row(b_kv_norm[j]),
                              wq, wqs, wk, wvt, qg, qgs, kg, krg, krgs, _rope_freq())
            attn = _b_attn(q, k, vt)
            w_out = bf(b_w_out[j])
        ahead = (i + 1, ffn1_wg, ffn1_wu, ffn1_wd) if i + 1 < DEPTH else None
        xt, weights = _ffn(xt, gain2, i, *weights, attn=attn, w_out=w_out, cast_ahead=ahead)
    return xt.reshape(BATCH, SEQ, D_MODEL)
```

```python
import functools
import math

import numpy as np
import jax
import jax.numpy as jnp
from jax import lax
from jax.experimental import pallas as pl
from jax.experimental.pallas import tpu as pltpu

D_MODEL = 1024
BATCH = 8
SEQ = 2048
DEPTH = 2
N_MIXERS = 2
A_HEADS = 16
A_KV_HEADS = 2
A_HEAD_DIM = 64
WINDOW = 128
BLOCK = 128
NUM_BUCKETS = 32
MAX_DISTANCE = 128
B_HEADS = 16
Q_LORA = 256
KV_LORA = 128
QK_NOPE = 64
QK_ROPE = 32
V_DIM = 64
ROPE_BASE = 10000.0
D_FF = 2816
EPS = 1e-6
NEG = -1e30

TOKENS = BATCH * SEQ
LANES = 128
HALF = LANES // 2
B_DQK = QK_NOPE + QK_ROPE
VMEM_LIMIT = 60 * 1024 * 1024

LOG2E = math.log2(math.e)

F32 = jnp.float32
BF16 = jnp.bfloat16


def _rms_rows(x, gain):
    return x * lax.rsqrt(jnp.mean(x * x, axis=-1, keepdims=True) + EPS) * gain


def _dot(a, b):
    return jnp.dot(a, b, preferred_element_type=F32)


def _dot_nt(a, b):
    return lax.dot_general(a, b, (((1,), (1,)), ((), ())), preferred_element_type=F32)


def _const_spec(shape):
    nd = len(shape)
    return pl.BlockSpec(shape, lambda *_: (0,) * nd, pipeline_mode=pl.Buffered(1))


FFN_TM = 1024
MXU_WIDTH = 256
FFN_CHUNKS = ((0, 4 * MXU_WIDTH), (4 * MXU_WIDTH, 8 * MXU_WIDTH), (8 * MXU_WIDTH, D_FF))


def _ffn_body(x, g_ref, wgu_ref, wd_ref, o_ref):
    h = _rms_rows(x, g_ref[...]).astype(BF16)
    down = None
    for lo, hi in FFN_CHUNKS:
        gate_up = _dot(h, wgu_ref[:, 2 * lo:2 * hi])
        acts = []
        for c in range((hi - lo) // MXU_WIDTH):
            gate = gate_up[:, 2 * c * MXU_WIDTH:(2 * c + 1) * MXU_WIDTH]
            up = gate_up[:, (2 * c + 1) * MXU_WIDTH:(2 * c + 2) * MXU_WIDTH]
            acts.append((gate * jax.nn.sigmoid(gate) * up).astype(BF16))
        part = _dot(jnp.concatenate(acts, axis=1), wd_ref[lo:hi, :])
        down = part if down is None else down + part
    o_ref[...] = x + 0.5 * down


def _ffn_kernel(*refs, project, cast_ahead):
    refs = list(refs)
    x = refs.pop(0)[...]
    if project:
        a_ref, wo_ref = refs.pop(0), refs.pop(0)
        x = x + _dot(a_ref[...], wo_ref[...])
    g_ref, wgu_ref, wd_ref = refs[:3]
    refs = refs[3:]
    if cast_ahead:
        _cast_ffn_weights(*refs[:3], *refs[4:6])
        refs = refs[3:]
    _ffn_body(x, g_ref, wgu_ref, wd_ref, refs[0])


def _cast_ffn_weights(wg_src, wu_src, wd_src, wgu_dst, wd_dst):
    for c in range(D_FF // MXU_WIDTH):
        cols = slice(c * MXU_WIDTH, (c + 1) * MXU_WIDTH)
        wgu_dst[:, 2 * c * MXU_WIDTH:(2 * c + 1) * MXU_WIDTH] = wg_src[:, cols].astype(BF16)
        wgu_dst[:, (2 * c + 1) * MXU_WIDTH:(2 * c + 2) * MXU_WIDTH] = wu_src[:, cols].astype(BF16)
    wd_dst[...] = wd_src[...].astype(BF16)


def _cast_specs(layer, stacks, steps):
    in_specs = [pl.BlockSpec((None, w.shape[1] // steps, w.shape[2]), lambda i: (layer, i, 0)) for w in stacks]
    out_specs = [pl.BlockSpec((D_MODEL // steps, 2 * D_FF), lambda i: (i, 0)),
                 pl.BlockSpec((D_FF // steps, D_MODEL), lambda i: (i, 0))]
    out_shape = [jax.ShapeDtypeStruct((D_MODEL, 2 * D_FF), BF16), jax.ShapeDtypeStruct((D_FF, D_MODEL), BF16)]
    return in_specs, out_specs, out_shape


def _first_ffn_weights(layer, wg, wu, wd):
    steps = 8
    in_specs, out_specs, out_shape = _cast_specs(layer, (wg, wu, wd), steps)
    return pl.pallas_call(_cast_ffn_weights, grid=(steps,), in_specs=in_specs, out_specs=out_specs,
                          out_shape=out_shape, name="cast_ffn_weights")(wg, wu, wd)


def _ffn(x, gain, layer, wgu, wd, attn=None, w_out=None, cast_ahead=None):
    tm = FFN_TM
    steps = TOKENS // tm
    row_spec = pl.BlockSpec((tm, D_MODEL), lambda i: (i, 0))
    in_specs, args = [row_spec], [x]
    if attn is not None:
        in_specs += [row_spec, _const_spec((D_MODEL, D_MODEL))]
        args += [attn, w_out]
    in_specs += [pl.BlockSpec((None, 1, D_MODEL), lambda i: (layer, 0, 0), pipeline_mode=pl.Buffered(1)),
                 _const_spec((D_MODEL, 2 * D_FF)), _const_spec((D_FF, D_MODEL))]
    args += [gain, wgu, wd]
    out_specs = [row_spec]
    out_shape = [jax.ShapeDtypeStruct((TOKENS, D_MODEL), F32)]
    if cast_ahead is not None:
        next_layer, *stacks = cast_ahead
        cast_in, cast_out, cast_shape = _cast_specs(next_layer, stacks, steps)
        in_specs += cast_in
        args += stacks
        out_specs += cast_out
        out_shape += cast_shape
    outs = pl.pallas_call(
        functools.partial(_ffn_kernel, project=attn is not None, cast_ahead=cast_ahead is not None),
        grid=(steps,),
        in_specs=in_specs,
        out_specs=out_specs,
        out_shape=out_shape,
        compiler_params=pltpu.CompilerParams(vmem_limit_bytes=VMEM_LIMIT),
        name="ffn" if attn is None else "proj_ffn",
    )(*args)
    return outs[0], tuple(outs[1:])


A_TM = 512
A_QW = A_HEADS * A_HEAD_DIM
A_KW = 4 * LANES
A_PAIRS = A_HEADS // 2
A_ROWS = 512


def _a_qkv_kernel(x_ref, g_ref, w_ref, qg_ref, kg_ref, q_ref, k_ref, v_ref):
    h = _rms_rows(x_ref[...], g_ref[...]).astype(BF16)
    qkv = _dot(h, w_ref[...])
    lane = lax.broadcasted_iota(jnp.int32, (1, LANES), 1)
    low = lane < HALF
    q_table = qg_ref[...] * (math.sqrt(A_HEAD_DIM) * A_HEAD_DIM ** -0.5 * LOG2E)
    for p in range(A_PAIRS):
        qp = qkv[:, p * LANES:(p + 1) * LANES]
        sq = qp * qp
        ss_lo = jnp.sum(jnp.where(low, sq, 0.0), axis=-1, keepdims=True)
        ss_hi = jnp.sum(jnp.where(low, 0.0, sq), axis=-1, keepdims=True)
        r = jnp.where(low, lax.rsqrt(ss_lo + A_HEAD_DIM * EPS), lax.rsqrt(ss_hi + A_HEAD_DIM * EPS))
        q_ref[:, p * LANES:(p + 1) * LANES] = (qp * q_table * r).astype(BF16)
    for j in range(A_KW // LANES):
        kj = qkv[:, A_QW + j * LANES:A_QW + (j + 1) * LANES]
        ms = jnp.sum(kj * kj, axis=-1, keepdims=True) * (1.0 / A_HEAD_DIM)
        k_ref[:, j * LANES:(j + 1) * LANES] = (
            kj * lax.rsqrt(ms + EPS) * kg_ref[:, j * LANES:(j + 1) * LANES]).astype(BF16)
    v_ref[...] = qkv[:, A_QW + A_KW:].astype(BF16)


def _a_qkv(x, gain, w, qg, kg):
    tm = A_TM
    width = A_QW + 2 * A_KW
    return pl.pallas_call(
        _a_qkv_kernel,
        grid=(TOKENS // tm,),
        in_specs=[pl.BlockSpec((tm, D_MODEL), lambda i: (i, 0)), _const_spec((1, D_MODEL)),
                  _const_spec((D_MODEL, width)), _const_spec((1, LANES)), _const_spec((1, A_KW))],
        out_specs=[pl.BlockSpec((tm, A_QW), lambda i: (i, 0)), pl.BlockSpec((tm, A_KW), lambda i: (i, 0)),
                   pl.BlockSpec((tm, A_KW), lambda i: (i, 0))],
        out_shape=[jax.ShapeDtypeStruct((TOKENS, A_QW), BF16), jax.ShapeDtypeStruct((TOKENS, A_KW), BF16),
                   jax.ShapeDtypeStruct((TOKENS, A_KW), BF16)],
        compiler_params=pltpu.CompilerParams(vmem_limit_bytes=VMEM_LIMIT),
        name="a_qkv",
    )(x, gain, w, qg, kg)


def _t5_bucket(dist):
    n = jnp.maximum(dist, 0)
    max_exact = NUM_BUCKETS // 2
    large = max_exact + (jnp.log(jnp.maximum(n, 1).astype(F32) / max_exact)
                         / math.log(MAX_DISTANCE / max_exact)
                         * (NUM_BUCKETS - max_exact)).astype(jnp.int32)
    large = jnp.minimum(large, NUM_BUCKETS - 1)
    return jnp.where(n < max_exact, n, large)


SUBLANES = 8


def _a_attn_kernel(sink_ref, q_ref, kc_ref, kp_ref, vc_ref, vp_ref, pq_ref, pkc_ref, pkp_ref, tbl_ref, o_ref,
                   bias_ref, shared_ref):
    row = lax.broadcasted_iota(jnp.int32, (BLOCK, BLOCK), 0)
    col = lax.broadcasted_iota(jnp.int32, (BLOCK, BLOCK), 1)
    cur_ok = col <= row
    first_in_seq = pl.program_id(1) == 0
    no_prev = jnp.where(first_in_seq, NEG, 0.0)
    tables = [jnp.broadcast_to(tbl_ref[h:h + 1, :] * LOG2E, (SUBLANES, LANES)) for h in range(A_HEADS)]
    ones = jnp.ones((2 * BLOCK, LANES), BF16)
    group_heads = A_HEADS // A_KV_HEADS
    lane = lax.broadcasted_iota(jnp.int32, (1, LANES), 1)

    @pl.when(jnp.logical_and(pl.program_id(0) == 0, first_in_seq))
    def _():
        shared_ref[0] = 0

    def fill_shift_invariant():
        back = jnp.broadcast_to((BLOCK - lane) & (BLOCK - 1), (SUBLANES, LANES))
        idx = _t5_bucket(back)
        for h in range(A_HEADS):
            base = jnp.take_along_axis(tables[h], idx, axis=1)
            base = jnp.broadcast_to(base[0:1, :], (BLOCK, BLOCK))
            tile = pltpu.roll(base, 0, 1, stride=1, stride_axis=0)
            bias_ref[h] = jnp.where(cur_ok, tile, NEG)
            bias_ref[A_HEADS + h] = jnp.where(cur_ok, NEG, tile)

    def fill_general(pos_q, pos_cur, pos_prev):
        bucket = jnp.where(cur_ok, _t5_bucket(pos_q - pos_cur), _t5_bucket(pos_q - pos_prev))
        for c in range(BLOCK // SUBLANES):
            chunk = slice(c * SUBLANES, (c + 1) * SUBLANES)
            for h in range(A_HEADS):
                piece = jnp.take_along_axis(tables[h], bucket[chunk, :], axis=1)
                bias_ref[h, chunk, :] = jnp.where(cur_ok[chunk, :], piece, NEG)
                bias_ref[A_HEADS + h, chunk, :] = jnp.where(cur_ok[chunk, :], NEG, piece)

    blocks = range(A_ROWS // BLOCK)

    def rows_of(r):
        return slice(r * BLOCK, (r + 1) * BLOCK)

    def pairs_of(kv):
        return range(kv * group_heads // 2, (kv + 1) * group_heads // 2)

    def band(r):
        if r == 0:
            k_prev, v_prev, pos_prev = kp_ref[...], vp_ref[...], pkp_ref[0]
        else:
            k_prev, v_prev, pos_prev = kc_ref[rows_of(r - 1), :], vc_ref[rows_of(r - 1), :], pkc_ref[r - 1]
        k_band = jnp.concatenate([k_prev, kc_ref[rows_of(r), :]], axis=0)
        v_band = jnp.concatenate([v_prev, vc_ref[rows_of(r), :]], axis=0)
        return k_band, v_band, pos_prev

    def scores(r, kv, parity, bands):
        group = slice((2 * kv + parity) * LANES, (2 * kv + parity + 1) * LANES)
        q_stack = jnp.concatenate([q_ref[rows_of(r), p * LANES:(p + 1) * LANES] for p in pairs_of(kv)], axis=0)
        return _dot_nt(q_stack, bands[r][0][:, group])

    def finish(r, kv, parity, s_all, bias, bands):
        v_band = bands[r][1]
        group = slice((2 * kv + parity) * LANES, (2 * kv + parity + 1) * LANES)
        probs, sink_terms = [], []
        for i, p in enumerate(pairs_of(kv)):
            head = 2 * p + parity
            s = s_all[i * BLOCK:(i + 1) * BLOCK, :]
            s_prev = s[:, :BLOCK] + bias[A_HEADS + head]
            if r == 0:
                s_prev = s_prev + no_prev
            s_cur = s[:, BLOCK:] + bias[head]
            sink = sink_ref[head] * LOG2E
            m = jnp.maximum(jnp.max(jnp.maximum(s_prev, s_cur), axis=-1, keepdims=True), sink)
            probs.append(jnp.concatenate([jnp.exp2(s_prev - m), jnp.exp2(s_cur - m)], axis=1).astype(BF16))
            sink_terms.append(jnp.exp2(sink - m))
        v_ext = jnp.concatenate([v_band[:, group], ones], axis=1)
        out = _dot(jnp.concatenate(probs, axis=0), v_ext)
        scaled = []
        for i in range(len(probs)):
            o_i = out[i * BLOCK:(i + 1) * BLOCK, :]
            scaled.append(o_i[:, :LANES] * (1.0 / (o_i[:, LANES:] + sink_terms[i])))
        return scaled

    def attend(which_blocks):
        bias = [bias_ref[h] for h in range(2 * A_HEADS)]
        bands = {r: band(r) for r in which_blocks}
        units = [(r, kv, parity) for r in which_blocks for kv in range(A_KV_HEADS) for parity in range(2)]
        raw = {units[0]: scores(*units[0], bands)}
        done = {}
        for i, unit in enumerate(units):
            if i + 1 < len(units):
                raw[units[i + 1]] = scores(*units[i + 1], bands)
            done[unit] = finish(*unit, raw.pop(unit), bias, bands)
            r, kv, parity = unit
            if parity == 1:
                low, high = done.pop((r, kv, 0)), done.pop((r, kv, 1))
                for i_pair, p in enumerate(pairs_of(kv)):
                    o_ref[rows_of(r), p * LANES:(p + 1) * LANES] = (low[i_pair] + high[i_pair]).astype(BF16)

    @pl.when(shared_ref[0] == 0)
    def _():
        fill_shift_invariant()
        shared_ref[0] = 1

    attend(blocks)

    off = jnp.zeros((1, LANES), jnp.int32)
    for r in blocks:
        pos_cur, pos_prev = pkc_ref[r], band(r)[2]
        start = jnp.min(pos_cur, axis=-1, keepdims=True)
        off_prev = jnp.where(pos_prev - start == lane - BLOCK, 0, 1)
        if r == 0:
            off_prev = off_prev * jnp.where(first_in_seq, 0, 1)
        off = off + jnp.where(pos_cur - start == lane, 0, 1) + off_prev

    @pl.when(jnp.sum(off) != 0)
    def _():
        for r in blocks:
            fill_general(pq_ref[rows_of(r), :], pkc_ref[r], band(r)[2])
            attend([r])
        shared_ref[0] = 0


def _a_attn(q, k4, v4, pos_col, pos_row, table, sinks):
    steps = SEQ // A_ROWS
    blocks_per_step = A_ROWS // BLOCK
    blocks_per_seq = SEQ // BLOCK

    def cur(b, i):
        return (b * steps + i, 0)

    def prev(b, i):
        return (b * blocks_per_seq + jnp.maximum(i * blocks_per_step - 1, 0), 0)

    def cur3(b, i):
        return (b * steps + i, 0, 0)

    def prev3(b, i):
        return (b * blocks_per_seq + jnp.maximum(i * blocks_per_step - 1, 0), 0, 0)

    return pl.pallas_call(
        _a_attn_kernel,
        grid=(BATCH, steps),
        in_specs=[
            pl.BlockSpec(memory_space=pltpu.SMEM),
            pl.BlockSpec((A_ROWS, A_QW), cur),
            pl.BlockSpec((A_ROWS, A_KW), cur),
            pl.BlockSpec((BLOCK, A_KW), prev),
            pl.BlockSpec((A_ROWS, A_KW), cur),
            pl.BlockSpec((BLOCK, A_KW), prev),
            pl.BlockSpec((A_ROWS, 1), cur),
            pl.BlockSpec((blocks_per_step, 1, BLOCK), cur3),
            pl.BlockSpec((1, 1, BLOCK), prev3),
            _const_spec((A_HEADS, LANES)),
        ],
        out_specs=pl.BlockSpec((A_ROWS, A_QW), cur),
        out_shape=jax.ShapeDtypeStruct((TOKENS, A_QW), BF16),
        scratch_shapes=[pltpu.VMEM((2 * A_HEADS, BLOCK, BLOCK), F32), pltpu.SMEM((1,), jnp.int32)],
        compiler_params=pltpu.CompilerParams(dimension_semantics=("arbitrary", "arbitrary"),
                                             vmem_limit_bytes=VMEM_LIMIT),
        name="a_attn",
    )(sinks, q, k4, k4, v4, v4, pos_col, pos_row, pos_row, table)


B_TM = 256
B_CW = Q_LORA + KV_LORA + 2 * LANES
B_HW = B_HEADS * LANES
B_VW = B_HEADS * V_DIM
B_TQ = 512
B_TK = 512


def _b_proj_kernel(x_ref, pos_ref, g_ref, win_ref, qn_ref, kvn_ref, wq_ref, wqs_ref, wk_ref, wvt_ref,
                   qg_ref, qgs_ref, kg_ref, krg_ref, krgs_ref, freq_ref, q_ref, k_ref, vt_ref):
    h = _rms_rows(x_ref[...], g_ref[...]).astype(BF16)
    c = _dot(h, win_ref[...])
    cq = _rms_rows(c[:, :Q_LORA], qn_ref[...]).astype(BF16)
    ckv_f32 = _rms_rows(c[:, Q_LORA:Q_LORA + KV_LORA], kvn_ref[...])
    ckv = ckv_f32.astype(BF16)
    kr = c[:, Q_LORA + KV_LORA:Q_LORA + KV_LORA + LANES]
    kr_partner = c[:, Q_LORA + KV_LORA + LANES:]
    q = _dot(cq, wq_ref[...])
    q_partner = _dot(cq, wqs_ref[...])
    kn = _dot(ckv, wk_ref[...])
    vt_ref[...] = _dot(wvt_ref[...], ckv_f32.T.astype(BF16)).astype(BF16)

    lane = lax.broadcasted_iota(jnp.int32, (1, LANES), 1)
    ang = pos_ref[...].astype(F32) * freq_ref[...]
    cos = jnp.cos(ang)
    sin = jnp.sin(ang)
    half = QK_ROPE // 2
    sin_signed = jnp.where(lane < QK_NOPE + half, -sin, sin)

    root_d = math.sqrt(B_DQK)
    q_const = root_d * B_DQK ** -0.5 * LOG2E
    k_rope = (kr * krg_ref[...] * cos + kr_partner * krgs_ref[...] * sin_signed) * root_d
    k_gain = kg_ref[...] * root_d
    ss_rope = jnp.sum(kr * kr, axis=-1, keepdims=True) + B_DQK * EPS
    q_cos = qg_ref[...] * cos * q_const
    q_sin = qgs_ref[...] * sin_signed * q_const
    for hd in range(B_HEADS):
        g = slice(hd * LANES, (hd + 1) * LANES)
        qh = q[:, g]
        r = lax.rsqrt(jnp.sum(qh * qh, axis=-1, keepdims=True) + B_DQK * EPS)
        q_ref[:, g] = ((qh * q_cos + q_partner[:, g] * q_sin) * r).astype(BF16)
        kh = kn[:, g]
        rk = lax.rsqrt(jnp.sum(kh * kh, axis=-1, keepdims=True) + ss_rope)
        k_ref[:, g] = ((kh * k_gain + k_rope) * rk).astype(BF16)


def _b_proj(x, pos_col, gain, w_in, qn, kvn, wq, wqs, wk, wvt, qg, qgs, kg, krg, krgs, freq):
    tm = B_TM
    tiles_per_seq = SEQ // tm
    lane_vec = _const_spec((1, LANES))
    out_spec = pl.BlockSpec((tm, B_HW), lambda i: (i, 0))
    out_sds = jax.ShapeDtypeStruct((TOKENS, B_HW), BF16)
    vt_spec = pl.BlockSpec((None, B_VW, tm), lambda i: (i // tiles_per_seq, 0, i % tiles_per_seq))
    return pl.pallas_call(
        _b_proj_kernel,
        grid=(TOKENS // tm,),
        in_specs=[pl.BlockSpec((tm, D_MODEL), lambda i: (i, 0)), pl.BlockSpec((tm, 1), lambda i: (i, 0)),
                  _const_spec((1, D_MODEL)), _const_spec((D_MODEL, B_CW)), _const_spec((1, Q_LORA)),
                  _const_spec((1, KV_LORA)), _const_spec((Q_LORA, B_HW)), _const_spec((Q_LORA, B_HW)),
                  _const_spec((KV_LORA, B_HW)), _const_spec((B_VW, KV_LORA)),
                  lane_vec, lane_vec, lane_vec, lane_vec, lane_vec, lane_vec],
        out_specs=[out_spec, out_spec, vt_spec],
        out_shape=[out_sds, out_sds, jax.ShapeDtypeStruct((BATCH, B_VW, SEQ), BF16)],
        compiler_params=pltpu.CompilerParams(vmem_limit_bytes=VMEM_LIMIT),
        name="b_proj",
    )(x, pos_col, gain, w_in, qn, kvn, wq, wqs, wk, wvt, qg, qgs, kg, krg, krgs, freq)


B_HEADS_PER_STEP = 4
B_LOOKAHEAD = 3


def _b_attn_kernel(q_ref, k_ref, vt_ref, o_ref):
    heads = range(B_HEADS_PER_STEP)
    groups = [slice(hd * LANES, (hd + 1) * LANES) for hd in heads]
    n_query_tiles = SEQ // B_TQ
    half = B_TQ // 2

    def chain(qt, hd):
        pieces = [(qt, hd, j * B_TK, B_TK, 0, B_TQ) for j in range(qt * B_TQ // B_TK)]
        return pieces + [(qt, hd, qt * B_TQ, half, 0, B_TQ), (qt, hd, qt * B_TQ + half, half, half, half)]

    def scores(qt, hd, key_lo, n_keys, q_lo, n_q):
        queries = slice(qt * B_TQ + q_lo, qt * B_TQ + q_lo + n_q)
        return _dot_nt(k_ref[key_lo:key_lo + n_keys, groups[hd]], q_ref[queries, groups[hd]])

    def accumulate(unit, s, carry):
        qt, hd, key_lo, n_keys, q_lo, n_q = unit
        m_all, acc_all = carry
        m, acc = m_all[:, q_lo:q_lo + n_q], acc_all[:, q_lo:q_lo + n_q]
        first_query = qt * B_TQ + q_lo
        if key_lo + n_keys - 1 > first_query:
            key = lax.broadcasted_iota(jnp.int32, (n_keys, n_q), 0)
            query = lax.broadcasted_iota(jnp.int32, (n_keys, n_q), 1)
            s = jnp.where(key + (key_lo - first_query) <= query, s, NEG)
        m_new = jnp.maximum(m, jnp.max(s, axis=0, keepdims=True))
        alpha = jnp.exp2(m - m_new)
        e = jnp.exp2(s - m_new).astype(BF16)
        vt_ext = jnp.concatenate([vt_ref[hd * V_DIM:(hd + 1) * V_DIM, key_lo:key_lo + n_keys],
                                  jnp.ones((V_DIM, n_keys), BF16)], axis=0)
        acc_new = alpha * acc + _dot(vt_ext, e)
        if q_lo:
            m_new = jnp.concatenate([m_all[:, :q_lo], m_new], axis=1)
            acc_new = jnp.concatenate([acc_all[:, :q_lo], acc_new], axis=1)
        return m_new, acc_new

    chains = [chain(qt, hd) for qt in reversed(range(n_query_tiles)) for hd in heads]
    units = [c[i] for i in range(len(chains[0])) for c in chains if i < len(c)]
    last = {c[-1] for c in chains}
    init = (jnp.full((1, B_TQ), NEG, F32), jnp.zeros((2 * V_DIM, B_TQ), F32))
    carry = {(qt, hd): init for qt in range(n_query_tiles) for hd in heads}
    raw = {u: scores(*u) for u in units[:B_LOOKAHEAD]}
    for i, unit in enumerate(units):
        qt, hd = unit[0], unit[1]
        if i + B_LOOKAHEAD < len(units):
            ahead = units[i + B_LOOKAHEAD]
            raw[ahead] = scores(*ahead)
        carry[(qt, hd)] = accumulate(unit, raw.pop(unit), carry[(qt, hd)])
        if hd % 2 == 1 and unit in last:
            outs = [acc[:V_DIM, :] * (1.0 / acc[V_DIM:V_DIM + 1, :])
                    for _, acc in (carry.pop((qt, hd - 1)), carry.pop((qt, hd)))]
            o_ref[qt * B_TQ:(qt + 1) * B_TQ, (hd - 1) * V_DIM:(hd + 1) * V_DIM] = (
                jnp.concatenate(outs, axis=0).T.astype(BF16))


def _b_attn(q, k, vt):
    n = B_HEADS_PER_STEP
    return pl.pallas_call(
        _b_attn_kernel,
        grid=(BATCH, B_HEADS // n),
        in_specs=[pl.BlockSpec((SEQ, n * LANES), lambda b, p: (b, p)),
                  pl.BlockSpec((SEQ, n * LANES), lambda b, p: (b, p)),
                  pl.BlockSpec((None, n * V_DIM, SEQ), lambda b, p: (b, p, 0))],
        out_specs=pl.BlockSpec((SEQ, n * V_DIM), lambda b, p: (b, p)),
        out_shape=jax.ShapeDtypeStruct((TOKENS, B_VW), BF16),
        compiler_params=pltpu.CompilerParams(vmem_limit_bytes=VMEM_LIMIT),
        name="b_attn",
    )(q, k, vt)


def _lohi(cols):
    z = jnp.zeros_like(cols)
    return jnp.concatenate([cols, z, z, cols], axis=1)


def _prep_a(w_in, q_gain, k_gain):
    wq = w_in[:, :A_QW]
    k0 = w_in[:, A_QW:A_QW + A_HEAD_DIM]
    k1 = w_in[:, A_QW + A_HEAD_DIM:A_QW + 2 * A_HEAD_DIM]
    v0 = w_in[:, A_QW + 2 * A_HEAD_DIM:A_QW + 3 * A_HEAD_DIM]
    v1 = w_in[:, A_QW + 3 * A_HEAD_DIM:]
    w = jnp.concatenate([wq, _lohi(k0), _lohi(k1), _lohi(v0), _lohi(v1)], axis=1).astype(BF16)
    qg = jnp.concatenate([q_gain, q_gain])[None, :]
    z = jnp.zeros_like(k_gain)
    kg = jnp.concatenate([k_gain, z, z, k_gain, k_gain, z, z, k_gain])[None, :]
    return w, qg, kg


def _head_groups(w, per_head, src_lo, src_hi, dst_lo):
    rows = w.shape[0]
    w3 = w.reshape(rows, B_HEADS, per_head)[:, :, src_lo:src_hi]
    out = jnp.zeros((rows, B_HEADS, LANES), w.dtype)
    out = out.at[:, :, dst_lo:dst_lo + (src_hi - src_lo)].set(w3)
    return out


def _prep_b(w_in, w_uq, w_ukv, q_gain, k_gain):
    half = QK_ROPE // 2
    t1 = slice(QK_NOPE, QK_NOPE + half)
    t2 = slice(QK_NOPE + half, B_DQK)
    rows = w_in.shape[0]
    rope_cols = w_in[:, Q_LORA + KV_LORA:]
    zeros = lambda n: jnp.zeros((rows, n), w_in.dtype)
    rope_group = jnp.concatenate([zeros(QK_NOPE), rope_cols, zeros(LANES - B_DQK)], axis=1)
    partner_group = jnp.concatenate(
        [zeros(QK_NOPE), rope_cols[:, half:], rope_cols[:, :half], zeros(LANES - B_DQK)], axis=1)
    win = jnp.concatenate([w_in[:, :Q_LORA + KV_LORA], rope_group, partner_group], axis=1).astype(BF16)

    wq = _head_groups(w_uq, B_DQK, 0, B_DQK, 0)
    wqs = (_head_groups(w_uq, B_DQK, t2.start, t2.stop, t1.start)
           + _head_groups(w_uq, B_DQK, t1.start, t1.stop, t2.start))
    wk = _head_groups(w_ukv, QK_NOPE + V_DIM, 0, QK_NOPE, 0)
    wvt = w_ukv.reshape(KV_LORA, B_HEADS, QK_NOPE + V_DIM)[:, :, QK_NOPE:].reshape(KV_LORA, B_VW).T.astype(BF16)
    flat = lambda a: a.reshape(a.shape[0], B_HW).astype(BF16)

    def lane_vec(pieces):
        out = jnp.zeros((LANES,), F32)
        for lo, vals in pieces:
            out = out.at[lo:lo + vals.shape[0]].set(vals)
        return out[None, :]

    qg = lane_vec([(0, q_gain)])
    qgs = lane_vec([(t1.start, q_gain[t2]), (t2.start, q_gain[t1])])
    kg = lane_vec([(0, k_gain[:QK_NOPE])])
    krg = lane_vec([(QK_NOPE, k_gain[QK_NOPE:])])
    krgs = lane_vec([(t1.start, k_gain[t2]), (t2.start, k_gain[t1])])
    return win, flat(wq), flat(wqs), flat(wk), wvt, qg, qgs, kg, krg, krgs


def _rope_freq():
    inv = ROPE_BASE ** (-np.arange(0, QK_ROPE, 2, dtype=np.float32) / QK_ROPE)
    out = np.zeros((1, LANES), np.float32)
    half = QK_ROPE // 2
    out[0, QK_NOPE:QK_NOPE + half] = inv
    out[0, QK_NOPE + half:B_DQK] = inv
    return jnp.asarray(out)


def kernel(x, positions, rel_bias, ffn_norm1, ffn1_wg, ffn1_wu, ffn1_wd, mix_norm, ffn_norm2, ffn2_wg,
           ffn2_wu, ffn2_wd, a_w_in, a_q_gain, a_k_gain, a_sinks, a_w_out, b_w_in, b_q_norm, b_kv_norm,
           b_w_uq, b_w_ukv, b_q_gain, b_k_gain, b_w_out):
    assert x.shape == (BATCH, SEQ, D_MODEL) and positions.shape == (BATCH, SEQ)
    xt = x.reshape(TOKENS, D_MODEL)
    pos_col = positions.reshape(TOKENS, 1)
    pos_row = positions.reshape(TOKENS // BLOCK, 1, BLOCK)
    table = jnp.zeros((A_HEADS, LANES), F32).at[:, :NUM_BUCKETS].set(rel_bias.T)
    bf = lambda w: w.astype(BF16)
    row = lambda v: v[None, :]
    gain1, gain2 = ffn_norm1[:, None, :], ffn_norm2[:, None, :]
    weights = _first_ffn_weights(0, ffn1_wg, ffn1_wu, ffn1_wd)

    for i in range(DEPTH):
        xt, weights = _ffn(xt, gain1, i, *weights, cast_ahead=(i, ffn2_wg, ffn2_wu, ffn2_wd))
        j = i // N_MIXERS
        if i % N_MIXERS == 0:
            w, qg, kg = _prep_a(a_w_in[j], a_q_gain[j], a_k_gain[j])
            q, k4, v4 = _a_qkv(xt, row(mix_norm[i]), w, qg, kg)
            attn = _a_attn(q, k4, v4, pos_col, pos_row, table, a_sinks[j])
            w_out = bf(a_w_out[j])
        else:
            prep = _prep_b(b_w_in[j], b_w_uq[j], b_w_ukv[j], b_q_gain[j], b_k_gain[j])
            win, wq, wqs, wk, wvt, qg, qgs, kg, krg, krgs = prep
            q, k, vt = _b_proj(xt, pos_col, row(mix_norm[i]), win, row(b_q_norm[j]), row(b_kv_norm[j]),
                              wq, wqs, wk, wvt, qg, qgs, kg, krg, krgs, _rope_freq())
            attn = _b_attn(q, k, vt)
            w_out = bf(b_w_out[j])
        ahead = (i + 1, ffn1_wg, ffn1_wu, ffn1_wd) if i + 1 < DEPTH else None
        xt, weights = _ffn(xt, gain2, i, *weights, attn=attn, w_out=w_out, cast_ahead=ahead)
    return xt.reshape(BATCH, SEQ, D_MODEL)
```

```python
import functools
import math

import numpy as np
import jax
import jax.numpy as jnp
from jax import lax
from jax.experimental import pallas as pl
from jax.experimental.pallas import tpu as pltpu

D_MODEL = 1024
BATCH = 8
SEQ = 2048
DEPTH = 2
N_MIXERS = 2
A_HEADS = 16
A_KV_HEADS = 2
A_HEAD_DIM = 64
WINDOW = 128
BLOCK = 128
NUM_BUCKETS = 32
MAX_DISTANCE = 128
B_HEADS = 16
Q_LORA = 256
KV_LORA = 128
QK_NOPE = 64
QK_ROPE = 32
V_DIM = 64
ROPE_BASE = 10000.0
D_FF = 2816
EPS = 1e-6
NEG = -1e30

TOKENS = BATCH * SEQ
LANES = 128
HALF = LANES // 2
B_DQK = QK_NOPE + QK_ROPE
VMEM_LIMIT = 60 * 1024 * 1024

LOG2E = math.log2(math.e)

F32 = jnp.float32
BF16 = jnp.bfloat16


def _rms_rows(x, gain):
    return x * lax.rsqrt(jnp.mean(x * x, axis=-1, keepdims=True) + EPS) * gain


def _dot(a, b):
    return jnp.dot(a, b, preferred_element_type=F32)


def _dot_nt(a, b):
    return lax.dot_general(a, b, (((1,), (1,)), ((), ())), preferred_element_type=F32)


def _const_spec(shape):
    nd = len(shape)
    return pl.BlockSpec(shape, lambda *_: (0,) * nd, pipeline_mode=pl.Buffered(1))


FFN_TM = 1024
MXU_WIDTH = 256
FFN_CHUNKS = ((0, 4 * MXU_WIDTH), (4 * MXU_WIDTH, 8 * MXU_WIDTH), (8 * MXU_WIDTH, D_FF))


def _ffn_body(x, g_ref, wgu_ref, wd_ref, o_ref):
    h = _rms_rows(x, g_ref[...]).astype(BF16)
    down = None
    for lo, hi in FFN_CHUNKS:
        gate_up = _dot(h, wgu_ref[:, 2 * lo:2 * hi])
        acts = []
        for c in range((hi - lo) // MXU_WIDTH):
            gate = gate_up[:, 2 * c * MXU_WIDTH:(2 * c + 1) * MXU_WIDTH]
            up = gate_up[:, (2 * c + 1) * MXU_WIDTH:(2 * c + 2) * MXU_WIDTH]
            acts.append((gate * jax.nn.sigmoid(gate) * up).astype(BF16))
        part = _dot(jnp.concatenate(acts, axis=1), wd_ref[lo:hi, :])
        down = part if down is None else down + part
    o_ref[...] = x + 0.5 * down


def _ffn_kernel(*refs, project, cast_ahead):
    refs = list(refs)
    x = refs.pop(0)[...]
    if project:
        a_ref, wo_ref = refs.pop(0), refs.pop(0)
        x = x + _dot(a_ref[...], wo_ref[...])
    g_ref, wgu_ref, wd_ref = refs[:3]
    refs = refs[3:]
    if cast_ahead:
        _cast_ffn_weights(*refs[:3], *refs[4:6])
        refs = refs[3:]
    _ffn_body(x, g_ref, wgu_ref, wd_ref, refs[0])


def _cast_ffn_weights(wg_src, wu_src, wd_src, wgu_dst, wd_dst):
    for c in range(D_FF // MXU_WIDTH):
        cols = slice(c * MXU_WIDTH, (c + 1) * MXU_WIDTH)
        wgu_dst[:, 2 * c * MXU_WIDTH:(2 * c + 1) * MXU_WIDTH] = wg_src[:, cols].astype(BF16)
        wgu_dst[:, (2 * c + 1) * MXU_WIDTH:(2 * c + 2) * MXU_WIDTH] = wu_src[:, cols].astype(BF16)
    wd_dst[...] = wd_src[...].astype(BF16)


def _cast_specs(layer, stacks, steps):
    in_specs = [pl.BlockSpec((None, w.shape[1] // steps, w.shape[2]), lambda i: (layer, i, 0)) for w in stacks]
    out_specs = [pl.BlockSpec((D_MODEL // steps, 2 * D_FF), lambda i: (i, 0)),
                 pl.BlockSpec((D_FF // steps, D_MODEL), lambda i: (i, 0))]
    out_shape = [jax.ShapeDtypeStruct((D_MODEL, 2 * D_FF), BF16), jax.ShapeDtypeStruct((D_FF, D_MODEL), BF16)]
    return in_specs, out_specs, out_shape


def _first_ffn_weights(layer, wg, wu, wd):
    steps = 8
    in_specs, out_specs, out_shape = _cast_specs(layer, (wg, wu, wd), steps)
    return pl.pallas_call(_cast_ffn_weights, grid=(steps,), in_specs=in_specs, out_specs=out_specs,
                          out_shape=out_shape, name="cast_ffn_weights")(wg, wu, wd)


def _ffn(x, gain, layer, wgu, wd, attn=None, w_out=None, cast_ahead=None):
    tm = FFN_TM
    steps = TOKENS // tm
    row_spec = pl.BlockSpec((tm, D_MODEL), lambda i: (i, 0))
    in_specs, args = [row_spec], [x]
    if attn is not None:
        in_specs += [row_spec, _const_spec((D_MODEL, D_MODEL))]
        args += [attn, w_out]
    in_specs += [pl.BlockSpec((None, 1, D_MODEL), lambda i: (layer, 0, 0), pipeline_mode=pl.Buffered(1)),
                 _const_spec((D_MODEL, 2 * D_FF)), _const_spec((D_FF, D_MODEL))]
    args += [gain, wgu, wd]
    out_specs = [row_spec]
    out_shape = [jax.ShapeDtypeStruct((TOKENS, D_MODEL), F32)]
    if cast_ahead is not None:
        next_layer, *stacks = cast_ahead
        cast_in, cast_out, cast_shape = _cast_specs(next_layer, stacks, steps)
        in_specs += cast_in
        args += stacks
        out_specs += cast_out
        out_shape += cast_shape
    outs = pl.pallas_call(
        functools.partial(_ffn_kernel, project=attn is not None, cast_ahead=cast_ahead is not None),
        grid=(steps,),
        in_specs=in_specs,
        out_specs=out_specs,
        out_shape=out_shape,
        compiler_params=pltpu.CompilerParams(vmem_limit_bytes=VMEM_LIMIT),
        name="ffn" if attn is None else "proj_ffn",
    )(*args)
    return outs[0], tuple(outs[1:])


A_TM = 1024
A_QW = A_HEADS * A_HEAD_DIM
A_KW = 4 * LANES
A_PAIRS = A_HEADS // 2
A_ROWS = 512


def _a_qkv_kernel(x_ref, g_ref, w_ref, qg_ref, kg_ref, q_ref, k_ref, v_ref):
    h = _rms_rows(x_ref[...], g_ref[...]).astype(BF16)
    qkv = _dot(h, w_ref[...])
    lane = lax.broadcasted_iota(jnp.int32, (1, LANES), 1)
    low = lane < HALF
    q_table = qg_ref[...] * (math.sqrt(A_HEAD_DIM) * A_HEAD_DIM ** -0.5 * LOG2E)
    for p in range(A_PAIRS):
        qp = qkv[:, p * LANES:(p + 1) * LANES]
        sq = qp * qp
        ss_lo = jnp.sum(jnp.where(low, sq, 0.0), axis=-1, keepdims=True)
        ss_hi = jnp.sum(jnp.where(low, 0.0, sq), axis=-1, keepdims=True)
        r = jnp.where(low, lax.rsqrt(ss_lo + A_HEAD_DIM * EPS), lax.rsqrt(ss_hi + A_HEAD_DIM * EPS))
        q_ref[:, p * LANES:(p + 1) * LANES] = (qp * q_table * r).astype(BF16)
    for j in range(A_KW // LANES):
        kj = qkv[:, A_QW + j * LANES:A_QW + (j + 1) * LANES]
        ms = jnp.sum(kj * kj, axis=-1, keepdims=True) * (1.0 / A_HEAD_DIM)
        k_ref[:, j * LANES:(j + 1) * LANES] = (
            kj * lax.rsqrt(ms + EPS) * kg_ref[:, j * LANES:(j + 1) * LANES]).astype(BF16)
    v_ref[...] = qkv[:, A_QW + A_KW:].astype(BF16)


def _a_qkv(x, gain, w, qg, kg):
    tm = A_TM
    width = A_QW + 2 * A_KW
    return pl.pallas_call(
        _a_qkv_kernel,
        grid=(TOKENS // tm,),
        in_specs=[pl.BlockSpec((tm, D_MODEL), lambda i: (i, 0)), _const_spec((1, D_MODEL)),
                  _const_spec((D_MODEL, width)), _const_spec((1, LANES)), _const_spec((1, A_KW))],
        out_specs=[pl.BlockSpec((tm, A_QW), lambda i: (i, 0)), pl.BlockSpec((tm, A_KW), lambda i: (i, 0)),
                   pl.BlockSpec((tm, A_KW), lambda i: (i, 0))],
        out_shape=[jax.ShapeDtypeStruct((TOKENS, A_QW), BF16), jax.ShapeDtypeStruct((TOKENS, A_KW), BF16),
                   jax.ShapeDtypeStruct((TOKENS, A_KW), BF16)],
        compiler_params=pltpu.CompilerParams(vmem_limit_bytes=VMEM_LIMIT),
        name="a_qkv",
    )(x, gain, w, qg, kg)


def _t5_bucket(dist):
    n = jnp.maximum(dist, 0)
    max_exact = NUM_BUCKETS // 2
    large = max_exact + (jnp.log(jnp.maximum(n, 1).astype(F32) / max_exact)
                         / math.log(MAX_DISTANCE / max_exact)
                         * (NUM_BUCKETS - max_exact)).astype(jnp.int32)
    large = jnp.minimum(large, NUM_BUCKETS - 1)
    return jnp.where(n < max_exact, n, large)


SUBLANES = 8


def _a_attn_kernel(sink_ref, q_ref, kc_ref, kp_ref, vc_ref, vp_ref, pq_ref, pkc_ref, pkp_ref, tbl_ref, o_ref,
                   bias_ref, shared_ref):
    row = lax.broadcasted_iota(jnp.int32, (BLOCK, BLOCK), 0)
    col = lax.broadcasted_iota(jnp.int32, (BLOCK, BLOCK), 1)
    cur_ok = col <= row
    first_in_seq = pl.program_id(1) == 0
    no_prev = jnp.where(first_in_seq, NEG, 0.0)
    tables = [jnp.broadcast_to(tbl_ref[h:h + 1, :] * LOG2E, (SUBLANES, LANES)) for h in range(A_HEADS)]
    ones = jnp.ones((2 * BLOCK, LANES), BF16)
    group_heads = A_HEADS // A_KV_HEADS
    lane = lax.broadcasted_iota(jnp.int32, (1, LANES), 1)

    @pl.when(jnp.logical_and(pl.program_id(0) == 0, first_in_seq))
    def _():
        shared_ref[0] = 0

    def fill_shift_invariant():
        back = jnp.broadcast_to((BLOCK - lane) & (BLOCK - 1), (SUBLANES, LANES))
        idx = _t5_bucket(back)
        for h in range(A_HEADS):
            base = jnp.take_along_axis(tables[h], idx, axis=1)
            base = jnp.broadcast_to(base[0:1, :], (BLOCK, BLOCK))
            tile = pltpu.roll(base, 0, 1, stride=1, stride_axis=0)
            bias_ref[h] = jnp.where(cur_ok, tile, NEG)
            bias_ref[A_HEADS + h] = jnp.where(cur_ok, NEG, tile)

    def fill_general(pos_q, pos_cur, pos_prev):
        bucket = jnp.where(cur_ok, _t5_bucket(pos_q - pos_cur), _t5_bucket(pos_q - pos_prev))
        for c in range(BLOCK // SUBLANES):
            chunk = slice(c * SUBLANES, (c + 1) * SUBLANES)
            for h in range(A_HEADS):
                piece = jnp.take_along_axis(tables[h], bucket[chunk, :], axis=1)
                bias_ref[h, chunk, :] = jnp.where(cur_ok[chunk, :], piece, NEG)
                bias_ref[A_HEADS + h, chunk, :] = jnp.where(cur_ok[chunk, :], NEG, piece)

    blocks = range(A_ROWS // BLOCK)

    def rows_of(r):
        return slice(r * BLOCK, (r + 1) * BLOCK)

    def pairs_of(kv):
        return range(kv * group_heads // 2, (kv + 1) * group_heads // 2)

    def band(r):
        if r == 0:
            k_prev, v_prev, pos_prev = kp_ref[...], vp_ref[...], pkp_ref[0]
        else:
            k_prev, v_prev, pos_prev = kc_ref[rows_of(r - 1), :], vc_ref[rows_of(r - 1), :], pkc_ref[r - 1]
        k_band = jnp.concatenate([k_prev, kc_ref[rows_of(r), :]], axis=0)
        v_band = jnp.concatenate([v_prev, vc_ref[rows_of(r), :]], axis=0)
        return k_band, v_band, pos_prev

    def scores(r, kv, parity, bands):
        group = slice((2 * kv + parity) * LANES, (2 * kv + parity + 1) * LANES)
        q_stack = jnp.concatenate([q_ref[rows_of(r), p * LANES:(p + 1) * LANES] for p in pairs_of(kv)], axis=0)
        return _dot_nt(q_stack, bands[r][0][:, group])

    def finish(r, kv, parity, s_all, bias, bands):
        v_band = bands[r][1]
        group = slice((2 * kv + parity) * LANES, (2 * kv + parity + 1) * LANES)
        probs, sink_terms = [], []
        for i, p in enumerate(pairs_of(kv)):
            head = 2 * p + parity
            s = s_all[i * BLOCK:(i + 1) * BLOCK, :]
            s_prev = s[:, :BLOCK] + bias[A_HEADS + head]
            if r == 0:
                s_prev = s_prev + no_prev
            s_cur = s[:, BLOCK:] + bias[head]
            sink = sink_ref[head] * LOG2E
            m = jnp.maximum(jnp.max(jnp.maximum(s_prev, s_cur), axis=-1, keepdims=True), sink)
            probs.append(jnp.concatenate([jnp.exp2(s_prev - m), jnp.exp2(s_cur - m)], axis=1).astype(BF16))
            sink_terms.append(jnp.exp2(sink - m))
        v_ext = jnp.concatenate([v_band[:, group], ones], axis=1)
        out = _dot(jnp.concatenate(probs, axis=0), v_ext)
        scaled = []
        for i in range(len(probs)):
            o_i = out[i * BLOCK:(i + 1) * BLOCK, :]
            scaled.append(o_i[:, :LANES] * (1.0 / (o_i[:, LANES:] + sink_terms[i])))
        return scaled

    def attend(which_blocks):
        bias = [bias_ref[h] for h in range(2 * A_HEADS)]
        bands = {r: band(r) for r in which_blocks}
        units = [(r, kv, parity) for r in which_blocks for kv in range(A_KV_HEADS) for parity in range(2)]
        raw = {units[0]: scores(*units[0], bands)}
        done = {}
        for i, unit in enumerate(units):
            if i + 1 < len(units):
                raw[units[i + 1]] = scores(*units[i + 1], bands)
            done[unit] = finish(*unit, raw.pop(unit), bias, bands)
            r, kv, parity = unit
            if parity == 1:
                low, high = done.pop((r, kv, 0)), done.pop((r, kv, 1))
                for i_pair, p in enumerate(pairs_of(kv)):
                    o_ref[rows_of(r), p * LANES:(p + 1) * LANES] = (low[i_pair] + high[i_pair]).astype(BF16)

    @pl.when(shared_ref[0] == 0)
    def _():
        fill_shift_invariant()
        shared_ref[0] = 1

    attend(blocks)

    off = jnp.zeros((1, LANES), jnp.int32)
    for r in blocks:
        pos_cur, pos_prev = pkc_ref[r], band(r)[2]
        start = jnp.min(pos_cur, axis=-1, keepdims=True)
        off_prev = jnp.where(pos_prev - start == lane - BLOCK, 0, 1)
        if r == 0:
            off_prev = off_prev * jnp.where(first_in_seq, 0, 1)
        off = off + jnp.where(pos_cur - start == lane, 0, 1) + off_prev

    @pl.when(jnp.sum(off) != 0)
    def _():
        for r in blocks:
            fill_general(pq_ref[rows_of(r), :], pkc_ref[r], band(r)[2])
            attend([r])
        shared_ref[0] = 0


def _a_attn(q, k4, v4, pos_col, pos_row, table, sinks):
    steps = SEQ // A_ROWS
    blocks_per_step = A_ROWS // BLOCK
    blocks_per_seq = SEQ // BLOCK

    def cur(b, i):
        return (b * steps + i, 0)

    def prev(b, i):
        return (b * blocks_per_seq + jnp.maximum(i * blocks_per_step - 1, 0), 0)

    def cur3(b, i):
        return (b * steps + i, 0, 0)

    def prev3(b, i):
        return (b * blocks_per_seq + jnp.maximum(i * blocks_per_step - 1, 0), 0, 0)

    return pl.pallas_call(
        _a_attn_kernel,
        grid=(BATCH, steps),
        in_specs=[
            pl.BlockSpec(memory_space=pltpu.SMEM),
            pl.BlockSpec((A_ROWS, A_QW), cur),
            pl.BlockSpec((A_ROWS, A_KW), cur),
            pl.BlockSpec((BLOCK, A_KW), prev),
            pl.BlockSpec((A_ROWS, A_KW), cur),
            pl.BlockSpec((BLOCK, A_KW), prev),
            pl.BlockSpec((A_ROWS, 1), cur),
            pl.BlockSpec((blocks_per_step, 1, BLOCK), cur3),
            pl.BlockSpec((1, 1, BLOCK), prev3),
            _const_spec((A_HEADS, LANES)),
        ],
        out_specs=pl.BlockSpec((A_ROWS, A_QW), cur),
        out_shape=jax.ShapeDtypeStruct((TOKENS, A_QW), BF16),
        scratch_shapes=[pltpu.VMEM((2 * A_HEADS, BLOCK, BLOCK), F32), pltpu.SMEM((1,), jnp.int32)],
        compiler_params=pltpu.CompilerParams(dimension_semantics=("arbitrary", "arbitrary"),
                                             vmem_limit_bytes=VMEM_LIMIT),
        name="a_attn",
    )(sinks, q, k4, k4, v4, v4, pos_col, pos_row, pos_row, table)


B_TM = 512
B_CW = Q_LORA + KV_LORA + 2 * LANES
B_HW = B_HEADS * LANES
B_VW = B_HEADS * V_DIM
B_TQ = 512
B_TK = 512


def _b_proj_kernel(x_ref, pos_ref, g_ref, win_ref, qn_ref, kvn_ref, wq_ref, wqs_ref, wk_ref, wvt_ref,
                   qg_ref, qgs_ref, kg_ref, krg_ref, krgs_ref, freq_ref, q_ref, k_ref, vt_ref):
    h = _rms_rows(x_ref[...], g_ref[...]).astype(BF16)
    c = _dot(h, win_ref[...])
    cq = _rms_rows(c[:, :Q_LORA], qn_ref[...]).astype(BF16)
    ckv_f32 = _rms_rows(c[:, Q_LORA:Q_LORA + KV_LORA], kvn_ref[...])
    ckv = ckv_f32.astype(BF16)
    kr = c[:, Q_LORA + KV_LORA:Q_LORA + KV_LORA + LANES]
    kr_partner = c[:, Q_LORA + KV_LORA + LANES:]
    q = _dot(cq, wq_ref[...])
    q_partner = _dot(cq, wqs_ref[...])
    kn = _dot(ckv, wk_ref[...])
    vt_ref[...] = _dot(wvt_ref[...], ckv_f32.T.astype(BF16)).astype(BF16)

    lane = lax.broadcasted_iota(jnp.int32, (1, LANES), 1)
    ang = pos_ref[...].astype(F32) * freq_ref[...]
    cos = jnp.cos(ang)
    sin = jnp.sin(ang)
    half = QK_ROPE // 2
    sin_signed = jnp.where(lane < QK_NOPE + half, -sin, sin)

    root_d = math.sqrt(B_DQK)
    q_const = root_d * B_DQK ** -0.5 * LOG2E
    k_rope = (kr * krg_ref[...] * cos + kr_partner * krgs_ref[...] * sin_signed) * root_d
    k_gain = kg_ref[...] * root_d
    ss_rope = jnp.sum(kr * kr, axis=-1, keepdims=True) + B_DQK * EPS
    q_cos = qg_ref[...] * cos * q_const
    q_sin = qgs_ref[...] * sin_signed * q_const
    for hd in range(B_HEADS):
        g = slice(hd * LANES, (hd + 1) * LANES)
        qh = q[:, g]
        r = lax.rsqrt(jnp.sum(qh * qh, axis=-1, keepdims=True) + B_DQK * EPS)
        q_ref[:, g] = ((qh * q_cos + q_partner[:, g] * q_sin) * r).astype(BF16)
        kh = kn[:, g]
        rk = lax.rsqrt(jnp.sum(kh * kh, axis=-1, keepdims=True) + ss_rope)
        k_ref[:, g] = ((kh * k_gain + k_rope) * rk).astype(BF16)


def _b_proj(x, pos_col, gain, w_in, qn, kvn, wq, wqs, wk, wvt, qg, qgs, kg, krg, krgs, freq):
    tm = B_TM
    tiles_per_seq = SEQ // tm
    lane_vec = _const_spec((1, LANES))
    out_spec = pl.BlockSpec((tm, B_HW), lambda i: (i, 0))
    out_sds = jax.ShapeDtypeStruct((TOKENS, B_HW), BF16)
    vt_spec = pl.BlockSpec((None, B_VW, tm), lambda i: (i // tiles_per_seq, 0, i % tiles_per_seq))
    return pl.pallas_call(
        _b_proj_kernel,
        grid=(TOKENS // tm,),
        in_specs=[pl.BlockSpec((tm, D_MODEL), lambda i: (i, 0)), pl.BlockSpec((tm, 1), lambda i: (i, 0)),
                  _const_spec((1, D_MODEL)), _const_spec((D_MODEL, B_CW)), _const_spec((1, Q_LORA)),
                  _const_spec((1, KV_LORA)), _const_spec((Q_LORA, B_HW)), _const_spec((Q_LORA, B_HW)),
                  _const_spec((KV_LORA, B_HW)), _const_spec((B_VW, KV_LORA)),
                  lane_vec, lane_vec, lane_vec, lane_vec, lane_vec, lane_vec],
        out_specs=[out_spec, out_spec, vt_spec],
        out_shape=[out_sds, out_sds, jax.ShapeDtypeStruct((BATCH, B_VW, SEQ), BF16)],
        compiler_params=pltpu.CompilerParams(vmem_limit_bytes=VMEM_LIMIT),
        name="b_proj",
    )(x, pos_col, gain, w_in, qn, kvn, wq, wqs, wk, wvt, qg, qgs, kg, krg, krgs, freq)


B_HEADS_PER_STEP = 4
B_LOOKAHEAD = 3


def _b_attn_kernel(q_ref, k_ref, vt_ref, o_ref):
    heads = range(B_HEADS_PER_STEP)
    groups = [slice(hd * LANES, (hd + 1) * LANES) for hd in heads]
    n_query_tiles = SEQ // B_TQ
    half = B_TQ // 2

    def chain(qt, hd):
        pieces = [(qt, hd, j * B_TK, B_TK, 0, B_TQ) for j in range(qt * B_TQ // B_TK)]
        return pieces + [(qt, hd, qt * B_TQ, half, 0, B_TQ), (qt, hd, qt * B_TQ + half, half, half, half)]

    def scores(qt, hd, key_lo, n_keys, q_lo, n_q):
        queries = slice(qt * B_TQ + q_lo, qt * B_TQ + q_lo + n_q)
        return _dot_nt(k_ref[key_lo:key_lo + n_keys, groups[hd]], q_ref[queries, groups[hd]])

    def accumulate(unit, s, carry):
        qt, hd, key_lo, n_keys, q_lo, n_q = unit
        m_all, acc_all = carry
        m, acc = m_all[:, q_lo:q_lo + n_q], acc_all[:, q_lo:q_lo + n_q]
        first_query = qt * B_TQ + q_lo
        if key_lo + n_keys - 1 > first_query:
            key = lax.broadcasted_iota(jnp.int32, (n_keys, n_q), 0)
            query = lax.broadcasted_iota(jnp.int32, (n_keys, n_q), 1)
            s = jnp.where(key + (key_lo - first_query) <= query, s, NEG)
        m_new = jnp.maximum(m, jnp.max(s, axis=0, keepdims=True))
        alpha = jnp.exp2(m - m_new)
        e = jnp.exp2(s - m_new).astype(BF16)
        vt_ext = jnp.concatenate([vt_ref[hd * V_DIM:(hd + 1) * V_DIM, key_lo:key_lo + n_keys],
                                  jnp.ones((V_DIM, n_keys), BF16)], axis=0)
        acc_new = alpha * acc + _dot(vt_ext, e)
        if q_lo:
            m_new = jnp.concatenate([m_all[:, :q_lo], m_new], axis=1)
            acc_new = jnp.concatenate([acc_all[:, :q_lo], acc_new], axis=1)
        return m_new, acc_new

    chains = [chain(qt, hd) for qt in reversed(range(n_query_tiles)) for hd in heads]
    units = [c[i] for i in range(len(chains[0])) for c in chains if i < len(c)]
    last = {c[-1] for c in chains}
    init = (jnp.full((1, B_TQ), NEG, F32), jnp.zeros((2 * V_DIM, B_TQ), F32))
    carry = {(qt, hd): init for qt in range(n_query_tiles) for hd in heads}
    raw = {u: scores(*u) for u in units[:B_LOOKAHEAD]}
    for i, unit in enumerate(units):
        qt, hd = unit[0], unit[1]
        if i + B_LOOKAHEAD < len(units):
            ahead = units[i + B_LOOKAHEAD]
            raw[ahead] = scores(*ahead)
        carry[(qt, hd)] = accumulate(unit, raw.pop(unit), carry[(qt, hd)])
        if hd % 2 == 1 and unit in last:
            outs = [acc[:V_DIM, :] * (1.0 / acc[V_DIM:V_DIM + 1, :])
                    for _, acc in (carry.pop((qt, hd - 1)), carry.pop((qt, hd)))]
            o_ref[qt * B_TQ:(qt + 1) * B_TQ, (hd - 1) * V_DIM:(hd + 1) * V_DIM] = (
                jnp.concatenate(outs, axis=0).T.astype(BF16))


def _b_attn(q, k, vt):
    n = B_HEADS_PER_STEP
    return pl.pallas_call(
        _b_attn_kernel,
        grid=(BATCH, B_HEADS // n),
        in_specs=[pl.BlockSpec((SEQ, n * LANES), lambda b, p: (b, p)),
                  pl.BlockSpec((SEQ, n * LANES), lambda b, p: (b, p)),
                  pl.BlockSpec((None, n * V_DIM, SEQ), lambda b, p: (b, p, 0))],
        out_specs=pl.BlockSpec((SEQ, n * V_DIM), lambda b, p: (b, p)),
        out_shape=jax.ShapeDtypeStruct((TOKENS, B_VW), BF16),
        compiler_params=pltpu.CompilerParams(vmem_limit_bytes=VMEM_LIMIT),
        name="b_attn",
    )(q, k, vt)


def _lohi(cols):
    z = jnp.zeros_like(cols)
    return jnp.concatenate([cols, z, z, cols], axis=1)


def _prep_a(w_in, q_gain, k_gain):
    wq = w_in[:, :A_QW]
    k0 = w_in[:, A_QW:A_QW + A_HEAD_DIM]
    k1 = w_in[:, A_QW + A_HEAD_DIM:A_QW + 2 * A_HEAD_DIM]
    v0 = w_in[:, A_QW + 2 * A_HEAD_DIM:A_QW + 3 * A_HEAD_DIM]
    v1 = w_in[:, A_QW + 3 * A_HEAD_DIM:]
    w = jnp.concatenate([wq, _lohi(k0), _lohi(k1), _lohi(v0), _lohi(v1)], axis=1).astype(BF16)
    qg = jnp.concatenate([q_gain, q_gain])[None, :]
    z = jnp.zeros_like(k_gain)
    kg = jnp.concatenate([k_gain, z, z, k_gain, k_gain, z, z, k_gain])[None, :]
    return w, qg, kg


def _head_groups(w, per_head, src_lo, src_hi, dst_lo):
    rows = w.shape[0]
    w3 = w.reshape(rows, B_HEADS, per_head)[:, :, src_lo:src_hi]
    out = jnp.zeros((rows, B_HEADS, LANES), w.dtype)
    out = out.at[:, :, dst_lo:dst_lo + (src_hi - src_lo)].set(w3)
    return out


def _prep_b(w_in, w_uq, w_ukv, q_gain, k_gain):
    half = QK_ROPE // 2
    t1 = slice(QK_NOPE, QK_NOPE + half)
    t2 = slice(QK_NOPE + half, B_DQK)
    rows = w_in.shape[0]
    rope_cols = w_in[:, Q_LORA + KV_LORA:]
    zeros = lambda n: jnp.zeros((rows, n), w_in.dtype)
    rope_group = jnp.concatenate([zeros(QK_NOPE), rope_cols, zeros(LANES - B_DQK)], axis=1)
    partner_group = jnp.concatenate(
        [zeros(QK_NOPE), rope_cols[:, half:], rope_cols[:, :half], zeros(LANES - B_DQK)], axis=1)
    win = jnp.concatenate([w_in[:, :Q_LORA + KV_LORA], rope_group, partner_group], axis=1).astype(BF16)

    wq = _head_groups(w_uq, B_DQK, 0, B_DQK, 0)
    wqs = (_head_groups(w_uq, B_DQK, t2.start, t2.stop, t1.start)
           + _head_groups(w_uq, B_DQK, t1.start, t1.stop, t2.start))
    wk = _head_groups(w_ukv, QK_NOPE + V_DIM, 0, QK_NOPE, 0)
    wvt = w_ukv.reshape(KV_LORA, B_HEADS, QK_NOPE + V_DIM)[:, :, QK_NOPE:].reshape(KV_LORA, B_VW).T.astype(BF16)
    flat = lambda a: a.reshape(a.shape[0], B_HW).astype(BF16)

    def lane_vec(pieces):
        out = jnp.zeros((LANES,), F32)
        for lo, vals in pieces:
            out = out.at[lo:lo + vals.shape[0]].set(vals)
        return out[None, :]

    qg = lane_vec([(0, q_gain)])
    qgs = lane_vec([(t1.start, q_gain[t2]), (t2.start, q_gain[t1])])
    kg = lane_vec([(0, k_gain[:QK_NOPE])])
    krg = lane_vec([(QK_NOPE, k_gain[QK_NOPE:])])
    krgs = lane_vec([(t1.start, k_gain[t2]), (t2.start, k_gain[t1])])
    return win, flat(wq), flat(wqs), flat(wk), wvt, qg, qgs, kg, krg, krgs


def _rope_freq():
    inv = ROPE_BASE ** (-np.arange(0, QK_ROPE, 2, dtype=np.float32) / QK_ROPE)
    out = np.zeros((1, LANES), np.float32)
    half = QK_ROPE // 2
    out[0, QK_NOPE:QK_NOPE + half] = inv
    out[0, QK_NOPE + half:B_DQK] = inv
    return jnp.asarray(out)


def kernel(x, positions, rel_bias, ffn_norm1, ffn1_wg, ffn1_wu, ffn1_wd, mix_norm, ffn_norm2, ffn2_wg,
           ffn2_wu, ffn2_wd, a_w_in, a_q_gain, a_k_gain, a_sinks, a_w_out, b_w_in, b_q_norm, b_kv_norm,
           b_w_uq, b_w_ukv, b_q_gain, b_k_gain, b_w_out):
    assert x.shape == (BATCH, SEQ, D_MODEL) and positions.shape == (BATCH, SEQ)
    xt = x.reshape(TOKENS, D_MODEL)
    pos_col = positions.reshape(TOKENS, 1)
    pos_row = positions.reshape(TOKENS // BLOCK, 1, BLOCK)
    table = jnp.zeros((A_HEADS, LANES), F32).at[:, :NUM_BUCKETS].set(rel_bias.T)
    bf = lambda w: w.astype(BF16)
    row = lambda v: v[None, :]
    gain1, gain2 = ffn_norm1[:, None, :], ffn_norm2[:, None, :]
    weights = _first_ffn_weights(0, ffn1_wg, ffn1_wu, ffn1_wd)

    for i in range(DEPTH):
        xt, weights = _ffn(xt, gain1, i, *weights, cast_ahead=(i, ffn2_wg, ffn2_wu, ffn2_wd))
        j = i // N_MIXERS
        if i % N_MIXERS == 0:
            w, qg, kg = _prep_a(a_w_in[j], a_q_gain[j], a_k_gain[j])
            q, k4, v4 = _a_qkv(xt, row(mix_norm[i]), w, qg, kg)
            attn = _a_attn(q, k4, v4, pos_col, pos_row, table, a_sinks[j])
            w_out = bf(a_w_out[j])
        else:
            prep = _prep_b(b_w_in[j], b_w_uq[j], b_w_ukv[j], b_q_gain[j], b_k_gain[j])
            win, wq, wqs, wk, wvt, qg, qgs, kg, krg, krgs = prep
            q, k, vt = _b_proj(xt, pos_col, row(mix_norm[i]), win, row(b_q_norm[j]), row(b_kv_norm[j]),
                              wq, wqs, wk, wvt, qg, qgs, kg, krg, krgs, _rope_freq())
            attn = _b_attn(q, k, vt)
            w_out = bf(b_w_out[j])
        ahead = (i + 1, ffn1_wg, ffn1_wu, ffn1_wd) if i + 1 < DEPTH else None
        xt, weights = _ffn(xt, gain2, i, *weights, attn=attn, w_out=w_out, cast_ahead=ahead)
    return xt.reshape(BATCH, SEQ, D_MODEL)
```

```python
import functools
import math

import numpy as np
import jax
import jax.numpy as jnp
from jax import lax
from jax.experimental import pallas as pl
from jax.experimental.pallas import tpu as pltpu

D_MODEL = 1024
BATCH = 8
SEQ = 2048
DEPTH = 2
N_MIXERS = 2
A_HEADS = 16
A_KV_HEADS = 2
A_HEAD_DIM = 64
WINDOW = 128
BLOCK = 128
NUM_BUCKETS = 32
MAX_DISTANCE = 128
B_HEADS = 16
Q_LORA = 256
KV_LORA = 128
QK_NOPE = 64
QK_ROPE = 32
V_DIM = 64
ROPE_BASE = 10000.0
D_FF = 2816
EPS = 1e-6
NEG = -1e30

TOKENS = BATCH * SEQ
LANES = 128
HALF = LANES // 2
B_DQK = QK_NOPE + QK_ROPE
VMEM_LIMIT = 60 * 1024 * 1024

LOG2E = math.log2(math.e)

F32 = jnp.float32
BF16 = jnp.bfloat16


def _rms_rows(x, gain):
    return x * lax.rsqrt(jnp.mean(x * x, axis=-1, keepdims=True) + EPS) * gain


def _dot(a, b):
    return jnp.dot(a, b, preferred_element_type=F32)


def _dot_nt(a, b):
    return lax.dot_general(a, b, (((1,), (1,)), ((), ())), preferred_element_type=F32)


def _const_spec(shape):
    nd = len(shape)
    return pl.BlockSpec(shape, lambda *_: (0,) * nd, pipeline_mode=pl.Buffered(1))


FFN_TM = 1024
MXU_WIDTH = 256
FFN_CHUNKS = ((0, 4 * MXU_WIDTH), (4 * MXU_WIDTH, 8 * MXU_WIDTH), (8 * MXU_WIDTH, D_FF))


def _ffn_body(x, g_ref, wgu_ref, wd_ref, o_ref):
    h = _rms_rows(x, g_ref[...]).astype(BF16)
    down = None
    for lo, hi in FFN_CHUNKS:
        gate_up = _dot(h, wgu_ref[:, 2 * lo:2 * hi])
        acts = []
        for c in range((hi - lo) // MXU_WIDTH):
            gate = gate_up[:, 2 * c * MXU_WIDTH:(2 * c + 1) * MXU_WIDTH]
            up = gate_up[:, (2 * c + 1) * MXU_WIDTH:(2 * c + 2) * MXU_WIDTH]
            acts.append((gate * jax.nn.sigmoid(gate) * up).astype(BF16))
        part = _dot(jnp.concatenate(acts, axis=1), wd_ref[lo:hi, :])
        down = part if down is None else down + part
    o_ref[...] = x + 0.5 * down


def _ffn_kernel(*refs, project, cast_ahead):
    refs = list(refs)
    x = refs.pop(0)[...]
    if project:
        a_ref, wo_ref = refs.pop(0), refs.pop(0)
        x = x + _dot(a_ref[...], wo_ref[...])
    g_ref, wgu_ref, wd_ref = refs[:3]
    refs = refs[3:]
    if cast_ahead:
        _cast_ffn_weights(*refs[:3], *refs[4:6])
        refs = refs[3:]
    _ffn_body(x, g_ref, wgu_ref, wd_ref, refs[0])


def _cast_ffn_weights(wg_src, wu_src, wd_src, wgu_dst, wd_dst):
    for c in range(D_FF // MXU_WIDTH):
        cols = slice(c * MXU_WIDTH, (c + 1) * MXU_WIDTH)
        wgu_dst[:, 2 * c * MXU_WIDTH:(2 * c + 1) * MXU_WIDTH] = wg_src[:, cols].astype(BF16)
        wgu_dst[:, (2 * c + 1) * MXU_WIDTH:(2 * c + 2) * MXU_WIDTH] = wu_src[:, cols].astype(BF16)
    wd_dst[...] = wd_src[...].astype(BF16)


def _cast_specs(layer, stacks, steps):
    in_specs = [pl.BlockSpec((None, w.shape[1] // steps, w.shape[2]), lambda i: (layer, i, 0)) for w in stacks]
    out_specs = [pl.BlockSpec((D_MODEL // steps, 2 * D_FF), lambda i: (i, 0)),
                 pl.BlockSpec((D_FF // steps, D_MODEL), lambda i: (i, 0))]
    out_shape = [jax.ShapeDtypeStruct((D_MODEL, 2 * D_FF), BF16), jax.ShapeDtypeStruct((D_FF, D_MODEL), BF16)]
    return in_specs, out_specs, out_shape


def _first_ffn_weights(layer, wg, wu, wd):
    steps = 8
    in_specs, out_specs, out_shape = _cast_specs(layer, (wg, wu, wd), steps)
    return pl.pallas_call(_cast_ffn_weights, grid=(steps,), in_specs=in_specs, out_specs=out_specs,
                          out_shape=out_shape, name="cast_ffn_weights")(wg, wu, wd)


def _ffn(x, gain, layer, wgu, wd, attn=None, w_out=None, cast_ahead=None):
    tm = FFN_TM
    steps = TOKENS // tm
    row_spec = pl.BlockSpec((tm, D_MODEL), lambda i: (i, 0))
    in_specs, args = [row_spec], [x]
    if attn is not None:
        in_specs += [row_spec, _const_spec((D_MODEL, D_MODEL))]
        args += [attn, w_out]
    in_specs += [pl.BlockSpec((None, 1, D_MODEL), lambda i: (layer, 0, 0), pipeline_mode=pl.Buffered(1)),
                 _const_spec((D_MODEL, 2 * D_FF)), _const_spec((D_FF, D_MODEL))]
    args += [gain, wgu, wd]
    out_specs = [row_spec]
    out_shape = [jax.ShapeDtypeStruct((TOKENS, D_MODEL), F32)]
    if cast_ahead is not None:
        next_layer, *stacks = cast_ahead
        cast_in, cast_out, cast_shape = _cast_specs(next_layer, stacks, steps)
        in_specs += cast_in
        args += stacks
        out_specs += cast_out
        out_shape += cast_shape
    outs = pl.pallas_call(
        functools.partial(_ffn_kernel, project=attn is not None, cast_ahead=cast_ahead is not None),
        grid=(steps,),
        in_specs=in_specs,
        out_specs=out_specs,
        out_shape=out_shape,
        compiler_params=pltpu.CompilerParams(vmem_limit_bytes=VMEM_LIMIT),
        name="ffn" if attn is None else "proj_ffn",
    )(*args)
    return outs[0], tuple(outs[1:])


A_TM = 512
A_QW = A_HEADS * A_HEAD_DIM
A_KW = 4 * LANES
A_PAIRS = A_HEADS // 2
A_ROWS = 512


def _a_qkv_kernel(x_ref, g_ref, w_ref, qg_ref, kg_ref, q_ref, k_ref, v_ref):
    h = _rms_rows(x_ref[...], g_ref[...]).astype(BF16)
    qkv = _dot(h, w_ref[...])
    lane = lax.broadcasted_iota(jnp.int32, (1, LANES), 1)
    low = lane < HALF
    q_table = qg_ref[...] * (math.sqrt(A_HEAD_DIM) * A_HEAD_DIM ** -0.5 * LOG2E)
    for p in range(A_PAIRS):
        qp = qkv[:, p * LANES:(p + 1) * LANES]
        sq = qp * qp
        ss_lo = jnp.sum(jnp.where(low, sq, 0.0), axis=-1, keepdims=True)
        ss_hi = jnp.sum(jnp.where(low, 0.0, sq), axis=-1, keepdims=True)
        r = jnp.where(low, lax.rsqrt(ss_lo + A_HEAD_DIM * EPS), lax.rsqrt(ss_hi + A_HEAD_DIM * EPS))
        q_ref[:, p * LANES:(p + 1) * LANES] = (qp * q_table * r).astype(BF16)
    for j in range(A_KW // LANES):
        kj = qkv[:, A_QW + j * LANES:A_QW + (j + 1) * LANES]
        ms = jnp.sum(kj * kj, axis=-1, keepdims=True) * (1.0 / A_HEAD_DIM)
        k_ref[:, j * LANES:(j + 1) * LANES] = (
            kj * lax.rsqrt(ms + EPS) * kg_ref[:, j * LANES:(j + 1) * LANES]).astype(BF16)
    v_ref[...] = qkv[:, A_QW + A_KW:].astype(BF16)


def _a_qkv(x, gain, w, qg, kg):
    tm = A_TM
    width = A_QW + 2 * A_KW
    return pl.pallas_call(
        _a_qkv_kernel,
        grid=(TOKENS // tm,),
        in_specs=[pl.BlockSpec((tm, D_MODEL), lambda i: (i, 0)), _const_spec((1, D_MODEL)),
                  _const_spec((D_MODEL, width)), _const_spec((1, LANES)), _const_spec((1, A_KW))],
        out_specs=[pl.BlockSpec((tm, A_QW), lambda i: (i, 0)), pl.BlockSpec((tm, A_KW), lambda i: (i, 0)),
                   pl.BlockSpec((tm, A_KW), lambda i: (i, 0))],
        out_shape=[jax.ShapeDtypeStruct((TOKENS, A_QW), BF16), jax.ShapeDtypeStruct((TOKENS, A_KW), BF16),
                   jax.ShapeDtypeStruct((TOKENS, A_KW), BF16)],
        compiler_params=pltpu.CompilerParams(vmem_limit_bytes=VMEM_LIMIT),
        name="a_qkv",
    )(x, gain, w, qg, kg)


def _t5_bucket(dist):
    n = jnp.maximum(dist, 0)
    max_exact = NUM_BUCKETS // 2
    large = max_exact + (jnp.log(jnp.maximum(n, 1).astype(F32) / max_exact)
                         / math.log(MAX_DISTANCE / max_exact)
                         * (NUM_BUCKETS - max_exact)).astype(jnp.int32)
    large = jnp.minimum(large, NUM_BUCKETS - 1)
    return jnp.where(n < max_exact, n, large)


SUBLANES = 8


def _a_attn_kernel(sink_ref, q_ref, kc_ref, kp_ref, vc_ref, vp_ref, pq_ref, pkc_ref, pkp_ref, tbl_ref, o_ref,
                   bias_ref, shared_ref):
    row = lax.broadcasted_iota(jnp.int32, (BLOCK, BLOCK), 0)
    col = lax.broadcasted_iota(jnp.int32, (BLOCK, BLOCK), 1)
    cur_ok = col <= row
    first_in_seq = pl.program_id(1) == 0
    no_prev = jnp.where(first_in_seq, NEG, 0.0)
    tables = [jnp.broadcast_to(tbl_ref[h:h + 1, :] * LOG2E, (SUBLANES, LANES)) for h in range(A_HEADS)]
    ones = jnp.ones((2 * BLOCK, LANES), BF16)
    group_heads = A_HEADS // A_KV_HEADS
    lane = lax.broadcasted_iota(jnp.int32, (1, LANES), 1)

    @pl.when(jnp.logical_and(pl.program_id(0) == 0, first_in_seq))
    def _():
        shared_ref[0] = 0

    def fill_shift_invariant():
        back = jnp.broadcast_to((BLOCK - lane) & (BLOCK - 1), (SUBLANES, LANES))
        idx = _t5_bucket(back)
        for h in range(A_HEADS):
            base = jnp.take_along_axis(tables[h], idx, axis=1)
            base = jnp.broadcast_to(base[0:1, :], (BLOCK, BLOCK))
            tile = pltpu.roll(base, 0, 1, stride=1, stride_axis=0)
            bias_ref[h] = jnp.where(cur_ok, tile, NEG)
            bias_ref[A_HEADS + h] = jnp.where(cur_ok, NEG, tile)

    def fill_general(pos_q, pos_cur, pos_prev):
        bucket = jnp.where(cur_ok, _t5_bucket(pos_q - pos_cur), _t5_bucket(pos_q - pos_prev))
        for c in range(BLOCK // SUBLANES):
            chunk = slice(c * SUBLANES, (c + 1) * SUBLANES)
            for h in range(A_HEADS):
                piece = jnp.take_along_axis(tables[h], bucket[chunk, :], axis=1)
                bias_ref[h, chunk, :] = jnp.where(cur_ok[chunk, :], piece, NEG)
                bias_ref[A_HEADS + h, chunk, :] = jnp.where(cur_ok[chunk, :], NEG, piece)

    blocks = range(A_ROWS // BLOCK)

    def rows_of(r):
        return slice(r * BLOCK, (r + 1) * BLOCK)

    def pairs_of(kv):
        return range(kv * group_heads // 2, (kv + 1) * group_heads // 2)

    def band(r):
        if r == 0:
            k_prev, v_prev, pos_prev = kp_ref[...], vp_ref[...], pkp_ref[0]
        else:
            k_prev, v_prev, pos_prev = kc_ref[rows_of(r - 1), :], vc_ref[rows_of(r - 1), :], pkc_ref[r - 1]
        k_band = jnp.concatenate([k_prev, kc_ref[rows_of(r), :]], axis=0)
        v_band = jnp.concatenate([v_prev, vc_ref[rows_of(r), :]], axis=0)
        return k_band, v_band, pos_prev

    def scores(r, kv, parity, bands):
        group = slice((2 * kv + parity) * LANES, (2 * kv + parity + 1) * LANES)
        q_stack = jnp.concatenate([q_ref[rows_of(r), p * LANES:(p + 1) * LANES] for p in pairs_of(kv)], axis=0)
        return _dot_nt(q_stack, bands[r][0][:, group])

    def finish(r, kv, parity, s_all, bias, bands):
        v_band = bands[r][1]
        group = slice((2 * kv + parity) * LANES, (2 * kv + parity + 1) * LANES)
        probs, sink_terms = [], []
        for i, p in enumerate(pairs_of(kv)):
            head = 2 * p + parity
            s = s_all[i * BLOCK:(i + 1) * BLOCK, :]
            s_prev = s[:, :BLOCK] + bias[A_HEADS + head]
            if r == 0:
                s_prev = s_prev + no_prev
            s_cur = s[:, BLOCK:] + bias[head]
            sink = sink_ref[head] * LOG2E
            m = jnp.maximum(jnp.max(jnp.maximum(s_prev, s_cur), axis=-1, keepdims=True), sink)
            probs.append(jnp.concatenate([jnp.exp2(s_prev - m), jnp.exp2(s_cur - m)], axis=1).astype(BF16))
            sink_terms.append(jnp.exp2(sink - m))
        v_ext = jnp.concatenate([v_band[:, group], ones], axis=1)
        out = _dot(jnp.concatenate(probs, axis=0), v_ext)
        scaled = []
        for i in range(len(probs)):
            o_i = out[i * BLOCK:(i + 1) * BLOCK, :]
            scaled.append(o_i[:, :LANES] * (1.0 / (o_i[:, LANES:] + sink_terms[i])))
        return scaled

    def attend(which_blocks):
        bias = [bias_ref[h] for h in range(2 * A_HEADS)]
        bands = {r: band(r) for r in which_blocks}
        units = [(r, kv, parity) for r in which_blocks for kv in range(A_KV_HEADS) for parity in range(2)]
        raw = {units[0]: scores(*units[0], bands)}
        done = {}
        for i, unit in enumerate(units):
            if i + 1 < len(units):
                raw[units[i + 1]] = scores(*units[i + 1], bands)
            done[unit] = finish(*unit, raw.pop(unit), bias, bands)
            r, kv, parity = unit
            if parity == 1:
                low, high = done.pop((r, kv, 0)), done.pop((r, kv, 1))
                for i_pair, p in enumerate(pairs_of(kv)):
                    o_ref[rows_of(r), p * LANES:(p + 1) * LANES] = (low[i_pair] + high[i_pair]).astype(BF16)

    @pl.when(shared_ref[0] == 0)
    def _():
        fill_shift_invariant()
        shared_ref[0] = 1

    attend(blocks)

    off = jnp.zeros((1, LANES), jnp.int32)
    for r in blocks:
        pos_cur, pos_prev = pkc_ref[r], band(r)[2]
        start = jnp.min(pos_cur, axis=-1, keepdims=True)
        off_prev = jnp.where(pos_prev - start == lane - BLOCK, 0, 1)
        if r == 0:
            off_prev = off_prev * jnp.where(first_in_seq, 0, 1)
        off = off + jnp.where(pos_cur - start == lane, 0, 1) + off_prev

    @pl.when(jnp.sum(off) != 0)
    def _():
        for r in blocks:
            fill_general(pq_ref[rows_of(r), :], pkc_ref[r], band(r)[2])
            attend([r])
        shared_ref[0] = 0


def _a_attn(q, k4, v4, pos_col, pos_row, table, sinks):
    steps = SEQ // A_ROWS
    blocks_per_step = A_ROWS // BLOCK
    blocks_per_seq = SEQ // BLOCK

    def cur(b, i):
        return (b * steps + i, 0)

    def prev(b, i):
        return (b * blocks_per_seq + jnp.maximum(i * blocks_per_step - 1, 0), 0)

    def cur3(b, i):
        return (b * steps + i, 0, 0)

    def prev3(b, i):
        return (b * blocks_per_seq + jnp.maximum(i * blocks_per_step - 1, 0), 0, 0)

    return pl.pallas_call(
        _a_attn_kernel,
        grid=(BATCH, steps),
        in_specs=[
            pl.BlockSpec(memory_space=pltpu.SMEM),
            pl.BlockSpec((A_ROWS, A_QW), cur),
            pl.BlockSpec((A_ROWS, A_KW), cur),
            pl.BlockSpec((BLOCK, A_KW), prev),
            pl.BlockSpec((A_ROWS, A_KW), cur),
            pl.BlockSpec((BLOCK, A_KW), prev),
            pl.BlockSpec((A_ROWS, 1), cur),
            pl.BlockSpec((blocks_per_step, 1, BLOCK), cur3),
            pl.BlockSpec((1, 1, BLOCK), prev3),
            _const_spec((A_HEADS, LANES)),
        ],
        out_specs=pl.BlockSpec((A_ROWS, A_QW), cur),
        out_shape=jax.ShapeDtypeStruct((TOKENS, A_QW), BF16),
        scratch_shapes=[pltpu.VMEM((2 * A_HEADS, BLOCK, BLOCK), F32), pltpu.SMEM((1,), jnp.int32)],
        compiler_params=pltpu.CompilerParams(dimension_semantics=("arbitrary", "arbitrary"),
                                             vmem_limit_bytes=VMEM_LIMIT),
        name="a_attn",
    )(sinks, q, k4, k4, v4, v4, pos_col, pos_row, pos_row, table)


B_TM = 1024
B_CW = Q_LORA + KV_LORA + 2 * LANES
B_HW = B_HEADS * LANES
B_VW = B_HEADS * V_DIM
B_TQ = 512
B_TK = 512


def _b_proj_kernel(x_ref, pos_ref, g_ref, win_ref, qn_ref, kvn_ref, wq_ref, wqs_ref, wk_ref, wvt_ref,
                   qg_ref, qgs_ref, kg_ref, krg_ref, krgs_ref, freq_ref, q_ref, k_ref, vt_ref):
    h = _rms_rows(x_ref[...], g_ref[...]).astype(BF16)
    c = _dot(h, win_ref[...])
    cq = _rms_rows(c[:, :Q_LORA], qn_ref[...]).astype(BF16)
    ckv_f32 = _rms_rows(c[:, Q_LORA:Q_LORA + KV_LORA], kvn_ref[...])
    ckv = ckv_f32.astype(BF16)
    kr = c[:, Q_LORA + KV_LORA:Q_LORA + KV_LORA + LANES]
    kr_partner = c[:, Q_LORA + KV_LORA + LANES:]
    q = _dot(cq, wq_ref[...])
    q_partner = _dot(cq, wqs_ref[...])
    kn = _dot(ckv, wk_ref[...])
    vt_ref[...] = _dot(wvt_ref[...], ckv_f32.T.astype(BF16)).astype(BF16)

    lane = lax.broadcasted_iota(jnp.int32, (1, LANES), 1)
    ang = pos_ref[...].astype(F32) * freq_ref[...]
    cos = jnp.cos(ang)
    sin = jnp.sin(ang)
    half = QK_ROPE // 2
    sin_signed = jnp.where(lane < QK_NOPE + half, -sin, sin)

    root_d = math.sqrt(B_DQK)
    q_const = root_d * B_DQK ** -0.5 * LOG2E
    k_rope = (kr * krg_ref[...] * cos + kr_partner * krgs_ref[...] * sin_signed) * root_d
    k_gain = kg_ref[...] * root_d
    ss_rope = jnp.sum(kr * kr, axis=-1, keepdims=True) + B_DQK * EPS
    q_cos = qg_ref[...] * cos * q_const
    q_sin = qgs_ref[...] * sin_signed * q_const
    for hd in range(B_HEADS):
        g = slice(hd * LANES, (hd + 1) * LANES)
        qh = q[:, g]
        r = lax.rsqrt(jnp.sum(qh * qh, axis=-1, keepdims=True) + B_DQK * EPS)
        q_ref[:, g] = ((qh * q_cos + q_partner[:, g] * q_sin) * r).astype(BF16)
        kh = kn[:, g]
        rk = lax.rsqrt(jnp.sum(kh * kh, axis=-1, keepdims=True) + ss_rope)
        k_ref[:, g] = ((kh * k_gain + k_rope) * rk).astype(BF16)


def _b_proj(x, pos_col, gain, w_in, qn, kvn, wq, wqs, wk, wvt, qg, qgs, kg, krg, krgs, freq):
    tm = B_TM
    tiles_per_seq = SEQ // tm
    lane_vec = _const_spec((1, LANES))
    out_spec = pl.BlockSpec((tm, B_HW), lambda i: (i, 0))
    out_sds = jax.ShapeDtypeStruct((TOKENS, B_HW), BF16)
    vt_spec = pl.BlockSpec((None, B_VW, tm), lambda i: (i // tiles_per_seq, 0, i % tiles_per_seq))
    return pl.pallas_call(
        _b_proj_kernel,
        grid=(TOKENS // tm,),
        in_specs=[pl.BlockSpec((tm, D_MODEL), lambda i: (i, 0)), pl.BlockSpec((tm, 1), lambda i: (i, 0)),
                  _const_spec((1, D_MODEL)), _const_spec((D_MODEL, B_CW)), _const_spec((1, Q_LORA)),
                  _const_spec((1, KV_LORA)), _const_spec((Q_LORA, B_HW)), _const_spec((Q_LORA, B_HW)),
                  _const_spec((KV_LORA, B_HW)), _const_spec((B_VW, KV_LORA)),
                  lane_vec, lane_vec, lane_vec, lane_vec, lane_vec, lane_vec],
        out_specs=[out_spec, out_spec, vt_spec],
        out_shape=[out_sds, out_sds, jax.ShapeDtypeStruct((BATCH, B_VW, SEQ), BF16)],
        compiler_params=pltpu.CompilerParams(vmem_limit_bytes=VMEM_LIMIT),
        name="b_proj",
    )(x, pos_col, gain, w_in, qn, kvn, wq, wqs, wk, wvt, qg, qgs, kg, krg, krgs, freq)


B_HEADS_PER_STEP = 4
B_LOOKAHEAD = 3


def _b_attn_kernel(q_ref, k_ref, vt_ref, o_ref):
    heads = range(B_HEADS_PER_STEP)
    groups = [slice(hd * LANES, (hd + 1) * LANES) for hd in heads]
    n_query_tiles = SEQ // B_TQ
    half = B_TQ // 2

    def chain(qt, hd):
        pieces = [(qt, hd, j * B_TK, B_TK, 0, B_TQ) for j in range(qt * B_TQ // B_TK)]
        return pieces + [(qt, hd, qt * B_TQ, half, 0, B_TQ), (qt, hd, qt * B_TQ + half, half, half, half)]

    def scores(qt, hd, key_lo, n_keys, q_lo, n_q):
        queries = slice(qt * B_TQ + q_lo, qt * B_TQ + q_lo + n_q)
        return _dot_nt(k_ref[key_lo:key_lo + n_keys, groups[hd]], q_ref[queries, groups[hd]])

    def accumulate(unit, s, carry):
        qt, hd, key_lo, n_keys, q_lo, n_q = unit
        m_all, acc_all = carry
        m, acc = m_all[:, q_lo:q_lo + n_q], acc_all[:, q_lo:q_lo + n_q]
        first_query = qt * B_TQ + q_lo
        if key_lo + n_keys - 1 > first_query:
            key = lax.broadcasted_iota(jnp.int32, (n_keys, n_q), 0)
            query = lax.broadcasted_iota(jnp.int32, (n_keys, n_q), 1)
            s = jnp.where(key + (key_lo - first_query) <= query, s, NEG)
        m_new = jnp.maximum(m, jnp.max(s, axis=0, keepdims=True))
        alpha = jnp.exp2(m - m_new)
        e = jnp.exp2(s - m_new).astype(BF16)
        vt_ext = jnp.concatenate([vt_ref[hd * V_DIM:(hd + 1) * V_DIM, key_lo:key_lo + n_keys],
                                  jnp.ones((V_DIM, n_keys), BF16)], axis=0)
        acc_new = alpha * acc + _dot(vt_ext, e)
        if q_lo:
            m_new = jnp.concatenate([m_all[:, :q_lo], m_new], axis=1)
            acc_new = jnp.concatenate([acc_all[:, :q_lo], acc_new], axis=1)
        return m_new, acc_new

    chains = [chain(qt, hd) for qt in reversed(range(n_query_tiles)) for hd in heads]
    units = [c[i] for i in range(len(chains[0])) for c in chains if i < len(c)]
    last = {c[-1] for c in chains}
    init = (jnp.full((1, B_TQ), NEG, F32), jnp.zeros((2 * V_DIM, B_TQ), F32))
    carry = {(qt, hd): init for qt in range(n_query_tiles) for hd in heads}
    raw = {u: scores(*u) for u in units[:B_LOOKAHEAD]}
    for i, unit in enumerate(units):
        qt, hd = unit[0], unit[1]
        if i + B_LOOKAHEAD < len(units):
            ahead = units[i + B_LOOKAHEAD]
            raw[ahead] = scores(*ahead)
        carry[(qt, hd)] = accumulate(unit, raw.pop(unit), carry[(qt, hd)])
        if hd % 2 == 1 and unit in last:
            outs = [acc[:V_DIM, :] * (1.0 / acc[V_DIM:V_DIM + 1, :])
                    for _, acc in (carry.pop((qt, hd - 1)), carry.pop((qt, hd)))]
            o_ref[qt * B_TQ:(qt + 1) * B_TQ, (hd - 1) * V_DIM:(hd + 1) * V_DIM] = (
                jnp.concatenate(outs, axis=0).T.astype(BF16))


def _b_attn(q, k, vt):
    n = B_HEADS_PER_STEP
    return pl.pallas_call(
        _b_attn_kernel,
        grid=(BATCH, B_HEADS // n),
        in_specs=[pl.BlockSpec((SEQ, n * LANES), lambda b, p: (b, p)),
                  pl.BlockSpec((SEQ, n * LANES), lambda b, p: (b, p)),
                  pl.BlockSpec((None, n * V_DIM, SEQ), lambda b, p: (b, p, 0))],
        out_specs=pl.BlockSpec((SEQ, n * V_DIM), lambda b, p: (b, p)),
        out_shape=jax.ShapeDtypeStruct((TOKENS, B_VW), BF16),
        compiler_params=pltpu.CompilerParams(vmem_limit_bytes=VMEM_LIMIT),
        name="b_attn",
    )(q, k, vt)


def _lohi(cols):
    z = jnp.zeros_like(cols)
    return jnp.concatenate([cols, z, z, cols], axis=1)


def _prep_a(w_in, q_gain, k_gain):
    wq = w_in[:, :A_QW]
    k0 = w_in[:, A_QW:A_QW + A_HEAD_DIM]
    k1 = w_in[:, A_QW + A_HEAD_DIM:A_QW + 2 * A_HEAD_DIM]
    v0 = w_in[:, A_QW + 2 * A_HEAD_DIM:A_QW + 3 * A_HEAD_DIM]
    v1 = w_in[:, A_QW + 3 * A_HEAD_DIM:]
    w = jnp.concatenate([wq, _lohi(k0), _lohi(k1), _lohi(v0), _lohi(v1)], axis=1).astype(BF16)
    qg = jnp.concatenate([q_gain, q_gain])[None, :]
    z = jnp.zeros_like(k_gain)
    kg = jnp.concatenate([k_gain, z, z, k_gain, k_gain, z, z, k_gain])[None, :]
    return w, qg, kg


def _head_groups(w, per_head, src_lo, src_hi, dst_lo):
    rows = w.shape[0]
    w3 = w.reshape(rows, B_HEADS, per_head)[:, :, src_lo:src_hi]
    out = jnp.zeros((rows, B_HEADS, LANES), w.dtype)
    out = out.at[:, :, dst_lo:dst_lo + (src_hi - src_lo)].set(w3)
    return out


def _prep_b(w_in, w_uq, w_ukv, q_gain, k_gain):
    half = QK_ROPE // 2
    t1 = slice(QK_NOPE, QK_NOPE + half)
    t2 = slice(QK_NOPE + half, B_DQK)
    rows = w_in.shape[0]
    rope_cols = w_in[:, Q_LORA + KV_LORA:]
    zeros = lambda n: jnp.zeros((rows, n), w_in.dtype)
    rope_group = jnp.concatenate([zeros(QK_NOPE), rope_cols, zeros(LANES - B_DQK)], axis=1)
    partner_group = jnp.concatenate(
        [zeros(QK_NOPE), rope_cols[:, half:], rope_cols[:, :half], zeros(LANES - B_DQK)], axis=1)
    win = jnp.concatenate([w_in[:, :Q_LORA + KV_LORA], rope_group, partner_group], axis=1).astype(BF16)

    wq = _head_groups(w_uq, B_DQK, 0, B_DQK, 0)
    wqs = (_head_groups(w_uq, B_DQK, t2.start, t2.stop, t1.start)
           + _head_groups(w_uq, B_DQK, t1.start, t1.stop, t2.start))
    wk = _head_groups(w_ukv, QK_NOPE + V_DIM, 0, QK_NOPE, 0)
    wvt = w_ukv.reshape(KV_LORA, B_HEADS, QK_NOPE + V_DIM)[:, :, QK_NOPE:].reshape(KV_LORA, B_VW).T.astype(BF16)
    flat = lambda a: a.reshape(a.shape[0], B_HW).astype(BF16)

    def lane_vec(pieces):
        out = jnp.zeros((LANES,), F32)
        for lo, vals in pieces:
            out = out.at[lo:lo + vals.shape[0]].set(vals)
        return out[None, :]

    qg = lane_vec([(0, q_gain)])
    qgs = lane_vec([(t1.start, q_gain[t2]), (t2.start, q_gain[t1])])
    kg = lane_vec([(0, k_gain[:QK_NOPE])])
    krg = lane_vec([(QK_NOPE, k_gain[QK_NOPE:])])
    krgs = lane_vec([(t1.start, k_gain[t2]), (t2.start, k_gain[t1])])
    return win, flat(wq), flat(wqs), flat(wk), wvt, qg, qgs, kg, krg, krgs


def _rope_freq():
    inv = ROPE_BASE ** (-np.arange(0, QK_ROPE, 2, dtype=np.float32) / QK_ROPE)
    out = np.zeros((1, LANES), np.float32)
    half = QK_ROPE // 2
    out[0, QK_NOPE:QK_NOPE + half] = inv
    out[0, QK_NOPE + half:B_DQK] = inv
    return jnp.asarray(out)


def kernel(x, positions, rel_bias, ffn_norm1, ffn1_wg, ffn1_wu, ffn1_wd, mix_norm, ffn_norm2, ffn2_wg,
           ffn2_wu, ffn2_wd, a_w_in, a_q_gain, a_k_gain, a_sinks, a_w_out, b_w_in, b_q_norm, b_kv_norm,
           b_w_uq, b_w_ukv, b_q_gain, b_k_gain, b_w_out):
    assert x.shape == (BATCH, SEQ, D_MODEL) and positions.shape == (BATCH, SEQ)
    xt = x.reshape(TOKENS, D_MODEL)
    pos_col = positions.reshape(TOKENS, 1)
    pos_row = positions.reshape(TOKENS // BLOCK, 1, BLOCK)
    table = jnp.zeros((A_HEADS, LANES), F32).at[:, :NUM_BUCKETS].set(rel_bias.T)
    bf = lambda w: w.astype(BF16)
    row = lambda v: v[None, :]
    gain1, gain2 = ffn_norm1[:, None, :], ffn_norm2[:, None, :]
    weights = _first_ffn_weights(0, ffn1_wg, ffn1_wu, ffn1_wd)

    for i in range(DEPTH):
        xt, weights = _ffn(xt, gain1, i, *weights, cast_ahead=(i, ffn2_wg, ffn2_wu, ffn2_wd))
        j = i // N_MIXERS
        if i % N_MIXERS == 0:
            w, qg, kg = _prep_a(a_w_in[j], a_q_gain[j], a_k_gain[j])
            q, k4, v4 = _a_qkv(xt, row(mix_norm[i]), w, qg, kg)
            attn = _a_attn(q, k4, v4, pos_col, pos_row, table, a_sinks[j])
            w_out = bf(a_w_out[j])
        else:
            prep = _prep_b(b_w_in[j], b_w_uq[j], b_w_ukv[j], b_q_gain[j], b_k_gain[j])
            win, wq, wqs, wk, wvt, qg, qgs, kg, krg, krgs = prep
            q, k, vt = _b_proj(xt, pos_col, row(mix_norm[i]), win, row(b_q_norm[j]), row(b_kv_norm[j]),
                              wq, wqs, wk, wvt, qg, qgs, kg, krg, krgs, _rope_freq())
            attn = _b_attn(q, k, vt)
            w_out = bf(b_w_out[j])
        ahead = (i + 1, ffn1_wg, ffn1_wu, ffn1_wd) if i + 1 < DEPTH else None
        xt, weights = _ffn(xt, gain2, i, *weights, attn=attn, w_out=w_out, cast_ahead=ahead)
    return xt.reshape(BATCH, SEQ, D_MODEL)
```

```python
import functools
import math

import numpy as np
import jax
import jax.numpy as jnp
from jax import lax
from jax.experimental import pallas as pl
from jax.experimental.pallas import tpu as pltpu

D_MODEL = 1024
BATCH = 8
SEQ = 2048
DEPTH = 2
N_MIXERS = 2
A_HEADS = 16
A_KV_HEADS = 2
A_HEAD_DIM = 64
WINDOW = 128
BLOCK = 128
NUM_BUCKETS = 32
MAX_DISTANCE = 128
B_HEADS = 16
Q_LORA = 256
KV_LORA = 128
QK_NOPE = 64
QK_ROPE = 32
V_DIM = 64
ROPE_BASE = 10000.0
D_FF = 2816
EPS = 1e-6
NEG = -1e30

TOKENS = BATCH * SEQ
LANES = 128
HALF = LANES // 2
B_DQK = QK_NOPE + QK_ROPE
VMEM_LIMIT = 60 * 1024 * 1024

LOG2E = math.log2(math.e)

F32 = jnp.float32
BF16 = jnp.bfloat16


def _rms_rows(x, gain):
    return x * lax.rsqrt(jnp.mean(x * x, axis=-1, keepdims=True) + EPS) * gain


def _dot(a, b):
    return jnp.dot(a, b, preferred_element_type=F32)


def _dot_nt(a, b):
    return lax.dot_general(a, b, (((1,), (1,)), ((), ())), preferred_element_type=F32)


def _const_spec(shape):
    nd = len(shape)
    return pl.BlockSpec(shape, lambda *_: (0,) * nd, pipeline_mode=pl.Buffered(1))


FFN_TM = 1024
MXU_WIDTH = 256
FFN_CHUNKS = ((0, 4 * MXU_WIDTH), (4 * MXU_WIDTH, 8 * MXU_WIDTH), (8 * MXU_WIDTH, D_FF))


def _ffn_body(x, g_ref, wgu_ref, wd_ref, o_ref):
    h = _rms_rows(x, g_ref[...]).astype(BF16)
    down = None
    for lo, hi in FFN_CHUNKS:
        gate_up = _dot(h, wgu_ref[:, 2 * lo:2 * hi])
        acts = []
        for c in range((hi - lo) // MXU_WIDTH):
            gate = gate_up[:, 2 * c * MXU_WIDTH:(2 * c + 1) * MXU_WIDTH]
            up = gate_up[:, (2 * c + 1) * MXU_WIDTH:(2 * c + 2) * MXU_WIDTH]
            acts.append((gate * jax.nn.sigmoid(gate) * up).astype(BF16))
        part = _dot(jnp.concatenate(acts, axis=1), wd_ref[lo:hi, :])
        down = part if down is None else down + part
    o_ref[...] = x + 0.5 * down


def _ffn_kernel(*refs, project, cast_ahead):
    refs = list(refs)
    x = refs.pop(0)[...]
    if project:
        a_ref, wo_ref = refs.pop(0), refs.pop(0)
        x = x + _dot(a_ref[...], wo_ref[...])
    g_ref, wgu_ref, wd_ref = refs[:3]
    refs = refs[3:]
    if cast_ahead:
        _cast_ffn_weights(*refs[:3], *refs[4:6])
        refs = refs[3:]
    _ffn_body(x, g_ref, wgu_ref, wd_ref, refs[0])


def _cast_ffn_weights(wg_src, wu_src, wd_src, wgu_dst, wd_dst):
    for c in range(D_FF // MXU_WIDTH):
        cols = slice(c * MXU_WIDTH, (c + 1) * MXU_WIDTH)
        wgu_dst[:, 2 * c * MXU_WIDTH:(2 * c + 1) * MXU_WIDTH] = wg_src[:, cols].astype(BF16)
        wgu_dst[:, (2 * c + 1) * MXU_WIDTH:(2 * c + 2) * MXU_WIDTH] = wu_src[:, cols].astype(BF16)
    wd_dst[...] = wd_src[...].astype(BF16)


def _cast_specs(layer, stacks, steps):
    in_specs = [pl.BlockSpec((None, w.shape[1] // steps, w.shape[2]), lambda i: (layer, i, 0)) for w in stacks]
    out_specs = [pl.BlockSpec((D_MODEL // steps, 2 * D_FF), lambda i: (i, 0)),
                 pl.BlockSpec((D_FF // steps, D_MODEL), lambda i: (i, 0))]
    out_shape = [jax.ShapeDtypeStruct((D_MODEL, 2 * D_FF), BF16), jax.ShapeDtypeStruct((D_FF, D_MODEL), BF16)]
    return in_specs, out_specs, out_shape


def _first_ffn_weights(layer, wg, wu, wd):
    steps = 8
    in_specs, out_specs, out_shape = _cast_specs(layer, (wg, wu, wd), steps)
    return pl.pallas_call(_cast_ffn_weights, grid=(steps,), in_specs=in_specs, out_specs=out_specs,
                          out_shape=out_shape, name="cast_ffn_weights")(wg, wu, wd)


def _ffn(x, gain, layer, wgu, wd, attn=None, w_out=None, cast_ahead=None):
    tm = FFN_TM
    steps = TOKENS // tm
    row_spec = pl.BlockSpec((tm, D_MODEL), lambda i: (i, 0))
    in_specs, args = [row_spec], [x]
    if attn is not None:
        in_specs += [row_spec, _const_spec((D_MODEL, D_MODEL))]
        args += [attn, w_out]
    in_specs += [pl.BlockSpec((None, 1, D_MODEL), lambda i: (layer, 0, 0), pipeline_mode=pl.Buffered(1)),
                 _const_spec((D_MODEL, 2 * D_FF)), _const_spec((D_FF, D_MODEL))]
    args += [gain, wgu, wd]
    out_specs = [row_spec]
    out_shape = [jax.ShapeDtypeStruct((TOKENS, D_MODEL), F32)]
    if cast_ahead is not None:
        next_layer, *stacks = cast_ahead
        cast_in, cast_out, cast_shape = _cast_specs(next_layer, stacks, steps)
        in_specs += cast_in
        args += stacks
        out_specs += cast_out
        out_shape += cast_shape
    outs = pl.pallas_call(
        functools.partial(_ffn_kernel, project=attn is not None, cast_ahead=cast_ahead is not None),
        grid=(steps,),
        in_specs=in_specs,
        out_specs=out_specs,
        out_shape=out_shape,
        compiler_params=pltpu.CompilerParams(vmem_limit_bytes=VMEM_LIMIT),
        name="ffn" if attn is None else "proj_ffn",
    )(*args)
    return outs[0], tuple(outs[1:])


A_TM = 512
A_QW = A_HEADS * A_HEAD_DIM
A_KW = 4 * LANES
A_PAIRS = A_HEADS // 2
A_ROWS = 512


def _a_qkv_kernel(x_ref, g_ref, w_ref, qg_ref, kg_ref, q_ref, k_ref, v_ref):
    h = _rms_rows(x_ref[...], g_ref[...]).astype(BF16)
    qkv = _dot(h, w_ref[...])
    lane = lax.broadcasted_iota(jnp.int32, (1, LANES), 1)
    low = lane < HALF
    root_d = math.sqrt(A_HEAD_DIM)
    q_table = qg_ref[...] * (root_d * A_HEAD_DIM ** -0.5 * LOG2E)
    k_table = kg_ref[...] * root_d

    def pair_norm(pair, table):
        sq = pair * pair
        ss_lo = jnp.sum(jnp.where(low, sq, 0.0), axis=-1, keepdims=True)
        ss_hi = jnp.sum(jnp.where(low, 0.0, sq), axis=-1, keepdims=True)
        r = jnp.where(low, lax.rsqrt(ss_lo + A_HEAD_DIM * EPS), lax.rsqrt(ss_hi + A_HEAD_DIM * EPS))
        return (pair * table * r).astype(BF16)

    for p in range(A_PAIRS):
        q_ref[:, p * LANES:(p + 1) * LANES] = pair_norm(qkv[:, p * LANES:(p + 1) * LANES], q_table)
    k_ref[...] = pair_norm(qkv[:, A_QW:A_QW + LANES], k_table)
    v_ref[...] = qkv[:, A_QW + LANES:].astype(BF16)


def _a_qkv(x, gain, w, qg, kg):
    tm = A_TM
    width = A_QW + 2 * LANES
    return pl.pallas_call(
        _a_qkv_kernel,
        grid=(TOKENS // tm,),
        in_specs=[pl.BlockSpec((tm, D_MODEL), lambda i: (i, 0)), _const_spec((1, D_MODEL)),
                  _const_spec((D_MODEL, width)), _const_spec((1, LANES)), _const_spec((1, LANES))],
        out_specs=[pl.BlockSpec((tm, A_QW), lambda i: (i, 0)), pl.BlockSpec((tm, LANES), lambda i: (i, 0)),
                   pl.BlockSpec((tm, LANES), lambda i: (i, 0))],
        out_shape=[jax.ShapeDtypeStruct((TOKENS, A_QW), BF16), jax.ShapeDtypeStruct((TOKENS, LANES), BF16),
                   jax.ShapeDtypeStruct((TOKENS, LANES), BF16)],
        compiler_params=pltpu.CompilerParams(vmem_limit_bytes=VMEM_LIMIT),
        name="a_qkv",
    )(x, gain, w, qg, kg)


def _t5_bucket(dist):
    n = jnp.maximum(dist, 0)
    max_exact = NUM_BUCKETS // 2
    large = max_exact + (jnp.log(jnp.maximum(n, 1).astype(F32) / max_exact)
                         / math.log(MAX_DISTANCE / max_exact)
                         * (NUM_BUCKETS - max_exact)).astype(jnp.int32)
    large = jnp.minimum(large, NUM_BUCKETS - 1)
    return jnp.where(n < max_exact, n, large)


SUBLANES = 8


def _a_attn_kernel(sink_ref, q_ref, kc_ref, kp_ref, vc_ref, vp_ref, pq_ref, pkc_ref, pkp_ref, tbl_ref, o_ref,
                   bias_ref, shared_ref):
    row = lax.broadcasted_iota(jnp.int32, (BLOCK, BLOCK), 0)
    col = lax.broadcasted_iota(jnp.int32, (BLOCK, BLOCK), 1)
    cur_ok = col <= row
    first_in_seq = pl.program_id(1) == 0
    no_prev = jnp.where(first_in_seq, NEG, 0.0)
    tables = [jnp.broadcast_to(tbl_ref[h:h + 1, :] * LOG2E, (SUBLANES, LANES)) for h in range(A_HEADS)]
    ones = jnp.ones((2 * BLOCK, LANES), BF16)
    group_heads = A_HEADS // A_KV_HEADS
    lane = lax.broadcasted_iota(jnp.int32, (1, LANES), 1)

    @pl.when(jnp.logical_and(pl.program_id(0) == 0, first_in_seq))
    def _():
        shared_ref[0] = 0

    def fill_shift_invariant():
        back = jnp.broadcast_to((BLOCK - lane) & (BLOCK - 1), (SUBLANES, LANES))
        idx = _t5_bucket(back)
        for h in range(A_HEADS):
            base = jnp.take_along_axis(tables[h], idx, axis=1)
            base = jnp.broadcast_to(base[0:1, :], (BLOCK, BLOCK))
            tile = pltpu.roll(base, 0, 1, stride=1, stride_axis=0)
            bias_ref[h] = jnp.where(cur_ok, tile, NEG)
            bias_ref[A_HEADS + h] = jnp.where(cur_ok, NEG, tile)

    def fill_general(pos_q, pos_cur, pos_prev):
        bucket = jnp.where(cur_ok, _t5_bucket(pos_q - pos_cur), _t5_bucket(pos_q - pos_prev))
        for c in range(BLOCK // SUBLANES):
            chunk = slice(c * SUBLANES, (c + 1) * SUBLANES)
            for h in range(A_HEADS):
                piece = jnp.take_along_axis(tables[h], bucket[chunk, :], axis=1)
                bias_ref[h, chunk, :] = jnp.where(cur_ok[chunk, :], piece, NEG)
                bias_ref[A_HEADS + h, chunk, :] = jnp.where(cur_ok[chunk, :], NEG, piece)

    blocks = range(A_ROWS // BLOCK)

    def rows_of(r):
        return slice(r * BLOCK, (r + 1) * BLOCK)

    def pairs_of(kv):
        return range(kv * group_heads // 2, (kv + 1) * group_heads // 2)

    def lane_groups(pair):
        low_half = lane < HALF
        swapped = pltpu.roll(pair, HALF, 1)
        zero = jnp.zeros_like(pair)
        return jnp.concatenate([jnp.where(low_half, pair, zero), jnp.where(low_half, zero, swapped),
                                jnp.where(low_half, swapped, zero), jnp.where(low_half, zero, pair)], axis=1)

    def band(r):
        if r == 0:
            k_prev, v_prev, pos_prev = kp_ref[...], vp_ref[...], pkp_ref[0]
        else:
            k_prev, v_prev, pos_prev = kc_ref[rows_of(r - 1), :], vc_ref[rows_of(r - 1), :], pkc_ref[r - 1]
        k_band = lane_groups(jnp.concatenate([k_prev, kc_ref[rows_of(r), :]], axis=0))
        v_band = lane_groups(jnp.concatenate([v_prev, vc_ref[rows_of(r), :]], axis=0))
        return k_band, v_band, pos_prev

    def scores(r, kv, parity, bands):
        group = slice((2 * kv + parity) * LANES, (2 * kv + parity + 1) * LANES)
        q_stack = jnp.concatenate([q_ref[rows_of(r), p * LANES:(p + 1) * LANES] for p in pairs_of(kv)], axis=0)
        return _dot_nt(q_stack, bands[r][0][:, group])

    def finish(r, kv, parity, s_all, bias, bands):
        v_band = bands[r][1]
        group = slice((2 * kv + parity) * LANES, (2 * kv + parity + 1) * LANES)
        probs, sink_terms = [], []
        for i, p in enumerate(pairs_of(kv)):
            head = 2 * p + parity
            s = s_all[i * BLOCK:(i + 1) * BLOCK, :]
            s_prev = s[:, :BLOCK] + bias[A_HEADS + head]
            if r == 0:
                s_prev = s_prev + no_prev
            s_cur = s[:, BLOCK:] + bias[head]
            sink = sink_ref[head] * LOG2E
            m = jnp.maximum(jnp.max(jnp.maximum(s_prev, s_cur), axis=-1, keepdims=True), sink)
            probs.append(jnp.concatenate([jnp.exp2(s_prev - m), jnp.exp2(s_cur - m)], axis=1).astype(BF16))
            sink_terms.append(jnp.exp2(sink - m))
        v_ext = jnp.concatenate([v_band[:, group], ones], axis=1)
        out = _dot(jnp.concatenate(probs, axis=0), v_ext)
        scaled = []
        for i in range(len(probs)):
            o_i = out[i * BLOCK:(i + 1) * BLOCK, :]
            scaled.append(o_i[:, :LANES] * (1.0 / (o_i[:, LANES:] + sink_terms[i])))
        return scaled

    def attend(which_blocks):
        bias = [bias_ref[h] for h in range(2 * A_HEADS)]
        bands = {r: band(r) for r in which_blocks}
        units = [(r, kv, parity) for r in which_blocks for kv in range(A_KV_HEADS) for parity in range(2)]
        raw = {units[0]: scores(*units[0], bands)}
        done = {}
        for i, unit in enumerate(units):
            if i + 1 < len(units):
                raw[units[i + 1]] = scores(*units[i + 1], bands)
            done[unit] = finish(*unit, raw.pop(unit), bias, bands)
            r, kv, parity = unit
            if parity == 1:
                low, high = done.pop((r, kv, 0)), done.pop((r, kv, 1))
                for i_pair, p in enumerate(pairs_of(kv)):
                    o_ref[rows_of(r), p * LANES:(p + 1) * LANES] = (low[i_pair] + high[i_pair]).astype(BF16)

    @pl.when(shared_ref[0] == 0)
    def _():
        fill_shift_invariant()
        shared_ref[0] = 1

    attend(blocks)

    off = jnp.zeros((1, LANES), jnp.int32)
    for r in blocks:
        pos_cur, pos_prev = pkc_ref[r], band(r)[2]
        start = jnp.min(pos_cur, axis=-1, keepdims=True)
        off_prev = jnp.where(pos_prev - start == lane - BLOCK, 0, 1)
        if r == 0:
            off_prev = off_prev * jnp.where(first_in_seq, 0, 1)
        off = off + jnp.where(pos_cur - start == lane, 0, 1) + off_prev

    @pl.when(jnp.sum(off) != 0)
    def _():
        for r in blocks:
            fill_general(pq_ref[rows_of(r), :], pkc_ref[r], band(r)[2])
            attend([r])
        shared_ref[0] = 0


def _a_attn(q, k2, v2, pos_col, pos_row, table, sinks):
    steps = SEQ // A_ROWS
    blocks_per_step = A_ROWS // BLOCK
    blocks_per_seq = SEQ // BLOCK

    def cur(b, i):
        return (b * steps + i, 0)

    def prev(b, i):
        return (b * blocks_per_seq + jnp.maximum(i * blocks_per_step - 1, 0), 0)

    def cur3(b, i):
        return (b * steps + i, 0, 0)

    def prev3(b, i):
        return (b * blocks_per_seq + jnp.maximum(i * blocks_per_step - 1, 0), 0, 0)

    return pl.pallas_call(
        _a_attn_kernel,
        grid=(BATCH, steps),
        in_specs=[
            pl.BlockSpec(memory_space=pltpu.SMEM),
            pl.BlockSpec((A_ROWS, A_QW), cur),
            pl.BlockSpec((A_ROWS, LANES), cur),
            pl.BlockSpec((BLOCK, LANES), prev),
            pl.BlockSpec((A_ROWS, LANES), cur),
            pl.BlockSpec((BLOCK, LANES), prev),
            pl.BlockSpec((A_ROWS, 1), cur),
            pl.BlockSpec((blocks_per_step, 1, BLOCK), cur3),
            pl.BlockSpec((1, 1, BLOCK), prev3),
            _const_spec((A_HEADS, LANES)),
        ],
        out_specs=pl.BlockSpec((A_ROWS, A_QW), cur),
        out_shape=jax.ShapeDtypeStruct((TOKENS, A_QW), BF16),
        scratch_shapes=[pltpu.VMEM((2 * A_HEADS, BLOCK, BLOCK), F32), pltpu.SMEM((1,), jnp.int32)],
        compiler_params=pltpu.CompilerParams(dimension_semantics=("arbitrary", "arbitrary"),
                                             vmem_limit_bytes=VMEM_LIMIT),
        name="a_attn",
    )(sinks, q, k2, k2, v2, v2, pos_col, pos_row, pos_row, table)


B_TM = 1024
B_CW = Q_LORA + KV_LORA + 2 * LANES
B_HW = B_HEADS * LANES
B_VW = B_HEADS * V_DIM
B_TQ = 512
B_TK = 512


def _b_proj_kernel(x_ref, pos_ref, g_ref, win_ref, qn_ref, kvn_ref, wq_ref, wqs_ref, wk_ref, wvt_ref,
                   qg_ref, qgs_ref, kg_ref, krg_ref, krgs_ref, freq_ref, q_ref, k_ref, vt_ref):
    h = _rms_rows(x_ref[...], g_ref[...]).astype(BF16)
    c = _dot(h, win_ref[...])
    cq = _rms_rows(c[:, :Q_LORA], qn_ref[...]).astype(BF16)
    ckv_f32 = _rms_rows(c[:, Q_LORA:Q_LORA + KV_LORA], kvn_ref[...])
    ckv = ckv_f32.astype(BF16)
    kr = c[:, Q_LORA + KV_LORA:Q_LORA + KV_LORA + LANES]
    kr_partner = c[:, Q_LORA + KV_LORA + LANES:]
    q = _dot(cq, wq_ref[...])
    q_partner = _dot(cq, wqs_ref[...])
    kn = _dot(ckv, wk_ref[...])
    vt_ref[...] = _dot(wvt_ref[...], ckv_f32.T.astype(BF16)).astype(BF16)

    lane = lax.broadcasted_iota(jnp.int32, (1, LANES), 1)
    ang = pos_ref[...].astype(F32) * freq_ref[...]
    cos = jnp.cos(ang)
    sin = jnp.sin(ang)
    half = QK_ROPE // 2
    sin_signed = jnp.where(lane < QK_NOPE + half, -sin, sin)

    root_d = math.sqrt(B_DQK)
    q_const = root_d * B_DQK ** -0.5 * LOG2E
    k_rope = (kr * krg_ref[...] * cos + kr_partner * krgs_ref[...] * sin_signed) * root_d
    k_gain = kg_ref[...] * root_d
    ss_rope = jnp.sum(kr * kr, axis=-1, keepdims=True) + B_DQK * EPS
    q_cos = qg_ref[...] * cos * q_const
    q_sin = qgs_ref[...] * sin_signed * q_const
    for hd in range(B_HEADS):
        g = slice(hd * LANES, (hd + 1) * LANES)
        qh = q[:, g]
        r = lax.rsqrt(jnp.sum(qh * qh, axis=-1, keepdims=True) + B_DQK * EPS)
        q_ref[:, g] = ((qh * q_cos + q_partner[:, g] * q_sin) * r).astype(BF16)
        kh = kn[:, g]
        rk = lax.rsqrt(jnp.sum(kh * kh, axis=-1, keepdims=True) + ss_rope)
        k_ref[:, g] = ((kh * k_gain + k_rope) * rk).astype(BF16)


def _b_proj(x, pos_col, gain, w_in, qn, kvn, wq, wqs, wk, wvt, qg, qgs, kg, krg, krgs, freq):
    tm = B_TM
    tiles_per_seq = SEQ // tm
    lane_vec = _const_spec((1, LANES))
    out_spec = pl.BlockSpec((tm, B_HW), lambda i: (i, 0))
    out_sds = jax.ShapeDtypeStruct((TOKENS, B_HW), BF16)
    vt_spec = pl.BlockSpec((None, B_VW, tm), lambda i: (i // tiles_per_seq, 0, i % tiles_per_seq))
    return pl.pallas_call(
        _b_proj_kernel,
        grid=(TOKENS // tm,),
        in_specs=[pl.BlockSpec((tm, D_MODEL), lambda i: (i, 0)), pl.BlockSpec((tm, 1), lambda i: (i, 0)),
                  _const_spec((1, D_MODEL)), _const_spec((D_MODEL, B_CW)), _const_spec((1, Q_LORA)),
                  _const_spec((1, KV_LORA)), _const_spec((Q_LORA, B_HW)), _const_spec((Q_LORA, B_HW)),
                  _const_spec((KV_LORA, B_HW)), _const_spec((B_VW, KV_LORA)),
                  lane_vec, lane_vec, lane_vec, lane_vec, lane_vec, lane_vec],
        out_specs=[out_spec, out_spec, vt_spec],
        out_shape=[out_sds, out_sds, jax.ShapeDtypeStruct((BATCH, B_VW, SEQ), BF16)],
        compiler_params=pltpu.CompilerParams(vmem_limit_bytes=VMEM_LIMIT),
        name="b_proj",
    )(x, pos_col, gain, w_in, qn, kvn, wq, wqs, wk, wvt, qg, qgs, kg, krg, krgs, freq)


B_HEADS_PER_STEP = 4
B_LOOKAHEAD = 3


def _b_attn_kernel(q_ref, k_ref, vt_ref, o_ref):
    heads = range(B_HEADS_PER_STEP)
    groups = [slice(hd * LANES, (hd + 1) * LANES) for hd in heads]
    n_query_tiles = SEQ // B_TQ
    half = B_TQ // 2

    def chain(qt, hd):
        pieces = [(qt, hd, j * B_TK, B_TK, 0, B_TQ) for j in range(qt * B_TQ // B_TK)]
        return pieces + [(qt, hd, qt * B_TQ, half, 0, B_TQ), (qt, hd, qt * B_TQ + half, half, half, half)]

    def scores(qt, hd, key_lo, n_keys, q_lo, n_q):
        queries = slice(qt * B_TQ + q_lo, qt * B_TQ + q_lo + n_q)
        return _dot_nt(k_ref[key_lo:key_lo + n_keys, groups[hd]], q_ref[queries, groups[hd]])

    def accumulate(unit, s, carry):
        qt, hd, key_lo, n_keys, q_lo, n_q = unit
        m_all, acc_all = carry
        m, acc = m_all[:, q_lo:q_lo + n_q], acc_all[:, q_lo:q_lo + n_q]
        first_query = qt * B_TQ + q_lo
        if key_lo + n_keys - 1 > first_query:
            key = lax.broadcasted_iota(jnp.int32, (n_keys, n_q), 0)
            query = lax.broadcasted_iota(jnp.int32, (n_keys, n_q), 1)
            s = jnp.where(key + (key_lo - first_query) <= query, s, NEG)
        m_new = jnp.maximum(m, jnp.max(s, axis=0, keepdims=True))
        alpha = jnp.exp2(m - m_new)
        e = jnp.exp2(s - m_new).astype(BF16)
        vt_ext = jnp.concatenate([vt_ref[hd * V_DIM:(hd + 1) * V_DIM, key_lo:key_lo + n_keys],
                                  jnp.ones((V_DIM, n_keys), BF16)], axis=0)
        acc_new = alpha * acc + _dot(vt_ext, e)
        if q_lo:
            m_new = jnp.concatenate([m_all[:, :q_lo], m_new], axis=1)
            acc_new = jnp.concatenate([acc_all[:, :q_lo], acc_new], axis=1)
        return m_new, acc_new

    chains = [chain(qt, hd) for qt in reversed(range(n_query_tiles)) for hd in heads]
    units = [c[i] for i in range(len(chains[0])) for c in chains if i < len(c)]
    last = {c[-1] for c in chains}
    init = (jnp.full((1, B_TQ), NEG, F32), jnp.zeros((2 * V_DIM, B_TQ), F32))
    carry = {(qt, hd): init for qt in range(n_query_tiles) for hd in heads}
    raw = {u: scores(*u) for u in units[:B_LOOKAHEAD]}
    for i, unit in enumerate(units):
        qt, hd = unit[0], unit[1]
        if i + B_LOOKAHEAD < len(units):
            ahead = units[i + B_LOOKAHEAD]
            raw[ahead] = scores(*ahead)
        carry[(qt, hd)] = accumulate(unit, raw.pop(unit), carry[(qt, hd)])
        if hd % 2 == 1 and unit in last:
            outs = [acc[:V_DIM, :] * (1.0 / acc[V_DIM:V_DIM + 1, :])
                    for _, acc in (carry.pop((qt, hd - 1)), carry.pop((qt, hd)))]
            o_ref[qt * B_TQ:(qt + 1) * B_TQ, (hd - 1) * V_DIM:(hd + 1) * V_DIM] = (
                jnp.concatenate(outs, axis=0).T.astype(BF16))


def _b_attn(q, k, vt):
    n = B_HEADS_PER_STEP
    return pl.pallas_call(
        _b_attn_kernel,
        grid=(BATCH, B_HEADS // n),
        in_specs=[pl.BlockSpec((SEQ, n * LANES), lambda b, p: (b, p)),
                  pl.BlockSpec((SEQ, n * LANES), lambda b, p: (b, p)),
                  pl.BlockSpec((None, n * V_DIM, SEQ), lambda b, p: (b, p, 0))],
        out_specs=pl.BlockSpec((SEQ, n * V_DIM), lambda b, p: (b, p)),
        out_shape=jax.ShapeDtypeStruct((TOKENS, B_VW), BF16),
        compiler_params=pltpu.CompilerParams(vmem_limit_bytes=VMEM_LIMIT),
        name="b_attn",
    )(q, k, vt)


def _prep_a(w_in, q_gain, k_gain):
    qg = jnp.concatenate([q_gain, q_gain])[None, :]
    kg = jnp.concatenate([k_gain, k_gain])[None, :]
    return w_in.astype(BF16), qg, kg


def _head_groups(w, per_head, src_lo, src_hi, dst_lo):
    rows = w.shape[0]
    w3 = w.reshape(rows, B_HEADS, per_head)[:, :, src_lo:src_hi]
    out = jnp.zeros((rows, B_HEADS, LANES), w.dtype)
    out = out.at[:, :, dst_lo:dst_lo + (src_hi - src_lo)].set(w3)
    return out


def _prep_b(w_in, w_uq, w_ukv, q_gain, k_gain):
    half = QK_ROPE // 2
    t1 = slice(QK_NOPE, QK_NOPE + half)
    t2 = slice(QK_NOPE + half, B_DQK)
    rows = w_in.shape[0]
    rope_cols = w_in[:, Q_LORA + KV_LORA:]
    zeros = lambda n: jnp.zeros((rows, n), w_in.dtype)
    rope_group = jnp.concatenate([zeros(QK_NOPE), rope_cols, zeros(LANES - B_DQK)], axis=1)
    partner_group = jnp.concatenate(
        [zeros(QK_NOPE), rope_cols[:, half:], rope_cols[:, :half], zeros(LANES - B_DQK)], axis=1)
    win = jnp.concatenate([w_in[:, :Q_LORA + KV_LORA], rope_group, partner_group], axis=1).astype(BF16)

    wq = _head_groups(w_uq, B_DQK, 0, B_DQK, 0)
    wqs = (_head_groups(w_uq, B_DQK, t2.start, t2.stop, t1.start)
           + _head_groups(w_uq, B_DQK, t1.start, t1.stop, t2.start))
    wk = _head_groups(w_ukv, QK_NOPE + V_DIM, 0, QK_NOPE, 0)
    wvt = w_ukv.reshape(KV_LORA, B_HEADS, QK_NOPE + V_DIM)[:, :, QK_NOPE:].reshape(KV_LORA, B_VW).T.astype(BF16)
    flat = lambda a: a.reshape(a.shape[0], B_HW).astype(BF16)

    def lane_vec(pieces):
        out = jnp.zeros((LANES,), F32)
        for lo, vals in pieces:
            out = out.at[lo:lo + vals.shape[0]].set(vals)
        return out[None, :]

    qg = lane_vec([(0, q_gain)])
    qgs = lane_vec([(t1.start, q_gain[t2]), (t2.start, q_gain[t1])])
    kg = lane_vec([(0, k_gain[:QK_NOPE])])
    krg = lane_vec([(QK_NOPE, k_gain[QK_NOPE:])])
    krgs = lane_vec([(t1.start, k_gain[t2]), (t2.start, k_gain[t1])])
    return win, flat(wq), flat(wqs), flat(wk), wvt, qg, qgs, kg, krg, krgs


def _rope_freq():
    inv = ROPE_BASE ** (-np.arange(0, QK_ROPE, 2, dtype=np.float32) / QK_ROPE)
    out = np.zeros((1, LANES), np.float32)
    half = QK_ROPE // 2
    out[0, QK_NOPE:QK_NOPE + half] = inv
    out[0, QK_NOPE + half:B_DQK] = inv
    return jnp.asarray(out)


def kernel(x, positions, rel_bias, ffn_norm1, ffn1_wg, ffn1_wu, ffn1_wd, mix_norm, ffn_norm2, ffn2_wg,
           ffn2_wu, ffn2_wd, a_w_in, a_q_gain, a_k_gain, a_sinks, a_w_out, b_w_in, b_q_norm, b_kv_norm,
           b_w_uq, b_w_ukv, b_q_gain, b_k_gain, b_w_out):
    assert x.shape == (BATCH, SEQ, D_MODEL) and positions.shape == (BATCH, SEQ)
    xt = x.reshape(TOKENS, D_MODEL)
    pos_col = positions.reshape(TOKENS, 1)
    pos_row = positions.reshape(TOKENS // BLOCK, 1, BLOCK)
    table = jnp.zeros((A_HEADS, LANES), F32).at[:, :NUM_BUCKETS].set(rel_bias.T)
    bf = lambda w: w.astype(BF16)
    row = lambda v: v[None, :]
    gain1, gain2 = ffn_norm1[:, None, :], ffn_norm2[:, None, :]
    weights = _first_ffn_weights(0, ffn1_wg, ffn1_wu, ffn1_wd)

    for i in range(DEPTH):
        xt, weights = _ffn(xt, gain1, i, *weights, cast_ahead=(i, ffn2_wg, ffn2_wu, ffn2_wd))
        j = i // N_MIXERS
        if i % N_MIXERS == 0:
            w, qg, kg = _prep_a(a_w_in[j], a_q_gain[j], a_k_gain[j])
            q, k2, v2 = _a_qkv(xt, row(mix_norm[i]), w, qg, kg)
            attn = _a_attn(q, k2, v2, pos_col, pos_row, table, a_sinks[j])
            w_out = bf(a_w_out[j])
        else:
            prep = _prep_b(b_w_in[j], b_w_uq[j], b_w_ukv[j], b_q_gain[j], b_k_gain[j])
            win, wq, wqs, wk, wvt, qg, qgs, kg, krg, krgs = prep
            q, k, vt = _b_proj(xt, pos_col, row(mix_norm[i]), win, row(b_q_norm[j]), row(b_kv_norm[j]),
                              wq, wqs, wk, wvt, qg, qgs, kg, krg, krgs, _rope_freq())
            attn = _b_attn(q, k, vt)
            w_out = bf(b_w_out[j])
        ahead = (i + 1, ffn1_wg, ffn1_wu, ffn1_wd) if i + 1 < DEPTH else None
        xt, weights = _ffn(xt, gain2, i, *weights, attn=attn, w_out=w_out, cast_ahead=ahead)
    return xt.reshape(BATCH, SEQ, D_MODEL)
```

```python
import functools
import math

import numpy as np
import jax
import jax.numpy as jnp
from jax import lax
from jax.experimental import pallas as pl
from jax.experimental.pallas import tpu as pltpu

D_MODEL = 1024
BATCH = 8
SEQ = 2048
DEPTH = 2
N_MIXERS = 2
A_HEADS = 16
A_KV_HEADS = 2
A_HEAD_DIM = 64
WINDOW = 128
BLOCK = 128
NUM_BUCKETS = 32
MAX_DISTANCE = 128
B_HEADS = 16
Q_LORA = 256
KV_LORA = 128
QK_NOPE = 64
QK_ROPE = 32
V_DIM = 64
ROPE_BASE = 10000.0
D_FF = 2816
EPS = 1e-6
NEG = -1e30

TOKENS = BATCH * SEQ
LANES = 128
HALF = LANES // 2
B_DQK = QK_NOPE + QK_ROPE
VMEM_LIMIT = 60 * 1024 * 1024

LOG2E = math.log2(math.e)

F32 = jnp.float32
BF16 = jnp.bfloat16


def _rms_rows(x, gain):
    return x * lax.rsqrt(jnp.mean(x * x, axis=-1, keepdims=True) + EPS) * gain


def _dot(a, b):
    return jnp.dot(a, b, preferred_element_type=F32)


def _dot_nt(a, b):
    return lax.dot_general(a, b, (((1,), (1,)), ((), ())), preferred_element_type=F32)


def _const_spec(shape):
    nd = len(shape)
    return pl.BlockSpec(shape, lambda *_: (0,) * nd, pipeline_mode=pl.Buffered(1))


FFN_TM = 1024
MXU_WIDTH = 256
FFN_CHUNKS = ((0, 4 * MXU_WIDTH), (4 * MXU_WIDTH, 8 * MXU_WIDTH), (8 * MXU_WIDTH, D_FF))


def _ffn_body(x, g_ref, wgu_ref, wd_ref, o_ref):
    h = _rms_rows(x, g_ref[...]).astype(BF16)
    down = None
    for lo, hi in FFN_CHUNKS:
        gate_up = _dot(h, wgu_ref[:, 2 * lo:2 * hi])
        acts = []
        for c in range((hi - lo) // MXU_WIDTH):
            gate = gate_up[:, 2 * c * MXU_WIDTH:(2 * c + 1) * MXU_WIDTH]
            up = gate_up[:, (2 * c + 1) * MXU_WIDTH:(2 * c + 2) * MXU_WIDTH]
            acts.append((gate * jax.nn.sigmoid(gate) * up).astype(BF16))
        part = _dot(jnp.concatenate(acts, axis=1), wd_ref[lo:hi, :])
        down = part if down is None else down + part
    o_ref[...] = x + 0.5 * down


def _ffn_kernel(*refs, project, cast_ahead):
    refs = list(refs)
    x = refs.pop(0)[...]
    if project:
        a_ref, wo_ref = refs.pop(0), refs.pop(0)
        x = x + _dot(a_ref[...], wo_ref[...])
    g_ref, wgu_ref, wd_ref = refs[:3]
    refs = refs[3:]
    if cast_ahead:
        _cast_ffn_weights(*refs[:3], *refs[4:6])
        refs = refs[3:]
    _ffn_body(x, g_ref, wgu_ref, wd_ref, refs[0])


def _cast_ffn_weights(wg_src, wu_src, wd_src, wgu_dst, wd_dst):
    for c in range(D_FF // MXU_WIDTH):
        cols = slice(c * MXU_WIDTH, (c + 1) * MXU_WIDTH)
        wgu_dst[:, 2 * c * MXU_WIDTH:(2 * c + 1) * MXU_WIDTH] = wg_src[:, cols].astype(BF16)
        wgu_dst[:, (2 * c + 1) * MXU_WIDTH:(2 * c + 2) * MXU_WIDTH] = wu_src[:, cols].astype(BF16)
    wd_dst[...] = wd_src[...].astype(BF16)


def _cast_specs(layer, stacks, steps):
    in_specs = [pl.BlockSpec((None, w.shape[1] // steps, w.shape[2]), lambda i: (layer, i, 0)) for w in stacks]
    out_specs = [pl.BlockSpec((D_MODEL // steps, 2 * D_FF), lambda i: (i, 0)),
                 pl.BlockSpec((D_FF // steps, D_MODEL), lambda i: (i, 0))]
    out_shape = [jax.ShapeDtypeStruct((D_MODEL, 2 * D_FF), BF16), jax.ShapeDtypeStruct((D_FF, D_MODEL), BF16)]
    return in_specs, out_specs, out_shape


def _first_ffn_weights(layer, wg, wu, wd):
    steps = 8
    in_specs, out_specs, out_shape = _cast_specs(layer, (wg, wu, wd), steps)
    return pl.pallas_call(_cast_ffn_weights, grid=(steps,), in_specs=in_specs, out_specs=out_specs,
                          out_shape=out_shape, name="cast_ffn_weights")(wg, wu, wd)


def _ffn(x, gain, layer, wgu, wd, attn=None, w_out=None, cast_ahead=None):
    tm = FFN_TM
    steps = TOKENS // tm
    row_spec = pl.BlockSpec((tm, D_MODEL), lambda i: (i, 0))
    in_specs, args = [row_spec], [x]
    if attn is not None:
        in_specs += [row_spec, _const_spec((D_MODEL, D_MODEL))]
        args += [attn, w_out]
    in_specs += [pl.BlockSpec((None, 1, D_MODEL), lambda i: (layer, 0, 0), pipeline_mode=pl.Buffered(1)),
                 _const_spec((D_MODEL, 2 * D_FF)), _const_spec((D_FF, D_MODEL))]
    args += [gain, wgu, wd]
    out_specs = [row_spec]
    out_shape = [jax.ShapeDtypeStruct((TOKENS, D_MODEL), F32)]
    if cast_ahead is not None:
        next_layer, *stacks = cast_ahead
        cast_in, cast_out, cast_shape = _cast_specs(next_layer, stacks, steps)
        in_specs += cast_in
        args += stacks
        out_specs += cast_out
        out_shape += cast_shape
    outs = pl.pallas_call(
        functools.partial(_ffn_kernel, project=attn is not None, cast_ahead=cast_ahead is not None),
        grid=(steps,),
        in_specs=in_specs,
        out_specs=out_specs,
        out_shape=out_shape,
        compiler_params=pltpu.CompilerParams(vmem_limit_bytes=VMEM_LIMIT),
        name="ffn" if attn is None else "proj_ffn",
    )(*args)
    return outs[0], tuple(outs[1:])


A_TM = 512
A_QW = A_HEADS * A_HEAD_DIM
A_KW = 4 * LANES
A_PAIRS = A_HEADS // 2
A_ROWS = 512
A_LOOKAHEAD = 1


def _a_qkv_kernel(x_ref, g_ref, w_ref, qg_ref, kg_ref, q_ref, k_ref, v_ref):
    h = _rms_rows(x_ref[...], g_ref[...]).astype(BF16)
    qkv = _dot(h, w_ref[...])
    lane = lax.broadcasted_iota(jnp.int32, (1, LANES), 1)
    low = lane < HALF
    root_d = math.sqrt(A_HEAD_DIM)
    q_table = qg_ref[...] * (root_d * A_HEAD_DIM ** -0.5 * LOG2E)
    k_table = kg_ref[...] * root_d

    def pair_norm(pair, table):
        sq = pair * pair
        ss_lo = jnp.sum(jnp.where(low, sq, 0.0), axis=-1, keepdims=True)
        ss_hi = jnp.sum(jnp.where(low, 0.0, sq), axis=-1, keepdims=True)
        r = jnp.where(low, lax.rsqrt(ss_lo + A_HEAD_DIM * EPS), lax.rsqrt(ss_hi + A_HEAD_DIM * EPS))
        return (pair * table * r).astype(BF16)

    for p in range(A_PAIRS):
        q_ref[:, p * LANES:(p + 1) * LANES] = pair_norm(qkv[:, p * LANES:(p + 1) * LANES], q_table)
    k_ref[...] = pair_norm(qkv[:, A_QW:A_QW + LANES], k_table)
    v_ref[...] = qkv[:, A_QW + LANES:].astype(BF16)


def _a_qkv(x, gain, w, qg, kg):
    tm = A_TM
    width = A_QW + 2 * LANES
    return pl.pallas_call(
        _a_qkv_kernel,
        grid=(TOKENS // tm,),
        in_specs=[pl.BlockSpec((tm, D_MODEL), lambda i: (i, 0)), _const_spec((1, D_MODEL)),
                  _const_spec((D_MODEL, width)), _const_spec((1, LANES)), _const_spec((1, LANES))],
        out_specs=[pl.BlockSpec((tm, A_QW), lambda i: (i, 0)), pl.BlockSpec((tm, LANES), lambda i: (i, 0)),
                   pl.BlockSpec((tm, LANES), lambda i: (i, 0))],
        out_shape=[jax.ShapeDtypeStruct((TOKENS, A_QW), BF16), jax.ShapeDtypeStruct((TOKENS, LANES), BF16),
                   jax.ShapeDtypeStruct((TOKENS, LANES), BF16)],
        compiler_params=pltpu.CompilerParams(vmem_limit_bytes=VMEM_LIMIT),
        name="a_qkv",
    )(x, gain, w, qg, kg)


def _t5_bucket(dist):
    n = jnp.maximum(dist, 0)
    max_exact = NUM_BUCKETS // 2
    large = max_exact + (jnp.log(jnp.maximum(n, 1).astype(F32) / max_exact)
                         / math.log(MAX_DISTANCE / max_exact)
                         * (NUM_BUCKETS - max_exact)).astype(jnp.int32)
    large = jnp.minimum(large, NUM_BUCKETS - 1)
    return jnp.where(n < max_exact, n, large)


SUBLANES = 8


def _a_attn_kernel(sink_ref, q_ref, kc_ref, kp_ref, vc_ref, vp_ref, pq_ref, pkc_ref, pkp_ref, tbl_ref, o_ref,
                   bias_ref, shared_ref):
    row = lax.broadcasted_iota(jnp.int32, (BLOCK, BLOCK), 0)
    col = lax.broadcasted_iota(jnp.int32, (BLOCK, BLOCK), 1)
    cur_ok = col <= row
    first_in_seq = pl.program_id(1) == 0
    no_prev = jnp.where(first_in_seq, NEG, 0.0)
    tables = [jnp.broadcast_to(tbl_ref[h:h + 1, :] * LOG2E, (SUBLANES, LANES)) for h in range(A_HEADS)]
    group_heads = A_HEADS // A_KV_HEADS
    lane = lax.broadcasted_iota(jnp.int32, (1, LANES), 1)

    @pl.when(jnp.logical_and(pl.program_id(0) == 0, first_in_seq))
    def _():
        shared_ref[0] = 0

    def fill_shift_invariant():
        back = jnp.broadcast_to((BLOCK - lane) & (BLOCK - 1), (SUBLANES, LANES))
        idx = _t5_bucket(back)
        for h in range(A_HEADS):
            base = jnp.take_along_axis(tables[h], idx, axis=1)
            base = jnp.broadcast_to(base[0:1, :], (BLOCK, BLOCK))
            tile = pltpu.roll(base, 0, 1, stride=1, stride_axis=0)
            bias_ref[h] = jnp.where(cur_ok, tile, NEG)
            bias_ref[A_HEADS + h] = jnp.where(cur_ok, NEG, tile)

    def fill_general(pos_q, pos_cur, pos_prev):
        bucket = jnp.where(cur_ok, _t5_bucket(pos_q - pos_cur), _t5_bucket(pos_q - pos_prev))
        for c in range(BLOCK // SUBLANES):
            chunk = slice(c * SUBLANES, (c + 1) * SUBLANES)
            for h in range(A_HEADS):
                piece = jnp.take_along_axis(tables[h], bucket[chunk, :], axis=1)
                bias_ref[h, chunk, :] = jnp.where(cur_ok[chunk, :], piece, NEG)
                bias_ref[A_HEADS + h, chunk, :] = jnp.where(cur_ok[chunk, :], NEG, piece)

    blocks = range(A_ROWS // BLOCK)

    def rows_of(r):
        return slice(r * BLOCK, (r + 1) * BLOCK)

    def pairs_of(kv):
        return range(kv * group_heads // 2, (kv + 1) * group_heads // 2)

    def lane_groups(pair):
        low_half = lane < HALF
        swapped = pltpu.roll(pair, HALF, 1)
        zero = jnp.zeros_like(pair)
        return jnp.concatenate([jnp.where(low_half, pair, zero), jnp.where(low_half, zero, swapped),
                                jnp.where(low_half, swapped, zero), jnp.where(low_half, zero, pair)], axis=1)

    def pos_prev_of(r):
        return pkp_ref[0] if r == 0 else pkc_ref[r - 1]

    def all_bands(which_blocks):
        first, stop = which_blocks[0], which_blocks[-1] + 1
        if first == 0:
            k_rows = jnp.concatenate([kp_ref[...], kc_ref[:stop * BLOCK, :]], axis=0)
            v_rows = jnp.concatenate([vp_ref[...], vc_ref[:stop * BLOCK, :]], axis=0)
        else:
            k_rows = kc_ref[(first - 1) * BLOCK:stop * BLOCK, :]
            v_rows = vc_ref[(first - 1) * BLOCK:stop * BLOCK, :]
        k_groups, v_groups = lane_groups(k_rows), lane_groups(v_rows)
        return {r: (k_groups[(r - first) * BLOCK:(r - first + 2) * BLOCK, :],
                    v_groups[(r - first) * BLOCK:(r - first + 2) * BLOCK, :]) for r in which_blocks}

    low_ones = jnp.where(lane < HALF, 1.0, 0.0).astype(BF16)
    ones_lo = jnp.broadcast_to(low_ones, (2 * BLOCK, LANES))
    ones_hi = jnp.broadcast_to(1.0 - low_ones, (2 * BLOCK, LANES)).astype(BF16)

    def scores(r, kv, bands):
        k_band = bands[r][0]
        keys = jnp.concatenate([k_band[:, (2 * kv + parity) * LANES:(2 * kv + parity + 1) * LANES]
                                for parity in range(2)], axis=0)
        q_stack = jnp.concatenate([q_ref[rows_of(r), p * LANES:(p + 1) * LANES] for p in pairs_of(kv)], axis=0)
        return _dot_nt(q_stack, keys)

    def finish(r, kv, s_all, bias, bands):
        v_band = bands[r][1]
        probs, sink_terms = [], []
        for i, p in enumerate(pairs_of(kv)):
            pair_probs, pair_sinks = [], []
            for parity in range(2):
                head = 2 * p + parity
                s = s_all[i * BLOCK:(i + 1) * BLOCK, parity * 2 * BLOCK:(parity + 1) * 2 * BLOCK]
                s_prev = s[:, :BLOCK] + bias[A_HEADS + head]
                if r == 0:
                    s_prev = s_prev + no_prev
                s_cur = s[:, BLOCK:] + bias[head]
                sink = sink_ref[head] * LOG2E
                m = jnp.maximum(jnp.max(jnp.maximum(s_prev, s_cur), axis=-1, keepdims=True), sink)
                pair_probs += [jnp.exp2(s_prev - m), jnp.exp2(s_cur - m)]
                pair_sinks.append(jnp.exp2(sink - m))
            probs.append(jnp.concatenate(pair_probs, axis=1).astype(BF16))
            sink_terms.append(jnp.where(lane < HALF, pair_sinks[0], pair_sinks[1]))
        even, odd = (slice((2 * kv + parity) * LANES, (2 * kv + parity + 1) * LANES) for parity in range(2))
        v_ext = jnp.concatenate([jnp.concatenate([v_band[:, even], ones_lo], axis=1),
                                 jnp.concatenate([v_band[:, odd], ones_hi], axis=1)], axis=0)
        out = _dot(jnp.concatenate(probs, axis=0), v_ext)
        for i, p in enumerate(pairs_of(kv)):
            o_i = out[i * BLOCK:(i + 1) * BLOCK, :]
            o_ref[rows_of(r), p * LANES:(p + 1) * LANES] = (
                o_i[:, :LANES] * (1.0 / (o_i[:, LANES:] + sink_terms[i]))).astype(BF16)

    def attend(which_blocks):
        bias = [bias_ref[h] for h in range(2 * A_HEADS)]
        bands = all_bands(list(which_blocks))
        units = [(r, kv) for r in which_blocks for kv in range(A_KV_HEADS)]
        raw = {u: scores(*u, bands) for u in units[:A_LOOKAHEAD]}
        for i, unit in enumerate(units):
            if i + A_LOOKAHEAD < len(units):
                ahead = units[i + A_LOOKAHEAD]
                raw[ahead] = scores(*ahead, bands)
            finish(*unit, raw.pop(unit), bias, bands)

    @pl.when(shared_ref[0] == 0)
    def _():
        fill_shift_invariant()
        shared_ref[0] = 1

    attend(blocks)

    off = jnp.zeros((1, LANES), jnp.int32)
    for r in blocks:
        pos_cur, pos_prev = pkc_ref[r], pos_prev_of(r)
        start = jnp.min(pos_cur, axis=-1, keepdims=True)
        off_prev = jnp.where(pos_prev - start == lane - BLOCK, 0, 1)
        if r == 0:
            off_prev = off_prev * jnp.where(first_in_seq, 0, 1)
        off = off + jnp.where(pos_cur - start == lane, 0, 1) + off_prev

    @pl.when(jnp.sum(off) != 0)
    def _():
        for r in blocks:
            fill_general(pq_ref[rows_of(r), :], pkc_ref[r], pos_prev_of(r))
            attend([r])
        shared_ref[0] = 0


def _a_attn(q, k2, v2, pos_col, pos_row, table, sinks):
    steps = SEQ // A_ROWS
    blocks_per_step = A_ROWS // BLOCK
    blocks_per_seq = SEQ // BLOCK

    def cur(b, i):
        return (b * steps + i, 0)

    def prev(b, i):
        return (b * blocks_per_seq + jnp.maximum(i * blocks_per_step - 1, 0), 0)

    def cur3(b, i):
        return (b * steps + i, 0, 0)

    def prev3(b, i):
        return (b * blocks_per_seq + jnp.maximum(i * blocks_per_step - 1, 0), 0, 0)

    return pl.pallas_call(
        _a_attn_kernel,
        grid=(BATCH, steps),
        in_specs=[
            pl.BlockSpec(memory_space=pltpu.SMEM),
            pl.BlockSpec((A_ROWS, A_QW), cur),
            pl.BlockSpec((A_ROWS, LANES), cur),
            pl.BlockSpec((BLOCK, LANES), prev),
            pl.BlockSpec((A_ROWS, LANES), cur),
            pl.BlockSpec((BLOCK, LANES), prev),
            pl.BlockSpec((A_ROWS, 1), cur),
            pl.BlockSpec((blocks_per_step, 1, BLOCK), cur3),
            pl.BlockSpec((1, 1, BLOCK), prev3),
            _const_spec((A_HEADS, LANES)),
        ],
        out_specs=pl.BlockSpec((A_ROWS, A_QW), cur),
        out_shape=jax.ShapeDtypeStruct((TOKENS, A_QW), BF16),
        scratch_shapes=[pltpu.VMEM((2 * A_HEADS, BLOCK, BLOCK), F32), pltpu.SMEM((1,), jnp.int32)],
        compiler_params=pltpu.CompilerParams(dimension_semantics=("arbitrary", "arbitrary"),
                                             vmem_limit_bytes=VMEM_LIMIT),
        name="a_attn",
    )(sinks, q, k2, k2, v2, v2, pos_col, pos_row, pos_row, table)


B_TM = 1024
B_CW = Q_LORA + KV_LORA + 2 * LANES
B_HW = B_HEADS * LANES
B_VW = B_HEADS * V_DIM
B_TQ = 512
B_TK = 512


def _b_proj_kernel(x_ref, pos_ref, g_ref, win_ref, qn_ref, kvn_ref, wq_ref, wqs_ref, wk_ref, wvt_ref,
                   qg_ref, qgs_ref, kg_ref, krg_ref, krgs_ref, freq_ref, q_ref, k_ref, vt_ref):
    h = _rms_rows(x_ref[...], g_ref[...]).astype(BF16)
    c = _dot(h, win_ref[...])
    cq = _rms_rows(c[:, :Q_LORA], qn_ref[...]).astype(BF16)
    ckv_f32 = _rms_rows(c[:, Q_LORA:Q_LORA + KV_LORA], kvn_ref[...])
    ckv = ckv_f32.astype(BF16)
    kr = c[:, Q_LORA + KV_LORA:Q_LORA + KV_LORA + LANES]
    kr_partner = c[:, Q_LORA + KV_LORA + LANES:]
    q = _dot(cq, wq_ref[...])
    q_partner = _dot(cq, wqs_ref[...])
    kn = _dot(ckv, wk_ref[...])
    vt_ref[...] = _dot(wvt_ref[...], ckv_f32.T.astype(BF16)).astype(BF16)

    lane = lax.broadcasted_iota(jnp.int32, (1, LANES), 1)
    ang = pos_ref[...].astype(F32) * freq_ref[...]
    cos = jnp.cos(ang)
    sin = jnp.sin(ang)
    half = QK_ROPE // 2
    sin_signed = jnp.where(lane < QK_NOPE + half, -sin, sin)

    root_d = math.sqrt(B_DQK)
    q_const = root_d * B_DQK ** -0.5 * LOG2E
    k_rope = (kr * krg_ref[...] * cos + kr_partner * krgs_ref[...] * sin_signed) * root_d
    k_gain = kg_ref[...] * root_d
    ss_rope = jnp.sum(kr * kr, axis=-1, keepdims=True) + B_DQK * EPS
    q_cos = qg_ref[...] * cos * q_const
    q_sin = qgs_ref[...] * sin_signed * q_const
    for hd in range(B_HEADS):
        g = slice(hd * LANES, (hd + 1) * LANES)
        qh = q[:, g]
        r = lax.rsqrt(jnp.sum(qh * qh, axis=-1, keepdims=True) + B_DQK * EPS)
        q_ref[:, g] = ((qh * q_cos + q_partner[:, g] * q_sin) * r).astype(BF16)
        kh = kn[:, g]
        rk = lax.rsqrt(jnp.sum(kh * kh, axis=-1, keepdims=True) + ss_rope)
        k_ref[:, g] = ((kh * k_gain + k_rope) * rk).astype(BF16)


def _b_proj(x, pos_col, gain, w_in, qn, kvn, wq, wqs, wk, wvt, qg, qgs, kg, krg, krgs, freq):
    tm = B_TM
    tiles_per_seq = SEQ // tm
    lane_vec = _const_spec((1, LANES))
    out_spec = pl.BlockSpec((tm, B_HW), lambda i: (i, 0))
    out_sds = jax.ShapeDtypeStruct((TOKENS, B_HW), BF16)
    vt_spec = pl.BlockSpec((None, B_VW, tm), lambda i: (i // tiles_per_seq, 0, i % tiles_per_seq))
    return pl.pallas_call(
        _b_proj_kernel,
        grid=(TOKENS // tm,),
        in_specs=[pl.BlockSpec((tm, D_MODEL), lambda i: (i, 0)), pl.BlockSpec((tm, 1), lambda i: (i, 0)),
                  _const_spec((1, D_MODEL)), _const_spec((D_MODEL, B_CW)), _const_spec((1, Q_LORA)),
                  _const_spec((1, KV_LORA)), _const_spec((Q_LORA, B_HW)), _const_spec((Q_LORA, B_HW)),
                  _const_spec((KV_LORA, B_HW)), _const_spec((B_VW, KV_LORA)),
                  lane_vec, lane_vec, lane_vec, lane_vec, lane_vec, lane_vec],
        out_specs=[out_spec, out_spec, vt_spec],
        out_shape=[out_sds, out_sds, jax.ShapeDtypeStruct((BATCH, B_VW, SEQ), BF16)],
        compiler_params=pltpu.CompilerParams(vmem_limit_bytes=VMEM_LIMIT),
        name="b_proj",
    )(x, pos_col, gain, w_in, qn, kvn, wq, wqs, wk, wvt, qg, qgs, kg, krg, krgs, freq)


B_HEADS_PER_STEP = 4
B_LOOKAHEAD = 3


def _b_attn_kernel(q_ref, k_ref, vt_ref, o_ref):
    heads = range(B_HEADS_PER_STEP)
    groups = [slice(hd * LANES, (hd + 1) * LANES) for hd in heads]
    n_query_tiles = SEQ // B_TQ
    half = B_TQ // 2

    def chain(qt, hd):
        pieces = [(qt, hd, j * B_TK, B_TK, 0, B_TQ) for j in range(qt * B_TQ // B_TK)]
        return pieces + [(qt, hd, qt * B_TQ, half, 0, B_TQ), (qt, hd, qt * B_TQ + half, half, half, half)]

    def scores(qt, hd, key_lo, n_keys, q_lo, n_q):
        queries = slice(qt * B_TQ + q_lo, qt * B_TQ + q_lo + n_q)
        return _dot_nt(k_ref[key_lo:key_lo + n_keys, groups[hd]], q_ref[queries, groups[hd]])

    def accumulate(unit, s, carry):
        qt, hd, key_lo, n_keys, q_lo, n_q = unit
        m_all, acc_all = carry
        m, acc = m_all[:, q_lo:q_lo + n_q], acc_all[:, q_lo:q_lo + n_q]
        first_query = qt * B_TQ + q_lo
        if key_lo + n_keys - 1 > first_query:
            key = lax.broadcasted_iota(jnp.int32, (n_keys, n_q), 0)
            query = lax.broadcasted_iota(jnp.int32, (n_keys, n_q), 1)
            s = jnp.where(key + (key_lo - first_query) <= query, s, NEG)
        m_new = jnp.maximum(m, jnp.max(s, axis=0, keepdims=True))
        alpha = jnp.exp2(m - m_new)
        e = jnp.exp2(s - m_new).astype(BF16)
        vt_ext = jnp.concatenate([vt_ref[hd * V_DIM:(hd + 1) * V_DIM, key_lo:key_lo + n_keys],
                                  jnp.ones((V_DIM, n_keys), BF16)], axis=0)
        acc_new = alpha * acc + _dot(vt_ext, e)
        if q_lo:
            m_new = jnp.concatenate([m_all[:, :q_lo], m_new], axis=1)
            acc_new = jnp.concatenate([acc_all[:, :q_lo], acc_new], axis=1)
        return m_new, acc_new

    chains = [chain(qt, hd) for qt in reversed(range(n_query_tiles)) for hd in heads]
    units = [c[i] for i in range(len(chains[0])) for c in chains if i < len(c)]
    last = {c[-1] for c in chains}
    init = (jnp.full((1, B_TQ), NEG, F32), jnp.zeros((2 * V_DIM, B_TQ), F32))
    carry = {(qt, hd): init for qt in range(n_query_tiles) for hd in heads}
    raw = {u: scores(*u) for u in units[:B_LOOKAHEAD]}
    for i, unit in enumerate(units):
        qt, hd = unit[0], unit[1]
        if i + B_LOOKAHEAD < len(units):
            ahead = units[i + B_LOOKAHEAD]
            raw[ahead] = scores(*ahead)
        carry[(qt, hd)] = accumulate(unit, raw.pop(unit), carry[(qt, hd)])
        if hd % 2 == 1 and unit in last:
            outs = [acc[:V_DIM, :] * (1.0 / acc[V_DIM:V_DIM + 1, :])
                    for _, acc in (carry.pop((qt, hd - 1)), carry.pop((qt, hd)))]
            o_ref[qt * B_TQ:(qt + 1) * B_TQ, (hd - 1) * V_DIM:(hd + 1) * V_DIM] = (
                jnp.concatenate(outs, axis=0).T.astype(BF16))


def _b_attn(q, k, vt):
    n = B_HEADS_PER_STEP
    return pl.pallas_call(
        _b_attn_kernel,
        grid=(BATCH, B_HEADS // n),
        in_specs=[pl.BlockSpec((SEQ, n * LANES), lambda b, p: (b, p)),
                  pl.BlockSpec((SEQ, n * LANES), lambda b, p: (b, p)),
                  pl.BlockSpec((None, n * V_DIM, SEQ), lambda b, p: (b, p, 0))],
        out_specs=pl.BlockSpec((SEQ, n * V_DIM), lambda b, p: (b, p)),
        out_shape=jax.ShapeDtypeStruct((TOKENS, B_VW), BF16),
        compiler_params=pltpu.CompilerParams(vmem_limit_bytes=VMEM_LIMIT),
        name="b_attn",
    )(q, k, vt)


def _prep_a(w_in, q_gain, k_gain):
    qg = jnp.concatenate([q_gain, q_gain])[None, :]
    kg = jnp.concatenate([k_gain, k_gain])[None, :]
    return w_in.astype(BF16), qg, kg


def _head_groups(w, per_head, src_lo, src_hi, dst_lo):
    rows = w.shape[0]
    w3 = w.reshape(rows, B_HEADS, per_head)[:, :, src_lo:src_hi]
    out = jnp.zeros((rows, B_HEADS, LANES), w.dtype)
    out = out.at[:, :, dst_lo:dst_lo + (src_hi - src_lo)].set(w3)
    return out


def _prep_b(w_in, w_uq, w_ukv, q_gain, k_gain):
    half = QK_ROPE // 2
    t1 = slice(QK_NOPE, QK_NOPE + half)
    t2 = slice(QK_NOPE + half, B_DQK)
    rows = w_in.shape[0]
    rope_cols = w_in[:, Q_LORA + KV_LORA:]
    zeros = lambda n: jnp.zeros((rows, n), w_in.dtype)
    rope_group = jnp.concatenate([zeros(QK_NOPE), rope_cols, zeros(LANES - B_DQK)], axis=1)
    partner_group = jnp.concatenate(
        [zeros(QK_NOPE), rope_cols[:, half:], rope_cols[:, :half], zeros(LANES - B_DQK)], axis=1)
    win = jnp.concatenate([w_in[:, :Q_LORA + KV_LORA], rope_group, partner_group], axis=1).astype(BF16)

    wq = _head_groups(w_uq, B_DQK, 0, B_DQK, 0)
    wqs = (_head_groups(w_uq, B_DQK, t2.start, t2.stop, t1.start)
           + _head_groups(w_uq, B_DQK, t1.start, t1.stop, t2.start))
    wk = _head_groups(w_ukv, QK_NOPE + V_DIM, 0, QK_NOPE, 0)
    wvt = w_ukv.reshape(KV_LORA, B_HEADS, QK_NOPE + V_DIM)[:, :, QK_NOPE:].reshape(KV_LORA, B_VW).T.astype(BF16)
    flat = lambda a: a.reshape(a.shape[0], B_HW).astype(BF16)

    def lane_vec(pieces):
        out = jnp.zeros((LANES,), F32)
        for lo, vals in pieces:
            out = out.at[lo:lo + vals.shape[0]].set(vals)
        return out[None, :]

    qg = lane_vec([(0, q_gain)])
    qgs = lane_vec([(t1.start, q_gain[t2]), (t2.start, q_gain[t1])])
    kg = lane_vec([(0, k_gain[:QK_NOPE])])
    krg = lane_vec([(QK_NOPE, k_gain[QK_NOPE:])])
    krgs = lane_vec([(t1.start, k_gain[t2]), (t2.start, k_gain[t1])])
    return win, flat(wq), flat(wqs), flat(wk), wvt, qg, qgs, kg, krg, krgs


def _rope_freq():
    inv = ROPE_BASE ** (-np.arange(0, QK_ROPE, 2, dtype=np.float32) / QK_ROPE)
    out = np.zeros((1, LANES), np.float32)
    half = QK_ROPE // 2
    out[0, QK_NOPE:QK_NOPE + half] = inv
    out[0, QK_NOPE + half:B_DQK] = inv
    return jnp.asarray(out)


def kernel(x, positions, rel_bias, ffn_norm1, ffn1_wg, ffn1_wu, ffn1_wd, mix_norm, ffn_norm2, ffn2_wg,
           ffn2_wu, ffn2_wd, a_w_in, a_q_gain, a_k_gain, a_sinks, a_w_out, b_w_in, b_q_norm, b_kv_norm,
           b_w_uq, b_w_ukv, b_q_gain, b_k_gain, b_w_out):
    assert x.shape == (BATCH, SEQ, D_MODEL) and positions.shape == (BATCH, SEQ)
    xt = x.reshape(TOKENS, D_MODEL)
    pos_col = positions.reshape(TOKENS, 1)
    pos_row = positions.reshape(TOKENS // BLOCK, 1, BLOCK)
    table = jnp.zeros((A_HEADS, LANES), F32).at[:, :NUM_BUCKETS].set(rel_bias.T)
    bf = lambda w: w.astype(BF16)
    row = lambda v: v[None, :]
    gain1, gain2 = ffn_norm1[:, None, :], ffn_norm2[:, None, :]
    weights = _first_ffn_weights(0, ffn1_wg, ffn1_wu, ffn1_wd)

    for i in range(DEPTH):
        xt, weights = _ffn(xt, gain1, i, *weights, cast_ahead=(i, ffn2_wg, ffn2_wu, ffn2_wd))
        j = i // N_MIXERS
        if i % N_MIXERS == 0:
            w, qg, kg = _prep_a(a_w_in[j], a_q_gain[j], a_k_gain[j])
            q, k2, v2 = _a_qkv(xt, row(mix_norm[i]), w, qg, kg)
            attn = _a_attn(q, k2, v2, pos_col, pos_row, table, a_sinks[j])
            w_out = bf(a_w_out[j])
        else:
            prep = _prep_b(b_w_in[j], b_w_uq[j], b_w_ukv[j], b_q_gain[j], b_k_gain[j])
            win, wq, wqs, wk, wvt, qg, qgs, kg, krg, krgs = prep
            q, k, vt = _b_proj(xt, pos_col, row(mix_norm[i]), win, row(b_q_norm[j]), row(b_kv_norm[j]),
                              wq, wqs, wk, wvt, qg, qgs, kg, krg, krgs, _rope_freq())
            attn = _b_attn(q, k, vt)
            w_out = bf(b_w_out[j])
        ahead = (i + 1, ffn1_wg, ffn1_wu, ffn1_wd) if i + 1 < DEPTH else None
        xt, weights = _ffn(xt, gain2, i, *weights, attn=attn, w_out=w_out, cast_ahead=ahead)
    return xt.reshape(BATCH, SEQ, D_MODEL)
```

```python
import functools
import math

import numpy as np
import jax
import jax.numpy as jnp
from jax import lax
from jax.experimental import pallas as pl
from jax.experimental.pallas import tpu as pltpu

D_MODEL = 1024
BATCH = 8
SEQ = 2048
DEPTH = 2
N_MIXERS = 2
A_HEADS = 16
A_KV_HEADS = 2
A_HEAD_DIM = 64
WINDOW = 128
BLOCK = 128
NUM_BUCKETS = 32
MAX_DISTANCE = 128
B_HEADS = 16
Q_LORA = 256
KV_LORA = 128
QK_NOPE = 64
QK_ROPE = 32
V_DIM = 64
ROPE_BASE = 10000.0
D_FF = 2816
EPS = 1e-6
NEG = -1e30

TOKENS = BATCH * SEQ
LANES = 128
HALF = LANES // 2
B_DQK = QK_NOPE + QK_ROPE
VMEM_LIMIT = 60 * 1024 * 1024

LOG2E = math.log2(math.e)

F32 = jnp.float32
BF16 = jnp.bfloat16


def _rms_rows(x, gain):
    return x * lax.rsqrt(jnp.mean(x * x, axis=-1, keepdims=True) + EPS) * gain


def _dot(a, b):
    return jnp.dot(a, b, preferred_element_type=F32)


def _dot_nt(a, b):
    return lax.dot_general(a, b, (((1,), (1,)), ((), ())), preferred_element_type=F32)


def _const_spec(shape):
    nd = len(shape)
    return pl.BlockSpec(shape, lambda *_: (0,) * nd, pipeline_mode=pl.Buffered(1))


FFN_TM = 1024
MXU_WIDTH = 256
FFN_CHUNKS = ((0, 4 * MXU_WIDTH), (4 * MXU_WIDTH, 8 * MXU_WIDTH), (8 * MXU_WIDTH, D_FF))


def _ffn_body(x, g_ref, wgu_ref, wd_ref, o_ref):
    h = _rms_rows(x, g_ref[...]).astype(BF16)
    down = None
    for lo, hi in FFN_CHUNKS:
        gate_up = _dot(h, wgu_ref[:, 2 * lo:2 * hi])
        acts = []
        for c in range((hi - lo) // MXU_WIDTH):
            gate = gate_up[:, 2 * c * MXU_WIDTH:(2 * c + 1) * MXU_WIDTH]
            up = gate_up[:, (2 * c + 1) * MXU_WIDTH:(2 * c + 2) * MXU_WIDTH]
            acts.append((gate * jax.nn.sigmoid(gate) * up).astype(BF16))
        part = _dot(jnp.concatenate(acts, axis=1), wd_ref[lo:hi, :])
        down = part if down is None else down + part
    o_ref[...] = x + 0.5 * down


def _ffn_kernel(*refs, project, cast_ahead):
    refs = list(refs)
    x = refs.pop(0)[...]
    if project:
        a_ref, wo_ref = refs.pop(0), refs.pop(0)
        x = x + _dot(a_ref[...], wo_ref[...])
    g_ref, wgu_ref, wd_ref = refs[:3]
    refs = refs[3:]
    if cast_ahead:
        _cast_ffn_weights(*refs[:3], *refs[4:6])
        refs = refs[3:]
    _ffn_body(x, g_ref, wgu_ref, wd_ref, refs[0])


def _cast_ffn_weights(wg_src, wu_src, wd_src, wgu_dst, wd_dst):
    for c in range(D_FF // MXU_WIDTH):
        cols = slice(c * MXU_WIDTH, (c + 1) * MXU_WIDTH)
        wgu_dst[:, 2 * c * MXU_WIDTH:(2 * c + 1) * MXU_WIDTH] = wg_src[:, cols].astype(BF16)
        wgu_dst[:, (2 * c + 1) * MXU_WIDTH:(2 * c + 2) * MXU_WIDTH] = wu_src[:, cols].astype(BF16)
    wd_dst[...] = wd_src[...].astype(BF16)


def _cast_specs(layer, stacks, steps):
    in_specs = [pl.BlockSpec((None, w.shape[1] // steps, w.shape[2]), lambda i: (layer, i, 0)) for w in stacks]
    out_specs = [pl.BlockSpec((D_MODEL // steps, 2 * D_FF), lambda i: (i, 0)),
                 pl.BlockSpec((D_FF // steps, D_MODEL), lambda i: (i, 0))]
    out_shape = [jax.ShapeDtypeStruct((D_MODEL, 2 * D_FF), BF16), jax.ShapeDtypeStruct((D_FF, D_MODEL), BF16)]
    return in_specs, out_specs, out_shape


def _first_ffn_weights(layer, wg, wu, wd):
    steps = 8
    in_specs, out_specs, out_shape = _cast_specs(layer, (wg, wu, wd), steps)
    return pl.pallas_call(_cast_ffn_weights, grid=(steps,), in_specs=in_specs, out_specs=out_specs,
                          out_shape=out_shape, name="cast_ffn_weights")(wg, wu, wd)


def _ffn(x, gain, layer, wgu, wd, attn=None, w_out=None, cast_ahead=None):
    tm = FFN_TM
    steps = TOKENS // tm
    row_spec = pl.BlockSpec((tm, D_MODEL), lambda i: (i, 0))
    in_specs, args = [row_spec], [x]
    if attn is not None:
        in_specs += [row_spec, _const_spec((D_MODEL, D_MODEL))]
        args += [attn, w_out]
    in_specs += [pl.BlockSpec((None, 1, D_MODEL), lambda i: (layer, 0, 0), pipeline_mode=pl.Buffered(1)),
                 _const_spec((D_MODEL, 2 * D_FF)), _const_spec((D_FF, D_MODEL))]
    args += [gain, wgu, wd]
    out_specs = [row_spec]
    out_shape = [jax.ShapeDtypeStruct((TOKENS, D_MODEL), F32)]
    if cast_ahead is not None:
        next_layer, *stacks = cast_ahead
        cast_in, cast_out, cast_shape = _cast_specs(next_layer, stacks, steps)
        in_specs += cast_in
        args += stacks
        out_specs += cast_out
        out_shape += cast_shape
    outs = pl.pallas_call(
        functools.partial(_ffn_kernel, project=attn is not None, cast_ahead=cast_ahead is not None),
        grid=(steps,),
        in_specs=in_specs,
        out_specs=out_specs,
        out_shape=out_shape,
        compiler_params=pltpu.CompilerParams(vmem_limit_bytes=VMEM_LIMIT),
        name="ffn" if attn is None else "proj_ffn",
    )(*args)
    return outs[0], tuple(outs[1:])


A_TM = 512
A_QW = A_HEADS * A_HEAD_DIM
A_KW = 4 * LANES
A_PAIRS = A_HEADS // 2
A_ROWS = 1024
A_LOOKAHEAD = 1


def _a_qkv_kernel(x_ref, g_ref, w_ref, qg_ref, kg_ref, q_ref, k_ref, v_ref):
    h = _rms_rows(x_ref[...], g_ref[...]).astype(BF16)
    qkv = _dot(h, w_ref[...])
    lane = lax.broadcasted_iota(jnp.int32, (1, LANES), 1)
    low = lane < HALF
    root_d = math.sqrt(A_HEAD_DIM)
    q_table = qg_ref[...] * (root_d * A_HEAD_DIM ** -0.5 * LOG2E)
    k_table = kg_ref[...] * root_d

    def pair_norm(pair, table):
        sq = pair * pair
        ss_lo = jnp.sum(jnp.where(low, sq, 0.0), axis=-1, keepdims=True)
        ss_hi = jnp.sum(jnp.where(low, 0.0, sq), axis=-1, keepdims=True)
        r = jnp.where(low, lax.rsqrt(ss_lo + A_HEAD_DIM * EPS), lax.rsqrt(ss_hi + A_HEAD_DIM * EPS))
        return (pair * table * r).astype(BF16)

    for p in range(A_PAIRS):
        q_ref[:, p * LANES:(p + 1) * LANES] = pair_norm(qkv[:, p * LANES:(p + 1) * LANES], q_table)
    k_ref[...] = pair_norm(qkv[:, A_QW:A_QW + LANES], k_table)
    v_ref[...] = qkv[:, A_QW + LANES:].astype(BF16)


def _a_qkv(x, gain, w, qg, kg):
    tm = A_TM
    width = A_QW + 2 * LANES
    return pl.pallas_call(
        _a_qkv_kernel,
        grid=(TOKENS // tm,),
        in_specs=[pl.BlockSpec((tm, D_MODEL), lambda i: (i, 0)), _const_spec((1, D_MODEL)),
                  _const_spec((D_MODEL, width)), _const_spec((1, LANES)), _const_spec((1, LANES))],
        out_specs=[pl.BlockSpec((tm, A_QW), lambda i: (i, 0)), pl.BlockSpec((tm, LANES), lambda i: (i, 0)),
                   pl.BlockSpec((tm, LANES), lambda i: (i, 0))],
        out_shape=[jax.ShapeDtypeStruct((TOKENS, A_QW), BF16), jax.ShapeDtypeStruct((TOKENS, LANES), BF16),
                   jax.ShapeDtypeStruct((TOKENS, LANES), BF16)],
        compiler_params=pltpu.CompilerParams(vmem_limit_bytes=VMEM_LIMIT),
        name="a_qkv",
    )(x, gain, w, qg, kg)


def _t5_bucket(dist):
    n = jnp.maximum(dist, 0)
    max_exact = NUM_BUCKETS // 2
    large = max_exact + (jnp.log(jnp.maximum(n, 1).astype(F32) / max_exact)
                         / math.log(MAX_DISTANCE / max_exact)
                         * (NUM_BUCKETS - max_exact)).astype(jnp.int32)
    large = jnp.minimum(large, NUM_BUCKETS - 1)
    return jnp.where(n < max_exact, n, large)


SUBLANES = 8


def _a_attn_kernel(sink_ref, q_ref, kc_ref, kp_ref, vc_ref, vp_ref, pq_ref, pkc_ref, pkp_ref, tbl_ref, o_ref,
                   bias_ref, shared_ref):
    row = lax.broadcasted_iota(jnp.int32, (BLOCK, BLOCK), 0)
    col = lax.broadcasted_iota(jnp.int32, (BLOCK, BLOCK), 1)
    cur_ok = col <= row
    first_in_seq = pl.program_id(1) == 0
    no_prev = jnp.where(first_in_seq, NEG, 0.0)
    tables = [jnp.broadcast_to(tbl_ref[h:h + 1, :] * LOG2E, (SUBLANES, LANES)) for h in range(A_HEADS)]
    group_heads = A_HEADS // A_KV_HEADS
    lane = lax.broadcasted_iota(jnp.int32, (1, LANES), 1)

    @pl.when(jnp.logical_and(pl.program_id(0) == 0, first_in_seq))
    def _():
        shared_ref[0] = 0

    def fill_shift_invariant():
        back = jnp.broadcast_to((BLOCK - lane) & (BLOCK - 1), (SUBLANES, LANES))
        idx = _t5_bucket(back)
        for h in range(A_HEADS):
            base = jnp.take_along_axis(tables[h], idx, axis=1)
            base = jnp.broadcast_to(base[0:1, :], (BLOCK, BLOCK))
            tile = pltpu.roll(base, 0, 1, stride=1, stride_axis=0)
            bias_ref[h] = jnp.where(cur_ok, tile, NEG)
            bias_ref[A_HEADS + h] = jnp.where(cur_ok, NEG, tile)

    def fill_general(pos_q, pos_cur, pos_prev):
        bucket = jnp.where(cur_ok, _t5_bucket(pos_q - pos_cur), _t5_bucket(pos_q - pos_prev))
        for c in range(BLOCK // SUBLANES):
            chunk = slice(c * SUBLANES, (c + 1) * SUBLANES)
            for h in range(A_HEADS):
                piece = jnp.take_along_axis(tables[h], bucket[chunk, :], axis=1)
                bias_ref[h, chunk, :] = jnp.where(cur_ok[chunk, :], piece, NEG)
                bias_ref[A_HEADS + h, chunk, :] = jnp.where(cur_ok[chunk, :], NEG, piece)

    blocks = range(A_ROWS // BLOCK)

    def rows_of(r):
        return slice(r * BLOCK, (r + 1) * BLOCK)

    def pairs_of(kv):
        return range(kv * group_heads // 2, (kv + 1) * group_heads // 2)

    def lane_groups(pair):
        low_half = lane < HALF
        swapped = pltpu.roll(pair, HALF, 1)
        zero = jnp.zeros_like(pair)
        return jnp.concatenate([jnp.where(low_half, pair, zero), jnp.where(low_half, zero, swapped),
                                jnp.where(low_half, swapped, zero), jnp.where(low_half, zero, pair)], axis=1)

    def pos_prev_of(r):
        return pkp_ref[0] if r == 0 else pkc_ref[r - 1]

    def all_bands(which_blocks):
        first, stop = which_blocks[0], which_blocks[-1] + 1
        if first == 0:
            k_rows = jnp.concatenate([kp_ref[...], kc_ref[:stop * BLOCK, :]], axis=0)
            v_rows = jnp.concatenate([vp_ref[...], vc_ref[:stop * BLOCK, :]], axis=0)
        else:
            k_rows = kc_ref[(first - 1) * BLOCK:stop * BLOCK, :]
            v_rows = vc_ref[(first - 1) * BLOCK:stop * BLOCK, :]
        k_groups, v_groups = lane_groups(k_rows), lane_groups(v_rows)
        return {r: (k_groups[(r - first) * BLOCK:(r - first + 2) * BLOCK, :],
                    v_groups[(r - first) * BLOCK:(r - first + 2) * BLOCK, :]) for r in which_blocks}

    low_ones = jnp.where(lane < HALF, 1.0, 0.0).astype(BF16)
    ones_lo = jnp.broadcast_to(low_ones, (2 * BLOCK, LANES))
    ones_hi = jnp.broadcast_to(1.0 - low_ones, (2 * BLOCK, LANES)).astype(BF16)

    def scores(r, kv, bands):
        k_band = bands[r][0]
        keys = jnp.concatenate([k_band[:, (2 * kv + parity) * LANES:(2 * kv + parity + 1) * LANES]
                                for parity in range(2)], axis=0)
        q_stack = jnp.concatenate([q_ref[rows_of(r), p * LANES:(p + 1) * LANES] for p in pairs_of(kv)], axis=0)
        return _dot_nt(q_stack, keys)

    def finish(r, kv, s_all, bias, bands):
        v_band = bands[r][1]
        probs, sink_terms = [], []
        for i, p in enumerate(pairs_of(kv)):
            pair_probs, pair_sinks = [], []
            for parity in range(2):
                head = 2 * p + parity
                s = s_all[i * BLOCK:(i + 1) * BLOCK, parity * 2 * BLOCK:(parity + 1) * 2 * BLOCK]
                s_prev = s[:, :BLOCK] + bias[A_HEADS + head]
                if r == 0:
                    s_prev = s_prev + no_prev
                s_cur = s[:, BLOCK:] + bias[head]
                sink = sink_ref[head] * LOG2E
                m = jnp.maximum(jnp.max(jnp.maximum(s_prev, s_cur), axis=-1, keepdims=True), sink)
                pair_probs += [jnp.exp2(s_prev - m), jnp.exp2(s_cur - m)]
                pair_sinks.append(jnp.exp2(sink - m))
            probs.append(jnp.concatenate(pair_probs, axis=1).astype(BF16))
            sink_terms.append(jnp.where(lane < HALF, pair_sinks[0], pair_sinks[1]))
        even, odd = (slice((2 * kv + parity) * LANES, (2 * kv + parity + 1) * LANES) for parity in range(2))
        v_ext = jnp.concatenate([jnp.concatenate([v_band[:, even], ones_lo], axis=1),
                                 jnp.concatenate([v_band[:, odd], ones_hi], axis=1)], axis=0)
        out = _dot(jnp.concatenate(probs, axis=0), v_ext)
        for i, p in enumerate(pairs_of(kv)):
            o_i = out[i * BLOCK:(i + 1) * BLOCK, :]
            o_ref[rows_of(r), p * LANES:(p + 1) * LANES] = (
                o_i[:, :LANES] * (1.0 / (o_i[:, LANES:] + sink_terms[i]))).astype(BF16)

    def attend(which_blocks):
        bias = [bias_ref[h] for h in range(2 * A_HEADS)]
        bands = all_bands(list(which_blocks))
        units = [(r, kv) for r in which_blocks for kv in range(A_KV_HEADS)]
        raw = {u: scores(*u, bands) for u in units[:A_LOOKAHEAD]}
        for i, unit in enumerate(units):
            if i + A_LOOKAHEAD < len(units):
                ahead = units[i + A_LOOKAHEAD]
                raw[ahead] = scores(*ahead, bands)
            finish(*unit, raw.pop(unit), bias, bands)

    @pl.when(shared_ref[0] == 0)
    def _():
        fill_shift_invariant()
        shared_ref[0] = 1

    attend(blocks)

    off = jnp.zeros((1, LANES), jnp.int32)
    for r in blocks:
        pos_cur, pos_prev = pkc_ref[r], pos_prev_of(r)
        start = jnp.min(pos_cur, axis=-1, keepdims=True)
        off_prev = jnp.where(pos_prev - start == lane - BLOCK, 0, 1)
        if r == 0:
            off_prev = off_prev * jnp.where(first_in_seq, 0, 1)
        off = off + jnp.where(pos_cur - start == lane, 0, 1) + off_prev

    @pl.when(jnp.sum(off) != 0)
    def _():
        for r in blocks:
            fill_general(pq_ref[rows_of(r), :], pkc_ref[r], pos_prev_of(r))
            attend([r])
        shared_ref[0] = 0


def _a_attn(q, k2, v2, pos_col, pos_row, table, sinks):
    steps = SEQ // A_ROWS
    blocks_per_step = A_ROWS // BLOCK
    blocks_per_seq = SEQ // BLOCK

    def cur(b, i):
        return (b * steps + i, 0)

    def prev(b, i):
        return (b * blocks_per_seq + jnp.maximum(i * blocks_per_step - 1, 0), 0)

    def cur3(b, i):
        return (b * steps + i, 0, 0)

    def prev3(b, i):
        return (b * blocks_per_seq + jnp.maximum(i * blocks_per_step - 1, 0), 0, 0)

    return pl.pallas_call(
        _a_attn_kernel,
        grid=(BATCH, steps),
        in_specs=[
            pl.BlockSpec(memory_space=pltpu.SMEM),
            pl.BlockSpec((A_ROWS, A_QW), cur),
            pl.BlockSpec((A_ROWS, LANES), cur),
            pl.BlockSpec((BLOCK, LANES), prev),
            pl.BlockSpec((A_ROWS, LANES), cur),
            pl.BlockSpec((BLOCK, LANES), prev),
            pl.BlockSpec((A_ROWS, 1), cur),
            pl.BlockSpec((blocks_per_step, 1, BLOCK), cur3),
            pl.BlockSpec((1, 1, BLOCK), prev3),
            _const_spec((A_HEADS, LANES)),
        ],
        out_specs=pl.BlockSpec((A_ROWS, A_QW), cur),
        out_shape=jax.ShapeDtypeStruct((TOKENS, A_QW), BF16),
        scratch_shapes=[pltpu.VMEM((2 * A_HEADS, BLOCK, BLOCK), F32), pltpu.SMEM((1,), jnp.int32)],
        compiler_params=pltpu.CompilerParams(dimension_semantics=("arbitrary", "arbitrary"),
                                             vmem_limit_bytes=VMEM_LIMIT),
        name="a_attn",
    )(sinks, q, k2, k2, v2, v2, pos_col, pos_row, pos_row, table)


B_TM = 1024
B_CW = Q_LORA + KV_LORA + 2 * LANES
B_HW = B_HEADS * LANES
B_VW = B_HEADS * V_DIM
B_TQ = 512
B_TK = 512


def _b_proj_kernel(x_ref, pos_ref, g_ref, win_ref, qn_ref, kvn_ref, wq_ref, wqs_ref, wk_ref, wvt_ref,
                   qg_ref, qgs_ref, kg_ref, krg_ref, krgs_ref, freq_ref, q_ref, k_ref, vt_ref):
    h = _rms_rows(x_ref[...], g_ref[...]).astype(BF16)
    c = _dot(h, win_ref[...])
    cq = _rms_rows(c[:, :Q_LORA], qn_ref[...]).astype(BF16)
    ckv_f32 = _rms_rows(c[:, Q_LORA:Q_LORA + KV_LORA], kvn_ref[...])
    ckv = ckv_f32.astype(BF16)
    kr = c[:, Q_LORA + KV_LORA:Q_LORA + KV_LORA + LANES]
    kr_partner = c[:, Q_LORA + KV_LORA + LANES:]
    q = _dot(cq, wq_ref[...])
    q_partner = _dot(cq, wqs_ref[...])
    kn = _dot(ckv, wk_ref[...])
    vt_ref[...] = _dot(wvt_ref[...], ckv_f32.T.astype(BF16)).astype(BF16)

    lane = lax.broadcasted_iota(jnp.int32, (1, LANES), 1)
    ang = pos_ref[...].astype(F32) * freq_ref[...]
    cos = jnp.cos(ang)
    sin = jnp.sin(ang)
    half = QK_ROPE // 2
    sin_signed = jnp.where(lane < QK_NOPE + half, -sin, sin)

    root_d = math.sqrt(B_DQK)
    q_const = root_d * B_DQK ** -0.5 * LOG2E
    k_rope = (kr * krg_ref[...] * cos + kr_partner * krgs_ref[...] * sin_signed) * root_d
    k_gain = kg_ref[...] * root_d
    ss_rope = jnp.sum(kr * kr, axis=-1, keepdims=True) + B_DQK * EPS
    q_cos = qg_ref[...] * cos * q_const
    q_sin = qgs_ref[...] * sin_signed * q_const
    for hd in range(B_HEADS):
        g = slice(hd * LANES, (hd + 1) * LANES)
        qh = q[:, g]
        r = lax.rsqrt(jnp.sum(qh * qh, axis=-1, keepdims=True) + B_DQK * EPS)
        q_ref[:, g] = ((qh * q_cos + q_partner[:, g] * q_sin) * r).astype(BF16)
        kh = kn[:, g]
        rk = lax.rsqrt(jnp.sum(kh * kh, axis=-1, keepdims=True) + ss_rope)
        k_ref[:, g] = ((kh * k_gain + k_rope) * rk).astype(BF16)


def _b_proj(x, pos_col, gain, w_in, qn, kvn, wq, wqs, wk, wvt, qg, qgs, kg, krg, krgs, freq):
    tm = B_TM
    tiles_per_seq = SEQ // tm
    lane_vec = _const_spec((1, LANES))
    out_spec = pl.BlockSpec((tm, B_HW), lambda i: (i, 0))
    out_sds = jax.ShapeDtypeStruct((TOKENS, B_HW), BF16)
    vt_spec = pl.BlockSpec((None, B_VW, tm), lambda i: (i // tiles_per_seq, 0, i % tiles_per_seq))
    return pl.pallas_call(
        _b_proj_kernel,
        grid=(TOKENS // tm,),
        in_specs=[pl.BlockSpec((tm, D_MODEL), lambda i: (i, 0)), pl.BlockSpec((tm, 1), lambda i: (i, 0)),
                  _const_spec((1, D_MODEL)), _const_spec((D_MODEL, B_CW)), _const_spec((1, Q_LORA)),
                  _const_spec((1, KV_LORA)), _const_spec((Q_LORA, B_HW)), _const_spec((Q_LORA, B_HW)),
                  _const_spec((KV_LORA, B_HW)), _const_spec((B_VW, KV_LORA)),
                  lane_vec, lane_vec, lane_vec, lane_vec, lane_vec, lane_vec],
        out_specs=[out_spec, out_spec, vt_spec],
        out_shape=[out_sds, out_sds, jax.ShapeDtypeStruct((BATCH, B_VW, SEQ), BF16)],
        compiler_params=pltpu.CompilerParams(vmem_limit_bytes=VMEM_LIMIT),
        name="b_proj",
    )(x, pos_col, gain, w_in, qn, kvn, wq, wqs, wk, wvt, qg, qgs, kg, krg, krgs, freq)


B_HEADS_PER_STEP = 4
B_LOOKAHEAD = 3


def _b_attn_kernel(q_ref, k_ref, vt_ref, o_ref):
    heads = range(B_HEADS_PER_STEP)
    groups = [slice(hd * LANES, (hd + 1) * LANES) for hd in heads]
    n_query_tiles = SEQ // B_TQ
    half = B_TQ // 2

    def chain(qt, hd):
        pieces = [(qt, hd, j * B_TK, B_TK, 0, B_TQ) for j in range(qt * B_TQ // B_TK)]
        return pieces + [(qt, hd, qt * B_TQ, half, 0, B_TQ), (qt, hd, qt * B_TQ + half, half, half, half)]

    def scores(qt, hd, key_lo, n_keys, q_lo, n_q):
        queries = slice(qt * B_TQ + q_lo, qt * B_TQ + q_lo + n_q)
        return _dot_nt(k_ref[key_lo:key_lo + n_keys, groups[hd]], q_ref[queries, groups[hd]])

    def accumulate(unit, s, carry):
        qt, hd, key_lo, n_keys, q_lo, n_q = unit
        m_all, acc_all = carry
        m, acc = m_all[:, q_lo:q_lo + n_q], acc_all[:, q_lo:q_lo + n_q]
        first_query = qt * B_TQ + q_lo
        if key_lo + n_keys - 1 > first_query:
            key = lax.broadcasted_iota(jnp.int32, (n_keys, n_q), 0)
            query = lax.broadcasted_iota(jnp.int32, (n_keys, n_q), 1)
            s = jnp.where(key + (key_lo - first_query) <= query, s, NEG)
        m_new = jnp.maximum(m, jnp.max(s, axis=0, keepdims=True))
        alpha = jnp.exp2(m - m_new)
        e = jnp.exp2(s - m_new).astype(BF16)
        vt_ext = jnp.concatenate([vt_ref[hd * V_DIM:(hd + 1) * V_DIM, key_lo:key_lo + n_keys],
                                  jnp.ones((V_DIM, n_keys), BF16)], axis=0)
        acc_new = alpha * acc + _dot(vt_ext, e)
        if q_lo:
            m_new = jnp.concatenate([m_all[:, :q_lo], m_new], axis=1)
            acc_new = jnp.concatenate([acc_all[:, :q_lo], acc_new], axis=1)
        return m_new, acc_new

    chains = [chain(qt, hd) for qt in reversed(range(n_query_tiles)) for hd in heads]
    units = [c[i] for i in range(len(chains[0])) for c in chains if i < len(c)]
    last = {c[-1] for c in chains}
    init = (jnp.full((1, B_TQ), NEG, F32), jnp.zeros((2 * V_DIM, B_TQ), F32))
    carry = {(qt, hd): init for qt in range(n_query_tiles) for hd in heads}
    raw = {u: scores(*u) for u in units[:B_LOOKAHEAD]}
    for i, unit in enumerate(units):
        qt, hd = unit[0], unit[1]
        if i + B_LOOKAHEAD < len(units):
            ahead = units[i + B_LOOKAHEAD]
            raw[ahead] = scores(*ahead)
        carry[(qt, hd)] = accumulate(unit, raw.pop(unit), carry[(qt, hd)])
        if hd % 2 == 1 and unit in last:
            outs = [acc[:V_DIM, :] * (1.0 / acc[V_DIM:V_DIM + 1, :])
                    for _, acc in (carry.pop((qt, hd - 1)), carry.pop((qt, hd)))]
            o_ref[qt * B_TQ:(qt + 1) * B_TQ, (hd - 1) * V_DIM:(hd + 1) * V_DIM] = (
                jnp.concatenate(outs, axis=0).T.astype(BF16))


def _b_attn(q, k, vt):
    n = B_HEADS_PER_STEP
    return pl.pallas_call(
        _b_attn_kernel,
        grid=(BATCH, B_HEADS // n),
        in_specs=[pl.BlockSpec((SEQ, n * LANES), lambda b, p: (b, p)),
                  pl.BlockSpec((SEQ, n * LANES), lambda b, p: (b, p)),
                  pl.BlockSpec((None, n * V_DIM, SEQ), lambda b, p: (b, p, 0))],
        out_specs=pl.BlockSpec((SEQ, n * V_DIM), lambda b, p: (b, p)),
        out_shape=jax.ShapeDtypeStruct((TOKENS, B_VW), BF16),
        compiler_params=pltpu.CompilerParams(vmem_limit_bytes=VMEM_LIMIT),
        name="b_attn",
    )(q, k, vt)


def _prep_a(w_in, q_gain, k_gain):
    qg = jnp.concatenate([q_gain, q_gain])[None, :]
    kg = jnp.concatenate([k_gain, k_gain])[None, :]
    return w_in.astype(BF16), qg, kg


def _head_groups(w, per_head, src_lo, src_hi, dst_lo):
    rows = w.shape[0]
    w3 = w.reshape(rows, B_HEADS, per_head)[:, :, src_lo:src_hi]
    out = jnp.zeros((rows, B_HEADS, LANES), w.dtype)
    out = out.at[:, :, dst_lo:dst_lo + (src_hi - src_lo)].set(w3)
    return out


def _prep_b(w_in, w_uq, w_ukv, q_gain, k_gain):
    half = QK_ROPE // 2
    t1 = slice(QK_NOPE, QK_NOPE + half)
    t2 = slice(QK_NOPE + half, B_DQK)
    rows = w_in.shape[0]
    rope_cols = w_in[:, Q_LORA + KV_LORA:]
    zeros = lambda n: jnp.zeros((rows, n), w_in.dtype)
    rope_group = jnp.concatenate([zeros(QK_NOPE), rope_cols, zeros(LANES - B_DQK)], axis=1)
    partner_group = jnp.concatenate(
        [zeros(QK_NOPE), rope_cols[:, half:], rope_cols[:, :half], zeros(LANES - B_DQK)], axis=1)
    win = jnp.concatenate([w_in[:, :Q_LORA + KV_LORA], rope_group, partner_group], axis=1).astype(BF16)

    wq = _head_groups(w_uq, B_DQK, 0, B_DQK, 0)
    wqs = (_head_groups(w_uq, B_DQK, t2.start, t2.stop, t1.start)
           + _head_groups(w_uq, B_DQK, t1.start, t1.stop, t2.start))
    wk = _head_groups(w_ukv, QK_NOPE + V_DIM, 0, QK_NOPE, 0)
    wvt = w_ukv.reshape(KV_LORA, B_HEADS, QK_NOPE + V_DIM)[:, :, QK_NOPE:].reshape(KV_LORA, B_VW).T.astype(BF16)
    flat = lambda a: a.reshape(a.shape[0], B_HW).astype(BF16)

    def lane_vec(pieces):
        out = jnp.zeros((LANES,), F32)
        for lo, vals in pieces:
            out = out.at[lo:lo + vals.shape[0]].set(vals)
        return out[None, :]

    qg = lane_vec([(0, q_gain)])
    qgs = lane_vec([(t1.start, q_gain[t2]), (t2.start, q_gain[t1])])
    kg = lane_vec([(0, k_gain[:QK_NOPE])])
    krg = lane_vec([(QK_NOPE, k_gain[QK_NOPE:])])
    krgs = lane_vec([(t1.start, k_gain[t2]), (t2.start, k_gain[t1])])
    return win, flat(wq), flat(wqs), flat(wk), wvt, qg, qgs, kg, krg, krgs


def _rope_freq():
    inv = ROPE_BASE ** (-np.arange(0, QK_ROPE, 2, dtype=np.float32) / QK_ROPE)
    out = np.zeros((1, LANES), np.float32)
    half = QK_ROPE // 2
    out[0, QK_NOPE:QK_NOPE + half] = inv
    out[0, QK_NOPE + half:B_DQK] = inv
    return jnp.asarray(out)


def kernel(x, positions, rel_bias, ffn_norm1, ffn1_wg, ffn1_wu, ffn1_wd, mix_norm, ffn_norm2, ffn2_wg,
           ffn2_wu, ffn2_wd, a_w_in, a_q_gain, a_k_gain, a_sinks, a_w_out, b_w_in, b_q_norm, b_kv_norm,
           b_w_uq, b_w_ukv, b_q_gain, b_k_gain, b_w_out):
    assert x.shape == (BATCH, SEQ, D_MODEL) and positions.shape == (BATCH, SEQ)
    xt = x.reshape(TOKENS, D_MODEL)
    pos_col = positions.reshape(TOKENS, 1)
    pos_row = positions.reshape(TOKENS // BLOCK, 1, BLOCK)
    table = jnp.zeros((A_HEADS, LANES), F32).at[:, :NUM_BUCKETS].set(rel_bias.T)
    bf = lambda w: w.astype(BF16)
    row = lambda v: v[None, :]
    gain1, gain2 = ffn_norm1[:, None, :], ffn_norm2[:, None, :]
    weights = _first_ffn_weights(0, ffn1_wg, ffn1_wu, ffn1_wd)

    for i in range(DEPTH):
        xt, weights = _ffn(xt, gain1, i, *weights, cast_ahead=(i, ffn2_wg, ffn2_wu, ffn2_wd))
        j = i // N_MIXERS
        if i % N_MIXERS == 0:
            w, qg, kg = _prep_a(a_w_in[j], a_q_gain[j], a_k_gain[j])
            q, k2, v2 = _a_qkv(xt, row(mix_norm[i]), w, qg, kg)
            attn = _a_attn(q, k2, v2, pos_col, pos_row, table, a_sinks[j])
            w_out = bf(a_w_out[j])
        else:
            prep = _prep_b(b_w_in[j], b_w_uq[j], b_w_ukv[j], b_q_gain[j], b_k_gain[j])
            win, wq, wqs, wk, wvt, qg, qgs, kg, krg, krgs = prep
            q, k, vt = _b_proj(xt, pos_col, row(mix_norm[i]), win, row(b_q_norm[j]), row(b_kv_norm[j]),
                              wq, wqs, wk, wvt, qg, qgs, kg, krg, krgs, _rope_freq())
            attn = _b_attn(q, k, vt)
            w_out = bf(b_w_out[j])
        ahead = (i + 1, ffn1_wg, ffn1_wu, ffn1_wd) if i + 1 < DEPTH else None
        xt, weights = _ffn(xt, gain2, i, *weights, attn=attn, w_out=w_out, cast_ahead=ahead)
    return xt.reshape(BATCH, SEQ, D_MODEL)
```

```python
import functools
import math

import numpy as np
import jax
import jax.numpy as jnp
from jax import lax
from jax.experimental import pallas as pl
from jax.experimental.pallas import tpu as pltpu

D_MODEL = 1024
BATCH = 8
SEQ = 2048
DEPTH = 2
N_MIXERS = 2
A_HEADS = 16
A_KV_HEADS = 2
A_HEAD_DIM = 64
WINDOW = 128
BLOCK = 128
NUM_BUCKETS = 32
MAX_DISTANCE = 128
B_HEADS = 16
Q_LORA = 256
KV_LORA = 128
QK_NOPE = 64
QK_ROPE = 32
V_DIM = 64
ROPE_BASE = 10000.0
D_FF = 2816
EPS = 1e-6
NEG = -1e30

TOKENS = BATCH * SEQ
LANES = 128
HALF = LANES // 2
B_DQK = QK_NOPE + QK_ROPE
MIB = 1024 * 1024
VMEM_LIMIT = 60 * MIB
VMEM_SMALL = 16 * MIB
VMEM_MEDIUM = 24 * MIB
VMEM_LARGE = 56 * MIB

LOG2E = math.log2(math.e)

F32 = jnp.float32
BF16 = jnp.bfloat16


def _rms_rows(x, gain):
    return x * lax.rsqrt(jnp.mean(x * x, axis=-1, keepdims=True) + EPS) * gain


def _dot(a, b):
    return jnp.dot(a, b, preferred_element_type=F32)


def _dot_nt(a, b):
    return lax.dot_general(a, b, (((1,), (1,)), ((), ())), preferred_element_type=F32)


def _const_spec(shape):
    nd = len(shape)
    return pl.BlockSpec(shape, lambda *_: (0,) * nd, pipeline_mode=pl.Buffered(1))


FFN_TM = 1024
MXU_WIDTH = 256
FFN_CHUNKS = ((0, 4 * MXU_WIDTH), (4 * MXU_WIDTH, 8 * MXU_WIDTH), (8 * MXU_WIDTH, D_FF))


def _ffn_body(x, g_ref, wgu_ref, wd_ref, o_ref):
    h = _rms_rows(x, g_ref[...]).astype(BF16)
    down = None
    for lo, hi in FFN_CHUNKS:
        gate_up = _dot(h, wgu_ref[:, 2 * lo:2 * hi])
        acts = []
        for c in range((hi - lo) // MXU_WIDTH):
            gate = gate_up[:, 2 * c * MXU_WIDTH:(2 * c + 1) * MXU_WIDTH]
            up = gate_up[:, (2 * c + 1) * MXU_WIDTH:(2 * c + 2) * MXU_WIDTH]
            acts.append((gate * jax.nn.sigmoid(gate) * up).astype(BF16))
        part = _dot(jnp.concatenate(acts, axis=1), wd_ref[lo:hi, :])
        down = part if down is None else down + part
    o_ref[...] = x + 0.5 * down


def _ffn_kernel(*refs, project, cast_ahead):
    refs = list(refs)
    x = refs.pop(0)[...]
    if project:
        a_ref, wo_ref = refs.pop(0), refs.pop(0)
        x = x + _dot(a_ref[...], wo_ref[...])
    g_ref, wgu_ref, wd_ref = refs[:3]
    refs = refs[3:]
    if cast_ahead:
        _cast_ffn_weights(*refs[:3], *refs[4:6])
        refs = refs[3:]
    _ffn_body(x, g_ref, wgu_ref, wd_ref, refs[0])


def _cast_ffn_weights(wg_src, wu_src, wd_src, wgu_dst, wd_dst):
    for c in range(D_FF // MXU_WIDTH):
        cols = slice(c * MXU_WIDTH, (c + 1) * MXU_WIDTH)
        wgu_dst[:, 2 * c * MXU_WIDTH:(2 * c + 1) * MXU_WIDTH] = wg_src[:, cols].astype(BF16)
        wgu_dst[:, (2 * c + 1) * MXU_WIDTH:(2 * c + 2) * MXU_WIDTH] = wu_src[:, cols].astype(BF16)
    wd_dst[...] = wd_src[...].astype(BF16)


def _cast_specs(layer, stacks, steps):
    in_specs = [pl.BlockSpec((None, w.shape[1] // steps, w.shape[2]), lambda i: (layer, i, 0)) for w in stacks]
    out_specs = [pl.BlockSpec((D_MODEL // steps, 2 * D_FF), lambda i: (i, 0)),
                 pl.BlockSpec((D_FF // steps, D_MODEL), lambda i: (i, 0))]
    out_shape = [jax.ShapeDtypeStruct((D_MODEL, 2 * D_FF), BF16), jax.ShapeDtypeStruct((D_FF, D_MODEL), BF16)]
    return in_specs, out_specs, out_shape


def _first_ffn_weights(layer, wg, wu, wd):
    steps = 8
    in_specs, out_specs, out_shape = _cast_specs(layer, (wg, wu, wd), steps)
    return pl.pallas_call(_cast_ffn_weights, grid=(steps,), in_specs=in_specs, out_specs=out_specs,
                          out_shape=out_shape, name="cast_ffn_weights")(wg, wu, wd)


def _ffn(x, gain, layer, wgu, wd, attn=None, w_out=None, cast_ahead=None):
    tm = FFN_TM
    steps = TOKENS // tm
    row_spec = pl.BlockSpec((tm, D_MODEL), lambda i: (i, 0))
    in_specs, args = [row_spec], [x]
    if attn is not None:
        in_specs += [row_spec, _const_spec((D_MODEL, D_MODEL))]
        args += [attn, w_out]
    in_specs += [pl.BlockSpec((None, 1, D_MODEL), lambda i: (layer, 0, 0), pipeline_mode=pl.Buffered(1)),
                 _const_spec((D_MODEL, 2 * D_FF)), _const_spec((D_FF, D_MODEL))]
    args += [gain, wgu, wd]
    out_specs = [row_spec]
    out_shape = [jax.ShapeDtypeStruct((TOKENS, D_MODEL), F32)]
    if cast_ahead is not None:
        next_layer, *stacks = cast_ahead
        cast_in, cast_out, cast_shape = _cast_specs(next_layer, stacks, steps)
        in_specs += cast_in
        args += stacks
        out_specs += cast_out
        out_shape += cast_shape
    outs = pl.pallas_call(
        functools.partial(_ffn_kernel, project=attn is not None, cast_ahead=cast_ahead is not None),
        grid=(steps,),
        in_specs=in_specs,
        out_specs=out_specs,
        out_shape=out_shape,
        compiler_params=pltpu.CompilerParams(vmem_limit_bytes=VMEM_LIMIT),
        name="ffn" if attn is None else "proj_ffn",
    )(*args)
    return outs[0], tuple(outs[1:])


A_TM = 512
A_QW = A_HEADS * A_HEAD_DIM
A_KW = 4 * LANES
A_PAIRS = A_HEADS // 2
A_ROWS = 1024
A_LOOKAHEAD = 1


def _a_qkv_kernel(x_ref, g_ref, w_ref, qg_ref, kg_ref, q_ref, k_ref, v_ref):
    h = _rms_rows(x_ref[...], g_ref[...]).astype(BF16)
    qkv = _dot(h, w_ref[...])
    lane = lax.broadcasted_iota(jnp.int32, (1, LANES), 1)
    low = lane < HALF
    root_d = math.sqrt(A_HEAD_DIM)
    q_table = qg_ref[...] * (root_d * A_HEAD_DIM ** -0.5 * LOG2E)
    k_table = kg_ref[...] * root_d

    def pair_norm(pair, table):
        sq = pair * pair
        ss_lo = jnp.sum(jnp.where(low, sq, 0.0), axis=-1, keepdims=True)
        ss_hi = jnp.sum(jnp.where(low, 0.0, sq), axis=-1, keepdims=True)
        r = jnp.where(low, lax.rsqrt(ss_lo + A_HEAD_DIM * EPS), lax.rsqrt(ss_hi + A_HEAD_DIM * EPS))
        return (pair * table * r).astype(BF16)

    for p in range(A_PAIRS):
        q_ref[:, p * LANES:(p + 1) * LANES] = pair_norm(qkv[:, p * LANES:(p + 1) * LANES], q_table)
    k_ref[...] = pair_norm(qkv[:, A_QW:A_QW + LANES], k_table)
    v_ref[...] = qkv[:, A_QW + LANES:].astype(BF16)


def _a_qkv(x, gain, w, qg, kg):
    tm = A_TM
    width = A_QW + 2 * LANES
    return pl.pallas_call(
        _a_qkv_kernel,
        grid=(TOKENS // tm,),
        in_specs=[pl.BlockSpec((tm, D_MODEL), lambda i: (i, 0)), _const_spec((1, D_MODEL)),
                  _const_spec((D_MODEL, width)), _const_spec((1, LANES)), _const_spec((1, LANES))],
        out_specs=[pl.BlockSpec((tm, A_QW), lambda i: (i, 0)), pl.BlockSpec((tm, LANES), lambda i: (i, 0)),
                   pl.BlockSpec((tm, LANES), lambda i: (i, 0))],
        out_shape=[jax.ShapeDtypeStruct((TOKENS, A_QW), BF16), jax.ShapeDtypeStruct((TOKENS, LANES), BF16),
                   jax.ShapeDtypeStruct((TOKENS, LANES), BF16)],
        compiler_params=pltpu.CompilerParams(vmem_limit_bytes=VMEM_SMALL),
        name="a_qkv",
    )(x, gain, w, qg, kg)


def _t5_bucket(dist):
    n = jnp.maximum(dist, 0)
    max_exact = NUM_BUCKETS // 2
    large = max_exact + (jnp.log(jnp.maximum(n, 1).astype(F32) / max_exact)
                         / math.log(MAX_DISTANCE / max_exact)
                         * (NUM_BUCKETS - max_exact)).astype(jnp.int32)
    large = jnp.minimum(large, NUM_BUCKETS - 1)
    return jnp.where(n < max_exact, n, large)


SUBLANES = 8


def _a_attn_kernel(sink_ref, q_ref, kc_ref, kp_ref, vc_ref, vp_ref, pq_ref, pkc_ref, pkp_ref, tbl_ref, o_ref,
                   bias_ref, shared_ref):
    row = lax.broadcasted_iota(jnp.int32, (BLOCK, BLOCK), 0)
    col = lax.broadcasted_iota(jnp.int32, (BLOCK, BLOCK), 1)
    cur_ok = col <= row
    first_in_seq = pl.program_id(1) == 0
    no_prev = jnp.where(first_in_seq, NEG, 0.0)
    tables = [jnp.broadcast_to(tbl_ref[h:h + 1, :] * LOG2E, (SUBLANES, LANES)) for h in range(A_HEADS)]
    group_heads = A_HEADS // A_KV_HEADS
    lane = lax.broadcasted_iota(jnp.int32, (1, LANES), 1)

    @pl.when(jnp.logical_and(pl.program_id(0) == 0, first_in_seq))
    def _():
        shared_ref[0] = 0

    def fill_shift_invariant():
        back = jnp.broadcast_to((BLOCK - lane) & (BLOCK - 1), (SUBLANES, LANES))
        idx = _t5_bucket(back)
        for h in range(A_HEADS):
            base = jnp.take_along_axis(tables[h], idx, axis=1)
            base = jnp.broadcast_to(base[0:1, :], (BLOCK, BLOCK))
            tile = pltpu.roll(base, 0, 1, stride=1, stride_axis=0)
            bias_ref[h] = jnp.where(cur_ok, tile, NEG)
            bias_ref[A_HEADS + h] = jnp.where(cur_ok, NEG, tile)

    def fill_general(pos_q, pos_cur, pos_prev):
        bucket = jnp.where(cur_ok, _t5_bucket(pos_q - pos_cur), _t5_bucket(pos_q - pos_prev))
        for c in range(BLOCK // SUBLANES):
            chunk = slice(c * SUBLANES, (c + 1) * SUBLANES)
            for h in range(A_HEADS):
                piece = jnp.take_along_axis(tables[h], bucket[chunk, :], axis=1)
                bias_ref[h, chunk, :] = jnp.where(cur_ok[chunk, :], piece, NEG)
                bias_ref[A_HEADS + h, chunk, :] = jnp.where(cur_ok[chunk, :], NEG, piece)

    blocks = range(A_ROWS // BLOCK)

    def rows_of(r):
        return slice(r * BLOCK, (r + 1) * BLOCK)

    def pairs_of(kv):
        return range(kv * group_heads // 2, (kv + 1) * group_heads // 2)

    def lane_groups(pair):
        low_half = lane < HALF
        swapped = pltpu.roll(pair, HALF, 1)
        zero = jnp.zeros_like(pair)
        return jnp.concatenate([jnp.where(low_half, pair, zero), jnp.where(low_half, zero, swapped),
                                jnp.where(low_half, swapped, zero), jnp.where(low_half, zero, pair)], axis=1)

    def pos_prev_of(r):
        return pkp_ref[0] if r == 0 else pkc_ref[r - 1]

    def all_bands(which_blocks):
        first, stop = which_blocks[0], which_blocks[-1] + 1
        if first == 0:
            k_rows = jnp.concatenate([kp_ref[...], kc_ref[:stop * BLOCK, :]], axis=0)
            v_rows = jnp.concatenate([vp_ref[...], vc_ref[:stop * BLOCK, :]], axis=0)
        else:
            k_rows = kc_ref[(first - 1) * BLOCK:stop * BLOCK, :]
            v_rows = vc_ref[(first - 1) * BLOCK:stop * BLOCK, :]
        k_groups, v_groups = lane_groups(k_rows), lane_groups(v_rows)
        return {r: (k_groups[(r - first) * BLOCK:(r - first + 2) * BLOCK, :],
                    v_groups[(r - first) * BLOCK:(r - first + 2) * BLOCK, :]) for r in which_blocks}

    low_ones = jnp.where(lane < HALF, 1.0, 0.0).astype(BF16)
    ones_lo = jnp.broadcast_to(low_ones, (2 * BLOCK, LANES))
    ones_hi = jnp.broadcast_to(1.0 - low_ones, (2 * BLOCK, LANES)).astype(BF16)

    def scores(r, kv, bands):
        k_band = bands[r][0]
        keys = jnp.concatenate([k_band[:, (2 * kv + parity) * LANES:(2 * kv + parity + 1) * LANES]
                                for parity in range(2)], axis=0)
        q_stack = jnp.concatenate([q_ref[rows_of(r), p * LANES:(p + 1) * LANES] for p in pairs_of(kv)], axis=0)
        return _dot_nt(q_stack, keys)

    def finish(r, kv, s_all, bias, bands):
        v_band = bands[r][1]
        probs, sink_terms = [], []
        for i, p in enumerate(pairs_of(kv)):
            pair_probs, pair_sinks = [], []
            for parity in range(2):
                head = 2 * p + parity
                s = s_all[i * BLOCK:(i + 1) * BLOCK, parity * 2 * BLOCK:(parity + 1) * 2 * BLOCK]
                s_prev = s[:, :BLOCK] + bias[A_HEADS + head]
                if r == 0:
                    s_prev = s_prev + no_prev
                s_cur = s[:, BLOCK:] + bias[head]
                sink = sink_ref[head] * LOG2E
                m = jnp.maximum(jnp.max(jnp.maximum(s_prev, s_cur), axis=-1, keepdims=True), sink)
                pair_probs += [jnp.exp2(s_prev - m), jnp.exp2(s_cur - m)]
                pair_sinks.append(jnp.exp2(sink - m))
            probs.append(jnp.concatenate(pair_probs, axis=1).astype(BF16))
            sink_terms.append(jnp.where(lane < HALF, pair_sinks[0], pair_sinks[1]))
        even, odd = (slice((2 * kv + parity) * LANES, (2 * kv + parity + 1) * LANES) for parity in range(2))
        v_ext = jnp.concatenate([jnp.concatenate([v_band[:, even], ones_lo], axis=1),
                                 jnp.concatenate([v_band[:, odd], ones_hi], axis=1)], axis=0)
        out = _dot(jnp.concatenate(probs, axis=0), v_ext)
        for i, p in enumerate(pairs_of(kv)):
            o_i = out[i * BLOCK:(i + 1) * BLOCK, :]
            o_ref[rows_of(r), p * LANES:(p + 1) * LANES] = (
                o_i[:, :LANES] * (1.0 / (o_i[:, LANES:] + sink_terms[i]))).astype(BF16)

    def attend(which_blocks):
        bias = [bias_ref[h] for h in range(2 * A_HEADS)]
        bands = all_bands(list(which_blocks))
        units = [(r, kv) for r in which_blocks for kv in range(A_KV_HEADS)]
        raw = {u: scores(*u, bands) for u in units[:A_LOOKAHEAD]}
        for i, unit in enumerate(units):
            if i + A_LOOKAHEAD < len(units):
                ahead = units[i + A_LOOKAHEAD]
                raw[ahead] = scores(*ahead, bands)
            finish(*unit, raw.pop(unit), bias, bands)

    @pl.when(shared_ref[0] == 0)
    def _():
        fill_shift_invariant()
        shared_ref[0] = 1

    attend(blocks)

    off = jnp.zeros((1, LANES), jnp.int32)
    for r in blocks:
        pos_cur, pos_prev = pkc_ref[r], pos_prev_of(r)
        start = jnp.min(pos_cur, axis=-1, keepdims=True)
        off_prev = jnp.where(pos_prev - start == lane - BLOCK, 0, 1)
        if r == 0:
            off_prev = off_prev * jnp.where(first_in_seq, 0, 1)
        off = off + jnp.where(pos_cur - start == lane, 0, 1) + off_prev

    @pl.when(jnp.sum(off) != 0)
    def _():
        for r in blocks:
            fill_general(pq_ref[rows_of(r), :], pkc_ref[r], pos_prev_of(r))
            attend([r])
        shared_ref[0] = 0


def _a_attn(q, k2, v2, pos_col, pos_row, table, sinks):
    steps = SEQ // A_ROWS
    blocks_per_step = A_ROWS // BLOCK
    blocks_per_seq = SEQ // BLOCK

    def cur(b, i):
        return (b * steps + i, 0)

    def prev(b, i):
        return (b * blocks_per_seq + jnp.maximum(i * blocks_per_step - 1, 0), 0)

    def cur3(b, i):
        return (b * steps + i, 0, 0)

    def prev3(b, i):
        return (b * blocks_per_seq + jnp.maximum(i * blocks_per_step - 1, 0), 0, 0)

    return pl.pallas_call(
        _a_attn_kernel,
        grid=(BATCH, steps),
        in_specs=[
            pl.BlockSpec(memory_space=pltpu.SMEM),
            pl.BlockSpec((A_ROWS, A_QW), cur),
            pl.BlockSpec((A_ROWS, LANES), cur),
            pl.BlockSpec((BLOCK, LANES), prev),
            pl.BlockSpec((A_ROWS, LANES), cur),
            pl.BlockSpec((BLOCK, LANES), prev),
            pl.BlockSpec((A_ROWS, 1), cur),
            pl.BlockSpec((blocks_per_step, 1, BLOCK), cur3),
            pl.BlockSpec((1, 1, BLOCK), prev3),
            _const_spec((A_HEADS, LANES)),
        ],
        out_specs=pl.BlockSpec((A_ROWS, A_QW), cur),
        out_shape=jax.ShapeDtypeStruct((TOKENS, A_QW), BF16),
        scratch_shapes=[pltpu.VMEM((2 * A_HEADS, BLOCK, BLOCK), F32), pltpu.SMEM((1,), jnp.int32)],
        compiler_params=pltpu.CompilerParams(dimension_semantics=("arbitrary", "arbitrary"),
                                             vmem_limit_bytes=VMEM_MEDIUM),
        name="a_attn",
    )(sinks, q, k2, k2, v2, v2, pos_col, pos_row, pos_row, table)


B_TM = 1024
B_CW = Q_LORA + KV_LORA + 2 * LANES
B_HW = B_HEADS * LANES
B_VW = B_HEADS * V_DIM
B_TQ = 512
B_TK = 512


def _b_proj_kernel(x_ref, pos_ref, g_ref, win_ref, qn_ref, kvn_ref, wq_ref, wqs_ref, wk_ref, wvt_ref,
                   qg_ref, qgs_ref, kg_ref, krg_ref, krgs_ref, freq_ref, q_ref, k_ref, vt_ref):
    h = _rms_rows(x_ref[...], g_ref[...]).astype(BF16)
    c = _dot(h, win_ref[...])
    cq = _rms_rows(c[:, :Q_LORA], qn_ref[...]).astype(BF16)
    ckv_f32 = _rms_rows(c[:, Q_LORA:Q_LORA + KV_LORA], kvn_ref[...])
    ckv = ckv_f32.astype(BF16)
    kr = c[:, Q_LORA + KV_LORA:Q_LORA + KV_LORA + LANES]
    kr_partner = c[:, Q_LORA + KV_LORA + LANES:]
    q = _dot(cq, wq_ref[...])
    q_partner = _dot(cq, wqs_ref[...])
    kn = _dot(ckv, wk_ref[...])
    vt_ref[...] = _dot(wvt_ref[...], ckv_f32.T.astype(BF16)).astype(BF16)

    lane = lax.broadcasted_iota(jnp.int32, (1, LANES), 1)
    ang = pos_ref[...].astype(F32) * freq_ref[...]
    cos = jnp.cos(ang)
    sin = jnp.sin(ang)
    half = QK_ROPE // 2
    sin_signed = jnp.where(lane < QK_NOPE + half, -sin, sin)

    root_d = math.sqrt(B_DQK)
    q_const = root_d * B_DQK ** -0.5 * LOG2E
    k_rope = (kr * krg_ref[...] * cos + kr_partner * krgs_ref[...] * sin_signed) * root_d
    k_gain = kg_ref[...] * root_d
    ss_rope = jnp.sum(kr * kr, axis=-1, keepdims=True) + B_DQK * EPS
    q_cos = qg_ref[...] * cos * q_const
    q_sin = qgs_ref[...] * sin_signed * q_const
    for hd in range(B_HEADS):
        g = slice(hd * LANES, (hd + 1) * LANES)
        qh = q[:, g]
        r = lax.rsqrt(jnp.sum(qh * qh, axis=-1, keepdims=True) + B_DQK * EPS)
        q_ref[:, g] = ((qh * q_cos + q_partner[:, g] * q_sin) * r).astype(BF16)
        kh = kn[:, g]
        rk = lax.rsqrt(jnp.sum(kh * kh, axis=-1, keepdims=True) + ss_rope)
        k_ref[:, g] = ((kh * k_gain + k_rope) * rk).astype(BF16)


def _b_proj(x, pos_col, gain, w_in, qn, kvn, wq, wqs, wk, wvt, qg, qgs, kg, krg, krgs, freq):
    tm = B_TM
    tiles_per_seq = SEQ // tm
    lane_vec = _const_spec((1, LANES))
    out_spec = pl.BlockSpec((tm, B_HW), lambda i: (i, 0))
    out_sds = jax.ShapeDtypeStruct((TOKENS, B_HW), BF16)
    vt_spec = pl.BlockSpec((None, B_VW, tm), lambda i: (i // tiles_per_seq, 0, i % tiles_per_seq))
    return pl.pallas_call(
        _b_proj_kernel,
        grid=(TOKENS // tm,),
        in_specs=[pl.BlockSpec((tm, D_MODEL), lambda i: (i, 0)), pl.BlockSpec((tm, 1), lambda i: (i, 0)),
                  _const_spec((1, D_MODEL)), _const_spec((D_MODEL, B_CW)), _const_spec((1, Q_LORA)),
                  _const_spec((1, KV_LORA)), _const_spec((Q_LORA, B_HW)), _const_spec((Q_LORA, B_HW)),
                  _const_spec((KV_LORA, B_HW)), _const_spec((B_VW, KV_LORA)),
                  lane_vec, lane_vec, lane_vec, lane_vec, lane_vec, lane_vec],
        out_specs=[out_spec, out_spec, vt_spec],
        out_shape=[out_sds, out_sds, jax.ShapeDtypeStruct((BATCH, B_VW, SEQ), BF16)],
        compiler_params=pltpu.CompilerParams(vmem_limit_bytes=VMEM_LARGE),
        name="b_proj",
    )(x, pos_col, gain, w_in, qn, kvn, wq, wqs, wk, wvt, qg, qgs, kg, krg, krgs, freq)


B_HEADS_PER_STEP = 4
B_LOOKAHEAD = 3


def _b_attn_kernel(q_ref, k_ref, vt_ref, o_ref):
    heads = range(B_HEADS_PER_STEP)
    groups = [slice(hd * LANES, (hd + 1) * LANES) for hd in heads]
    n_query_tiles = SEQ // B_TQ
    half = B_TQ // 2

    def chain(qt, hd):
        pieces = [(qt, hd, j * B_TK, B_TK, 0, B_TQ) for j in range(qt * B_TQ // B_TK)]
        return pieces + [(qt, hd, qt * B_TQ, half, 0, B_TQ), (qt, hd, qt * B_TQ + half, half, half, half)]

    def scores(qt, hd, key_lo, n_keys, q_lo, n_q):
        queries = slice(qt * B_TQ + q_lo, qt * B_TQ + q_lo + n_q)
        return _dot_nt(k_ref[key_lo:key_lo + n_keys, groups[hd]], q_ref[queries, groups[hd]])

    def accumulate(unit, s, carry):
        qt, hd, key_lo, n_keys, q_lo, n_q = unit
        m_all, acc_all = carry
        m, acc = m_all[:, q_lo:q_lo + n_q], acc_all[:, q_lo:q_lo + n_q]
        first_query = qt * B_TQ + q_lo
        if key_lo + n_keys - 1 > first_query:
            key = lax.broadcasted_iota(jnp.int32, (n_keys, n_q), 0)
            query = lax.broadcasted_iota(jnp.int32, (n_keys, n_q), 1)
            s = jnp.where(key + (key_lo - first_query) <= query, s, NEG)
        m_new = jnp.maximum(m, jnp.max(s, axis=0, keepdims=True))
        alpha = jnp.exp2(m - m_new)
        e = jnp.exp2(s - m_new).astype(BF16)
        vt_ext = jnp.concatenate([vt_ref[hd * V_DIM:(hd + 1) * V_DIM, key_lo:key_lo + n_keys],
                                  jnp.ones((V_DIM, n_keys), BF16)], axis=0)
        acc_new = alpha * acc + _dot(vt_ext, e)
        if q_lo:
            m_new = jnp.concatenate([m_all[:, :q_lo], m_new], axis=1)
            acc_new = jnp.concatenate([acc_all[:, :q_lo], acc_new], axis=1)
        return m_new, acc_new

    chains = [chain(qt, hd) for qt in reversed(range(n_query_tiles)) for hd in heads]
    units = [c[i] for i in range(len(chains[0])) for c in chains if i < len(c)]
    last = {c[-1] for c in chains}
    init = (jnp.full((1, B_TQ), NEG, F32), jnp.zeros((2 * V_DIM, B_TQ), F32))
    carry = {(qt, hd): init for qt in range(n_query_tiles) for hd in heads}
    raw = {u: scores(*u) for u in units[:B_LOOKAHEAD]}
    for i, unit in enumerate(units):
        qt, hd = unit[0], unit[1]
        if i + B_LOOKAHEAD < len(units):
            ahead = units[i + B_LOOKAHEAD]
            raw[ahead] = scores(*ahead)
        carry[(qt, hd)] = accumulate(unit, raw.pop(unit), carry[(qt, hd)])
        if hd % 2 == 1 and unit in last:
            outs = [acc[:V_DIM, :] * (1.0 / acc[V_DIM:V_DIM + 1, :])
                    for _, acc in (carry.pop((qt, hd - 1)), carry.pop((qt, hd)))]
            o_ref[qt * B_TQ:(qt + 1) * B_TQ, (hd - 1) * V_DIM:(hd + 1) * V_DIM] = (
                jnp.concatenate(outs, axis=0).T.astype(BF16))


def _b_attn(q, k, vt):
    n = B_HEADS_PER_STEP
    return pl.pallas_call(
        _b_attn_kernel,
        grid=(BATCH, B_HEADS // n),
        in_specs=[pl.BlockSpec((SEQ, n * LANES), lambda b, p: (b, p)),
                  pl.BlockSpec((SEQ, n * LANES), lambda b, p: (b, p)),
                  pl.BlockSpec((None, n * V_DIM, SEQ), lambda b, p: (b, p, 0))],
        out_specs=pl.BlockSpec((SEQ, n * V_DIM), lambda b, p: (b, p)),
        out_shape=jax.ShapeDtypeStruct((TOKENS, B_VW), BF16),
        compiler_params=pltpu.CompilerParams(vmem_limit_bytes=VMEM_MEDIUM),
        name="b_attn",
    )(q, k, vt)


def _prep_a(w_in, q_gain, k_gain):
    qg = jnp.concatenate([q_gain, q_gain])[None, :]
    kg = jnp.concatenate([k_gain, k_gain])[None, :]
    return w_in.astype(BF16), qg, kg


def _head_groups(w, per_head, src_lo, src_hi, dst_lo):
    rows = w.shape[0]
    w3 = w.reshape(rows, B_HEADS, per_head)[:, :, src_lo:src_hi]
    out = jnp.zeros((rows, B_HEADS, LANES), w.dtype)
    out = out.at[:, :, dst_lo:dst_lo + (src_hi - src_lo)].set(w3)
    return out


def _prep_b(w_in, w_uq, w_ukv, q_gain, k_gain):
    half = QK_ROPE // 2
    t1 = slice(QK_NOPE, QK_NOPE + half)
    t2 = slice(QK_NOPE + half, B_DQK)
    rows = w_in.shape[0]
    rope_cols = w_in[:, Q_LORA + KV_LORA:]
    zeros = lambda n: jnp.zeros((rows, n), w_in.dtype)
    rope_group = jnp.concatenate([zeros(QK_NOPE), rope_cols, zeros(LANES - B_DQK)], axis=1)
    partner_group = jnp.concatenate(
        [zeros(QK_NOPE), rope_cols[:, half:], rope_cols[:, :half], zeros(LANES - B_DQK)], axis=1)
    win = jnp.concatenate([w_in[:, :Q_LORA + KV_LORA], rope_group, partner_group], axis=1).astype(BF16)

    wq = _head_groups(w_uq, B_DQK, 0, B_DQK, 0)
    wqs = (_head_groups(w_uq, B_DQK, t2.start, t2.stop, t1.start)
           + _head_groups(w_uq, B_DQK, t1.start, t1.stop, t2.start))
    wk = _head_groups(w_ukv, QK_NOPE + V_DIM, 0, QK_NOPE, 0)
    wvt = w_ukv.reshape(KV_LORA, B_HEADS, QK_NOPE + V_DIM)[:, :, QK_NOPE:].reshape(KV_LORA, B_VW).T.astype(BF16)
    flat = lambda a: a.reshape(a.shape[0], B_HW).astype(BF16)

    def lane_vec(pieces):
        out = jnp.zeros((LANES,), F32)
        for lo, vals in pieces:
            out = out.at[lo:lo + vals.shape[0]].set(vals)
        return out[None, :]

    qg = lane_vec([(0, q_gain)])
    qgs = lane_vec([(t1.start, q_gain[t2]), (t2.start, q_gain[t1])])
    kg = lane_vec([(0, k_gain[:QK_NOPE])])
    krg = lane_vec([(QK_NOPE, k_gain[QK_NOPE:])])
    krgs = lane_vec([(t1.start, k_gain[t2]), (t2.start, k_gain[t1])])
    return win, flat(wq), flat(wqs), flat(wk), wvt, qg, qgs, kg, krg, krgs


def _rope_freq():
    inv = ROPE_BASE ** (-np.arange(0, QK_ROPE, 2, dtype=np.float32) / QK_ROPE)
    out = np.zeros((1, LANES), np.float32)
    half = QK_ROPE // 2
    out[0, QK_NOPE:QK_NOPE + half] = inv
    out[0, QK_NOPE + half:B_DQK] = inv
    return jnp.asarray(out)


def kernel(x, positions, rel_bias, ffn_norm1, ffn1_wg, ffn1_wu, ffn1_wd, mix_norm, ffn_norm2, ffn2_wg,
           ffn2_wu, ffn2_wd, a_w_in, a_q_gain, a_k_gain, a_sinks, a_w_out, b_w_in, b_q_norm, b_kv_norm,
           b_w_uq, b_w_ukv, b_q_gain, b_k_gain, b_w_out):
    assert x.shape == (BATCH, SEQ, D_MODEL) and positions.shape == (BATCH, SEQ)
    xt = x.reshape(TOKENS, D_MODEL)
    pos_col = positions.reshape(TOKENS, 1)
    pos_row = positions.reshape(TOKENS // BLOCK, 1, BLOCK)
    table = jnp.zeros((A_HEADS, LANES), F32).at[:, :NUM_BUCKETS].set(rel_bias.T)
    bf = lambda w: w.astype(BF16)
    row = lambda v: v[None, :]
    gain1, gain2 = ffn_norm1[:, None, :], ffn_norm2[:, None, :]
    weights = _first_ffn_weights(0, ffn1_wg, ffn1_wu, ffn1_wd)

    for i in range(DEPTH):
        xt, weights = _ffn(xt, gain1, i, *weights, cast_ahead=(i, ffn2_wg, ffn2_wu, ffn2_wd))
        j = i // N_MIXERS
        if i % N_MIXERS == 0:
            w, qg, kg = _prep_a(a_w_in[j], a_q_gain[j], a_k_gain[j])
            q, k2, v2 = _a_qkv(xt, row(mix_norm[i]), w, qg, kg)
            attn = _a_attn(q, k2, v2, pos_col, pos_row, table, a_sinks[j])
            w_out = bf(a_w_out[j])
        else:
            prep = _prep_b(b_w_in[j], b_w_uq[j], b_w_ukv[j], b_q_gain[j], b_k_gain[j])
            win, wq, wqs, wk, wvt, qg, qgs, kg, krg, krgs = prep
            q, k, vt = _b_proj(xt, pos_col, row(mix_norm[i]), win, row(b_q_norm[j]), row(b_kv_norm[j]),
                              wq, wqs, wk, wvt, qg, qgs, kg, krg, krgs, _rope_freq())
            attn = _b_attn(q, k, vt)
            w_out = bf(b_w_out[j])
        ahead = (i + 1, ffn1_wg, ffn1_wu, ffn1_wd) if i + 1 < DEPTH else None
        xt, weights = _ffn(xt, gain2, i, *weights, attn=attn, w_out=w_out, cast_ahead=ahead)
    return xt.reshape(BATCH, SEQ, D_MODEL)
```

```python
import functools
import math

import numpy as np
import jax
import jax.numpy as jnp
from jax import lax
from jax.experimental import pallas as pl
from jax.experimental.pallas import tpu as pltpu

D_MODEL = 1024
BATCH = 8
SEQ = 2048
DEPTH = 2
N_MIXERS = 2
A_HEADS = 16
A_KV_HEADS = 2
A_HEAD_DIM = 64
WINDOW = 128
BLOCK = 128
NUM_BUCKETS = 32
MAX_DISTANCE = 128
B_HEADS = 16
Q_LORA = 256
KV_LORA = 128
QK_NOPE = 64
QK_ROPE = 32
V_DIM = 64
ROPE_BASE = 10000.0
D_FF = 2816
EPS = 1e-6
NEG = -1e30

TOKENS = BATCH * SEQ
LANES = 128
HALF = LANES // 2
B_DQK = QK_NOPE + QK_ROPE
MIB = 1024 * 1024
VMEM_LIMIT = 60 * MIB
VMEM_SMALL = 16 * MIB
VMEM_MEDIUM = 24 * MIB
VMEM_LARGE = 56 * MIB

LOG2E = math.log2(math.e)

F32 = jnp.float32
BF16 = jnp.bfloat16


def _rms_rows(x, gain):
    return x * lax.rsqrt(jnp.mean(x * x, axis=-1, keepdims=True) + EPS) * gain


def _dot(a, b):
    return jnp.dot(a, b, preferred_element_type=F32)


def _dot_nt(a, b):
    return lax.dot_general(a, b, (((1,), (1,)), ((), ())), preferred_element_type=F32)


def _const_spec(shape):
    nd = len(shape)
    return pl.BlockSpec(shape, lambda *_: (0,) * nd, pipeline_mode=pl.Buffered(1))


FFN_TM = 1024
MXU_WIDTH = 256
FFN_CHUNKS = ((0, 4 * MXU_WIDTH), (4 * MXU_WIDTH, 8 * MXU_WIDTH), (8 * MXU_WIDTH, D_FF))


def _ffn_body(x, g_ref, wgu_ref, wd_ref, o_ref):
    h = _rms_rows(x, g_ref[...]).astype(BF16)
    down = None
    for lo, hi in FFN_CHUNKS:
        gate_up = _dot(h, wgu_ref[:, 2 * lo:2 * hi])
        acts = []
        for c in range((hi - lo) // MXU_WIDTH):
            gate = gate_up[:, 2 * c * MXU_WIDTH:(2 * c + 1) * MXU_WIDTH]
            up = gate_up[:, (2 * c + 1) * MXU_WIDTH:(2 * c + 2) * MXU_WIDTH]
            acts.append((gate * jax.nn.sigmoid(gate) * up).astype(BF16))
        part = _dot(jnp.concatenate(acts, axis=1), wd_ref[lo:hi, :])
        down = part if down is None else down + part
    o_ref[...] = x + 0.5 * down


def _ffn_kernel(*refs, project, cast_ahead):
    refs = list(refs)
    x = refs.pop(0)[...]
    if project:
        a_ref, wo_ref = refs.pop(0), refs.pop(0)
        x = x + _dot(a_ref[...], wo_ref[...])
    g_ref, wgu_ref, wd_ref = refs[:3]
    refs = refs[3:]
    if cast_ahead:
        _cast_ffn_weights(*refs[:3], *refs[4:6])
        refs = refs[3:]
    _ffn_body(x, g_ref, wgu_ref, wd_ref, refs[0])


def _cast_ffn_weights(wg_src, wu_src, wd_src, wgu_dst, wd_dst):
    for c in range(D_FF // MXU_WIDTH):
        cols = slice(c * MXU_WIDTH, (c + 1) * MXU_WIDTH)
        wgu_dst[:, 2 * c * MXU_WIDTH:(2 * c + 1) * MXU_WIDTH] = wg_src[:, cols].astype(BF16)
        wgu_dst[:, (2 * c + 1) * MXU_WIDTH:(2 * c + 2) * MXU_WIDTH] = wu_src[:, cols].astype(BF16)
    wd_dst[...] = wd_src[...].astype(BF16)


def _cast_specs(layer, stacks, steps):
    in_specs = [pl.BlockSpec((None, w.shape[1] // steps, w.shape[2]), lambda i: (layer, i, 0)) for w in stacks]
    out_specs = [pl.BlockSpec((D_MODEL // steps, 2 * D_FF), lambda i: (i, 0)),
                 pl.BlockSpec((D_FF // steps, D_MODEL), lambda i: (i, 0))]
    out_shape = [jax.ShapeDtypeStruct((D_MODEL, 2 * D_FF), BF16), jax.ShapeDtypeStruct((D_FF, D_MODEL), BF16)]
    return in_specs, out_specs, out_shape


def _first_ffn_weights(layer, wg, wu, wd):
    steps = 8
    in_specs, out_specs, out_shape = _cast_specs(layer, (wg, wu, wd), steps)
    return pl.pallas_call(_cast_ffn_weights, grid=(steps,), in_specs=in_specs, out_specs=out_specs,
                          out_shape=out_shape, name="cast_ffn_weights")(wg, wu, wd)


def _ffn(x, gain, layer, wgu, wd, attn=None, w_out=None, cast_ahead=None):
    tm = FFN_TM
    steps = TOKENS // tm
    row_spec = pl.BlockSpec((tm, D_MODEL), lambda i: (i, 0))
    in_specs, args = [row_spec], [x]
    if attn is not None:
        in_specs += [row_spec, _const_spec((D_MODEL, D_MODEL))]
        args += [attn, w_out]
    in_specs += [pl.BlockSpec((None, 1, D_MODEL), lambda i: (layer, 0, 0), pipeline_mode=pl.Buffered(1)),
                 _const_spec((D_MODEL, 2 * D_FF)), _const_spec((D_FF, D_MODEL))]
    args += [gain, wgu, wd]
    out_specs = [row_spec]
    out_shape = [jax.ShapeDtypeStruct((TOKENS, D_MODEL), F32)]
    if cast_ahead is not None:
        next_layer, *stacks = cast_ahead
        cast_in, cast_out, cast_shape = _cast_specs(next_layer, stacks, steps)
        in_specs += cast_in
        args += stacks
        out_specs += cast_out
        out_shape += cast_shape
    outs = pl.pallas_call(
        functools.partial(_ffn_kernel, project=attn is not None, cast_ahead=cast_ahead is not None),
        grid=(steps,),
        in_specs=in_specs,
        out_specs=out_specs,
        out_shape=out_shape,
        compiler_params=pltpu.CompilerParams(vmem_limit_bytes=VMEM_LIMIT),
        name="ffn" if attn is None else "proj_ffn",
    )(*args)
    return outs[0], tuple(outs[1:])


A_TM = 512
A_QW = A_HEADS * A_HEAD_DIM
A_KW = 4 * LANES
A_PAIRS = A_HEADS // 2
A_ROWS = 512
A_LOOKAHEAD = 1


def _a_qkv_kernel(x_ref, g_ref, w_ref, qg_ref, kg_ref, q_ref, k_ref, v_ref):
    h = _rms_rows(x_ref[...], g_ref[...]).astype(BF16)
    qkv = _dot(h, w_ref[...])
    lane = lax.broadcasted_iota(jnp.int32, (1, LANES), 1)
    low = lane < HALF
    root_d = math.sqrt(A_HEAD_DIM)
    q_table = qg_ref[...] * (root_d * A_HEAD_DIM ** -0.5 * LOG2E)
    k_table = kg_ref[...] * root_d

    def pair_norm(pair, table):
        sq = pair * pair
        ss_lo = jnp.sum(jnp.where(low, sq, 0.0), axis=-1, keepdims=True)
        ss_hi = jnp.sum(jnp.where(low, 0.0, sq), axis=-1, keepdims=True)
        r = jnp.where(low, lax.rsqrt(ss_lo + A_HEAD_DIM * EPS), lax.rsqrt(ss_hi + A_HEAD_DIM * EPS))
        return (pair * table * r).astype(BF16)

    for p in range(A_PAIRS):
        q_ref[:, p * LANES:(p + 1) * LANES] = pair_norm(qkv[:, p * LANES:(p + 1) * LANES], q_table)
    k_ref[...] = pair_norm(qkv[:, A_QW:A_QW + LANES], k_table)
    v_ref[...] = qkv[:, A_QW + LANES:].astype(BF16)


def _a_qkv(x, gain, w, qg, kg):
    tm = A_TM
    width = A_QW + 2 * LANES
    return pl.pallas_call(
        _a_qkv_kernel,
        grid=(TOKENS // tm,),
        in_specs=[pl.BlockSpec((tm, D_MODEL), lambda i: (i, 0)), _const_spec((1, D_MODEL)),
                  _const_spec((D_MODEL, width)), _const_spec((1, LANES)), _const_spec((1, LANES))],
        out_specs=[pl.BlockSpec((tm, A_QW), lambda i: (i, 0)), pl.BlockSpec((tm, LANES), lambda i: (i, 0)),
                   pl.BlockSpec((tm, LANES), lambda i: (i, 0))],
        out_shape=[jax.ShapeDtypeStruct((TOKENS, A_QW), BF16), jax.ShapeDtypeStruct((TOKENS, LANES), BF16),
                   jax.ShapeDtypeStruct((TOKENS, LANES), BF16)],
        compiler_params=pltpu.CompilerParams(vmem_limit_bytes=VMEM_SMALL),
        name="a_qkv",
    )(x, gain, w, qg, kg)


def _t5_bucket(dist):
    n = jnp.maximum(dist, 0)
    max_exact = NUM_BUCKETS // 2
    large = max_exact + (jnp.log(jnp.maximum(n, 1).astype(F32) / max_exact)
                         / math.log(MAX_DISTANCE / max_exact)
                         * (NUM_BUCKETS - max_exact)).astype(jnp.int32)
    large = jnp.minimum(large, NUM_BUCKETS - 1)
    return jnp.where(n < max_exact, n, large)


SUBLANES = 8


def _a_attn_kernel(sink_ref, q_ref, kc_ref, kp_ref, vc_ref, vp_ref, pq_ref, pkc_ref, pkp_ref, tbl_ref, o_ref,
                   bias_ref, shared_ref):
    row = lax.broadcasted_iota(jnp.int32, (BLOCK, BLOCK), 0)
    col = lax.broadcasted_iota(jnp.int32, (BLOCK, BLOCK), 1)
    cur_ok = col <= row
    first_in_seq = pl.program_id(1) == 0
    no_prev = jnp.where(first_in_seq, NEG, 0.0)
    tables = [jnp.broadcast_to(tbl_ref[h:h + 1, :] * LOG2E, (SUBLANES, LANES)) for h in range(A_HEADS)]
    group_heads = A_HEADS // A_KV_HEADS
    lane = lax.broadcasted_iota(jnp.int32, (1, LANES), 1)

    @pl.when(jnp.logical_and(pl.program_id(0) == 0, first_in_seq))
    def _():
        shared_ref[0] = 0

    def fill_shift_invariant():
        back = jnp.broadcast_to((BLOCK - lane) & (BLOCK - 1), (SUBLANES, LANES))
        idx = _t5_bucket(back)
        for h in range(A_HEADS):
            base = jnp.take_along_axis(tables[h], idx, axis=1)
            base = jnp.broadcast_to(base[0:1, :], (BLOCK, BLOCK))
            tile = pltpu.roll(base, 0, 1, stride=1, stride_axis=0)
            bias_ref[h] = jnp.where(cur_ok, tile, NEG)
            bias_ref[A_HEADS + h] = jnp.where(cur_ok, NEG, tile)

    def fill_general(pos_q, pos_cur, pos_prev):
        bucket = jnp.where(cur_ok, _t5_bucket(pos_q - pos_cur), _t5_bucket(pos_q - pos_prev))
        for c in range(BLOCK // SUBLANES):
            chunk = slice(c * SUBLANES, (c + 1) * SUBLANES)
            for h in range(A_HEADS):
                piece = jnp.take_along_axis(tables[h], bucket[chunk, :], axis=1)
                bias_ref[h, chunk, :] = jnp.where(cur_ok[chunk, :], piece, NEG)
                bias_ref[A_HEADS + h, chunk, :] = jnp.where(cur_ok[chunk, :], NEG, piece)

    blocks = range(A_ROWS // BLOCK)

    def rows_of(r):
        return slice(r * BLOCK, (r + 1) * BLOCK)

    def pairs_of(kv):
        return range(kv * group_heads // 2, (kv + 1) * group_heads // 2)

    def lane_groups(pair):
        low_half = lane < HALF
        swapped = pltpu.roll(pair, HALF, 1)
        zero = jnp.zeros_like(pair)
        return jnp.concatenate([jnp.where(low_half, pair, zero), jnp.where(low_half, zero, swapped),
                                jnp.where(low_half, swapped, zero), jnp.where(low_half, zero, pair)], axis=1)

    def pos_prev_of(r):
        return pkp_ref[0] if r == 0 else pkc_ref[r - 1]

    def all_bands(which_blocks):
        first, stop = which_blocks[0], which_blocks[-1] + 1
        if first == 0:
            k_rows = jnp.concatenate([kp_ref[...], kc_ref[:stop * BLOCK, :]], axis=0)
            v_rows = jnp.concatenate([vp_ref[...], vc_ref[:stop * BLOCK, :]], axis=0)
        else:
            k_rows = kc_ref[(first - 1) * BLOCK:stop * BLOCK, :]
            v_rows = vc_ref[(first - 1) * BLOCK:stop * BLOCK, :]
        k_groups, v_groups = lane_groups(k_rows), lane_groups(v_rows)
        return {r: (k_groups[(r - first) * BLOCK:(r - first + 2) * BLOCK, :],
                    v_groups[(r - first) * BLOCK:(r - first + 2) * BLOCK, :]) for r in which_blocks}

    low_ones = jnp.where(lane < HALF, 1.0, 0.0).astype(BF16)
    ones_lo = jnp.broadcast_to(low_ones, (2 * BLOCK, LANES))
    ones_hi = jnp.broadcast_to(1.0 - low_ones, (2 * BLOCK, LANES)).astype(BF16)

    def scores(r, kv, bands):
        k_band = bands[r][0]
        keys = jnp.concatenate([k_band[:, (2 * kv + parity) * LANES:(2 * kv + parity + 1) * LANES]
                                for parity in range(2)], axis=0)
        q_stack = jnp.concatenate([q_ref[rows_of(r), p * LANES:(p + 1) * LANES] for p in pairs_of(kv)], axis=0)
        return _dot_nt(q_stack, keys)

    def finish(r, kv, s_all, bias, bands):
        v_band = bands[r][1]
        probs, sink_terms = [], []
        for i, p in enumerate(pairs_of(kv)):
            pair_probs, pair_sinks = [], []
            for parity in range(2):
                head = 2 * p + parity
                s = s_all[i * BLOCK:(i + 1) * BLOCK, parity * 2 * BLOCK:(parity + 1) * 2 * BLOCK]
                s_prev = s[:, :BLOCK] + bias[A_HEADS + head]
                if r == 0:
                    s_prev = s_prev + no_prev
                s_cur = s[:, BLOCK:] + bias[head]
                sink = sink_ref[head] * LOG2E
                m = jnp.maximum(jnp.max(jnp.maximum(s_prev, s_cur), axis=-1, keepdims=True), sink)
                pair_probs += [jnp.exp2(s_prev - m), jnp.exp2(s_cur - m)]
                pair_sinks.append(jnp.exp2(sink - m))
            probs.append(jnp.concatenate(pair_probs, axis=1).astype(BF16))
            sink_terms.append(jnp.where(lane < HALF, pair_sinks[0], pair_sinks[1]))
        even, odd = (slice((2 * kv + parity) * LANES, (2 * kv + parity + 1) * LANES) for parity in range(2))
        v_ext = jnp.concatenate([jnp.concatenate([v_band[:, even], ones_lo], axis=1),
                                 jnp.concatenate([v_band[:, odd], ones_hi], axis=1)], axis=0)
        out = _dot(jnp.concatenate(probs, axis=0), v_ext)
        for i, p in enumerate(pairs_of(kv)):
            o_i = out[i * BLOCK:(i + 1) * BLOCK, :]
            o_ref[rows_of(r), p * LANES:(p + 1) * LANES] = (
                o_i[:, :LANES] * (1.0 / (o_i[:, LANES:] + sink_terms[i]))).astype(BF16)

    def attend(which_blocks):
        bias = [bias_ref[h] for h in range(2 * A_HEADS)]
        bands = all_bands(list(which_blocks))
        units = [(r, kv) for r in which_blocks for kv in range(A_KV_HEADS)]
        raw = {u: scores(*u, bands) for u in units[:A_LOOKAHEAD]}
        for i, unit in enumerate(units):
            if i + A_LOOKAHEAD < len(units):
                ahead = units[i + A_LOOKAHEAD]
                raw[ahead] = scores(*ahead, bands)
            finish(*unit, raw.pop(unit), bias, bands)

    @pl.when(shared_ref[0] == 0)
    def _():
        fill_shift_invariant()
        shared_ref[0] = 1

    attend(blocks)

    off = jnp.zeros((1, LANES), jnp.int32)
    for r in blocks:
        pos_cur, pos_prev = pkc_ref[r], pos_prev_of(r)
        start = jnp.min(pos_cur, axis=-1, keepdims=True)
        off_prev = jnp.where(pos_prev - start == lane - BLOCK, 0, 1)
        if r == 0:
            off_prev = off_prev * jnp.where(first_in_seq, 0, 1)
        off = off + jnp.where(pos_cur - start == lane, 0, 1) + off_prev

    @pl.when(jnp.sum(off) != 0)
    def _():
        for r in blocks:
            fill_general(pq_ref[rows_of(r), :], pkc_ref[r], pos_prev_of(r))
            attend([r])
        shared_ref[0] = 0


def _a_attn(q, k2, v2, pos_col, pos_row, table, sinks):
    steps = SEQ // A_ROWS
    blocks_per_step = A_ROWS // BLOCK
    blocks_per_seq = SEQ // BLOCK

    def cur(b, i):
        return (b * steps + i, 0)

    def prev(b, i):
        return (b * blocks_per_seq + jnp.maximum(i * blocks_per_step - 1, 0), 0)

    def cur3(b, i):
        return (b * steps + i, 0, 0)

    def prev3(b, i):
        return (b * blocks_per_seq + jnp.maximum(i * blocks_per_step - 1, 0), 0, 0)

    return pl.pallas_call(
        _a_attn_kernel,
        grid=(BATCH, steps),
        in_specs=[
            pl.BlockSpec(memory_space=pltpu.SMEM),
            pl.BlockSpec((A_ROWS, A_QW), cur),
            pl.BlockSpec((A_ROWS, LANES), cur),
            pl.BlockSpec((BLOCK, LANES), prev),
            pl.BlockSpec((A_ROWS, LANES), cur),
            pl.BlockSpec((BLOCK, LANES), prev),
            pl.BlockSpec((A_ROWS, 1), cur),
            pl.BlockSpec((blocks_per_step, 1, BLOCK), cur3),
            pl.BlockSpec((1, 1, BLOCK), prev3),
            _const_spec((A_HEADS, LANES)),
        ],
        out_specs=pl.BlockSpec((A_ROWS, A_QW), cur),
        out_shape=jax.ShapeDtypeStruct((TOKENS, A_QW), BF16),
        scratch_shapes=[pltpu.VMEM((2 * A_HEADS, BLOCK, BLOCK), F32), pltpu.SMEM((1,), jnp.int32)],
        compiler_params=pltpu.CompilerParams(dimension_semantics=("arbitrary", "arbitrary"),
                                             vmem_limit_bytes=VMEM_MEDIUM),
        name="a_attn",
    )(sinks, q, k2, k2, v2, v2, pos_col, pos_row, pos_row, table)


B_TM = 1024
B_CW = Q_LORA + KV_LORA + 2 * LANES
B_HW = B_HEADS * LANES
B_VW = B_HEADS * V_DIM
B_TQ = 512
B_TK = 512


def _b_proj_kernel(x_ref, pos_ref, g_ref, win_ref, qn_ref, kvn_ref, wq_ref, wqs_ref, wk_ref, wvt_ref,
                   qg_ref, qgs_ref, kg_ref, krg_ref, krgs_ref, freq_ref, q_ref, k_ref, vt_ref):
    h = _rms_rows(x_ref[...], g_ref[...]).astype(BF16)
    c = _dot(h, win_ref[...])
    cq = _rms_rows(c[:, :Q_LORA], qn_ref[...]).astype(BF16)
    ckv_f32 = _rms_rows(c[:, Q_LORA:Q_LORA + KV_LORA], kvn_ref[...])
    ckv = ckv_f32.astype(BF16)
    kr = c[:, Q_LORA + KV_LORA:Q_LORA + KV_LORA + LANES]
    kr_partner = c[:, Q_LORA + KV_LORA + LANES:]
    q = _dot(cq, wq_ref[...])
    q_partner = _dot(cq, wqs_ref[...])
    kn = _dot(ckv, wk_ref[...])
    vt_ref[...] = _dot(wvt_ref[...], ckv_f32.T.astype(BF16)).astype(BF16)

    lane = lax.broadcasted_iota(jnp.int32, (1, LANES), 1)
    ang = pos_ref[...].astype(F32) * freq_ref[...]
    cos = jnp.cos(ang)
    sin = jnp.sin(ang)
    half = QK_ROPE // 2
    sin_signed = jnp.where(lane < QK_NOPE + half, -sin, sin)

    root_d = math.sqrt(B_DQK)
    q_const = root_d * B_DQK ** -0.5 * LOG2E
    k_rope = (kr * krg_ref[...] * cos + kr_partner * krgs_ref[...] * sin_signed) * root_d
    k_gain = kg_ref[...] * root_d
    ss_rope = jnp.sum(kr * kr, axis=-1, keepdims=True) + B_DQK * EPS
    q_cos = qg_ref[...] * cos * q_const
    q_sin = qgs_ref[...] * sin_signed * q_const
    for hd in range(B_HEADS):
        g = slice(hd * LANES, (hd + 1) * LANES)
        qh = q[:, g]
        r = lax.rsqrt(jnp.sum(qh * qh, axis=-1, keepdims=True) + B_DQK * EPS)
        q_ref[:, g] = ((qh * q_cos + q_partner[:, g] * q_sin) * r).astype(BF16)
        kh = kn[:, g]
        rk = lax.rsqrt(jnp.sum(kh * kh, axis=-1, keepdims=True) + ss_rope)
        k_ref[:, g] = ((kh * k_gain + k_rope) * rk).astype(BF16)


def _b_proj(x, pos_col, gain, w_in, qn, kvn, wq, wqs, wk, wvt, qg, qgs, kg, krg, krgs, freq):
    tm = B_TM
    tiles_per_seq = SEQ // tm
    lane_vec = _const_spec((1, LANES))
    out_spec = pl.BlockSpec((tm, B_HW), lambda i: (i, 0))
    out_sds = jax.ShapeDtypeStruct((TOKENS, B_HW), BF16)
    vt_spec = pl.BlockSpec((None, B_VW, tm), lambda i: (i // tiles_per_seq, 0, i % tiles_per_seq))
    return pl.pallas_call(
        _b_proj_kernel,
        grid=(TOKENS // tm,),
        in_specs=[pl.BlockSpec((tm, D_MODEL), lambda i: (i, 0)), pl.BlockSpec((tm, 1), lambda i: (i, 0)),
                  _const_spec((1, D_MODEL)), _const_spec((D_MODEL, B_CW)), _const_spec((1, Q_LORA)),
                  _const_spec((1, KV_LORA)), _const_spec((Q_LORA, B_HW)), _const_spec((Q_LORA, B_HW)),
                  _const_spec((KV_LORA, B_HW)), _const_spec((B_VW, KV_LORA)),
                  lane_vec, lane_vec, lane_vec, lane_vec, lane_vec, lane_vec],
        out_specs=[out_spec, out_spec, vt_spec],
        out_shape=[out_sds, out_sds, jax.ShapeDtypeStruct((BATCH, B_VW, SEQ), BF16)],
        compiler_params=pltpu.CompilerParams(vmem_limit_bytes=VMEM_LARGE),
        name="b_proj",
    )(x, pos_col, gain, w_in, qn, kvn, wq, wqs, wk, wvt, qg, qgs, kg, krg, krgs, freq)


B_HEADS_PER_STEP = 4
B_LOOKAHEAD = 3


def _b_attn_kernel(q_ref, k_ref, vt_ref, o_ref):
    heads = range(B_HEADS_PER_STEP)
    groups = [slice(hd * LANES, (hd + 1) * LANES) for hd in heads]
    n_query_tiles = SEQ // B_TQ
    half = B_TQ // 2

    def chain(qt, hd):
        pieces = [(qt, hd, j * B_TK, B_TK, 0, B_TQ) for j in range(qt * B_TQ // B_TK)]
        return pieces + [(qt, hd, qt * B_TQ, half, 0, B_TQ), (qt, hd, qt * B_TQ + half, half, half, half)]

    def scores(qt, hd, key_lo, n_keys, q_lo, n_q):
        queries = slice(qt * B_TQ + q_lo, qt * B_TQ + q_lo + n_q)
        return _dot_nt(k_ref[key_lo:key_lo + n_keys, groups[hd]], q_ref[queries, groups[hd]])

    def accumulate(unit, s, carry):
        qt, hd, key_lo, n_keys, q_lo, n_q = unit
        m_all, acc_all = carry
        m, acc = m_all[:, q_lo:q_lo + n_q], acc_all[:, q_lo:q_lo + n_q]
        first_query = qt * B_TQ + q_lo
        if key_lo + n_keys - 1 > first_query:
            key = lax.broadcasted_iota(jnp.int32, (n_keys, n_q), 0)
            query = lax.broadcasted_iota(jnp.int32, (n_keys, n_q), 1)
            s = jnp.where(key + (key_lo - first_query) <= query, s, NEG)
        m_new = jnp.maximum(m, jnp.max(s, axis=0, keepdims=True))
        alpha = jnp.exp2(m - m_new)
        e = jnp.exp2(s - m_new).astype(BF16)
        vt_ext = jnp.concatenate([vt_ref[hd * V_DIM:(hd + 1) * V_DIM, key_lo:key_lo + n_keys],
                                  jnp.ones((V_DIM, n_keys), BF16)], axis=0)
        acc_new = alpha * acc + _dot(vt_ext, e)
        if q_lo:
            m_new = jnp.concatenate([m_all[:, :q_lo], m_new], axis=1)
            acc_new = jnp.concatenate([acc_all[:, :q_lo], acc_new], axis=1)
        return m_new, acc_new

    chains = [chain(qt, hd) for qt in reversed(range(n_query_tiles)) for hd in heads]
    units = [c[i] for i in range(len(chains[0])) for c in chains if i < len(c)]
    last = {c[-1] for c in chains}
    init = (jnp.full((1, B_TQ), NEG, F32), jnp.zeros((2 * V_DIM, B_TQ), F32))
    carry = {(qt, hd): init for qt in range(n_query_tiles) for hd in heads}
    raw = {u: scores(*u) for u in units[:B_LOOKAHEAD]}
    for i, unit in enumerate(units):
        qt, hd = unit[0], unit[1]
        if i + B_LOOKAHEAD < len(units):
            ahead = units[i + B_LOOKAHEAD]
            raw[ahead] = scores(*ahead)
        carry[(qt, hd)] = accumulate(unit, raw.pop(unit), carry[(qt, hd)])
        if hd % 2 == 1 and unit in last:
            outs = [acc[:V_DIM, :] * (1.0 / acc[V_DIM:V_DIM + 1, :])
                    for _, acc in (carry.pop((qt, hd - 1)), carry.pop((qt, hd)))]
            o_ref[qt * B_TQ:(qt + 1) * B_TQ, (hd - 1) * V_DIM:(hd + 1) * V_DIM] = (
                jnp.concatenate(outs, axis=0).T.astype(BF16))


def _b_attn(q, k, vt):
    n = B_HEADS_PER_STEP
    return pl.pallas_call(
        _b_attn_kernel,
        grid=(BATCH, B_HEADS // n),
        in_specs=[pl.BlockSpec((SEQ, n * LANES), lambda b, p: (b, p)),
                  pl.BlockSpec((SEQ, n * LANES), lambda b, p: (b, p)),
                  pl.BlockSpec((None, n * V_DIM, SEQ), lambda b, p: (b, p, 0))],
        out_specs=pl.BlockSpec((SEQ, n * V_DIM), lambda b, p: (b, p)),
        out_shape=jax.ShapeDtypeStruct((TOKENS, B_VW), BF16),
        compiler_params=pltpu.CompilerParams(vmem_limit_bytes=VMEM_MEDIUM),
        name="b_attn",
    )(q, k, vt)


def _prep_a(w_in, q_gain, k_gain):
    qg = jnp.concatenate([q_gain, q_gain])[None, :]
    kg = jnp.concatenate([k_gain, k_gain])[None, :]
    return w_in.astype(BF16), qg, kg


def _head_groups(w, per_head, src_lo, src_hi, dst_lo):
    rows = w.shape[0]
    w3 = w.reshape(rows, B_HEADS, per_head)[:, :, src_lo:src_hi]
    out = jnp.zeros((rows, B_HEADS, LANES), w.dtype)
    out = out.at[:, :, dst_lo:dst_lo + (src_hi - src_lo)].set(w3)
    return out


def _prep_b(w_in, w_uq, w_ukv, q_gain, k_gain):
    half = QK_ROPE // 2
    t1 = slice(QK_NOPE, QK_NOPE + half)
    t2 = slice(QK_NOPE + half, B_DQK)
    rows = w_in.shape[0]
    rope_cols = w_in[:, Q_LORA + KV_LORA:]
    zeros = lambda n: jnp.zeros((rows, n), w_in.dtype)
    rope_group = jnp.concatenate([zeros(QK_NOPE), rope_cols, zeros(LANES - B_DQK)], axis=1)
    partner_group = jnp.concatenate(
        [zeros(QK_NOPE), rope_cols[:, half:], rope_cols[:, :half], zeros(LANES - B_DQK)], axis=1)
    win = jnp.concatenate([w_in[:, :Q_LORA + KV_LORA], rope_group, partner_group], axis=1).astype(BF16)

    wq = _head_groups(w_uq, B_DQK, 0, B_DQK, 0)
    wqs = (_head_groups(w_uq, B_DQK, t2.start, t2.stop, t1.start)
           + _head_groups(w_uq, B_DQK, t1.start, t1.stop, t2.start))
    wk = _head_groups(w_ukv, QK_NOPE + V_DIM, 0, QK_NOPE, 0)
    wvt = w_ukv.reshape(KV_LORA, B_HEADS, QK_NOPE + V_DIM)[:, :, QK_NOPE:].reshape(KV_LORA, B_VW).T.astype(BF16)
    flat = lambda a: a.reshape(a.shape[0], B_HW).astype(BF16)

    def lane_vec(pieces):
        out = jnp.zeros((LANES,), F32)
        for lo, vals in pieces:
            out = out.at[lo:lo + vals.shape[0]].set(vals)
        return out[None, :]

    qg = lane_vec([(0, q_gain)])
    qgs = lane_vec([(t1.start, q_gain[t2]), (t2.start, q_gain[t1])])
    kg = lane_vec([(0, k_gain[:QK_NOPE])])
    krg = lane_vec([(QK_NOPE, k_gain[QK_NOPE:])])
    krgs = lane_vec([(t1.start, k_gain[t2]), (t2.start, k_gain[t1])])
    return win, flat(wq), flat(wqs), flat(wk), wvt, qg, qgs, kg, krg, krgs


def _rope_freq():
    inv = ROPE_BASE ** (-np.arange(0, QK_ROPE, 2, dtype=np.float32) / QK_ROPE)
    out = np.zeros((1, LANES), np.float32)
    half = QK_ROPE // 2
    out[0, QK_NOPE:QK_NOPE + half] = inv
    out[0, QK_NOPE + half:B_DQK] = inv
    return jnp.asarray(out)


def kernel(x, positions, rel_bias, ffn_norm1, ffn1_wg, ffn1_wu, ffn1_wd, mix_norm, ffn_norm2, ffn2_wg,
           ffn2_wu, ffn2_wd, a_w_in, a_q_gain, a_k_gain, a_sinks, a_w_out, b_w_in, b_q_norm, b_kv_norm,
           b_w_uq, b_w_ukv, b_q_gain, b_k_gain, b_w_out):
    assert x.shape == (BATCH, SEQ, D_MODEL) and positions.shape == (BATCH, SEQ)
    xt = x.reshape(TOKENS, D_MODEL)
    pos_col = positions.reshape(TOKENS, 1)
    pos_row = positions.reshape(TOKENS // BLOCK, 1, BLOCK)
    table = jnp.zeros((A_HEADS, LANES), F32).at[:, :NUM_BUCKETS].set(rel_bias.T)
    bf = lambda w: w.astype(BF16)
    row = lambda v: v[None, :]
    gain1, gain2 = ffn_norm1[:, None, :], ffn_norm2[:, None, :]
    weights = _first_ffn_weights(0, ffn1_wg, ffn1_wu, ffn1_wd)

    for i in range(DEPTH):
        xt, weights = _ffn(xt, gain1, i, *weights, cast_ahead=(i, ffn2_wg, ffn2_wu, ffn2_wd))
        j = i // N_MIXERS
        if i % N_MIXERS == 0:
            w, qg, kg = _prep_a(a_w_in[j], a_q_gain[j], a_k_gain[j])
            q, k2, v2 = _a_qkv(xt, row(mix_norm[i]), w, qg, kg)
            attn = _a_attn(q, k2, v2, pos_col, pos_row, table, a_sinks[j])
            w_out = bf(a_w_out[j])
        else:
            prep = _prep_b(b_w_in[j], b_w_uq[j], b_w_ukv[j], b_q_gain[j], b_k_gain[j])
            win, wq, wqs, wk, wvt, qg, qgs, kg, krg, krgs = prep
            q, k, vt = _b_proj(xt, pos_col, row(mix_norm[i]), win, row(b_q_norm[j]), row(b_kv_norm[j]),
                              wq, wqs, wk, wvt, qg, qgs, kg, krg, krgs, _rope_freq())
            attn = _b_attn(q, k, vt)
            w_out = bf(b_w_out[j])
        ahead = (i + 1, ffn1_wg, ffn1_wu, ffn1_wd) if i + 1 < DEPTH else None
        xt, weights = _ffn(xt, gain2, i, *weights, attn=attn, w_out=w_out, cast_ahead=ahead)
    return xt.reshape(BATCH, SEQ, D_MODEL)
```

```python
import functools
import math

import numpy as np
import jax
import jax.numpy as jnp
from jax import lax
from jax.experimental import pallas as pl
from jax.experimental.pallas import tpu as pltpu

D_MODEL = 1024
BATCH = 8
SEQ = 2048
DEPTH = 2
N_MIXERS = 2
A_HEADS = 16
A_KV_HEADS = 2
A_HEAD_DIM = 64
WINDOW = 128
BLOCK = 128
NUM_BUCKETS = 32
MAX_DISTANCE = 128
B_HEADS = 16
Q_LORA = 256
KV_LORA = 128
QK_NOPE = 64
QK_ROPE = 32
V_DIM = 64
ROPE_BASE = 10000.0
D_FF = 2816
EPS = 1e-6
NEG = -1e30

TOKENS = BATCH * SEQ
LANES = 128
HALF = LANES // 2
B_DQK = QK_NOPE + QK_ROPE
MIB = 1024 * 1024
VMEM_LIMIT = 60 * MIB
VMEM_SMALL = 16 * MIB
VMEM_MEDIUM = 24 * MIB
VMEM_LARGE = 56 * MIB

LOG2E = math.log2(math.e)

F32 = jnp.float32
BF16 = jnp.bfloat16


def _rms_rows(x, gain):
    return x * lax.rsqrt(jnp.mean(x * x, axis=-1, keepdims=True) + EPS) * gain


def _dot(a, b):
    return jnp.dot(a, b, preferred_element_type=F32)


def _dot_nt(a, b):
    return lax.dot_general(a, b, (((1,), (1,)), ((), ())), preferred_element_type=F32)


def _const_spec(shape):
    nd = len(shape)
    return pl.BlockSpec(shape, lambda *_: (0,) * nd, pipeline_mode=pl.Buffered(1))


FFN_TM = 512
MXU_WIDTH = 256
FFN_CHUNKS = ((0, 4 * MXU_WIDTH), (4 * MXU_WIDTH, 8 * MXU_WIDTH), (8 * MXU_WIDTH, D_FF))


def _ffn_body(x, g_ref, wgu_ref, wd_ref, o_ref):
    h = _rms_rows(x, g_ref[...]).astype(BF16)
    down = None
    for lo, hi in FFN_CHUNKS:
        gate_up = _dot(h, wgu_ref[:, 2 * lo:2 * hi])
        acts = []
        for c in range((hi - lo) // MXU_WIDTH):
            gate = gate_up[:, 2 * c * MXU_WIDTH:(2 * c + 1) * MXU_WIDTH]
            up = gate_up[:, (2 * c + 1) * MXU_WIDTH:(2 * c + 2) * MXU_WIDTH]
            acts.append((gate * jax.nn.sigmoid(gate) * up).astype(BF16))
        part = _dot(jnp.concatenate(acts, axis=1), wd_ref[lo:hi, :])
        down = part if down is None else down + part
    o_ref[...] = x + 0.5 * down


def _ffn_kernel(*refs, project, cast_ahead):
    refs = list(refs)
    x = refs.pop(0)[...]
    if project:
        a_ref, wo_ref = refs.pop(0), refs.pop(0)
        x = x + _dot(a_ref[...], wo_ref[...])
    g_ref, wgu_ref, wd_ref = refs[:3]
    refs = refs[3:]
    if cast_ahead:
        cast_refs = refs[:3] + refs[4:6]
        per = (TOKENS // FFN_TM) // CAST_CHUNKS

        @pl.when(pl.program_id(0) % per == 0)
        def _():
            _cast_ffn_weights(*cast_refs)
        refs = refs[3:]
    _ffn_body(x, g_ref, wgu_ref, wd_ref, refs[0])


def _cast_ffn_weights(wg_src, wu_src, wd_src, wgu_dst, wd_dst):
    for c in range(D_FF // MXU_WIDTH):
        cols = slice(c * MXU_WIDTH, (c + 1) * MXU_WIDTH)
        wgu_dst[:, 2 * c * MXU_WIDTH:(2 * c + 1) * MXU_WIDTH] = wg_src[:, cols].astype(BF16)
        wgu_dst[:, (2 * c + 1) * MXU_WIDTH:(2 * c + 2) * MXU_WIDTH] = wu_src[:, cols].astype(BF16)
    wd_dst[...] = wd_src[...].astype(BF16)


CAST_CHUNKS = 16


def _cast_specs(layer, stacks, steps):
    per = steps // CAST_CHUNKS
    in_specs = [pl.BlockSpec((None, w.shape[1] // CAST_CHUNKS, w.shape[2]), lambda i: (layer, i // per, 0))
                for w in stacks]
    out_specs = [pl.BlockSpec((D_MODEL // CAST_CHUNKS, 2 * D_FF), lambda i: (i // per, 0)),
                 pl.BlockSpec((D_FF // CAST_CHUNKS, D_MODEL), lambda i: (i // per, 0))]
    out_shape = [jax.ShapeDtypeStruct((D_MODEL, 2 * D_FF), BF16), jax.ShapeDtypeStruct((D_FF, D_MODEL), BF16)]
    return in_specs, out_specs, out_shape


def _first_ffn_weights(layer, wg, wu, wd):
    steps = CAST_CHUNKS
    in_specs, out_specs, out_shape = _cast_specs(layer, (wg, wu, wd), steps)
    return pl.pallas_call(_cast_ffn_weights, grid=(steps,), in_specs=in_specs, out_specs=out_specs,
                          out_shape=out_shape, name="cast_ffn_weights")(wg, wu, wd)


def _ffn(x, gain, layer, wgu, wd, attn=None, w_out=None, cast_ahead=None):
    tm = FFN_TM
    steps = TOKENS // tm
    row_spec = pl.BlockSpec((tm, D_MODEL), lambda i: (i, 0))
    in_specs, args = [row_spec], [x]
    if attn is not None:
        in_specs += [row_spec, _const_spec((D_MODEL, D_MODEL))]
        args += [attn, w_out]
    in_specs += [pl.BlockSpec((None, 1, D_MODEL), lambda i: (layer, 0, 0), pipeline_mode=pl.Buffered(1)),
                 _const_spec((D_MODEL, 2 * D_FF)), _const_spec((D_FF, D_MODEL))]
    args += [gain, wgu, wd]
    out_specs = [row_spec]
    out_shape = [jax.ShapeDtypeStruct((TOKENS, D_MODEL), F32)]
    if cast_ahead is not None:
        next_layer, *stacks = cast_ahead
        cast_in, cast_out, cast_shape = _cast_specs(next_layer, stacks, steps)
        in_specs += cast_in
        args += stacks
        out_specs += cast_out
        out_shape += cast_shape
    outs = pl.pallas_call(
        functools.partial(_ffn_kernel, project=attn is not None, cast_ahead=cast_ahead is not None),
        grid=(steps,),
        in_specs=in_specs,
        out_specs=out_specs,
        out_shape=out_shape,
        compiler_params=pltpu.CompilerParams(vmem_limit_bytes=VMEM_LIMIT),
        name="ffn" if attn is None else "proj_ffn",
    )(*args)
    return outs[0], tuple(outs[1:])


A_TM = 512
A_QW = A_HEADS * A_HEAD_DIM
A_KW = 4 * LANES
A_PAIRS = A_HEADS // 2
A_ROWS = 512
A_LOOKAHEAD = 1


def _a_qkv_kernel(x_ref, g_ref, w_ref, qg_ref, kg_ref, q_ref, k_ref, v_ref):
    h = _rms_rows(x_ref[...], g_ref[...]).astype(BF16)
    qkv = _dot(h, w_ref[...])
    lane = lax.broadcasted_iota(jnp.int32, (1, LANES), 1)
    low = lane < HALF
    root_d = math.sqrt(A_HEAD_DIM)
    q_table = qg_ref[...] * (root_d * A_HEAD_DIM ** -0.5 * LOG2E)
    k_table = kg_ref[...] * root_d

    def pair_norm(pair, table):
        sq = pair * pair
        ss_lo = jnp.sum(jnp.where(low, sq, 0.0), axis=-1, keepdims=True)
        ss_hi = jnp.sum(jnp.where(low, 0.0, sq), axis=-1, keepdims=True)
        r = jnp.where(low, lax.rsqrt(ss_lo + A_HEAD_DIM * EPS), lax.rsqrt(ss_hi + A_HEAD_DIM * EPS))
        return (pair * table * r).astype(BF16)

    for p in range(A_PAIRS):
        q_ref[:, p * LANES:(p + 1) * LANES] = pair_norm(qkv[:, p * LANES:(p + 1) * LANES], q_table)
    k_ref[...] = pair_norm(qkv[:, A_QW:A_QW + LANES], k_table)
    v_ref[...] = qkv[:, A_QW + LANES:].astype(BF16)


def _a_qkv(x, gain, w, qg, kg):
    tm = A_TM
    width = A_QW + 2 * LANES
    return pl.pallas_call(
        _a_qkv_kernel,
        grid=(TOKENS // tm,),
        in_specs=[pl.BlockSpec((tm, D_MODEL), lambda i: (i, 0)), _const_spec((1, D_MODEL)),
                  _const_spec((D_MODEL, width)), _const_spec((1, LANES)), _const_spec((1, LANES))],
        out_specs=[pl.BlockSpec((tm, A_QW), lambda i: (i, 0)), pl.BlockSpec((tm, LANES), lambda i: (i, 0)),
                   pl.BlockSpec((tm, LANES), lambda i: (i, 0))],
        out_shape=[jax.ShapeDtypeStruct((TOKENS, A_QW), BF16), jax.ShapeDtypeStruct((TOKENS, LANES), BF16),
                   jax.ShapeDtypeStruct((TOKENS, LANES), BF16)],
        compiler_params=pltpu.CompilerParams(vmem_limit_bytes=VMEM_SMALL),
        name="a_qkv",
    )(x, gain, w, qg, kg)


def _t5_bucket(dist):
    n = jnp.maximum(dist, 0)
    max_exact = NUM_BUCKETS // 2
    large = max_exact + (jnp.log(jnp.maximum(n, 1).astype(F32) / max_exact)
                         / math.log(MAX_DISTANCE / max_exact)
                         * (NUM_BUCKETS - max_exact)).astype(jnp.int32)
    large = jnp.minimum(large, NUM_BUCKETS - 1)
    return jnp.where(n < max_exact, n, large)


SUBLANES = 8


def _a_attn_kernel(sink_ref, q_ref, kc_ref, kp_ref, vc_ref, vp_ref, pq_ref, pkc_ref, pkp_ref, tbl_ref, o_ref,
                   bias_ref, shared_ref):
    row = lax.broadcasted_iota(jnp.int32, (BLOCK, BLOCK), 0)
    col = lax.broadcasted_iota(jnp.int32, (BLOCK, BLOCK), 1)
    cur_ok = col <= row
    first_in_seq = pl.program_id(1) == 0
    no_prev = jnp.where(first_in_seq, NEG, 0.0)
    tables = [jnp.broadcast_to(tbl_ref[h:h + 1, :] * LOG2E, (SUBLANES, LANES)) for h in range(A_HEADS)]
    group_heads = A_HEADS // A_KV_HEADS
    lane = lax.broadcasted_iota(jnp.int32, (1, LANES), 1)

    @pl.when(jnp.logical_and(pl.program_id(0) == 0, first_in_seq))
    def _():
        shared_ref[0] = 0

    def fill_shift_invariant():
        back = jnp.broadcast_to((BLOCK - lane) & (BLOCK - 1), (SUBLANES, LANES))
        idx = _t5_bucket(back)
        for h in range(A_HEADS):
            base = jnp.take_along_axis(tables[h], idx, axis=1)
            base = jnp.broadcast_to(base[0:1, :], (BLOCK, BLOCK))
            tile = pltpu.roll(base, 0, 1, stride=1, stride_axis=0)
            bias_ref[h] = jnp.where(cur_ok, tile, NEG)
            bias_ref[A_HEADS + h] = jnp.where(cur_ok, NEG, tile)

    def fill_general(pos_q, pos_cur, pos_prev):
        bucket = jnp.where(cur_ok, _t5_bucket(pos_q - pos_cur), _t5_bucket(pos_q - pos_prev))
        for c in range(BLOCK // SUBLANES):
            chunk = slice(c * SUBLANES, (c + 1) * SUBLANES)
            for h in range(A_HEADS):
                piece = jnp.take_along_axis(tables[h], bucket[chunk, :], axis=1)
                bias_ref[h, chunk, :] = jnp.where(cur_ok[chunk, :], piece, NEG)
                bias_ref[A_HEADS + h, chunk, :] = jnp.where(cur_ok[chunk, :], NEG, piece)

    blocks = range(A_ROWS // BLOCK)

    def rows_of(r):
        return slice(r * BLOCK, (r + 1) * BLOCK)

    def pairs_of(kv):
        return range(kv * group_heads // 2, (kv + 1) * group_heads // 2)

    def lane_groups(pair):
        low_half = lane < HALF
        swapped = pltpu.roll(pair, HALF, 1)
        zero = jnp.zeros_like(pair)
        return jnp.concatenate([jnp.where(low_half, pair, zero), jnp.where(low_half, zero, swapped),
                                jnp.where(low_half, swapped, zero), jnp.where(low_half, zero, pair)], axis=1)

    def pos_prev_of(r):
        return pkp_ref[0] if r == 0 else pkc_ref[r - 1]

    def all_bands(which_blocks):
        first, stop = which_blocks[0], which_blocks[-1] + 1
        if first == 0:
            k_rows = jnp.concatenate([kp_ref[...], kc_ref[:stop * BLOCK, :]], axis=0)
            v_rows = jnp.concatenate([vp_ref[...], vc_ref[:stop * BLOCK, :]], axis=0)
        else:
            k_rows = kc_ref[(first - 1) * BLOCK:stop * BLOCK, :]
            v_rows = vc_ref[(first - 1) * BLOCK:stop * BLOCK, :]
        k_groups, v_groups = lane_groups(k_rows), lane_groups(v_rows)
        return {r: (k_groups[(r - first) * BLOCK:(r - first + 2) * BLOCK, :],
                    v_groups[(r - first) * BLOCK:(r - first + 2) * BLOCK, :]) for r in which_blocks}

    low_ones = jnp.where(lane < HALF, 1.0, 0.0).astype(BF16)
    ones_lo = jnp.broadcast_to(low_ones, (2 * BLOCK, LANES))
    ones_hi = jnp.broadcast_to(1.0 - low_ones, (2 * BLOCK, LANES)).astype(BF16)

    def scores(r, kv, bands):
        k_band = bands[r][0]
        keys = jnp.concatenate([k_band[:, (2 * kv + parity) * LANES:(2 * kv + parity + 1) * LANES]
                                for parity in range(2)], axis=0)
        q_stack = jnp.concatenate([q_ref[rows_of(r), p * LANES:(p + 1) * LANES] for p in pairs_of(kv)], axis=0)
        return _dot_nt(q_stack, keys)

    def finish(r, kv, s_all, bias, bands):
        v_band = bands[r][1]
        probs, sink_terms = [], []
        for i, p in enumerate(pairs_of(kv)):
            pair_probs, pair_sinks = [], []
            for parity in range(2):
                head = 2 * p + parity
                s = s_all[i * BLOCK:(i + 1) * BLOCK, parity * 2 * BLOCK:(parity + 1) * 2 * BLOCK]
                s_prev = s[:, :BLOCK] + bias[A_HEADS + head]
                if r == 0:
                    s_prev = s_prev + no_prev
                s_cur = s[:, BLOCK:] + bias[head]
                sink = sink_ref[head] * LOG2E
                m = jnp.maximum(jnp.max(jnp.maximum(s_prev, s_cur), axis=-1, keepdims=True), sink)
                pair_probs += [jnp.exp2(s_prev - m), jnp.exp2(s_cur - m)]
                pair_sinks.append(jnp.exp2(sink - m))
            probs.append(jnp.concatenate(pair_probs, axis=1).astype(BF16))
            sink_terms.append(jnp.where(lane < HALF, pair_sinks[0], pair_sinks[1]))
        even, odd = (slice((2 * kv + parity) * LANES, (2 * kv + parity + 1) * LANES) for parity in range(2))
        v_ext = jnp.concatenate([jnp.concatenate([v_band[:, even], ones_lo], axis=1),
                                 jnp.concatenate([v_band[:, odd], ones_hi], axis=1)], axis=0)
        out = _dot(jnp.concatenate(probs, axis=0), v_ext)
        for i, p in enumerate(pairs_of(kv)):
            o_i = out[i * BLOCK:(i + 1) * BLOCK, :]
            o_ref[rows_of(r), p * LANES:(p + 1) * LANES] = (
                o_i[:, :LANES] * (1.0 / (o_i[:, LANES:] + sink_terms[i]))).astype(BF16)

    def attend(which_blocks):
        bias = [bias_ref[h] for h in range(2 * A_HEADS)]
        bands = all_bands(list(which_blocks))
        units = [(r, kv) for r in which_blocks for kv in range(A_KV_HEADS)]
        raw = {u: scores(*u, bands) for u in units[:A_LOOKAHEAD]}
        for i, unit in enumerate(units):
            if i + A_LOOKAHEAD < len(units):
                ahead = units[i + A_LOOKAHEAD]
                raw[ahead] = scores(*ahead, bands)
            finish(*unit, raw.pop(unit), bias, bands)

    @pl.when(shared_ref[0] == 0)
    def _():
        fill_shift_invariant()
        shared_ref[0] = 1

    attend(blocks)

    off = jnp.zeros((1, LANES), jnp.int32)
    for r in blocks:
        pos_cur, pos_prev = pkc_ref[r], pos_prev_of(r)
        start = jnp.min(pos_cur, axis=-1, keepdims=True)
        off_prev = jnp.where(pos_prev - start == lane - BLOCK, 0, 1)
        if r == 0:
            off_prev = off_prev * jnp.where(first_in_seq, 0, 1)
        off = off + jnp.where(pos_cur - start == lane, 0, 1) + off_prev

    @pl.when(jnp.sum(off) != 0)
    def _():
        for r in blocks:
            fill_general(pq_ref[rows_of(r), :], pkc_ref[r], pos_prev_of(r))
            attend([r])
        shared_ref[0] = 0


def _a_attn(q, k2, v2, pos_col, pos_row, table, sinks):
    steps = SEQ // A_ROWS
    blocks_per_step = A_ROWS // BLOCK
    blocks_per_seq = SEQ // BLOCK

    def cur(b, i):
        return (b * steps + i, 0)

    def prev(b, i):
        return (b * blocks_per_seq + jnp.maximum(i * blocks_per_step - 1, 0), 0)

    def cur3(b, i):
        return (b * steps + i, 0, 0)

    def prev3(b, i):
        return (b * blocks_per_seq + jnp.maximum(i * blocks_per_step - 1, 0), 0, 0)

    return pl.pallas_call(
        _a_attn_kernel,
        grid=(BATCH, steps),
        in_specs=[
            pl.BlockSpec(memory_space=pltpu.SMEM),
            pl.BlockSpec((A_ROWS, A_QW), cur),
            pl.BlockSpec((A_ROWS, LANES), cur),
            pl.BlockSpec((BLOCK, LANES), prev),
            pl.BlockSpec((A_ROWS, LANES), cur),
            pl.BlockSpec((BLOCK, LANES), prev),
            pl.BlockSpec((A_ROWS, 1), cur),
            pl.BlockSpec((blocks_per_step, 1, BLOCK), cur3),
            pl.BlockSpec((1, 1, BLOCK), prev3),
            _const_spec((A_HEADS, LANES)),
        ],
        out_specs=pl.BlockSpec((A_ROWS, A_QW), cur),
        out_shape=jax.ShapeDtypeStruct((TOKENS, A_QW), BF16),
        scratch_shapes=[pltpu.VMEM((2 * A_HEADS, BLOCK, BLOCK), F32), pltpu.SMEM((1,), jnp.int32)],
        compiler_params=pltpu.CompilerParams(dimension_semantics=("arbitrary", "arbitrary"),
                                             vmem_limit_bytes=VMEM_MEDIUM),
        name="a_attn",
    )(sinks, q, k2, k2, v2, v2, pos_col, pos_row, pos_row, table)


B_TM = 1024
B_CW = Q_LORA + KV_LORA + 2 * LANES
B_HW = B_HEADS * LANES
B_VW = B_HEADS * V_DIM
B_TQ = 512
B_TK = 512


def _b_proj_kernel(x_ref, pos_ref, g_ref, win_ref, qn_ref, kvn_ref, wq_ref, wqs_ref, wk_ref, wvt_ref,
                   qg_ref, qgs_ref, kg_ref, krg_ref, krgs_ref, freq_ref, q_ref, k_ref, vt_ref):
    h = _rms_rows(x_ref[...], g_ref[...]).astype(BF16)
    c = _dot(h, win_ref[...])
    cq = _rms_rows(c[:, :Q_LORA], qn_ref[...]).astype(BF16)
    ckv_f32 = _rms_rows(c[:, Q_LORA:Q_LORA + KV_LORA], kvn_ref[...])
    ckv = ckv_f32.astype(BF16)
    kr = c[:, Q_LORA + KV_LORA:Q_LORA + KV_LORA + LANES]
    kr_partner = c[:, Q_LORA + KV_LORA + LANES:]
    q = _dot(cq, wq_ref[...])
    q_partner = _dot(cq, wqs_ref[...])
    kn = _dot(ckv, wk_ref[...])
    vt_ref[...] = _dot(wvt_ref[...], ckv_f32.T.astype(BF16)).astype(BF16)

    lane = lax.broadcasted_iota(jnp.int32, (1, LANES), 1)
    ang = pos_ref[...].astype(F32) * freq_ref[...]
    cos = jnp.cos(ang)
    sin = jnp.sin(ang)
    half = QK_ROPE // 2
    sin_signed = jnp.where(lane < QK_NOPE + half, -sin, sin)

    root_d = math.sqrt(B_DQK)
    q_const = root_d * B_DQK ** -0.5 * LOG2E
    k_rope = (kr * krg_ref[...] * cos + kr_partner * krgs_ref[...] * sin_signed) * root_d
    k_gain = kg_ref[...] * root_d
    ss_rope = jnp.sum(kr * kr, axis=-1, keepdims=True) + B_DQK * EPS
    q_cos = qg_ref[...] * cos * q_const
    q_sin = qgs_ref[...] * sin_signed * q_const
    for hd in range(B_HEADS):
        g = slice(hd * LANES, (hd + 1) * LANES)
        qh = q[:, g]
        r = lax.rsqrt(jnp.sum(qh * qh, axis=-1, keepdims=True) + B_DQK * EPS)
        q_ref[:, g] = ((qh * q_cos + q_partner[:, g] * q_sin) * r).astype(BF16)
        kh = kn[:, g]
        rk = lax.rsqrt(jnp.sum(kh * kh, axis=-1, keepdims=True) + ss_rope)
        k_ref[:, g] = ((kh * k_gain + k_rope) * rk).astype(BF16)


def _b_proj(x, pos_col, gain, w_in, qn, kvn, wq, wqs, wk, wvt, qg, qgs, kg, krg, krgs, freq):
    tm = B_TM
    tiles_per_seq = SEQ // tm
    lane_vec = _const_spec((1, LANES))
    out_spec = pl.BlockSpec((tm, B_HW), lambda i: (i, 0))
    out_sds = jax.ShapeDtypeStruct((TOKENS, B_HW), BF16)
    vt_spec = pl.BlockSpec((None, B_VW, tm), lambda i: (i // tiles_per_seq, 0, i % tiles_per_seq))
    return pl.pallas_call(
        _b_proj_kernel,
        grid=(TOKENS // tm,),
        in_specs=[pl.BlockSpec((tm, D_MODEL), lambda i: (i, 0)), pl.BlockSpec((tm, 1), lambda i: (i, 0)),
                  _const_spec((1, D_MODEL)), _const_spec((D_MODEL, B_CW)), _const_spec((1, Q_LORA)),
                  _const_spec((1, KV_LORA)), _const_spec((Q_LORA, B_HW)), _const_spec((Q_LORA, B_HW)),
                  _const_spec((KV_LORA, B_HW)), _const_spec((B_VW, KV_LORA)),
                  lane_vec, lane_vec, lane_vec, lane_vec, lane_vec, lane_vec],
        out_specs=[out_spec, out_spec, vt_spec],
        out_shape=[out_sds, out_sds, jax.ShapeDtypeStruct((BATCH, B_VW, SEQ), BF16)],
        compiler_params=pltpu.CompilerParams(vmem_limit_bytes=VMEM_LARGE),
        name="b_proj",
    )(x, pos_col, gain, w_in, qn, kvn, wq, wqs, wk, wvt, qg, qgs, kg, krg, krgs, freq)


B_HEADS_PER_STEP = 4
B_LOOKAHEAD = 3


def _b_attn_kernel(q_ref, k_ref, vt_ref, o_ref):
    heads = range(B_HEADS_PER_STEP)
    groups = [slice(hd * LANES, (hd + 1) * LANES) for hd in heads]
    n_query_tiles = SEQ // B_TQ
    half = B_TQ // 2

    def chain(qt, hd):
        pieces = [(qt, hd, j * B_TK, B_TK, 0, B_TQ) for j in range(qt * B_TQ // B_TK)]
        return pieces + [(qt, hd, qt * B_TQ, half, 0, B_TQ), (qt, hd, qt * B_TQ + half, half, half, half)]

    def scores(qt, hd, key_lo, n_keys, q_lo, n_q):
        queries = slice(qt * B_TQ + q_lo, qt * B_TQ + q_lo + n_q)
        return _dot_nt(k_ref[key_lo:key_lo + n_keys, groups[hd]], q_ref[queries, groups[hd]])

    def accumulate(unit, s, carry):
        qt, hd, key_lo, n_keys, q_lo, n_q = unit
        m_all, acc_all = carry
        m, acc = m_all[:, q_lo:q_lo + n_q], acc_all[:, q_lo:q_lo + n_q]
        first_query = qt * B_TQ + q_lo
        if key_lo + n_keys - 1 > first_query:
            key = lax.broadcasted_iota(jnp.int32, (n_keys, n_q), 0)
            query = lax.broadcasted_iota(jnp.int32, (n_keys, n_q), 1)
            s = jnp.where(key + (key_lo - first_query) <= query, s, NEG)
        m_new = jnp.maximum(m, jnp.max(s, axis=0, keepdims=True))
        alpha = jnp.exp2(m - m_new)
        e = jnp.exp2(s - m_new).astype(BF16)
        vt_ext = jnp.concatenate([vt_ref[hd * V_DIM:(hd + 1) * V_DIM, key_lo:key_lo + n_keys],
                                  jnp.ones((V_DIM, n_keys), BF16)], axis=0)
        acc_new = alpha * acc + _dot(vt_ext, e)
        if q_lo:
            m_new = jnp.concatenate([m_all[:, :q_lo], m_new], axis=1)
            acc_new = jnp.concatenate([acc_all[:, :q_lo], acc_new], axis=1)
        return m_new, acc_new

    chains = [chain(qt, hd) for qt in reversed(range(n_query_tiles)) for hd in heads]
    units = [c[i] for i in range(len(chains[0])) for c in chains if i < len(c)]
    last = {c[-1] for c in chains}
    init = (jnp.full((1, B_TQ), NEG, F32), jnp.zeros((2 * V_DIM, B_TQ), F32))
    carry = {(qt, hd): init for qt in range(n_query_tiles) for hd in heads}
    raw = {u: scores(*u) for u in units[:B_LOOKAHEAD]}
    for i, unit in enumerate(units):
        qt, hd = unit[0], unit[1]
        if i + B_LOOKAHEAD < len(units):
            ahead = units[i + B_LOOKAHEAD]
            raw[ahead] = scores(*ahead)
        carry[(qt, hd)] = accumulate(unit, raw.pop(unit), carry[(qt, hd)])
        if hd % 2 == 1 and unit in last:
            outs = [acc[:V_DIM, :] * (1.0 / acc[V_DIM:V_DIM + 1, :])
                    for _, acc in (carry.pop((qt, hd - 1)), carry.pop((qt, hd)))]
            o_ref[qt * B_TQ:(qt + 1) * B_TQ, (hd - 1) * V_DIM:(hd + 1) * V_DIM] = (
                jnp.concatenate(outs, axis=0).T.astype(BF16))


def _b_attn(q, k, vt):
    n = B_HEADS_PER_STEP
    return pl.pallas_call(
        _b_attn_kernel,
        grid=(BATCH, B_HEADS // n),
        in_specs=[pl.BlockSpec((SEQ, n * LANES), lambda b, p: (b, p)),
                  pl.BlockSpec((SEQ, n * LANES), lambda b, p: (b, p)),
                  pl.BlockSpec((None, n * V_DIM, SEQ), lambda b, p: (b, p, 0))],
        out_specs=pl.BlockSpec((SEQ, n * V_DIM), lambda b, p: (b, p)),
        out_shape=jax.ShapeDtypeStruct((TOKENS, B_VW), BF16),
        compiler_params=pltpu.CompilerParams(vmem_limit_bytes=VMEM_MEDIUM),
        name="b_attn",
    )(q, k, vt)


def _prep_a(w_in, q_gain, k_gain):
    qg = jnp.concatenate([q_gain, q_gain])[None, :]
    kg = jnp.concatenate([k_gain, k_gain])[None, :]
    return w_in.astype(BF16), qg, kg


def _head_groups(w, per_head, src_lo, src_hi, dst_lo):
    rows = w.shape[0]
    w3 = w.reshape(rows, B_HEADS, per_head)[:, :, src_lo:src_hi]
    out = jnp.zeros((rows, B_HEADS, LANES), w.dtype)
    out = out.at[:, :, dst_lo:dst_lo + (src_hi - src_lo)].set(w3)
    return out


def _prep_b(w_in, w_uq, w_ukv, q_gain, k_gain):
    half = QK_ROPE // 2
    t1 = slice(QK_NOPE, QK_NOPE + half)
    t2 = slice(QK_NOPE + half, B_DQK)
    rows = w_in.shape[0]
    rope_cols = w_in[:, Q_LORA + KV_LORA:]
    zeros = lambda n: jnp.zeros((rows, n), w_in.dtype)
    rope_group = jnp.concatenate([zeros(QK_NOPE), rope_cols, zeros(LANES - B_DQK)], axis=1)
    partner_group = jnp.concatenate(
        [zeros(QK_NOPE), rope_cols[:, half:], rope_cols[:, :half], zeros(LANES - B_DQK)], axis=1)
    win = jnp.concatenate([w_in[:, :Q_LORA + KV_LORA], rope_group, partner_group], axis=1).astype(BF16)

    wq = _head_groups(w_uq, B_DQK, 0, B_DQK, 0)
    wqs = (_head_groups(w_uq, B_DQK, t2.start, t2.stop, t1.start)
           + _head_groups(w_uq, B_DQK, t1.start, t1.stop, t2.start))
    wk = _head_groups(w_ukv, QK_NOPE + V_DIM, 0, QK_NOPE, 0)
    wvt = w_ukv.reshape(KV_LORA, B_HEADS, QK_NOPE + V_DIM)[:, :, QK_NOPE:].reshape(KV_LORA, B_VW).T.astype(BF16)
    flat = lambda a: a.reshape(a.shape[0], B_HW).astype(BF16)

    def lane_vec(pieces):
        out = jnp.zeros((LANES,), F32)
        for lo, vals in pieces:
            out = out.at[lo:lo + vals.shape[0]].set(vals)
        return out[None, :]

    qg = lane_vec([(0, q_gain)])
    qgs = lane_vec([(t1.start, q_gain[t2]), (t2.start, q_gain[t1])])
    kg = lane_vec([(0, k_gain[:QK_NOPE])])
    krg = lane_vec([(QK_NOPE, k_gain[QK_NOPE:])])
    krgs = lane_vec([(t1.start, k_gain[t2]), (t2.start, k_gain[t1])])
    return win, flat(wq), flat(wqs), flat(wk), wvt, qg, qgs, kg, krg, krgs


def _rope_freq():
    inv = ROPE_BASE ** (-np.arange(0, QK_ROPE, 2, dtype=np.float32) / QK_ROPE)
    out = np.zeros((1, LANES), np.float32)
    half = QK_ROPE // 2
    out[0, QK_NOPE:QK_NOPE + half] = inv
    out[0, QK_NOPE + half:B_DQK] = inv
    return jnp.asarray(out)


def kernel(x, positions, rel_bias, ffn_norm1, ffn1_wg, ffn1_wu, ffn1_wd, mix_norm, ffn_norm2, ffn2_wg,
           ffn2_wu, ffn2_wd, a_w_in, a_q_gain, a_k_gain, a_sinks, a_w_out, b_w_in, b_q_norm, b_kv_norm,
           b_w_uq, b_w_ukv, b_q_gain, b_k_gain, b_w_out):
    assert x.shape == (BATCH, SEQ, D_MODEL) and positions.shape == (BATCH, SEQ)
    xt = x.reshape(TOKENS, D_MODEL)
    pos_col = positions.reshape(TOKENS, 1)
    pos_row = positions.reshape(TOKENS // BLOCK, 1, BLOCK)
    table = jnp.zeros((A_HEADS, LANES), F32).at[:, :NUM_BUCKETS].set(rel_bias.T)
    bf = lambda w: w.astype(BF16)
    row = lambda v: v[None, :]
    gain1, gain2 = ffn_norm1[:, None, :], ffn_norm2[:, None, :]
    weights = _first_ffn_weights(0, ffn1_wg, ffn1_wu, ffn1_wd)

    for i in range(DEPTH):
        xt, weights = _ffn(xt, gain1, i, *weights, cast_ahead=(i, ffn2_wg, ffn2_wu, ffn2_wd))
        j = i // N_MIXERS
        if i % N_MIXERS == 0:
            w, qg, kg = _prep_a(a_w_in[j], a_q_gain[j], a_k_gain[j])
            q, k2, v2 = _a_qkv(xt, row(mix_norm[i]), w, qg, kg)
            attn = _a_attn(q, k2, v2, pos_col, pos_row, table, a_sinks[j])
            w_out = bf(a_w_out[j])
        else:
            prep = _prep_b(b_w_in[j], b_w_uq[j], b_w_ukv[j], b_q_gain[j], b_k_gain[j])
            win, wq, wqs, wk, wvt, qg, qgs, kg, krg, krgs = prep
            q, k, vt = _b_proj(xt, pos_col, row(mix_norm[i]), win, row(b_q_norm[j]), row(b_kv_norm[j]),
                              wq, wqs, wk, wvt, qg, qgs, kg, krg, krgs, _rope_freq())
            attn = _b_attn(q, k, vt)
            w_out = bf(b_w_out[j])
        ahead = (i + 1, ffn1_wg, ffn1_wu, ffn1_wd) if i + 1 < DEPTH else None
        xt, weights = _ffn(xt, gain2, i, *weights, attn=attn, w_out=w_out, cast_ahead=ahead)
    return xt.reshape(BATCH, SEQ, D_MODEL)
```

```python
import functools
import math

import numpy as np
import jax
import jax.numpy as jnp
from jax import lax
from jax.experimental import pallas as pl
from jax.experimental.pallas import tpu as pltpu

D_MODEL = 1024
BATCH = 8
SEQ = 2048
DEPTH = 2
N_MIXERS = 2
A_HEADS = 16
A_KV_HEADS = 2
A_HEAD_DIM = 64
WINDOW = 128
BLOCK = 128
NUM_BUCKETS = 32
MAX_DISTANCE = 128
B_HEADS = 16
Q_LORA = 256
KV_LORA = 128
QK_NOPE = 64
QK_ROPE = 32
V_DIM = 64
ROPE_BASE = 10000.0
D_FF = 2816
EPS = 1e-6
NEG = -1e30

TOKENS = BATCH * SEQ
LANES = 128
HALF = LANES // 2
B_DQK = QK_NOPE + QK_ROPE
MIB = 1024 * 1024
VMEM_LIMIT = 60 * MIB
VMEM_SMALL = 16 * MIB
VMEM_MEDIUM = 24 * MIB
VMEM_LARGE = 56 * MIB

LOG2E = math.log2(math.e)

F32 = jnp.float32
BF16 = jnp.bfloat16


def _rms_rows(x, gain):
    return x * lax.rsqrt(jnp.mean(x * x, axis=-1, keepdims=True) + EPS) * gain


def _dot(a, b):
    return jnp.dot(a, b, preferred_element_type=F32)


def _dot_nt(a, b):
    return lax.dot_general(a, b, (((1,), (1,)), ((), ())), preferred_element_type=F32)


def _const_spec(shape):
    nd = len(shape)
    return pl.BlockSpec(shape, lambda *_: (0,) * nd, pipeline_mode=pl.Buffered(1))


FFN_TM = 1024
MXU_WIDTH = 256
FFN_CHUNKS = ((0, 4 * MXU_WIDTH), (4 * MXU_WIDTH, 8 * MXU_WIDTH), (8 * MXU_WIDTH, D_FF))


def _ffn_body(x, g_ref, wgu_ref, wd_ref, o_ref):
    h = _rms_rows(x, g_ref[...]).astype(BF16)
    down = None
    for lo, hi in FFN_CHUNKS:
        gate_up = _dot(h, wgu_ref[:, 2 * lo:2 * hi])
        acts = []
        for c in range((hi - lo) // MXU_WIDTH):
            gate = gate_up[:, 2 * c * MXU_WIDTH:(2 * c + 1) * MXU_WIDTH]
            up = gate_up[:, (2 * c + 1) * MXU_WIDTH:(2 * c + 2) * MXU_WIDTH]
            acts.append((gate * jax.nn.sigmoid(gate) * up).astype(BF16))
        part = _dot(jnp.concatenate(acts, axis=1), wd_ref[lo:hi, :])
        down = part if down is None else down + part
    o_ref[...] = x + 0.5 * down


def _ffn_kernel(*refs, project, cast_ahead):
    refs = list(refs)
    x = refs.pop(0)[...]
    if project:
        a_ref, wo_ref = refs.pop(0), refs.pop(0)
        x = x + _dot(a_ref[...], wo_ref[...])
    g_ref, wgu_ref, wd_ref = refs[:3]
    refs = refs[3:]
    if cast_ahead:
        _cast_ffn_weights(*refs[:3], *refs[4:6])
        refs = refs[3:]
    _ffn_body(x, g_ref, wgu_ref, wd_ref, refs[0])


def _cast_ffn_weights(wg_src, wu_src, wd_src, wgu_dst, wd_dst):
    for c in range(D_FF // MXU_WIDTH):
        cols = slice(c * MXU_WIDTH, (c + 1) * MXU_WIDTH)
        wgu_dst[:, 2 * c * MXU_WIDTH:(2 * c + 1) * MXU_WIDTH] = wg_src[:, cols].astype(BF16)
        wgu_dst[:, (2 * c + 1) * MXU_WIDTH:(2 * c + 2) * MXU_WIDTH] = wu_src[:, cols].astype(BF16)
    wd_dst[...] = wd_src[...].astype(BF16)


def _cast_specs(layer, stacks, steps):
    in_specs = [pl.BlockSpec((None, w.shape[1] // steps, w.shape[2]), lambda i: (layer, i, 0)) for w in stacks]
    out_specs = [pl.BlockSpec((D_MODEL // steps, 2 * D_FF), lambda i: (i, 0)),
                 pl.BlockSpec((D_FF // steps, D_MODEL), lambda i: (i, 0))]
    out_shape = [jax.ShapeDtypeStruct((D_MODEL, 2 * D_FF), BF16), jax.ShapeDtypeStruct((D_FF, D_MODEL), BF16)]
    return in_specs, out_specs, out_shape


def _first_ffn_weights(layer, wg, wu, wd):
    steps = 8
    in_specs, out_specs, out_shape = _cast_specs(layer, (wg, wu, wd), steps)
    return pl.pallas_call(_cast_ffn_weights, grid=(steps,), in_specs=in_specs, out_specs=out_specs,
                          out_shape=out_shape, name="cast_ffn_weights")(wg, wu, wd)


def _ffn(x, gain, layer, wgu, wd, attn=None, w_out=None, cast_ahead=None):
    tm = FFN_TM
    steps = TOKENS // tm
    row_spec = pl.BlockSpec((tm, D_MODEL), lambda i: (i, 0))
    in_specs, args = [row_spec], [x]
    if attn is not None:
        in_specs += [row_spec, _const_spec((D_MODEL, D_MODEL))]
        args += [attn, w_out]
    in_specs += [pl.BlockSpec((None, 1, D_MODEL), lambda i: (layer, 0, 0), pipeline_mode=pl.Buffered(1)),
                 _const_spec((D_MODEL, 2 * D_FF)), _const_spec((D_FF, D_MODEL))]
    args += [gain, wgu, wd]
    out_specs = [row_spec]
    out_shape = [jax.ShapeDtypeStruct((TOKENS, D_MODEL), F32)]
    if cast_ahead is not None:
        next_layer, *stacks = cast_ahead
        cast_in, cast_out, cast_shape = _cast_specs(next_layer, stacks, steps)
        in_specs += cast_in
        args += stacks
        out_specs += cast_out
        out_shape += cast_shape
    outs = pl.pallas_call(
        functools.partial(_ffn_kernel, project=attn is not None, cast_ahead=cast_ahead is not None),
        grid=(steps,),
        in_specs=in_specs,
        out_specs=out_specs,
        out_shape=out_shape,
        compiler_params=pltpu.CompilerParams(vmem_limit_bytes=VMEM_LIMIT),
        name="ffn" if attn is None else "proj_ffn",
    )(*args)
    return outs[0], tuple(outs[1:])


A_TM = 512
A_QW = A_HEADS * A_HEAD_DIM
A_KW = 4 * LANES
A_PAIRS = A_HEADS // 2
A_ROWS = 512
A_LOOKAHEAD = 1


def _a_qkv_kernel(x_ref, g_ref, w_ref, qg_ref, kg_ref, q_ref, k_ref, v_ref):
    h = _rms_rows(x_ref[...], g_ref[...]).astype(BF16)
    qkv = _dot(h, w_ref[...])
    lane = lax.broadcasted_iota(jnp.int32, (1, LANES), 1)
    low = lane < HALF
    root_d = math.sqrt(A_HEAD_DIM)
    q_table = qg_ref[...] * (root_d * A_HEAD_DIM ** -0.5 * LOG2E)
    k_table = kg_ref[...] * root_d

    def pair_norm(pair, table):
        sq = pair * pair
        ss_lo = jnp.sum(jnp.where(low, sq, 0.0), axis=-1, keepdims=True)
        ss_hi = jnp.sum(jnp.where(low, 0.0, sq), axis=-1, keepdims=True)
        r = jnp.where(low, lax.rsqrt(ss_lo + A_HEAD_DIM * EPS), lax.rsqrt(ss_hi + A_HEAD_DIM * EPS))
        return (pair * table * r).astype(BF16)

    for p in range(A_PAIRS):
        q_ref[:, p * LANES:(p + 1) * LANES] = pair_norm(qkv[:, p * LANES:(p + 1) * LANES], q_table)
    k_ref[...] = pair_norm(qkv[:, A_QW:A_QW + LANES], k_table)
    v_ref[...] = qkv[:, A_QW + LANES:].astype(BF16)


def _a_qkv(x, gain, w, qg, kg):
    tm = A_TM
    width = A_QW + 2 * LANES
    return pl.pallas_call(
        _a_qkv_kernel,
        grid=(TOKENS // tm,),
        in_specs=[pl.BlockSpec((tm, D_MODEL), lambda i: (i, 0)), _const_spec((1, D_MODEL)),
                  _const_spec((D_MODEL, width)), _const_spec((1, LANES)), _const_spec((1, LANES))],
        out_specs=[pl.BlockSpec((tm, A_QW), lambda i: (i, 0)), pl.BlockSpec((tm, LANES), lambda i: (i, 0)),
                   pl.BlockSpec((tm, LANES), lambda i: (i, 0))],
        out_shape=[jax.ShapeDtypeStruct((TOKENS, A_QW), BF16), jax.ShapeDtypeStruct((TOKENS, LANES), BF16),
                   jax.ShapeDtypeStruct((TOKENS, LANES), BF16)],
        compiler_params=pltpu.CompilerParams(vmem_limit_bytes=VMEM_SMALL),
        name="a_qkv",
    )(x, gain, w, qg, kg)


def _t5_bucket(dist):
    n = jnp.maximum(dist, 0)
    max_exact = NUM_BUCKETS // 2
    large = max_exact + (jnp.log(jnp.maximum(n, 1).astype(F32) / max_exact)
                         / math.log(MAX_DISTANCE / max_exact)
                         * (NUM_BUCKETS - max_exact)).astype(jnp.int32)
    large = jnp.minimum(large, NUM_BUCKETS - 1)
    return jnp.where(n < max_exact, n, large)


SUBLANES = 8


def _a_attn_kernel(sink_ref, q_ref, kc_ref, kp_ref, vc_ref, vp_ref, pq_ref, pkc_ref, pkp_ref, tbl_ref, o_ref,
                   bias_ref, shared_ref):
    row = lax.broadcasted_iota(jnp.int32, (BLOCK, BLOCK), 0)
    col = lax.broadcasted_iota(jnp.int32, (BLOCK, BLOCK), 1)
    cur_ok = col <= row
    first_in_seq = pl.program_id(1) == 0
    no_prev = jnp.where(first_in_seq, NEG, 0.0)
    tables = [jnp.broadcast_to(tbl_ref[h:h + 1, :] * LOG2E, (SUBLANES, LANES)) for h in range(A_HEADS)]
    group_heads = A_HEADS // A_KV_HEADS
    lane = lax.broadcasted_iota(jnp.int32, (1, LANES), 1)

    @pl.when(jnp.logical_and(pl.program_id(0) == 0, first_in_seq))
    def _():
        shared_ref[0] = 0

    def fill_shift_invariant():
        back = jnp.broadcast_to((BLOCK - lane) & (BLOCK - 1), (SUBLANES, LANES))
        idx = _t5_bucket(back)
        for h in range(A_HEADS):
            base = jnp.take_along_axis(tables[h], idx, axis=1)
            base = jnp.broadcast_to(base[0:1, :], (BLOCK, BLOCK))
            tile = pltpu.roll(base, 0, 1, stride=1, stride_axis=0)
            bias_ref[h] = jnp.where(cur_ok, tile, NEG)
            bias_ref[A_HEADS + h] = jnp.where(cur_ok, NEG, tile)

    def fill_general(pos_q, pos_cur, pos_prev):
        bucket = jnp.where(cur_ok, _t5_bucket(pos_q - pos_cur), _t5_bucket(pos_q - pos_prev))
        for c in range(BLOCK // SUBLANES):
            chunk = slice(c * SUBLANES, (c + 1) * SUBLANES)
            for h in range(A_HEADS):
                piece = jnp.take_along_axis(tables[h], bucket[chunk, :], axis=1)
                bias_ref[h, chunk, :] = jnp.where(cur_ok[chunk, :], piece, NEG)
                bias_ref[A_HEADS + h, chunk, :] = jnp.where(cur_ok[chunk, :], NEG, piece)

    blocks = range(A_ROWS // BLOCK)

    def rows_of(r):
        return slice(r * BLOCK, (r + 1) * BLOCK)

    def pairs_of(kv):
        return range(kv * group_heads // 2, (kv + 1) * group_heads // 2)

    def lane_groups(pair):
        low_half = lane < HALF
        swapped = pltpu.roll(pair, HALF, 1)
        zero = jnp.zeros_like(pair)
        return jnp.concatenate([jnp.where(low_half, pair, zero), jnp.where(low_half, zero, swapped),
                                jnp.where(low_half, swapped, zero), jnp.where(low_half, zero, pair)], axis=1)

    def pos_prev_of(r):
        return pkp_ref[0] if r == 0 else pkc_ref[r - 1]

    def all_bands(which_blocks):
        first, stop = which_blocks[0], which_blocks[-1] + 1
        if first == 0:
            k_rows = jnp.concatenate([kp_ref[...], kc_ref[:stop * BLOCK, :]], axis=0)
            v_rows = jnp.concatenate([vp_ref[...], vc_ref[:stop * BLOCK, :]], axis=0)
        else:
            k_rows = kc_ref[(first - 1) * BLOCK:stop * BLOCK, :]
            v_rows = vc_ref[(first - 1) * BLOCK:stop * BLOCK, :]
        k_groups, v_groups = lane_groups(k_rows), lane_groups(v_rows)
        return {r: (k_groups[(r - first) * BLOCK:(r - first + 2) * BLOCK, :],
                    v_groups[(r - first) * BLOCK:(r - first + 2) * BLOCK, :]) for r in which_blocks}

    low_ones = jnp.where(lane < HALF, 1.0, 0.0).astype(BF16)
    ones_lo = jnp.broadcast_to(low_ones, (2 * BLOCK, LANES))
    ones_hi = jnp.broadcast_to(1.0 - low_ones, (2 * BLOCK, LANES)).astype(BF16)

    def scores(r, kv, bands):
        k_band = bands[r][0]
        keys = jnp.concatenate([k_band[:, (2 * kv + parity) * LANES:(2 * kv + parity + 1) * LANES]
                                for parity in range(2)], axis=0)
        q_stack = jnp.concatenate([q_ref[rows_of(r), p * LANES:(p + 1) * LANES] for p in pairs_of(kv)], axis=0)
        return _dot_nt(q_stack, keys)

    def finish(r, kv, s_all, bias, bands):
        v_band = bands[r][1]
        probs, sink_terms = [], []
        for i, p in enumerate(pairs_of(kv)):
            pair_probs, pair_sinks = [], []
            for parity in range(2):
                head = 2 * p + parity
                s = s_all[i * BLOCK:(i + 1) * BLOCK, parity * 2 * BLOCK:(parity + 1) * 2 * BLOCK]
                s_prev = s[:, :BLOCK] + bias[A_HEADS + head]
                if r == 0:
                    s_prev = s_prev + no_prev
                s_cur = s[:, BLOCK:] + bias[head]
                sink = sink_ref[head] * LOG2E
                m = jnp.maximum(jnp.max(jnp.maximum(s_prev, s_cur), axis=-1, keepdims=True), sink)
                pair_probs += [jnp.exp2(s_prev - m), jnp.exp2(s_cur - m)]
                pair_sinks.append(jnp.exp2(sink - m))
            probs.append(jnp.concatenate(pair_probs, axis=1).astype(BF16))
            sink_terms.append(jnp.where(lane < HALF, pair_sinks[0], pair_sinks[1]))
        even, odd = (slice((2 * kv + parity) * LANES, (2 * kv + parity + 1) * LANES) for parity in range(2))
        v_ext = jnp.concatenate([jnp.concatenate([v_band[:, even], ones_lo], axis=1),
                                 jnp.concatenate([v_band[:, odd], ones_hi], axis=1)], axis=0)
        out = _dot(jnp.concatenate(probs, axis=0), v_ext)
        for i, p in enumerate(pairs_of(kv)):
            o_i = out[i * BLOCK:(i + 1) * BLOCK, :]
            o_ref[rows_of(r), p * LANES:(p + 1) * LANES] = (
                o_i[:, :LANES] * (1.0 / (o_i[:, LANES:] + sink_terms[i]))).astype(BF16)

    def attend(which_blocks):
        bias = [bias_ref[h] for h in range(2 * A_HEADS)]
        bands = all_bands(list(which_blocks))
        units = [(r, kv) for r in which_blocks for kv in range(A_KV_HEADS)]
        raw = {u: scores(*u, bands) for u in units[:A_LOOKAHEAD]}
        for i, unit in enumerate(units):
            if i + A_LOOKAHEAD < len(units):
                ahead = units[i + A_LOOKAHEAD]
                raw[ahead] = scores(*ahead, bands)
            finish(*unit, raw.pop(unit), bias, bands)

    @pl.when(shared_ref[0] == 0)
    def _():
        fill_shift_invariant()
        shared_ref[0] = 1

    attend(blocks)

    off = jnp.zeros((1, LANES), jnp.int32)
    for r in blocks:
        pos_cur, pos_prev = pkc_ref[r], pos_prev_of(r)
        start = jnp.min(pos_cur, axis=-1, keepdims=True)
        off_prev = jnp.where(pos_prev - start == lane - BLOCK, 0, 1)
        if r == 0:
            off_prev = off_prev * jnp.where(first_in_seq, 0, 1)
        off = off + jnp.where(pos_cur - start == lane, 0, 1) + off_prev

    @pl.when(jnp.sum(off) != 0)
    def _():
        for r in blocks:
            fill_general(pq_ref[rows_of(r), :], pkc_ref[r], pos_prev_of(r))
            attend([r])
        shared_ref[0] = 0


def _a_attn(q, k2, v2, pos_col, pos_row, table, sinks):
    steps = SEQ // A_ROWS
    blocks_per_step = A_ROWS // BLOCK
    blocks_per_seq = SEQ // BLOCK

    def cur(b, i):
        return (b * steps + i, 0)

    def prev(b, i):
        return (b * blocks_per_seq + jnp.maximum(i * blocks_per_step - 1, 0), 0)

    def cur3(b, i):
        return (b * steps + i, 0, 0)

    def prev3(b, i):
        return (b * blocks_per_seq + jnp.maximum(i * blocks_per_step - 1, 0), 0, 0)

    return pl.pallas_call(
        _a_attn_kernel,
        grid=(BATCH, steps),
        in_specs=[
            pl.BlockSpec(memory_space=pltpu.SMEM),
            pl.BlockSpec((A_ROWS, A_QW), cur),
            pl.BlockSpec((A_ROWS, LANES), cur),
            pl.BlockSpec((BLOCK, LANES), prev),
            pl.BlockSpec((A_ROWS, LANES), cur),
            pl.BlockSpec((BLOCK, LANES), prev),
            pl.BlockSpec((A_ROWS, 1), cur),
            pl.BlockSpec((blocks_per_step, 1, BLOCK), cur3),
            pl.BlockSpec((1, 1, BLOCK), prev3),
            _const_spec((A_HEADS, LANES)),
        ],
        out_specs=pl.BlockSpec((A_ROWS, A_QW), cur),
        out_shape=jax.ShapeDtypeStruct((TOKENS, A_QW), BF16),
        scratch_shapes=[pltpu.VMEM((2 * A_HEADS, BLOCK, BLOCK), F32), pltpu.SMEM((1,), jnp.int32)],
        compiler_params=pltpu.CompilerParams(dimension_semantics=("arbitrary", "arbitrary"),
                                             vmem_limit_bytes=VMEM_MEDIUM),
        name="a_attn",
    )(sinks, q, k2, k2, v2, v2, pos_col, pos_row, pos_row, table)


B_TM = 1024
B_CW = Q_LORA + KV_LORA + 2 * LANES
B_HW = B_HEADS * LANES
B_VW = B_HEADS * V_DIM
B_TQ = 512
B_TK = 512


def _b_proj_kernel(x_ref, pos_ref, g_ref, win_ref, qn_ref, kvn_ref, wq_ref, wqs_ref, wk_ref, wvt_ref,
                   qg_ref, qgs_ref, kg_ref, krg_ref, krgs_ref, freq_ref, q_ref, k_ref, vt_ref):
    h = _rms_rows(x_ref[...], g_ref[...]).astype(BF16)
    c = _dot(h, win_ref[...])
    cq = _rms_rows(c[:, :Q_LORA], qn_ref[...]).astype(BF16)
    ckv_f32 = _rms_rows(c[:, Q_LORA:Q_LORA + KV_LORA], kvn_ref[...])
    ckv = ckv_f32.astype(BF16)
    kr = c[:, Q_LORA + KV_LORA:Q_LORA + KV_LORA + LANES]
    kr_partner = c[:, Q_LORA + KV_LORA + LANES:]
    q = _dot(cq, wq_ref[...])
    q_partner = _dot(cq, wqs_ref[...])
    kn = _dot(ckv, wk_ref[...])
    vt_ref[...] = _dot(wvt_ref[...], ckv_f32.T.astype(BF16)).astype(BF16)

    lane = lax.broadcasted_iota(jnp.int32, (1, LANES), 1)
    ang = pos_ref[...].astype(F32) * freq_ref[...]
    fold = LANES // QK_ROPE
    rows = ang.shape[0] // fold
    packed = ang[:rows]
    for b in range(1, fold):
        packed = packed + pltpu.roll(ang[b * rows:(b + 1) * rows], b * QK_ROPE, 1)
    cos_packed, sin_packed = jnp.cos(packed), jnp.sin(packed)
    rope_lanes = jnp.logical_and(lane >= QK_NOPE, lane < B_DQK)

    def unpack(table, outside):
        blocks = [table] + [pltpu.roll(table, LANES - b * QK_ROPE, 1) for b in range(1, fold)]
        return jnp.concatenate([jnp.where(rope_lanes, blk, outside) for blk in blocks], axis=0)

    cos = unpack(cos_packed, 1.0)
    sin = unpack(sin_packed, 0.0)
    half = QK_ROPE // 2
    sin_signed = jnp.where(lane < QK_NOPE + half, -sin, sin)

    root_d = math.sqrt(B_DQK)
    q_const = root_d * B_DQK ** -0.5 * LOG2E
    k_rope = (kr * krg_ref[...] * cos + kr_partner * krgs_ref[...] * sin_signed) * root_d
    k_gain = kg_ref[...] * root_d
    ss_rope = jnp.sum(kr * kr, axis=-1, keepdims=True) + B_DQK * EPS
    q_cos = qg_ref[...] * cos * q_const
    q_sin = qgs_ref[...] * sin_signed * q_const
    for hd in range(B_HEADS):
        g = slice(hd * LANES, (hd + 1) * LANES)
        qh = q[:, g]
        r = lax.rsqrt(jnp.sum(qh * qh, axis=-1, keepdims=True) + B_DQK * EPS)
        q_ref[:, g] = ((qh * q_cos + q_partner[:, g] * q_sin) * r).astype(BF16)
        kh = kn[:, g]
        rk = lax.rsqrt(jnp.sum(kh * kh, axis=-1, keepdims=True) + ss_rope)
        k_ref[:, g] = ((kh * k_gain + k_rope) * rk).astype(BF16)


def _b_proj(x, pos_col, gain, w_in, qn, kvn, wq, wqs, wk, wvt, qg, qgs, kg, krg, krgs, freq):
    tm = B_TM
    tiles_per_seq = SEQ // tm
    lane_vec = _const_spec((1, LANES))
    out_spec = pl.BlockSpec((tm, B_HW), lambda i: (i, 0))
    out_sds = jax.ShapeDtypeStruct((TOKENS, B_HW), BF16)
    vt_spec = pl.BlockSpec((None, B_VW, tm), lambda i: (i // tiles_per_seq, 0, i % tiles_per_seq))
    return pl.pallas_call(
        _b_proj_kernel,
        grid=(TOKENS // tm,),
        in_specs=[pl.BlockSpec((tm, D_MODEL), lambda i: (i, 0)), pl.BlockSpec((tm, 1), lambda i: (i, 0)),
                  _const_spec((1, D_MODEL)), _const_spec((D_MODEL, B_CW)), _const_spec((1, Q_LORA)),
                  _const_spec((1, KV_LORA)), _const_spec((Q_LORA, B_HW)), _const_spec((Q_LORA, B_HW)),
                  _const_spec((KV_LORA, B_HW)), _const_spec((B_VW, KV_LORA)),
                  lane_vec, lane_vec, lane_vec, lane_vec, lane_vec, lane_vec],
        out_specs=[out_spec, out_spec, vt_spec],
        out_shape=[out_sds, out_sds, jax.ShapeDtypeStruct((BATCH, B_VW, SEQ), BF16)],
        compiler_params=pltpu.CompilerParams(vmem_limit_bytes=VMEM_LARGE),
        name="b_proj",
    )(x, pos_col, gain, w_in, qn, kvn, wq, wqs, wk, wvt, qg, qgs, kg, krg, krgs, freq)


B_HEADS_PER_STEP = 4
B_LOOKAHEAD = 3


def _b_attn_kernel(q_ref, k_ref, vt_ref, o_ref):
    heads = range(B_HEADS_PER_STEP)
    groups = [slice(hd * LANES, (hd + 1) * LANES) for hd in heads]
    n_query_tiles = SEQ // B_TQ
    half = B_TQ // 2

    def chain(qt, hd):
        pieces = [(qt, hd, j * B_TK, B_TK, 0, B_TQ) for j in range(qt * B_TQ // B_TK)]
        return pieces + [(qt, hd, qt * B_TQ, half, 0, B_TQ), (qt, hd, qt * B_TQ + half, half, half, half)]

    def scores(qt, hd, key_lo, n_keys, q_lo, n_q):
        queries = slice(qt * B_TQ + q_lo, qt * B_TQ + q_lo + n_q)
        return _dot_nt(k_ref[key_lo:key_lo + n_keys, groups[hd]], q_ref[queries, groups[hd]])

    def accumulate(unit, s, carry):
        qt, hd, key_lo, n_keys, q_lo, n_q = unit
        m_all, acc_all = carry
        m, acc = m_all[:, q_lo:q_lo + n_q], acc_all[:, q_lo:q_lo + n_q]
        first_query = qt * B_TQ + q_lo
        if key_lo + n_keys - 1 > first_query:
            key = lax.broadcasted_iota(jnp.int32, (n_keys, n_q), 0)
            query = lax.broadcasted_iota(jnp.int32, (n_keys, n_q), 1)
            s = jnp.where(key + (key_lo - first_query) <= query, s, NEG)
        m_new = jnp.maximum(m, jnp.max(s, axis=0, keepdims=True))
        alpha = jnp.exp2(m - m_new)
        e = jnp.exp2(s - m_new).astype(BF16)
        vt_ext = jnp.concatenate([vt_ref[hd * V_DIM:(hd + 1) * V_DIM, key_lo:key_lo + n_keys],
                                  jnp.ones((V_DIM, n_keys), BF16)], axis=0)
        acc_new = alpha * acc + _dot(vt_ext, e)
        if q_lo:
            m_new = jnp.concatenate([m_all[:, :q_lo], m_new], axis=1)
            acc_new = jnp.concatenate([acc_all[:, :q_lo], acc_new], axis=1)
        return m_new, acc_new

    chains = [chain(qt, hd) for qt in reversed(range(n_query_tiles)) for hd in heads]
    units = [c[i] for i in range(len(chains[0])) for c in chains if i < len(c)]
    last = {c[-1] for c in chains}
    init = (jnp.full((1, B_TQ), NEG, F32), jnp.zeros((2 * V_DIM, B_TQ), F32))
    carry = {(qt, hd): init for qt in range(n_query_tiles) for hd in heads}
    raw = {u: scores(*u) for u in units[:B_LOOKAHEAD]}
    for i, unit in enumerate(units):
        qt, hd = unit[0], unit[1]
        if i + B_LOOKAHEAD < len(units):
            ahead = units[i + B_LOOKAHEAD]
            raw[ahead] = scores(*ahead)
        carry[(qt, hd)] = accumulate(unit, raw.pop(unit), carry[(qt, hd)])
        if hd % 2 == 1 and unit in last:
            outs = [acc[:V_DIM, :] * (1.0 / acc[V_DIM:V_DIM + 1, :])
                    for _, acc in (carry.pop((qt, hd - 1)), carry.pop((qt, hd)))]
            o_ref[qt * B_TQ:(qt + 1) * B_TQ, (hd - 1) * V_DIM:(hd + 1) * V_DIM] = (
                jnp.concatenate(outs, axis=0).T.astype(BF16))


def _b_attn(q, k, vt):
    n = B_HEADS_PER_STEP
    return pl.pallas_call(
        _b_attn_kernel,
        grid=(BATCH, B_HEADS // n),
        in_specs=[pl.BlockSpec((SEQ, n * LANES), lambda b, p: (b, p)),
                  pl.BlockSpec((SEQ, n * LANES), lambda b, p: (b, p)),
                  pl.BlockSpec((None, n * V_DIM, SEQ), lambda b, p: (b, p, 0))],
        out_specs=pl.BlockSpec((SEQ, n * V_DIM), lambda b, p: (b, p)),
        out_shape=jax.ShapeDtypeStruct((TOKENS, B_VW), BF16),
        compiler_params=pltpu.CompilerParams(vmem_limit_bytes=VMEM_MEDIUM),
        name="b_attn",
    )(q, k, vt)


def _prep_a(w_in, q_gain, k_gain):
    qg = jnp.concatenate([q_gain, q_gain])[None, :]
    kg = jnp.concatenate([k_gain, k_gain])[None, :]
    return w_in.astype(BF16), qg, kg


def _head_groups(w, per_head, src_lo, src_hi, dst_lo):
    rows = w.shape[0]
    w3 = w.reshape(rows, B_HEADS, per_head)[:, :, src_lo:src_hi]
    out = jnp.zeros((rows, B_HEADS, LANES), w.dtype)
    out = out.at[:, :, dst_lo:dst_lo + (src_hi - src_lo)].set(w3)
    return out


def _prep_b(w_in, w_uq, w_ukv, q_gain, k_gain):
    half = QK_ROPE // 2
    t1 = slice(QK_NOPE, QK_NOPE + half)
    t2 = slice(QK_NOPE + half, B_DQK)
    rows = w_in.shape[0]
    rope_cols = w_in[:, Q_LORA + KV_LORA:]
    zeros = lambda n: jnp.zeros((rows, n), w_in.dtype)
    rope_group = jnp.concatenate([zeros(QK_NOPE), rope_cols, zeros(LANES - B_DQK)], axis=1)
    partner_group = jnp.concatenate(
        [zeros(QK_NOPE), rope_cols[:, half:], rope_cols[:, :half], zeros(LANES - B_DQK)], axis=1)
    win = jnp.concatenate([w_in[:, :Q_LORA + KV_LORA], rope_group, partner_group], axis=1).astype(BF16)

    wq = _head_groups(w_uq, B_DQK, 0, B_DQK, 0)
    wqs = (_head_groups(w_uq, B_DQK, t2.start, t2.stop, t1.start)
           + _head_groups(w_uq, B_DQK, t1.start, t1.stop, t2.start))
    wk = _head_groups(w_ukv, QK_NOPE + V_DIM, 0, QK_NOPE, 0)
    wvt = w_ukv.reshape(KV_LORA, B_HEADS, QK_NOPE + V_DIM)[:, :, QK_NOPE:].reshape(KV_LORA, B_VW).T.astype(BF16)
    flat = lambda a: a.reshape(a.shape[0], B_HW).astype(BF16)

    def lane_vec(pieces):
        out = jnp.zeros((LANES,), F32)
        for lo, vals in pieces:
            out = out.at[lo:lo + vals.shape[0]].set(vals)
        return out[None, :]

    qg = lane_vec([(0, q_gain)])
    qgs = lane_vec([(t1.start, q_gain[t2]), (t2.start, q_gain[t1])])
    kg = lane_vec([(0, k_gain[:QK_NOPE])])
    krg = lane_vec([(QK_NOPE, k_gain[QK_NOPE:])])
    krgs = lane_vec([(t1.start, k_gain[t2]), (t2.start, k_gain[t1])])
    return win, flat(wq), flat(wqs), flat(wk), wvt, qg, qgs, kg, krg, krgs


def _rope_freq():
    inv = ROPE_BASE ** (-np.arange(0, QK_ROPE, 2, dtype=np.float32) / QK_ROPE)
    out = np.zeros((1, LANES), np.float32)
    half = QK_ROPE // 2
    out[0, QK_NOPE:QK_NOPE + half] = inv
    out[0, QK_NOPE + half:B_DQK] = inv
    return jnp.asarray(out)


def kernel(x, positions, rel_bias, ffn_norm1, ffn1_wg, ffn1_wu, ffn1_wd, mix_norm, ffn_norm2, ffn2_wg,
           ffn2_wu, ffn2_wd, a_w_in, a_q_gain, a_k_gain, a_sinks, a_w_out, b_w_in, b_q_norm, b_kv_norm,
           b_w_uq, b_w_ukv, b_q_gain, b_k_gain, b_w_out):
    assert x.shape == (BATCH, SEQ, D_MODEL) and positions.shape == (BATCH, SEQ)
    xt = x.reshape(TOKENS, D_MODEL)
    pos_col = positions.reshape(TOKENS, 1)
    pos_row = positions.reshape(TOKENS // BLOCK, 1, BLOCK)
    table = jnp.zeros((A_HEADS, LANES), F32).at[:, :NUM_BUCKETS].set(rel_bias.T)
    bf = lambda w: w.astype(BF16)
    row = lambda v: v[None, :]
    gain1, gain2 = ffn_norm1[:, None, :], ffn_norm2[:, None, :]
    weights = _first_ffn_weights(0, ffn1_wg, ffn1_wu, ffn1_wd)

    for i in range(DEPTH):
        xt, weights = _ffn(xt, gain1, i, *weights, cast_ahead=(i, ffn2_wg, ffn2_wu, ffn2_wd))
        j = i // N_MIXERS
        if i % N_MIXERS == 0:
            w, qg, kg = _prep_a(a_w_in[j], a_q_gain[j], a_k_gain[j])
            q, k2, v2 = _a_qkv(xt, row(mix_norm[i]), w, qg, kg)
            attn = _a_attn(q, k2, v2, pos_col, pos_row, table, a_sinks[j])
            w_out = bf(a_w_out[j])
        else:
            prep = _prep_b(b_w_in[j], b_w_uq[j], b_w_ukv[j], b_q_gain[j], b_k_gain[j])
            win, wq, wqs, wk, wvt, qg, qgs, kg, krg, krgs = prep
            q, k, vt = _b_proj(xt, pos_col, row(mix_norm[i]), win, row(b_q_norm[j]), row(b_kv_norm[j]),
                              wq, wqs, wk, wvt, qg, qgs, kg, krg, krgs, _rope_freq())
            attn = _b_attn(q, k, vt)
            w_out = bf(b_w_out[j])
        ahead = (i + 1, ffn1_wg, ffn1_wu, ffn1_wd) if i + 1 < DEPTH else None
        xt, weights = _ffn(xt, gain2, i, *weights, attn=attn, w_out=w_out, cast_ahead=ahead)
    return xt.reshape(BATCH, SEQ, D_MODEL)
```
